```python
import jax, jax.numpy as jnp
from jax import lax
import numpy as np

D_MODEL = 2048
BATCH = 2
SEQ = 4096
DEPTH = 4
DEC_BATCH = 8
DEC_SEQ = 8
PAST_LEN = 16384
PAGE_SIZE = 128

N_MIXERS = 3
D_FF = 4 * D_MODEL
RMS_EPS = 1e-6
LN_EPS = 1e-5
CONV_WIDTH = 31
D_CONV = D_MODEL
N_CONV_LAYERS = len(range(0, DEPTH, N_MIXERS))
D_RNN = D_MODEL
LRU_BLOCK = 256
N_LRU_BLOCKS = D_RNN // LRU_BLOCK
LRU_CONV_WIDTH = 4
LRU_C = 8.0
HEAD_DIM = 128
HEADS_PER_GROUP = 8
GROUPS = ((128, 1), (512, 4), (2048, 16))
N_GROUPS = len(GROUPS)
N_ATT_HEADS = N_GROUPS * HEADS_PER_GROUP
ATT_WIDTH = HEADS_PER_GROUP * HEAD_DIM
N_DIL_KEYS = GROUPS[0][0] // GROUPS[0][1] + 1
BLK = N_DIL_KEYS - 1
N_BUCKETS = 32
MAX_DISTANCE = 2048
NEG_INF = -1e30

kernel_name = 'hybrid_conformer_rglru_dilated_attn_step'


def _rms_norm(x, g):
    xf = x.astype(jnp.float32)
    y = xf * lax.rsqrt(jnp.mean(xf * xf, axis=-1, keepdims=True) + RMS_EPS)
    return (y * g.astype(jnp.float32)).astype(x.dtype)


def _layer_norm(x, g, b):
    xf = x.astype(jnp.float32)
    mu = jnp.mean(xf, axis=-1, keepdims=True)
    var = jnp.mean(jnp.square(xf - mu), axis=-1, keepdims=True)
    y = (xf - mu) * lax.rsqrt(var + LN_EPS)
    return (y * g.astype(jnp.float32) + b.astype(jnp.float32)).astype(x.dtype)


def _sq_relu_mlp(x, w1, w2):
    return jnp.square(jax.nn.relu(x @ w1)) @ w2


def _causal_depthwise(x_ext, w, b):
    c = x_ext.shape[-1]
    y = lax.conv_general_dilated(x_ext, w[:, None, :].astype(x_ext.dtype), window_strides=(1,),
                                 padding='VALID', dimension_numbers=('NWC', 'WIO', 'NWC'),
                                 feature_group_count=c)
    return y + b


def _conv_module(h, buf, w_glu, dw_w, dw_b, ln_g, ln_b, w_out):
    a, gate = jnp.split(h @ w_glu, 2, axis=-1)
    u = a * jax.nn.sigmoid(gate)
    u_ext = jnp.concatenate([buf, u], axis=1)
    c = _causal_depthwise(u_ext, dw_w, dw_b)
    c = jax.nn.silu(_layer_norm(c, ln_g, ln_b))
    return c @ w_out, u_ext[:, -(CONV_WIDTH - 1):]


def _rglru_block(h, h0, buf, w_in, conv_w, conv_b, w_a, b_a, w_x, b_x, lam, w_out):
    B, L, _ = h.shape
    y_br, x_br = jnp.split(h @ w_in, 2, axis=-1)
    y_br = jax.nn.gelu(y_br, approximate=True)
    x_ext = jnp.concatenate([buf, x_br], axis=1)
    xc = _causal_depthwise(x_ext, conv_w, conv_b)
    xb = xc.reshape(B, L, N_LRU_BLOCKS, LRU_BLOCK)
    r = jax.nn.sigmoid(jnp.einsum('blnc,ncd->blnd', xb, w_a).reshape(B, L, D_RNN) + b_a)
    i = jax.nn.sigmoid(jnp.einsum('blnc,ncd->blnd', xb, w_x).reshape(B, L, D_RNN) + b_x)
    log_a = -LRU_C * r.astype(jnp.float32) * jax.nn.softplus(-lam.astype(jnp.float32))
    a = jnp.exp(log_a)
    b = jnp.sqrt(-jnp.expm1(2.0 * log_a)) * (i * xc).astype(jnp.float32)

    def step(hc, ab):
        hc = ab[0] * hc + ab[1]
        return hc, hc

    h_last, hs = lax.scan(step, h0.astype(jnp.float32), (jnp.swapaxes(a, 0, 1), jnp.swapaxes(b, 0, 1)))
    hs = jnp.swapaxes(hs, 0, 1).astype(h.dtype)
    return (hs * y_br) @ w_out, h_last.astype(h0.dtype), x_ext[:, -(LRU_CONV_WIDTH - 1):]


def _t5_bucket(dist):
    max_exact = N_BUCKETS // 2
    d = np.asarray(dist, dtype=np.int32)
    large = max_exact + (np.log(np.maximum(d, max_exact) / max_exact) / np.log(MAX_DISTANCE / max_exact)
                         * (N_BUCKETS - max_exact)).astype(np.int32)
    return np.where(d < max_exact, d, np.minimum(large, N_BUCKETS - 1)).astype(np.int32)


def _group_bias(rel_bias, g):
    dil = GROUPS[g][1]
    buckets = _t5_bucket(np.arange(N_DIL_KEYS) * dil)
    tab = rel_bias[:, g * HEADS_PER_GROUP:(g + 1) * HEADS_PER_GROUP].astype(jnp.float32)
    return tab[buckets]


def _dilated_prompt(q, k, v, dil, bias):
    B, S, H, Dh = q.shape
    unit = dil * BLK
    Sp = -(-S // unit) * unit
    M = Sp // dil
    nb = M // BLK

    def strided(t):
        t = jnp.pad(t, ((0, 0), (0, Sp - S), (0, 0), (0, 0)))
        return t.reshape(B, M, dil, H, Dh).transpose(0, 2, 1, 3, 4).reshape(B, dil, nb, BLK, H, Dh)

    def with_prev(t):
        prev = jnp.pad(t[:, :, :-1], ((0, 0), (0, 0), (1, 0), (0, 0), (0, 0), (0, 0)))
        return jnp.concatenate([prev, t], axis=3)

    qs = strided(q)
    kb = with_prev(strided(k))
    vb = with_prev(strided(v))
    s = jnp.einsum('brnqhe,brnkhe->brnhqk', qs, kb).astype(jnp.float32) * (Dh ** -0.5)
    qi = np.arange(BLK)[:, None]
    kj = np.arange(2 * BLK)[None, :]
    delta = BLK + qi - kj
    key_idx = np.arange(nb)[:, None, None] * BLK - BLK + kj[None]
    valid = (delta >= 0)[None] & (delta < N_DIL_KEYS)[None] & (key_idx >= 0)
    s = s + jnp.transpose(bias[np.clip(delta, 0, N_DIL_KEYS - 1)], (2, 0, 1))
    s = jnp.where(valid[:, None], s, NEG_INF)
    m = jnp.max(s, axis=-1, keepdims=True)
    p = jnp.exp(s - m)
    den = jnp.sum(p, axis=-1, keepdims=True)
    o = jnp.einsum('brnhqk,brnkhe->brnqhe', (p / den).astype(v.dtype), vb)
    lse = (m + jnp.log(den))[..., 0]
    o = o.reshape(B, dil, M, H, Dh).transpose(0, 2, 1, 3, 4).reshape(B, Sp, H, Dh)[:, :S]
    lse = lse.transpose(0, 1, 2, 4, 3).reshape(B, dil, M, H).transpose(0, 2, 1, 3).reshape(B, Sp, H)[:, :S]
    return o, lse


def _dilated_step(q, kv_all, dil, bias, n_buf):
    T, Dh = q.shape[1], q.shape[-1]
    idx = n_buf + np.arange(T)[:, None] - dil * np.arange(N_DIL_KEYS)[None, :]
    valid = idx >= 0
    kvg = kv_all[:, np.maximum(idx, 0)]
    s = jnp.einsum('bthe,btkhe->bhtk', q, kvg[:, :, :, 0]).astype(jnp.float32) * (Dh ** -0.5)
    s = s + bias.T[None, :, None, :]
    s = jnp.where(valid, s, NEG_INF)
    m = jnp.max(s, axis=-1, keepdims=True)
    p = jnp.exp(s - m)
    den = jnp.sum(p, axis=-1, keepdims=True)
    o = jnp.einsum('bhtk,btkhe->bthe', (p / den).astype(q.dtype), kvg[:, :, :, 1])
    lse = jnp.transpose((m + jnp.log(den))[..., 0], (0, 2, 1))
    return o, lse


def _dilated_attention(h, w_qkv, w_o, rel_bias, bufs):
    B, L, _ = h.shape
    qkv = (h @ w_qkv).reshape(B, L, 3, N_GROUPS, HEADS_PER_GROUP, HEAD_DIM)
    outs, lses, rows = [], [], []
    for g, (win, dil) in enumerate(GROUPS):
        q = qkv[:, :, 0, g]
        kv = qkv[:, :, 1:, g]
        bias = _group_bias(rel_bias, g)
        if bufs is None:
            o, lse = _dilated_prompt(q, kv[:, :, 0], kv[:, :, 1], dil, bias)
            rows.append(kv[:, -min(win, L):])
        else:
            buf = bufs[g]
            o, lse = _dilated_step(q, jnp.concatenate([buf, kv], axis=1), dil, bias, buf.shape[1])
            rows.append(kv)
        outs.append(o)
        lses.append(lse)
    wts = jax.nn.softmax(jnp.stack(lses), axis=0)
    o = jnp.sum(wts[..., None] * jnp.stack(outs).astype(jnp.float32), axis=0).astype(h.dtype)
    return o.reshape(B, L, ATT_WIDTH) @ w_o, rows


def setup_inputs(seed: int = 0) -> dict:
    key = jax.random.key(seed)
    keys = jax.random.split(key, 40)
    counter = [0]

    def nk():
        counter[0] += 1
        return keys[counter[0] - 1]

    def nrm(shape, scale):
        return jax.random.normal(nk(), shape, jnp.float32) * scale

    def gain(shape):
        return 1.0 + nrm(shape, 0.01)

    u = jax.random.uniform(nk(), (D_RNN,), jnp.float32, 0.9, 0.999) ** (1.0 / LRU_C)
    lru_lambda = jnp.log(u) - jnp.log1p(-u)
    kv_shape = lambda w: (DEC_BATCH, min(w, PAST_LEN), 2, HEADS_PER_GROUP, HEAD_DIM)
    return {
        'x_prompt': nrm((BATCH, SEQ, D_MODEL), 1.0),
        'x_sample': nrm((DEC_BATCH, DEC_SEQ, D_MODEL), 1.0),
        'state_conv_l0': nrm((DEC_BATCH, CONV_WIDTH - 1, D_CONV), 0.5),
        'state_lru_h_l1': nrm((DEC_BATCH, D_RNN), 0.3),
        'state_lru_conv_l1': nrm((DEC_BATCH, LRU_CONV_WIDTH - 1, D_RNN), 1.0),
        'cache_kv_w128_l2': nrm(kv_shape(GROUPS[0][0]), 1.0),
        'cache_kv_w512_l2': nrm(kv_shape(GROUPS[1][0]), 1.0),
        'cache_kv_w2048_l2': nrm(kv_shape(GROUPS[2][0]), 1.0),
        'state_conv_l3': nrm((DEC_BATCH, CONV_WIDTH - 1, D_CONV), 0.5),
        'norm_mix_g': gain((DEPTH, D_MODEL)),
        'norm_ffn_g': gain((DEPTH, D_MODEL)),
        'final_norm_g': gain((D_MODEL,)),
        'ffn_w1': nrm((DEPTH, D_MODEL, D_FF), D_MODEL ** -0.5),
        'ffn_w2': nrm((DEPTH, D_FF, D_MODEL), D_FF ** -0.5),
        'conv_w_glu': nrm((N_CONV_LAYERS, D_MODEL, 2 * D_CONV), D_MODEL ** -0.5),
        'conv_dw_w': nrm((N_CONV_LAYERS, CONV_WIDTH, D_CONV), CONV_WIDTH ** -0.5),
        'conv_dw_b': nrm((N_CONV_LAYERS, D_CONV), 0.01),
        'conv_ln_g': gain((N_CONV_LAYERS, D_CONV)),
        'conv_ln_b': nrm((N_CONV_LAYERS, D_CONV), 0.01),
        'conv_w_out': nrm((N_CONV_LAYERS, D_CONV, D_MODEL), D_CONV ** -0.5),
        'lru_w_in': nrm((D_MODEL, 2 * D_RNN), D_MODEL ** -0.5),
        'lru_conv_w': nrm((LRU_CONV_WIDTH, D_RNN), LRU_CONV_WIDTH ** -0.5),
        'lru_conv_b': nrm((D_RNN,), 0.01),
        'lru_w_a': nrm((N_LRU_BLOCKS, LRU_BLOCK, LRU_BLOCK), LRU_BLOCK ** -0.5),
        'lru_b_a': nrm((D_RNN,), 0.01),
        'lru_w_x': nrm((N_LRU_BLOCKS, LRU_BLOCK, LRU_BLOCK), LRU_BLOCK ** -0.5),
        'lru_b_x': nrm((D_RNN,), 0.01),
        'lru_lambda': lru_lambda,
        'lru_w_out': nrm((D_RNN, D_MODEL), D_RNN ** -0.5),
        'att_w_qkv': nrm((D_MODEL, 3 * N_GROUPS * ATT_WIDTH), D_MODEL ** -0.5),
        'att_w_o': nrm((ATT_WIDTH, D_MODEL), ATT_WIDTH ** -0.5),
        'rel_bias': nrm((N_BUCKETS, N_ATT_HEADS), 0.2),
    }


def reference(x_prompt, x_sample, state_conv_l0, state_lru_h_l1, state_lru_conv_l1,
              cache_kv_w128_l2, cache_kv_w512_l2, cache_kv_w2048_l2, state_conv_l3,
              norm_mix_g, norm_ffn_g, final_norm_g, ffn_w1, ffn_w2,
              conv_w_glu, conv_dw_w, conv_dw_b, conv_ln_g, conv_ln_b, conv_w_out,
              lru_w_in, lru_conv_w, lru_conv_b, lru_w_a, lru_b_a, lru_w_x, lru_b_x, lru_lambda, lru_w_out,
              att_w_qkv, att_w_o, rel_bias):

    def trunk(x, st):
        B = x.shape[0]
        new = []
        for i in range(DEPTH):
            kind = i % N_MIXERS
            h = _rms_norm(x, norm_mix_g[i])
            if kind == 0:
                j = i // N_MIXERS
                buf = jnp.zeros((B, CONV_WIDTH - 1, D_CONV), x.dtype) if st is None else st[i][0]
                y, nbuf = _conv_module(h, buf, conv_w_glu[j], conv_dw_w[j], conv_dw_b[j],
                                       conv_ln_g[j], conv_ln_b[j], conv_w_out[j])
                new.append((nbuf,))
            elif kind == 1:
                if st is None:
                    h0 = jnp.zeros((B, D_RNN), x.dtype)
                    cb = jnp.zeros((B, LRU_CONV_WIDTH - 1, D_RNN), x.dtype)
                else:
                    h0, cb = st[i]
                y, h_last, ncb = _rglru_block(h, h0, cb, lru_w_in, lru_conv_w, lru_conv_b, lru_w_a, lru_b_a,
                                              lru_w_x, lru_b_x, lru_lambda, lru_w_out)
                new.append((h_last, ncb))
            else:
                y, rows = _dilated_attention(h, att_w_qkv, att_w_o, rel_bias, None if st is None else st[i])
                new.append(rows)
            x = x + y
            x = x + _sq_relu_mlp(_rms_norm(x, norm_ffn_g[i]), ffn_w1[i], ffn_w2[i])
        return _rms_norm(x, final_norm_g), new

    y_prompt, pst = trunk(x_prompt, None)
    y_sample, sst = trunk(x_sample, [(state_conv_l0,), (state_lru_h_l1, state_lru_conv_l1),
                                     (cache_kv_w128_l2, cache_kv_w512_l2, cache_kv_w2048_l2),
                                     (state_conv_l3,)])
    return (y_prompt, y_sample,
            pst[0][0], sst[0][0],
            pst[1][0], sst[1][0],
            pst[1][1], sst[1][1],
            pst[2][0], sst[2][0],
            pst[2][1], sst[2][1],
            pst[2][2], sst[2][2],
            pst[3][0], sst[3][0])
```

```python
import functools

import numpy as np
import jax
import jax.numpy as jnp
from jax import lax
from jax.experimental import pallas as pl
from jax.experimental.pallas import tpu as pltpu

RMS_EPS = 1e-6
LN_EPS = 1e-5
LRU_C = 8.0
NEG_INF = -1e30
N_MIXERS = 3
GROUPS = ((128, 1), (512, 4), (2048, 16))
N_DIL_KEYS = 129
BLK = 128
N_BUCKETS = 32
MAX_DISTANCE = 2048

V7X_VMEM_BYTES = 64 * 1024 * 1024
SUBLANES = 8
CONV_HALO = 32
LRU_HALO = 8

BF16 = jnp.bfloat16
F32 = jnp.float32


def _cparams(semantics, vmem_bytes):
    limit = int(min(max(vmem_bytes * 5 // 4, 32 * 1024 * 1024), V7X_VMEM_BYTES - 8 * 1024 * 1024))
    return pltpu.CompilerParams(dimension_semantics=semantics, vmem_limit_bytes=limit)


def _tile(n, pref):
    if n <= pref:
        return n
    t = pref
    while n % t:
        t //= 2
    return t


def _rms_to_bf16(x, g):
    ms = jnp.mean(x * x, axis=-1, keepdims=True)
    return (x * lax.rsqrt(ms + RMS_EPS) * g).astype(BF16)


def _dot(a, b):
    return jnp.dot(a, b, preferred_element_type=F32)


def _norm_proj_kernel(x_ref, g_ref, *refs, mode):
    if mode == 'plain':
        w_ref, o_ref, h_ref = refs
    elif mode == 'glu':
        wa_ref, wg_ref, o_ref, h_ref = refs
    else:
        wa_ref, wg_ref, o_ref, o2_ref, h_ref = refs

    @pl.when(pl.program_id(1) == 0)
    def _():
        h_ref[...] = _rms_to_bf16(x_ref[...], g_ref[...])

    h = h_ref[...]
    if mode == 'plain':
        o_ref[...] = _dot(h, w_ref[...])
    elif mode == 'glu':
        o_ref[...] = _dot(h, wa_ref[...]) * jax.nn.sigmoid(_dot(h, wg_ref[...]))
    else:
        o_ref[...] = jax.nn.gelu(_dot(h, wa_ref[...]), approximate=True)
        o2_ref[...] = _dot(h, wg_ref[...])


def _norm_proj(x, g, w, mode, tm_pref=512, tn_pref=512):
    m, d = x.shape
    n = w.shape[1]
    n_out = n if mode == 'plain' else n // 2
    tm = _tile(m, tm_pref)
    tn = _tile(n_out, tn_pref)
    nj = n_out // tn
    in_specs = [pl.BlockSpec((tm, d), lambda i, j: (i, 0)),
                pl.BlockSpec((1, d), lambda i, j: (0, 0)),
                pl.BlockSpec((d, tn), lambda i, j: (0, j))]
    args = [x, g, w]
    if mode != 'plain':
        in_specs.append(pl.BlockSpec((d, tn), lambda i, j: (0, j + nj)))
        args.append(w)
    o_spec = pl.BlockSpec((tm, tn), lambda i, j: (i, j))
    o_shape = jax.ShapeDtypeStruct((m, n_out), F32)
    n_o = 2 if mode == 'split' else 1
    vmem = 2 * tm * d * 4 + tm * d * 2 + 2 * len(args[2:]) * d * tn * 2 + 2 * n_o * tm * tn * 4
    return pl.pallas_call(
        functools.partial(_norm_proj_kernel, mode=mode),
        out_shape=[o_shape] * n_o if n_o == 2 else o_shape,
        grid=(m // tm, nj),
        in_specs=in_specs,
        out_specs=[o_spec] * n_o if n_o == 2 else o_spec,
        scratch_shapes=[pltpu.VMEM((tm, d), BF16)],
        compiler_params=_cparams(("parallel", "arbitrary"), vmem),
    )(*args)


def _ffn_kernel(x_ref, g_ref, w1_ref, w2_ref, fg_ref, o_ref, h_ref, *, final):
    f = pl.program_id(1)

    @pl.when(f == 0)
    def _():
        x = x_ref[...]
        h_ref[...] = _rms_to_bf16(x, g_ref[...])
        o_ref[...] = x

    hid = jnp.square(jnp.maximum(_dot(h_ref[...], w1_ref[...]), 0.0)).astype(BF16)
    o_ref[...] += _dot(hid, w2_ref[...])

    if final:
        @pl.when(f == pl.num_programs(1) - 1)
        def _():
            y = o_ref[...]
            ms = jnp.mean(y * y, axis=-1, keepdims=True)
            o_ref[...] = y * lax.rsqrt(ms + RMS_EPS) * fg_ref[...]


def _ffn(x, g, w1, w2, final_g, final, tm_pref=512, tf_pref=512):
    m, d = x.shape
    dff = w1.shape[1]
    tm = _tile(m, tm_pref)
    tf = _tile(dff, tf_pref)
    vmem = 2 * tm * d * 4 + tm * d * 2 + 2 * 2 * d * tf * 2 + 2 * tm * d * 4 + 2 * tm * tf * 4
    return pl.pallas_call(
        functools.partial(_ffn_kernel, final=final),
        out_shape=jax.ShapeDtypeStruct((m, d), F32),
        grid=(m // tm, dff // tf),
        in_specs=[pl.BlockSpec((tm, d), lambda i, f: (i, 0)),
                  pl.BlockSpec((1, d), lambda i, f: (0, 0)),
                  pl.BlockSpec((d, tf), lambda i, f: (0, f)),
                  pl.BlockSpec((tf, d), lambda i, f: (f, 0)),
                  pl.BlockSpec((1, d), lambda i, f: (0, 0))],
        out_specs=pl.BlockSpec((tm, d), lambda i, f: (i, 0)),
        scratch_shapes=[pltpu.VMEM((tm, d), BF16)],
        compiler_params=_cparams(("parallel", "arbitrary"), vmem),
    )(x, g, w1, w2, final_g)


def _conv_tail_kernel(x_ref, u_ref, halo_ref, dww_ref, dwb_ref, lng_ref, lnb_ref, wout_ref,
                      o_ref, uext_ref, y_ref, *, n_taps, rows, lanes):
    tm, c = u_ref.shape
    uext_ref[0:CONV_HALO, :] = halo_ref[0]
    uext_ref[CONV_HALO:CONV_HALO + tm, :] = u_ref[...]
    off = CONV_HALO - (n_taps - 1)

    def row_chunk(r, carry):
        r0 = pl.multiple_of(r * rows, rows)
        for l0 in range(0, c, lanes):
            win = uext_ref[pl.ds(r0, rows + CONV_HALO), l0:l0 + lanes]
            acc = jnp.zeros((rows, lanes), F32)
            for k in range(n_taps):
                acc = acc + dww_ref[k:k + 1, l0:l0 + lanes] * win[off + k:off + k + rows]
            uext_ref[pl.ds(r0, rows), l0:l0 + lanes] = acc + dwb_ref[:, l0:l0 + lanes]
        return carry

    lax.fori_loop(0, tm // rows, row_chunk, 0)

    def ln_chunk(r, carry):
        r0 = pl.multiple_of(r * rows, rows)
        cv = uext_ref[pl.ds(r0, rows), :]
        mu = jnp.mean(cv, axis=-1, keepdims=True)
        cc = cv - mu
        var = jnp.mean(cc * cc, axis=-1, keepdims=True)
        y = cc * lax.rsqrt(var + LN_EPS) * lng_ref[...] + lnb_ref[...]
        y_ref[pl.ds(r0, rows), :] = (y * jax.nn.sigmoid(y)).astype(BF16)
        return carry

    lax.fori_loop(0, tm // rows, ln_chunk, 0)
    o_ref[...] = x_ref[...] + _dot(y_ref[...], wout_ref[...])


def _conv_tail(x, u, halo, dww, dwb, lng, lnb, wout, tm, n_taps):
    m, d = x.shape
    c = u.shape[1]
    rows = 16 if tm % 16 == 0 else 8
    lanes = _tile(c, 512)
    vmem = 4 * tm * d * 4 + 2 * tm * c * 4 + 2 * c * d * 2 + (tm + CONV_HALO) * c * 4 + tm * c * 2
    return pl.pallas_call(
        functools.partial(_conv_tail_kernel, n_taps=n_taps, rows=rows, lanes=lanes),
        out_shape=jax.ShapeDtypeStruct((m, d), F32),
        grid=(m // tm,),
        in_specs=[pl.BlockSpec((tm, d), lambda i: (i, 0)),
                  pl.BlockSpec((tm, c), lambda i: (i, 0)),
                  pl.BlockSpec((1, CONV_HALO, c), lambda i: (i, 0, 0)),
                  pl.BlockSpec(dww.shape, lambda i: (0, 0)),
                  pl.BlockSpec((1, c), lambda i: (0, 0)),
                  pl.BlockSpec((1, c), lambda i: (0, 0)),
                  pl.BlockSpec((1, c), lambda i: (0, 0)),
                  pl.BlockSpec((c, d), lambda i: (0, 0))],
        out_specs=pl.BlockSpec((tm, d), lambda i: (i, 0)),
        scratch_shapes=[pltpu.VMEM((tm + CONV_HALO, c), F32), pltpu.VMEM((tm, c), BF16)],
        compiler_params=_cparams(("parallel",), vmem),
    )(x, u, halo, dww, dwb, lng, lnb, wout)


def _last_rows(past, seq, n):
    if seq.shape[1] >= n:
        return seq[:, seq.shape[1] - n:]
    return jnp.concatenate([past, seq], axis=1)[:, -n:]


def _halos(seq, past, tm, halo_rows):
    b, l, c = seq.shape
    p = past.shape[1]
    first = jnp.concatenate([jnp.zeros((b, halo_rows - p, c), seq.dtype), past], axis=1)[:, None]
    nt = l // tm
    if nt == 1:
        return first.reshape(b, halo_rows, c)
    rest = seq.reshape(b, nt, tm, c)[:, :-1, tm - halo_rows:]
    return jnp.concatenate([first, rest], axis=1).reshape(b * nt, halo_rows, c)


def _lru_tail_kernel(x_ref, ybr_ref, xbr_ref, halo_ref, h0_ref, cw_ref, cb_ref, wa_ref, ba_ref,
                     wx_ref, bx_ref, lam_ref, wout_ref, o_ref, hlast_ref,
                     xext_ref, a_ref, b_ref, hc_ref, *, n_taps, lanes):
    tm, c = xbr_ref.shape
    n_blocks, blk, _ = wa_ref.shape
    t = pl.program_id(1)

    @pl.when(t == 0)
    def _():
        hc_ref[...] = jnp.broadcast_to(h0_ref[0], hc_ref.shape)

    xext_ref[0:LRU_HALO, :] = halo_ref[0]
    xext_ref[LRU_HALO:LRU_HALO + tm, :] = xbr_ref[...]
    off = LRU_HALO - (n_taps - 1)

    sp = jax.nn.softplus(-lam_ref[...])
    for n in range(n_blocks):
        sl = slice(n * blk, (n + 1) * blk)
        xc = jnp.zeros((tm, blk), F32)
        for k in range(n_taps):
            xc = xc + cw_ref[k:k + 1, sl] * xext_ref[off + k:off + k + tm, sl]
        xc = xc + cb_ref[:, sl]
        xcb = xc.astype(BF16)
        r = jax.nn.sigmoid(_dot(xcb, wa_ref[n]) + ba_ref[:, sl])
        gi = jax.nn.sigmoid(_dot(xcb, wx_ref[n]) + bx_ref[:, sl])
        log_a = -LRU_C * r * sp[:, sl]
        a = jnp.exp(log_a)
        a_ref[:, sl] = a
        b_ref[:, sl] = jnp.sqrt(-jnp.tanh(log_a) * (a * a + 1.0)) * (gi * xc)

    row = lax.broadcasted_iota(jnp.int32, (SUBLANES, lanes), 0)

    for l0 in range(0, c, lanes):
        def group(gidx, h):
            r0 = pl.multiple_of(gidx * SUBLANES, SUBLANES)
            av = a_ref[pl.ds(r0, SUBLANES), l0:l0 + lanes]
            bv = b_ref[pl.ds(r0, SUBLANES), l0:l0 + lanes]
            for s in (1, 2, 4):
                a_sh = jnp.where(row >= s, pltpu.roll(av, s, axis=0), 1.0)
                b_sh = jnp.where(row >= s, pltpu.roll(bv, s, axis=0), 0.0)
                bv = av * b_sh + bv
                av = av * a_sh
            hs = av * h + bv
            b_ref[pl.ds(r0, SUBLANES), l0:l0 + lanes] = hs
            return jnp.broadcast_to(hs[SUBLANES - 1:SUBLANES, :], (SUBLANES, lanes))

        h_fin = lax.fori_loop(0, tm // SUBLANES, group, hc_ref[:, l0:l0 + lanes], unroll=2)
        hc_ref[:, l0:l0 + lanes] = h_fin

    gated = (b_ref[...] * ybr_ref[...]).astype(BF16)
    o_ref[...] = x_ref[...] + _dot(gated, wout_ref[...])

    @pl.when(t == pl.num_programs(1) - 1)
    def _():
        hlast_ref[0] = hc_ref[...]


def _lru_tail(x, ybr, xbr, halo, h0, cw, cb, wa, ba, wx, bx, lam, wout, batch, tm, n_taps):
    m, d = x.shape
    c = xbr.shape[1]
    nt = m // batch // tm
    lanes = _tile(c, 512)
    row2 = lambda b, t: (b * nt + t, 0)
    const2 = lambda b, t: (0, 0)
    vec = pl.BlockSpec((1, c), const2)
    vmem = (4 * tm * d * 4 + 4 * tm * c * 4 + 2 * c * d * 2 + 4 * wa.size * 2
            + (tm + LRU_HALO) * c * 4 + 2 * tm * c * 4)
    return pl.pallas_call(
        functools.partial(_lru_tail_kernel, n_taps=n_taps, lanes=lanes),
        out_shape=[jax.ShapeDtypeStruct((m, d), F32),
                   jax.ShapeDtypeStruct((batch, SUBLANES, c), F32)],
        grid=(batch, nt),
        in_specs=[pl.BlockSpec((tm, d), row2),
                  pl.BlockSpec((tm, c), row2),
                  pl.BlockSpec((tm, c), row2),
                  pl.BlockSpec((1, LRU_HALO, c), lambda b, t: (b * nt + t, 0, 0)),
                  pl.BlockSpec((1, 1, c), lambda b, t: (b, 0, 0)),
                  pl.BlockSpec(cw.shape, const2),
                  vec,
                  pl.BlockSpec(wa.shape, lambda b, t: (0, 0, 0)),
                  vec,
                  pl.BlockSpec(wx.shape, lambda b, t: (0, 0, 0)),
                  vec, vec,
                  pl.BlockSpec((c, d), const2)],
        out_specs=[pl.BlockSpec((tm, d), row2),
                   pl.BlockSpec((1, SUBLANES, c), lambda b, t: (b, 0, 0))],
        scratch_shapes=[pltpu.VMEM((tm + LRU_HALO, c), F32), pltpu.VMEM((tm, c), F32),
                        pltpu.VMEM((tm, c), F32), pltpu.VMEM((SUBLANES, c), F32)],
        compiler_params=_cparams(("parallel", "arbitrary"), vmem),
    )(x, ybr, xbr, halo, h0, cw, cb, wa, ba, wx, bx, lam, wout)


def _softmax_pv(s, v_parts):
    m = s[0].max(axis=-1, keepdims=True)
    for piece in s[1:]:
        m = jnp.maximum(m, piece.max(axis=-1, keepdims=True))
    p = [jnp.exp(piece - m) for piece in s]
    den = p[0].sum(axis=-1, keepdims=True)
    for piece in p[1:]:
        den = den + piece.sum(axis=-1, keepdims=True)
    o = _dot((p[0] / den).astype(BF16), v_parts[0])
    for piece, v in zip(p[1:], v_parts[1:]):
        o = o + _dot((piece / den).astype(BF16), v)
    return o, m + jnp.log(den)


def _nt_dot(a, b):
    return lax.dot_general(a, b, (((1,), (1,)), ((), ())), preferred_element_type=F32)


def _attn_prompt_kernel(q_ref, kc_ref, kp_ref, vc_ref, vp_ref, bias_ref, o_ref, lse_ref, *, heads, dh):
    n = pl.program_id(2)
    qi = lax.broadcasted_iota(jnp.int32, (BLK, 2 * BLK), 0)
    kj = lax.broadcasted_iota(jnp.int32, (BLK, 2 * BLK), 1)
    delta = BLK + qi - kj
    valid = (delta >= 0) & (delta < N_DIL_KEYS) & ((kj >= BLK) | (n > 0))
    scale = dh ** -0.5
    for h in range(heads):
        sl = slice(h * dh, (h + 1) * dh)
        q = q_ref[0, :, sl].astype(BF16)
        k2 = jnp.concatenate([kp_ref[0, :, sl], kc_ref[0, :, sl]], axis=0).astype(BF16)
        v2 = jnp.concatenate([vp_ref[0, :, sl], vc_ref[0, :, sl]], axis=0).astype(BF16)
        s = _nt_dot(q, k2) * scale + bias_ref[h]
        s = jnp.where(valid, s, NEG_INF)
        o, lse = _softmax_pv([s], [v2])
        o_ref[0, :, sl] = o
        lse_ref[0, :, sl] = jnp.broadcast_to(lse, (BLK, dh))


def _attn_prompt(qkv, bias_mat, g, dil, batch, seq, heads, dh):
    width = heads * dh
    n_cols = qkv.shape[1] // width
    n_groups = n_cols // 3
    ms = seq // dil
    nb = ms // BLK
    qkv3 = qkv.reshape(batch, ms, dil * n_cols * width)
    blk = (1, BLK, width)
    cur = lambda comp: (lambda b, r, n: (b, n, r * n_cols + comp * n_groups + g))
    prev = lambda comp: (lambda b, r, n: (b, jnp.maximum(n - 1, 0), r * n_cols + comp * n_groups + g))
    out_spec = pl.BlockSpec(blk, lambda b, r, n: (b, n, r))
    out_shape = jax.ShapeDtypeStruct((batch, ms, dil * width), F32)
    vmem = 2 * 7 * BLK * width * 4 + 2 * heads * BLK * 2 * BLK * 4
    o, lse = pl.pallas_call(
        functools.partial(_attn_prompt_kernel, heads=heads, dh=dh),
        out_shape=[out_shape, out_shape],
        grid=(batch, dil, nb),
        in_specs=[pl.BlockSpec(blk, cur(0)),
                  pl.BlockSpec(blk, cur(1)), pl.BlockSpec(blk, prev(1)),
                  pl.BlockSpec(blk, cur(2)), pl.BlockSpec(blk, prev(2)),
                  pl.BlockSpec(bias_mat.shape, lambda b, r, n: (0, 0, 0))],
        out_specs=[out_spec, out_spec],
        compiler_params=_cparams(("parallel", "parallel", "arbitrary"), vmem),
    )(qkv3, qkv3, qkv3, qkv3, qkv3, bias_mat)
    return o.reshape(batch * seq, width), lse.reshape(batch * seq, width)


def _attn_step_kernel(q_ref, kn_ref, vn_ref, cache_ref, biasc_ref, maskc_ref, biasn_ref, maskn_ref,
                      o_ref, lse_ref, knp_ref, vnp_ref, *, heads, dh):
    width = heads * dh
    t = q_ref.shape[1]
    scale = dh ** -0.5
    knp_ref[...] = jnp.zeros_like(knp_ref)
    vnp_ref[...] = jnp.zeros_like(vnp_ref)
    knp_ref[0:t, :] = kn_ref[0]
    vnp_ref[0:t, :] = vn_ref[0]
    for h in range(heads):
        sl = slice(h * dh, (h + 1) * dh)
        q = q_ref[0, :, sl].astype(BF16)
        kc = cache_ref[0, :, sl].astype(BF16)
        vc = cache_ref[0, :, width + h * dh:width + (h + 1) * dh].astype(BF16)
        kn = knp_ref[:, sl].astype(BF16)
        vn = vnp_ref[:, sl].astype(BF16)
        sc = jnp.where(maskc_ref[...] > 0, _nt_dot(q, kc) * scale + biasc_ref[h], NEG_INF)
        sn = jnp.where(maskn_ref[...] > 0, _nt_dot(q, kn) * scale + biasn_ref[h], NEG_INF)
        o, lse = _softmax_pv([sc, sn], [vc, vn])
        o_ref[0, :, sl] = o
        lse_ref[0, :, sl] = jnp.broadcast_to(lse, (t, dh))


def _attn_step(qkv, cache, tabs, g, batch, t, heads, dh):
    width = heads * dh
    n_groups = qkv.shape[1] // width // 3
    n_buf = cache.shape[1]
    qkv3 = qkv.reshape(batch, t, qkv.shape[1])
    biasc, maskc, biasn, maskn = tabs
    blk = (1, t, width)
    col = lambda comp: (lambda b: (b, 0, comp * n_groups + g))
    const = lambda a: pl.BlockSpec(a.shape, lambda b: (0,) * a.ndim)
    out_spec = pl.BlockSpec(blk, lambda b: (b, 0, 0))
    out_shape = jax.ShapeDtypeStruct((batch, t, width), F32)
    vmem = 2 * n_buf * 2 * width * 4 + 2 * biasc.size * 4 + 4 * n_buf * width * 2
    o, lse = pl.pallas_call(
        functools.partial(_attn_step_kernel, heads=heads, dh=dh),
        out_shape=[out_shape, out_shape],
        grid=(batch,),
        in_specs=[pl.BlockSpec(blk, col(0)), pl.BlockSpec(blk, col(1)), pl.BlockSpec(blk, col(2)),
                  pl.BlockSpec((1, n_buf, 2 * width), lambda b: (b, 0, 0)),
                  const(biasc), const(maskc), const(biasn), const(maskn)],
        out_specs=[out_spec, out_spec],
        scratch_shapes=[pltpu.VMEM((BLK, width), F32), pltpu.VMEM((BLK, width), F32)],
        compiler_params=_cparams(("parallel",), vmem),
    )(qkv3, qkv3, qkv3, cache, biasc, maskc, biasn, maskn)
    return o.reshape(batch * t, width), lse.reshape(batch * t, width)


def _attn_out_kernel(x_ref, o0_ref, o1_ref, o2_ref, l0_ref, l1_ref, l2_ref, wo_ref, out_ref):
    l0, l1, l2 = l0_ref[...], l1_ref[...], l2_ref[...]
    mx = jnp.maximum(jnp.maximum(l0, l1), l2)
    e0, e1, e2 = jnp.exp(l0 - mx), jnp.exp(l1 - mx), jnp.exp(l2 - mx)
    den = e0 + e1 + e2
    o = (e0 / den) * o0_ref[...] + (e1 / den) * o1_ref[...] + (e2 / den) * o2_ref[...]
    out_ref[...] = x_ref[...] + _dot(o.astype(BF16), wo_ref[...])


def _attn_out(x, outs, lses, wo, tm_pref=256):
    m, d = x.shape
    width = wo.shape[0]
    tm = _tile(m, tm_pref)
    row = lambda n: pl.BlockSpec((tm, n), lambda i: (i, 0))
    vmem = 4 * tm * d * 4 + 12 * tm * width * 4 + 2 * width * d * 2 + 8 * tm * width * 4
    return pl.pallas_call(
        _attn_out_kernel,
        out_shape=jax.ShapeDtypeStruct((m, d), F32),
        grid=(m // tm,),
        in_specs=[row(d)] + [row(width)] * 6 + [pl.BlockSpec((width, d), lambda i: (0, 0))],
        out_specs=row(d),
        compiler_params=_cparams(("parallel",), vmem),
    )(x, *outs, *lses, wo)


def _t5_bucket(dist):
    max_exact = N_BUCKETS // 2
    d = np.asarray(dist, dtype=np.int32)
    large = max_exact + (np.log(np.maximum(d, max_exact) / max_exact) / np.log(MAX_DISTANCE / max_exact)
                         * (N_BUCKETS - max_exact)).astype(np.int32)
    return np.where(d < max_exact, d, np.minimum(large, N_BUCKETS - 1)).astype(np.int32)


def _group_bias(rel_bias, g, heads):
    buckets = _t5_bucket(np.arange(N_DIL_KEYS) * GROUPS[g][1])
    return rel_bias[:, g * heads:(g + 1) * heads].astype(F32)[buckets]


def _prompt_bias(rel_bias, g, heads):
    qi = np.arange(BLK)[:, None]
    kj = np.arange(2 * BLK)[None, :]
    delta = np.clip(BLK + qi - kj, 0, N_DIL_KEYS - 1)
    return jnp.transpose(_group_bias(rel_bias, g, heads)[delta], (2, 0, 1))


def _step_tables(rel_bias, g, heads, n_buf, t):
    dil = GROUPS[g][1]
    bias = _group_bias(rel_bias, g, heads)
    tabs = []
    assert t <= BLK
    for cols, base in ((np.arange(n_buf), n_buf), (np.arange(BLK), 0)):
        dist = base + np.arange(t)[:, None] - cols[None, :]
        ok = (dist >= 0) & (dist % dil == 0) & (dist // dil < N_DIL_KEYS) & (cols[None, :] < base + t)
        idx = np.clip(dist // dil, 0, N_DIL_KEYS - 1)
        tabs += [jnp.transpose(bias[idx], (2, 0, 1)), jnp.asarray(ok.astype(np.float32))]
    return tabs


def _trunk(x3, st, p, tiles):
    batch, seq, d = x3.shape
    m = batch * seq
    x = x3.reshape(m, d)
    new = []
    depth = p['ffn_w1'].shape[0]
    for i in range(depth):
        kind = i % N_MIXERS
        g_mix = p['norm_mix_g'][i][None]
        if kind == 0:
            j = i // N_MIXERS
            n_taps, c = p['conv_dw_w'].shape[1:]
            buf = jnp.zeros((batch, n_taps - 1, c), F32) if st is None else st[i][0]
            u = _norm_proj(x, g_mix, p['conv_w_glu'][j], 'glu', tiles['proj_m'])
            u3 = u.reshape(batch, seq, c)
            tm = _tile(seq, tiles['conv_m'])
            halo = _halos(u3, buf, tm, CONV_HALO)
            dww = jnp.pad(p['conv_dw_w'][j], ((0, CONV_HALO - n_taps), (0, 0)))
            x = _conv_tail(x, u, halo, dww, p['conv_dw_b'][j][None], p['conv_ln_g'][j][None],
                           p['conv_ln_b'][j][None], p['conv_w_out'][j], tm, n_taps)
            new.append((_last_rows(buf, u3, n_taps - 1),))
        elif kind == 1:
            n_taps, c = p['lru_conv_w'].shape
            if st is None:
                h0 = jnp.zeros((batch, c), F32)
                cb = jnp.zeros((batch, n_taps - 1, c), F32)
            else:
                h0, cb = st[i]
            ybr, xbr = _norm_proj(x, g_mix, p['lru_w_in'], 'split', tiles['proj_m'])
            xbr3 = xbr.reshape(batch, seq, c)
            tm = _tile(seq, tiles['lru_m'])
            halo = _halos(xbr3, cb, tm, LRU_HALO)
            cw = jnp.pad(p['lru_conv_w'], ((0, SUBLANES - n_taps), (0, 0)))
            x, hl = _lru_tail(x, ybr, xbr, halo, h0[:, None], cw, p['lru_conv_b'][None], p['lru_w_a'],
                              p['lru_b_a'][None], p['lru_w_x'], p['lru_b_x'][None], p['lru_lambda'][None],
                              p['lru_w_out'], batch, tm, n_taps)
            new.append((hl[:, 0], _last_rows(cb, xbr3, n_taps - 1)))
        else:
            rel_bias = p['rel_bias']
            heads = rel_bias.shape[1] // len(GROUPS)
            width = p['att_w_o'].shape[0]
            dh = width // heads
            qkv = _norm_proj(x, g_mix, p['att_w_qkv'], 'plain', tiles['proj_m'])
            kv = qkv.reshape(batch, seq, 3, len(GROUPS), heads, dh)[:, :, 1:]
            outs, lses, rows = [], [], []
            for g, (win, dil) in enumerate(GROUPS):
                if st is None:
                    o, lse = _attn_prompt(qkv, _prompt_bias(rel_bias, g, heads), g, dil, batch, seq, heads, dh)
                    rows.append(kv[:, -min(win, seq):, :, g])
                else:
                    cache = st[i][g]
                    n_buf = cache.shape[1]
                    o, lse = _attn_step(qkv, cache.reshape(batch, n_buf, 2 * width),
                                        _step_tables(rel_bias, g, heads, n_buf, seq), g, batch, seq, heads, dh)
                    rows.append(kv[:, :, :, g])
                outs.append(o)
                lses.append(lse)
            x = _attn_out(x, outs, lses, p['att_w_o'], tiles['attn_out_m'])
            new.append(rows)
        x = _ffn(x, p['norm_ffn_g'][i][None], p['ffn_w1'][i], p['ffn_w2'][i], p['final_norm_g'][None],
                 final=(i == depth - 1), tm_pref=tiles['ffn_m'])
    return x.reshape(batch, seq, d), new


PROMPT_TILES = dict(proj_m=512, conv_m=256, lru_m=256, attn_out_m=256, ffn_m=512)
SAMPLE_TILES = dict(proj_m=512, conv_m=256, lru_m=256, attn_out_m=256, ffn_m=512)
MATMUL_WEIGHTS = ('ffn_w1', 'ffn_w2', 'conv_w_glu', 'conv_w_out', 'lru_w_in', 'lru_w_a', 'lru_w_x',
                  'lru_w_out', 'att_w_qkv', 'att_w_o')


def kernel(x_prompt, x_sample, state_conv_l0, state_lru_h_l1, state_lru_conv_l1, cache_kv_w128_l2, cache_kv_w512_l2, cache_kv_w2048_l2, state_conv_l3, norm_mix_g, norm_ffn_g, final_norm_g, ffn_w1, ffn_w2, conv_w_glu, conv_dw_w, conv_dw_b, conv_ln_g, conv_ln_b, conv_w_out, lru_w_in, lru_conv_w, lru_conv_b, lru_w_a, lru_b_a, lru_w_x, lru_b_x, lru_lambda, lru_w_out, att_w_qkv, att_w_o, rel_bias):
    p = dict(norm_mix_g=norm_mix_g, norm_ffn_g=norm_ffn_g, final_norm_g=final_norm_g, ffn_w1=ffn_w1,
             ffn_w2=ffn_w2, conv_w_glu=conv_w_glu, conv_dw_w=conv_dw_w, conv_dw_b=conv_dw_b,
             conv_ln_g=conv_ln_g, conv_ln_b=conv_ln_b, conv_w_out=conv_w_out, lru_w_in=lru_w_in,
             lru_conv_w=lru_conv_w, lru_conv_b=lru_conv_b, lru_w_a=lru_w_a, lru_b_a=lru_b_a, lru_w_x=lru_w_x,
             lru_b_x=lru_b_x, lru_lambda=lru_lambda, lru_w_out=lru_w_out, att_w_qkv=att_w_qkv,
             att_w_o=att_w_o, rel_bias=rel_bias)
    for name in MATMUL_WEIGHTS:
        p[name] = p[name].astype(BF16)
    y_prompt, pst = _trunk(x_prompt, None, p, PROMPT_TILES)
    y_sample, sst = _trunk(x_sample, [(state_conv_l0,), (state_lru_h_l1, state_lru_conv_l1),
                                      (cache_kv_w128_l2, cache_kv_w512_l2, cache_kv_w2048_l2),
                                      (state_conv_l3,)], p, SAMPLE_TILES)
    return (y_prompt, y_sample,
            pst[0][0], sst[0][0],
            pst[1][0], sst[1][0],
            pst[1][1], sst[1][1],
            pst[2][0], sst[2][0],
            pst[2][1], sst[2][1],
            pst[2][2], sst[2][2],
            pst[3][0], sst[3][0])
```

```python
import functools

import numpy as np
import jax
import jax.numpy as jnp
from jax import lax
from jax.experimental import pallas as pl
from jax.experimental.pallas import tpu as pltpu

RMS_EPS = 1e-6
LN_EPS = 1e-5
LRU_C = 8.0
NEG_INF = -1e30
N_MIXERS = 3
GROUPS = ((128, 1), (512, 4), (2048, 16))
N_DIL_KEYS = 129
BLK = 128
CHUNK = BLK * max(d for _, d in GROUPS)
N_BUCKETS = 32
MAX_DISTANCE = 2048

V7X_VMEM_BYTES = 64 * 1024 * 1024
SUBLANES = 8
CONV_HALO = 32
LRU_HALO = 8

BF16 = jnp.bfloat16
F32 = jnp.float32


def _cparams(semantics, vmem_bytes):
    limit = int(min(max(vmem_bytes * 3 // 2, 32 * 1024 * 1024), V7X_VMEM_BYTES - 8 * 1024 * 1024))
    return pltpu.CompilerParams(dimension_semantics=semantics, vmem_limit_bytes=limit)


def _tile(n, pref):
    if n <= pref:
        return n
    t = pref
    while n % t:
        t //= 2
    return t


def _rms_to_bf16(x, g):
    ms = jnp.mean(x * x, axis=-1, keepdims=True)
    return (x * lax.rsqrt(ms + RMS_EPS) * g).astype(BF16)


def _dot(a, b):
    return jnp.dot(a, b, preferred_element_type=F32)


def _nt_dot(a, b):
    return lax.dot_general(a, b, (((1,), (1,)), ((), ())), preferred_element_type=F32)


def _norm_proj_kernel(x_ref, g_ref, *refs, mode):
    if mode == 'plain':
        w_ref, o_ref, h_ref = refs
    elif mode == 'glu':
        wa_ref, wg_ref, o_ref, h_ref = refs
    else:
        wa_ref, wg_ref, o_ref, o2_ref, h_ref = refs

    @pl.when(pl.program_id(1) == 0)
    def _():
        h_ref[...] = _rms_to_bf16(x_ref[...], g_ref[...])

    h = h_ref[...]
    if mode == 'plain':
        o_ref[...] = _dot(h, w_ref[...])
    elif mode == 'glu':
        o_ref[...] = _dot(h, wa_ref[...]) * jax.nn.sigmoid(_dot(h, wg_ref[...]))
    else:
        o_ref[...] = jax.nn.gelu(_dot(h, wa_ref[...]), approximate=True)
        o2_ref[...] = _dot(h, wg_ref[...])


def _norm_proj(x, g, w, mode, tm_pref=512, tn_pref=512):
    m, d = x.shape
    n = w.shape[1]
    n_out = n if mode == 'plain' else n // 2
    tm = _tile(m, tm_pref)
    tn = _tile(n_out, tn_pref)
    nj = n_out // tn
    in_specs = [pl.BlockSpec((tm, d), lambda i, j: (i, 0)),
                pl.BlockSpec((1, d), lambda i, j: (0, 0)),
                pl.BlockSpec((d, tn), lambda i, j: (0, j))]
    args = [x, g, w]
    if mode != 'plain':
        in_specs.append(pl.BlockSpec((d, tn), lambda i, j: (0, j + nj)))
        args.append(w)
    o_spec = pl.BlockSpec((tm, tn), lambda i, j: (i, j))
    o_shape = jax.ShapeDtypeStruct((m, n_out), F32)
    n_o = 2 if mode == 'split' else 1
    vmem = 2 * tm * d * 4 + tm * d * 2 + 2 * len(args[2:]) * d * tn * 2 + 2 * n_o * tm * tn * 4
    return pl.pallas_call(
        functools.partial(_norm_proj_kernel, mode=mode),
        name=f"norm_proj_{mode}_m{m}",
        out_shape=[o_shape] * n_o if n_o == 2 else o_shape,
        grid=(m // tm, nj),
        in_specs=in_specs,
        out_specs=[o_spec] * n_o if n_o == 2 else o_spec,
        scratch_shapes=[pltpu.VMEM((tm, d), BF16)],
        compiler_params=_cparams(("parallel", "arbitrary"), vmem),
    )(*args)


def _ffn_kernel(x_ref, g_ref, w1_ref, w2_ref, fg_ref, o_ref, h_ref, *, final):
    f = pl.program_id(1)

    @pl.when(f == 0)
    def _():
        x = x_ref[...]
        h_ref[...] = _rms_to_bf16(x, g_ref[...])
        o_ref[...] = x

    hid = jnp.square(jnp.maximum(_dot(h_ref[...], w1_ref[...]), 0.0)).astype(BF16)
    o_ref[...] += _dot(hid, w2_ref[...])

    if final:
        @pl.when(f == pl.num_programs(1) - 1)
        def _():
            y = o_ref[...]
            ms = jnp.mean(y * y, axis=-1, keepdims=True)
            o_ref[...] = y * lax.rsqrt(ms + RMS_EPS) * fg_ref[...]


def _ffn(x, g, w1, w2, final_g, final, tm_pref=512, tf_pref=512):
    m, d = x.shape
    dff = w1.shape[1]
    tm = _tile(m, tm_pref)
    tf = _tile(dff, tf_pref)
    vmem = 2 * tm * d * 4 + tm * d * 2 + 2 * 2 * d * tf * 2 + 2 * tm * d * 4 + 2 * tm * tf * 4
    return pl.pallas_call(
        functools.partial(_ffn_kernel, final=final),
        name=f"ffn_m{m}" + ("_final" if final else ""),
        out_shape=jax.ShapeDtypeStruct((m, d), F32),
        grid=(m // tm, dff // tf),
        in_specs=[pl.BlockSpec((tm, d), lambda i, f: (i, 0)),
                  pl.BlockSpec((1, d), lambda i, f: (0, 0)),
                  pl.BlockSpec((d, tf), lambda i, f: (0, f)),
                  pl.BlockSpec((tf, d), lambda i, f: (f, 0)),
                  pl.BlockSpec((1, d), lambda i, f: (0, 0))],
        out_specs=pl.BlockSpec((tm, d), lambda i, f: (i, 0)),
        scratch_shapes=[pltpu.VMEM((tm, d), BF16)],
        compiler_params=_cparams(("parallel", "arbitrary"), vmem),
    )(x, g, w1, w2, final_g)


def _conv_tail_kernel(x_ref, u_ref, halo_ref, dww_ref, dwb_ref, lng_ref, lnb_ref, wout_ref,
                      o_ref, uext_ref, y_ref, *, n_taps, rows, lanes):
    tm, c = u_ref.shape
    uext_ref[0:CONV_HALO, :] = halo_ref[0]
    uext_ref[CONV_HALO:CONV_HALO + tm, :] = u_ref[...]
    off = CONV_HALO - (n_taps - 1)

    def row_chunk(r, carry):
        r0 = pl.multiple_of(r * rows, rows)
        for l0 in range(0, c, lanes):
            win = uext_ref[pl.ds(r0, rows + CONV_HALO), l0:l0 + lanes]
            acc = jnp.zeros((rows, lanes), F32)
            for k in range(n_taps):
                acc = acc + dww_ref[k:k + 1, l0:l0 + lanes] * win[off + k:off + k + rows]
            uext_ref[pl.ds(r0, rows), l0:l0 + lanes] = acc + dwb_ref[:, l0:l0 + lanes]
        return carry

    lax.fori_loop(0, tm // rows, row_chunk, 0)

    def ln_chunk(r, carry):
        r0 = pl.multiple_of(r * rows, rows)
        cv = uext_ref[pl.ds(r0, rows), :]
        mu = jnp.mean(cv, axis=-1, keepdims=True)
        cc = cv - mu
        var = jnp.mean(cc * cc, axis=-1, keepdims=True)
        y = cc * lax.rsqrt(var + LN_EPS) * lng_ref[...] + lnb_ref[...]
        y_ref[pl.ds(r0, rows), :] = (y * jax.nn.sigmoid(y)).astype(BF16)
        return carry

    lax.fori_loop(0, tm // rows, ln_chunk, 0)
    o_ref[...] = x_ref[...] + _dot(y_ref[...], wout_ref[...])


def _conv_tail(x, u, halo, dww, dwb, lng, lnb, wout, tm, n_taps):
    m, d = x.shape
    c = u.shape[1]
    rows = 16 if tm % 16 == 0 else 8
    lanes = _tile(c, 512)
    vmem = 4 * tm * d * 4 + 2 * tm * c * 4 + 2 * c * d * 2 + (tm + CONV_HALO) * c * 4 + tm * c * 2
    return pl.pallas_call(
        functools.partial(_conv_tail_kernel, n_taps=n_taps, rows=rows, lanes=lanes),
        name=f"conv_tail_m{m}",
        out_shape=jax.ShapeDtypeStruct((m, d), F32),
        grid=(m // tm,),
        in_specs=[pl.BlockSpec((tm, d), lambda i: (i, 0)),
                  pl.BlockSpec((tm, c), lambda i: (i, 0)),
                  pl.BlockSpec((1, CONV_HALO, c), lambda i: (i, 0, 0)),
                  pl.BlockSpec(dww.shape, lambda i: (0, 0)),
                  pl.BlockSpec((1, c), lambda i: (0, 0)),
                  pl.BlockSpec((1, c), lambda i: (0, 0)),
                  pl.BlockSpec((1, c), lambda i: (0, 0)),
                  pl.BlockSpec((c, d), lambda i: (0, 0))],
        out_specs=pl.BlockSpec((tm, d), lambda i: (i, 0)),
        scratch_shapes=[pltpu.VMEM((tm + CONV_HALO, c), F32), pltpu.VMEM((tm, c), BF16)],
        compiler_params=_cparams(("parallel",), vmem),
    )(x, u, halo, dww, dwb, lng, lnb, wout)


def _last_rows(past, seq, n):
    if seq.shape[1] >= n:
        return seq[:, seq.shape[1] - n:]
    return jnp.concatenate([past, seq], axis=1)[:, -n:]


def _halos(seq, past, tm, halo_rows):
    b, l, c = seq.shape
    p = past.shape[1]
    first = jnp.concatenate([jnp.zeros((b, halo_rows - p, c), seq.dtype), past], axis=1)[:, None]
    nt = l // tm
    if nt == 1:
        return first.reshape(b, halo_rows, c)
    rest = seq.reshape(b, nt, tm, c)[:, :-1, tm - halo_rows:]
    return jnp.concatenate([first, rest], axis=1).reshape(b * nt, halo_rows, c)


def _lru_tail_kernel(x_ref, ybr_ref, xbr_ref, halo_ref, h0_ref, cw_ref, cb_ref, wa_ref, ba_ref,
                     wx_ref, bx_ref, lam_ref, wout_ref, o_ref, hlast_ref,
                     xext_ref, a_ref, b_ref, hc_ref, *, n_taps, lanes):
    tm, c = xbr_ref.shape
    n_blocks, blk, _ = wa_ref.shape
    t = pl.program_id(1)

    @pl.when(t == 0)
    def _():
        hc_ref[...] = jnp.broadcast_to(h0_ref[0], hc_ref.shape)

    xext_ref[0:LRU_HALO, :] = halo_ref[0]
    xext_ref[LRU_HALO:LRU_HALO + tm, :] = xbr_ref[...]
    off = LRU_HALO - (n_taps - 1)

    sp = jax.nn.softplus(-lam_ref[...])
    for n in range(n_blocks):
        sl = slice(n * blk, (n + 1) * blk)
        xc = jnp.zeros((tm, blk), F32)
        for k in range(n_taps):
            xc = xc + cw_ref[k:k + 1, sl] * xext_ref[off + k:off + k + tm, sl]
        xc = xc + cb_ref[:, sl]
        xcb = xc.astype(BF16)
        r = jax.nn.sigmoid(_dot(xcb, wa_ref[n]) + ba_ref[:, sl])
        gi = jax.nn.sigmoid(_dot(xcb, wx_ref[n]) + bx_ref[:, sl])
        log_a = -LRU_C * r * sp[:, sl]
        a = jnp.exp(log_a)
        a_ref[:, sl] = a
        b_ref[:, sl] = jnp.sqrt(-jnp.tanh(log_a) * (a * a + 1.0)) * (gi * xc)

    row = lax.broadcasted_iota(jnp.int32, (SUBLANES, lanes), 0)

    for l0 in range(0, c, lanes):
        def group(gidx, h):
            r0 = pl.multiple_of(gidx * SUBLANES, SUBLANES)
            av = a_ref[pl.ds(r0, SUBLANES), l0:l0 + lanes]
            bv = b_ref[pl.ds(r0, SUBLANES), l0:l0 + lanes]
            for s in (1, 2, 4):
                a_sh = jnp.where(row >= s, pltpu.roll(av, s, axis=0), 1.0)
                b_sh = jnp.where(row >= s, pltpu.roll(bv, s, axis=0), 0.0)
                bv = av * b_sh + bv
                av = av * a_sh
            hs = av * h + bv
            b_ref[pl.ds(r0, SUBLANES), l0:l0 + lanes] = hs
            return jnp.broadcast_to(hs[SUBLANES - 1:SUBLANES, :], (SUBLANES, lanes))

        h_fin = lax.fori_loop(0, tm // SUBLANES, group, hc_ref[:, l0:l0 + lanes], unroll=2)
        hc_ref[:, l0:l0 + lanes] = h_fin

    gated = (b_ref[...] * ybr_ref[...]).astype(BF16)
    o_ref[...] = x_ref[...] + _dot(gated, wout_ref[...])

    @pl.when(t == pl.num_programs(1) - 1)
    def _():
        hlast_ref[0] = hc_ref[...]


def _lru_tail(x, ybr, xbr, halo, h0, cw, cb, wa, ba, wx, bx, lam, wout, batch, tm, n_taps):
    m, d = x.shape
    c = xbr.shape[1]
    nt = m // batch // tm
    lanes = _tile(c, 512)
    row2 = lambda b, t: (b * nt + t, 0)
    const2 = lambda b, t: (0, 0)
    vec = pl.BlockSpec((1, c), const2)
    vmem = (4 * tm * d * 4 + 4 * tm * c * 4 + 2 * c * d * 2 + 4 * wa.size * 2
            + (tm + LRU_HALO) * c * 4 + 2 * tm * c * 4)
    return pl.pallas_call(
        functools.partial(_lru_tail_kernel, n_taps=n_taps, lanes=lanes),
        name=f"lru_tail_m{m}",
        out_shape=[jax.ShapeDtypeStruct((m, d), F32),
                   jax.ShapeDtypeStruct((batch, SUBLANES, c), F32)],
        grid=(batch, nt),
        in_specs=[pl.BlockSpec((tm, d), row2),
                  pl.BlockSpec((tm, c), row2),
                  pl.BlockSpec((tm, c), row2),
                  pl.BlockSpec((1, LRU_HALO, c), lambda b, t: (b * nt + t, 0, 0)),
                  pl.BlockSpec((1, 1, c), lambda b, t: (b, 0, 0)),
                  pl.BlockSpec(cw.shape, const2),
                  vec,
                  pl.BlockSpec(wa.shape, lambda b, t: (0, 0, 0)),
                  vec,
                  pl.BlockSpec(wx.shape, lambda b, t: (0, 0, 0)),
                  vec, vec,
                  pl.BlockSpec((c, d), const2)],
        out_specs=[pl.BlockSpec((tm, d), row2),
                   pl.BlockSpec((1, SUBLANES, c), lambda b, t: (b, 0, 0))],
        scratch_shapes=[pltpu.VMEM((tm + LRU_HALO, c), F32), pltpu.VMEM((tm, c), F32),
                        pltpu.VMEM((tm, c), F32), pltpu.VMEM((SUBLANES, c), F32)],
        compiler_params=_cparams(("parallel", "arbitrary"), vmem),
    )(x, ybr, xbr, halo, h0, cw, cb, wa, ba, wx, bx, lam, wout)


def _shifted_rows(vec, n_rows):
    return pltpu.roll(jnp.broadcast_to(vec, (n_rows, vec.shape[1])), 0, 1, stride=1, stride_axis=0)


def _block_softmax(s_parts, v_parts):
    m = s_parts[0].max(axis=-1, keepdims=True)
    for s in s_parts[1:]:
        m = jnp.maximum(m, s.max(axis=-1, keepdims=True))
    o, den = None, None
    for s, v in zip(s_parts, v_parts):
        p = jnp.exp(s - m)
        d = p.sum(axis=-1, keepdims=True)
        pv = _dot(p.astype(BF16), v)
        o, den = (pv, d) if o is None else (o + pv, den + d)
    return o, m, den


def _merge(state, o_b, m_b, l_b):
    if state is None:
        return o_b, m_b, l_b
    acc, m_old, l_old = state
    m_new = jnp.maximum(m_old, m_b)
    a_old = jnp.exp(m_old - m_new)
    a_b = jnp.exp(m_b - m_new)
    return acc * a_old + o_b * a_b, m_new, l_old * a_old + l_b * a_b


def _attn_prompt_kernel(*refs, heads, dh):
    n_g = len(GROUPS)
    ins = refs[:5 * n_g]
    bvec_ref, o_ref, acc_ref, m_ref, l_ref = refs[5 * n_g:]
    first_chunk = pl.program_id(1) == 0
    h = pl.program_id(2)
    scale = dh ** -0.5
    qi = lax.broadcasted_iota(jnp.int32, (BLK, 2 * BLK), 0)
    kj = lax.broadcasted_iota(jnp.int32, (BLK, 2 * BLK), 1)
    in_band = (kj - qi >= 0) & (kj - qi <= BLK)
    for g, (_, dil) in enumerate(GROUPS):
        q_ref, k_ref, v_ref, kp_ref, vp_ref = ins[5 * g:5 * g + 5]
        bias = jnp.where(in_band, _shifted_rows(bvec_ref[pl.ds(g * heads + h, 1), :], BLK), NEG_INF)
        bias_p, bias_c = bias[:, :BLK], bias[:, BLK:]
        for r in range(dil):
            prev_rows = pl.ds(r, BLK, stride=dil)
            k_prev = kp_ref[prev_rows, :].astype(BF16)
            v_prev = vp_ref[prev_rows, :].astype(BF16)
            for nb in range(CHUNK // (dil * BLK)):
                rows = pl.ds(r + dil * BLK * nb, BLK, stride=dil)
                q = q_ref[rows, :].astype(BF16)
                k_cur = k_ref[rows, :].astype(BF16)
                v_cur = v_ref[rows, :].astype(BF16)
                s_p = _nt_dot(q, k_prev) * scale + bias_p
                if nb == 0:
                    s_p = jnp.where(first_chunk, NEG_INF, s_p)
                s_c = _nt_dot(q, k_cur) * scale + bias_c
                o_b, m_b, l_b = _block_softmax([s_p, s_c], [v_prev, v_cur])
                m_b = jnp.broadcast_to(m_b, (BLK, dh))
                l_b = jnp.broadcast_to(l_b, (BLK, dh))
                state = None if g == 0 else (acc_ref[rows, :], m_ref[rows, :], l_ref[rows, :])
                acc, m_new, l_new = _merge(state, o_b, m_b, l_b)
                if g == n_g - 1:
                    o_ref[rows, :] = acc / l_new
                else:
                    acc_ref[rows, :] = acc
                    m_ref[rows, :] = m_new
                    l_ref[rows, :] = l_new
                k_prev, v_prev = k_cur, v_cur


def _attn_prompt(qkv, bvec, batch, seq, heads, dh):
    n_g = len(GROUPS)
    nc = seq // CHUNK
    col = lambda comp, g: (lambda b, n, h: (b * nc + n, (comp * n_g + g) * heads + h))
    in_specs, args = [], []
    for g, (win, _) in enumerate(GROUPS):
        per = CHUNK // win
        prev = lambda comp, g=g, per=per: (
            lambda b, n, h: (jnp.maximum((b * nc + n) * per - 1, 0), (comp * n_g + g) * heads + h))
        in_specs += [pl.BlockSpec((CHUNK, dh), col(0, g)), pl.BlockSpec((CHUNK, dh), col(1, g)),
                     pl.BlockSpec((CHUNK, dh), col(2, g)),
                     pl.BlockSpec((win, dh), prev(1)), pl.BlockSpec((win, dh), prev(2))]
        args += [qkv] * 5
    in_specs.append(pl.BlockSpec(bvec.shape, lambda b, n, h: (0, 0)))
    vmem = 2 * (9 * CHUNK + 2 * sum(w for w, _ in GROUPS)) * dh * 4 + 2 * CHUNK * dh * 4 + 3 * CHUNK * dh * 4
    return pl.pallas_call(
        functools.partial(_attn_prompt_kernel, heads=heads, dh=dh),
        name="attn_prompt",
        out_shape=jax.ShapeDtypeStruct((batch * seq, heads * dh), F32),
        grid=(batch, nc, heads),
        in_specs=in_specs,
        out_specs=pl.BlockSpec((CHUNK, dh), lambda b, n, h: (b * nc + n, h)),
        scratch_shapes=[pltpu.VMEM((CHUNK, dh), F32)] * 3,
        compiler_params=_cparams(("parallel", "parallel", "parallel"), vmem),
    )(*args, bvec)


def _attn_step_kernel(*refs, dh):
    n_g = len(GROUPS)
    ins = refs[:6 * n_g]
    o_ref, knp_ref, vnp_ref = refs[6 * n_g:]
    t = o_ref.shape[1]
    scale = dh ** -0.5
    state = None
    for g, (_, dil) in enumerate(GROUPS):
        q_ref, kn_ref, vn_ref, kc_ref, vc_ref, y_ref = ins[6 * g:6 * g + 6]
        n_buf = kc_ref.shape[1]
        knp_ref[...] = jnp.zeros_like(knp_ref)
        vnp_ref[...] = jnp.zeros_like(vnp_ref)
        knp_ref[0:t, :] = kn_ref[0]
        vnp_ref[0:t, :] = vn_ref[0]
        q = q_ref[0].astype(BF16)
        bias = _shifted_rows(y_ref[0], t)
        tc = lax.broadcasted_iota(jnp.int32, (t, n_buf), 0)
        cc = lax.broadcasted_iota(jnp.int32, (t, n_buf), 1)
        dist_c = n_buf + tc - cc
        ok_c = ((dist_c & (dil - 1)) == 0) & (dist_c <= (N_DIL_KEYS - 1) * dil)
        tn = lax.broadcasted_iota(jnp.int32, (t, BLK), 0)
        cn = lax.broadcasted_iota(jnp.int32, (t, BLK), 1)
        dist_n = tn - cn
        ok_n = (dist_n >= 0) & ((dist_n & (dil - 1)) == 0) & (dist_n <= (N_DIL_KEYS - 1) * dil)
        s_c = jnp.where(ok_c, _nt_dot(q, kc_ref[0].astype(BF16)) * scale + bias[:, BLK:BLK + n_buf], NEG_INF)
        s_n = jnp.where(ok_n, _nt_dot(q, knp_ref[...].astype(BF16)) * scale + bias[:, BLK + n_buf:], NEG_INF)
        o_b, m_b, l_b = _block_softmax([s_c, s_n], [vc_ref[0].astype(BF16), vnp_ref[...].astype(BF16)])
        state = _merge(state, o_b, jnp.broadcast_to(m_b, (t, dh)), jnp.broadcast_to(l_b, (t, dh)))
    o_ref[0] = state[0] / state[2]


def _attn_step(qkv, caches, ys, batch, t, heads, dh):
    n_g = len(GROUPS)
    qkv3 = qkv.reshape(batch, t, qkv.shape[1])
    in_specs, args = [], []
    vmem = 0
    for g in range(n_g):
        n_buf = caches[g].shape[1]
        new = lambda comp, g=g: pl.BlockSpec((1, t, dh), lambda b, h: (b, 0, (comp * n_g + g) * heads + h))
        in_specs += [new(0), new(1), new(2),
                     pl.BlockSpec((1, n_buf, dh), lambda b, h: (b, 0, h)),
                     pl.BlockSpec((1, n_buf, dh), lambda b, h: (b, 0, heads + h)),
                     pl.BlockSpec((1, 1, ys[g].shape[2]), lambda b, h: (h, 0, 0))]
        args += [qkv3, qkv3, qkv3, caches[g], caches[g], ys[g]]
        vmem += 2 * 2 * n_buf * dh * 4 + 8 * n_buf * dh
    return pl.pallas_call(
        functools.partial(_attn_step_kernel, dh=dh),
        name="attn_step",
        out_shape=jax.ShapeDtypeStruct((batch, t, heads * dh), F32),
        grid=(batch, heads),
        in_specs=in_specs,
        out_specs=pl.BlockSpec((1, t, dh), lambda b, h: (b, 0, h)),
        scratch_shapes=[pltpu.VMEM((BLK, dh), F32), pltpu.VMEM((BLK, dh), F32)],
        compiler_params=_cparams(("parallel", "parallel"), vmem),
    )(*args).reshape(batch * t, heads * dh)


def _attn_out_kernel(x_ref, o_ref, wo_ref, out_ref):
    out_ref[...] = x_ref[...] + _dot(o_ref[...].astype(BF16), wo_ref[...])


def _attn_out(x, o, wo, tm_pref=512):
    m, d = x.shape
    width = wo.shape[0]
    tm = _tile(m, tm_pref)
    row = lambda n: pl.BlockSpec((tm, n), lambda i: (i, 0))
    vmem = 4 * tm * d * 4 + 2 * tm * width * 4 + 2 * width * d * 2
    return pl.pallas_call(
        _attn_out_kernel,
        name=f"attn_out_m{m}",
        out_shape=jax.ShapeDtypeStruct((m, d), F32),
        grid=(m // tm,),
        in_specs=[row(d), row(width), pl.BlockSpec((width, d), lambda i: (0, 0))],
        out_specs=row(d),
        compiler_params=_cparams(("parallel",), vmem),
    )(x, o, wo)


def _t5_bucket(dist):
    max_exact = N_BUCKETS // 2
    d = np.asarray(dist, dtype=np.int32)
    large = max_exact + (np.log(np.maximum(d, max_exact) / max_exact) / np.log(MAX_DISTANCE / max_exact)
                         * (N_BUCKETS - max_exact)).astype(np.int32)
    return np.where(d < max_exact, d, np.minimum(large, N_BUCKETS - 1)).astype(np.int32)


def _group_bias(rel_bias, g, heads):
    buckets = _t5_bucket(np.arange(N_DIL_KEYS) * GROUPS[g][1])
    return rel_bias[:, g * heads:(g + 1) * heads].astype(F32)[buckets]


def _prompt_bias_vectors(rel_bias, heads):
    vecs = []
    for g in range(len(GROUPS)):
        rev = _group_bias(rel_bias, g, heads)[::-1]
        vecs.append(jnp.pad(rev, ((0, 2 * BLK - N_DIL_KEYS), (0, 0))).T)
    return jnp.concatenate(vecs, axis=0)


def _step_bias_vector(rel_bias, g, heads, n_buf):
    dist = n_buf + BLK - np.arange(n_buf + 2 * BLK)
    idx = np.clip(dist // GROUPS[g][1], 0, N_DIL_KEYS - 1)
    return _group_bias(rel_bias, g, heads)[idx].T[:, None]


def _kv_rows(qkv, g, batch, seq, n_rows, heads, dh):
    n_g = len(GROUPS)
    width = heads * dh
    q3 = qkv.reshape(batch, seq, qkv.shape[1])[:, seq - n_rows:]
    k = q3[:, :, (n_g + g) * width:(n_g + g + 1) * width]
    v = q3[:, :, (2 * n_g + g) * width:(2 * n_g + g + 1) * width]
    return jnp.stack([k, v], axis=2).reshape(batch, n_rows, 2, heads, dh)


def _trunk(x3, st, p, tiles):
    batch, seq, d = x3.shape
    m = batch * seq
    x = x3.reshape(m, d)
    new = []
    depth = p['ffn_w1'].shape[0]
    for i in range(depth):
        kind = i % N_MIXERS
        g_mix = p['norm_mix_g'][i][None]
        if kind == 0:
            j = i // N_MIXERS
            n_taps, c = p['conv_dw_w'].shape[1:]
            buf = jnp.zeros((batch, n_taps - 1, c), F32) if st is None else st[i][0]
            u = _norm_proj(x, g_mix, p['conv_w_glu'][j], 'glu', tiles['proj_m'])
            u3 = u.reshape(batch, seq, c)
            tm = _tile(seq, tiles['conv_m'])
            halo = _halos(u3, buf, tm, CONV_HALO)
            dww = jnp.pad(p['conv_dw_w'][j], ((0, CONV_HALO - n_taps), (0, 0)))
            x = _conv_tail(x, u, halo, dww, p['conv_dw_b'][j][None], p['conv_ln_g'][j][None],
                           p['conv_ln_b'][j][None], p['conv_w_out'][j], tm, n_taps)
            new.append((_last_rows(buf, u3, n_taps - 1),))
        elif kind == 1:
            n_taps, c = p['lru_conv_w'].shape
            if st is None:
                h0 = jnp.zeros((batch, c), F32)
                cb = jnp.zeros((batch, n_taps - 1, c), F32)
            else:
                h0, cb = st[i]
            ybr, xbr = _norm_proj(x, g_mix, p['lru_w_in'], 'split', tiles['proj_m'])
            xbr3 = xbr.reshape(batch, seq, c)
            tm = _tile(seq, tiles['lru_m'])
            halo = _halos(xbr3, cb, tm, LRU_HALO)
            cw = jnp.pad(p['lru_conv_w'], ((0, SUBLANES - n_taps), (0, 0)))
            x, hl = _lru_tail(x, ybr, xbr, halo, h0[:, None], cw, p['lru_conv_b'][None], p['lru_w_a'],
                              p['lru_b_a'][None], p['lru_w_x'], p['lru_b_x'][None], p['lru_lambda'][None],
                              p['lru_w_out'], batch, tm, n_taps)
            new.append((hl[:, 0], _last_rows(cb, xbr3, n_taps - 1)))
        else:
            rel_bias = p['rel_bias']
            heads = rel_bias.shape[1] // len(GROUPS)
            width = p['att_w_o'].shape[0]
            dh = width // heads
            qkv = _norm_proj(x, g_mix, p['att_w_qkv'], 'plain', tiles['proj_m'])
            if st is None:
                assert seq % CHUNK == 0
                o = _attn_prompt(qkv, _prompt_bias_vectors(rel_bias, heads), batch, seq, heads, dh)
                rows = [_kv_rows(qkv, g, batch, seq, min(win, seq), heads, dh) for g, (win, _) in enumerate(GROUPS)]
            else:
                assert seq <= BLK
                caches = [c.reshape(batch, c.shape[1], 2 * width) for c in st[i]]
                ys = [_step_bias_vector(rel_bias, g, heads, caches[g].shape[1]) for g in range(len(GROUPS))]
                o = _attn_step(qkv, caches, ys, batch, seq, heads, dh)
                rows = [_kv_rows(qkv, g, batch, seq, seq, heads, dh) for g in range(len(GROUPS))]
            x = _attn_out(x, o, p['att_w_o'], tiles['attn_out_m'])
            new.append(rows)
        x = _ffn(x, p['norm_ffn_g'][i][None], p['ffn_w1'][i], p['ffn_w2'][i], p['final_norm_g'][None],
                 final=(i == depth - 1), tm_pref=tiles['ffn_m'])
    return x.reshape(batch, seq, d), new


PROMPT_TILES = dict(proj_m=512, conv_m=256, lru_m=256, attn_out_m=512, ffn_m=512)
SAMPLE_TILES = dict(proj_m=512, conv_m=256, lru_m=256, attn_out_m=512, ffn_m=512)
MATMUL_WEIGHTS = ('ffn_w1', 'ffn_w2', 'conv_w_glu', 'conv_w_out', 'lru_w_in', 'lru_w_a', 'lru_w_x',
                  'lru_w_out', 'att_w_qkv', 'att_w_o')


def kernel(x_prompt, x_sample, state_conv_l0, state_lru_h_l1, state_lru_conv_l1, cache_kv_w128_l2, cache_kv_w512_l2, cache_kv_w2048_l2, state_conv_l3, norm_mix_g, norm_ffn_g, final_norm_g, ffn_w1, ffn_w2, conv_w_glu, conv_dw_w, conv_dw_b, conv_ln_g, conv_ln_b, conv_w_out, lru_w_in, lru_conv_w, lru_conv_b, lru_w_a, lru_b_a, lru_w_x, lru_b_x, lru_lambda, lru_w_out, att_w_qkv, att_w_o, rel_bias):
    p = dict(norm_mix_g=norm_mix_g, norm_ffn_g=norm_ffn_g, final_norm_g=final_norm_g, ffn_w1=ffn_w1,
             ffn_w2=ffn_w2, conv_w_glu=conv_w_glu, conv_dw_w=conv_dw_w, conv_dw_b=conv_dw_b,
             conv_ln_g=conv_ln_g, conv_ln_b=conv_ln_b, conv_w_out=conv_w_out, lru_w_in=lru_w_in,
             lru_conv_w=lru_conv_w, lru_conv_b=lru_conv_b, lru_w_a=lru_w_a, lru_b_a=lru_b_a, lru_w_x=lru_w_x,
             lru_b_x=lru_b_x, lru_lambda=lru_lambda, lru_w_out=lru_w_out, att_w_qkv=att_w_qkv,
             att_w_o=att_w_o, rel_bias=rel_bias)
    for name in MATMUL_WEIGHTS:
        p[name] = p[name].astype(BF16)
    y_prompt, pst = _trunk(x_prompt, None, p, PROMPT_TILES)
    y_sample, sst = _trunk(x_sample, [(state_conv_l0,), (state_lru_h_l1, state_lru_conv_l1),
                                      (cache_kv_w128_l2, cache_kv_w512_l2, cache_kv_w2048_l2),
                                      (state_conv_l3,)], p, SAMPLE_TILES)
    return (y_prompt, y_sample,
            pst[0][0], sst[0][0],
            pst[1][0], sst[1][0],
            pst[1][1], sst[1][1],
            pst[2][0], sst[2][0],
            pst[2][1], sst[2][1],
            pst[2][2], sst[2][2],
            pst[3][0], sst[3][0])
```

```python
import functools

import numpy as np
import jax
import jax.numpy as jnp
from jax import lax
from jax.experimental import pallas as pl
from jax.experimental.pallas import tpu as pltpu

RMS_EPS = 1e-6
LN_EPS = 1e-5
LRU_C = 8.0
NEG_INF = -1e30
N_MIXERS = 3
GROUPS = ((128, 1), (512, 4), (2048, 16))
N_DIL_KEYS = 129
BLK = 128
CHUNK = BLK * max(d for _, d in GROUPS)
N_BUCKETS = 32
MAX_DISTANCE = 2048

V7X_VMEM_BYTES = 64 * 1024 * 1024
SUBLANES = 8
CONV_HALO = 32
LRU_HALO = 8

BF16 = jnp.bfloat16
F32 = jnp.float32


def _cparams(semantics, vmem_bytes):
    limit = int(min(max(vmem_bytes * 3 // 2, 32 * 1024 * 1024), V7X_VMEM_BYTES - 8 * 1024 * 1024))
    return pltpu.CompilerParams(dimension_semantics=semantics, vmem_limit_bytes=limit)


def _tile(n, pref):
    if n <= pref:
        return n
    t = pref
    while n % t:
        t //= 2
    return t


def _rms_to_bf16(x, g):
    ms = jnp.mean(x * x, axis=-1, keepdims=True)
    return (x * lax.rsqrt(ms + RMS_EPS) * g).astype(BF16)


def _dot(a, b):
    return jnp.dot(a, b, preferred_element_type=F32)


def _nt_dot(a, b):
    return lax.dot_general(a, b, (((1,), (1,)), ((), ())), preferred_element_type=F32)


def _norm_proj_kernel(x_ref, g_ref, *refs, mode):
    if mode == 'plain':
        w_ref, o_ref, h_ref = refs
    elif mode == 'glu':
        wa_ref, wg_ref, o_ref, h_ref = refs
    else:
        wa_ref, wg_ref, o_ref, o2_ref, h_ref = refs

    @pl.when(pl.program_id(1) == 0)
    def _():
        h_ref[...] = _rms_to_bf16(x_ref[...], g_ref[...])

    h = h_ref[...]
    if mode == 'plain':
        o_ref[...] = _dot(h, w_ref[...])
    elif mode == 'glu':
        o_ref[...] = _dot(h, wa_ref[...]) * jax.nn.sigmoid(_dot(h, wg_ref[...]))
    else:
        o_ref[...] = jax.nn.gelu(_dot(h, wa_ref[...]), approximate=True)
        o2_ref[...] = _dot(h, wg_ref[...])


def _wspec(w, layer, block, index_map):
    if w.ndim == 2:
        return pl.BlockSpec(block, index_map)
    return pl.BlockSpec((None,) + block, lambda *ids: (layer,) + index_map(*ids))


def _norm_proj(x, g, w, mode, tm_pref=512, tn_pref=512, layer=0):
    m, d = x.shape
    n = w.shape[-1]
    n_out = n if mode == 'plain' else n // 2
    tm = _tile(m, tm_pref)
    tn = _tile(n_out, tn_pref)
    nj = n_out // tn
    in_specs = [pl.BlockSpec((tm, d), lambda i, j: (i, 0)),
                pl.BlockSpec((1, d), lambda i, j: (0, 0)),
                _wspec(w, layer, (d, tn), lambda i, j: (0, j))]
    args = [x, g, w]
    if mode != 'plain':
        in_specs.append(_wspec(w, layer, (d, tn), lambda i, j: (0, j + nj)))
        args.append(w)
    o_spec = pl.BlockSpec((tm, tn), lambda i, j: (i, j))
    o_shape = jax.ShapeDtypeStruct((m, n_out), F32)
    n_o = 2 if mode == 'split' else 1
    vmem = 2 * tm * d * 4 + tm * d * 2 + 2 * len(args[2:]) * d * tn * 2 + 2 * n_o * tm * tn * 4
    return pl.pallas_call(
        functools.partial(_norm_proj_kernel, mode=mode),
        name=f"norm_proj_{mode}_m{m}",
        out_shape=[o_shape] * n_o if n_o == 2 else o_shape,
        grid=(m // tm, nj),
        in_specs=in_specs,
        out_specs=[o_spec] * n_o if n_o == 2 else o_spec,
        scratch_shapes=[pltpu.VMEM((tm, d), BF16)],
        compiler_params=_cparams(("parallel", "arbitrary"), vmem),
    )(*args)


def _ffn_kernel(x_ref, g_ref, w1_ref, w2_ref, fg_ref, o_ref, h_ref, *, final):
    f = pl.program_id(1)

    @pl.when(f == 0)
    def _():
        x = x_ref[...]
        h_ref[...] = _rms_to_bf16(x, g_ref[...])
        o_ref[...] = x

    hid = jnp.square(jnp.maximum(_dot(h_ref[...], w1_ref[...]), 0.0)).astype(BF16)
    o_ref[...] += _dot(hid, w2_ref[...])

    if final:
        @pl.when(f == pl.num_programs(1) - 1)
        def _():
            y = o_ref[...]
            ms = jnp.mean(y * y, axis=-1, keepdims=True)
            o_ref[...] = y * lax.rsqrt(ms + RMS_EPS) * fg_ref[...]


def _ffn(x, g, w1, w2, layer, final_g, final, tm_pref=512, tf_pref=512):
    m, d = x.shape
    dff = w1.shape[-1]
    tm = _tile(m, tm_pref)
    tf = _tile(dff, tf_pref)
    vmem = 2 * tm * d * 4 + tm * d * 2 + 2 * 2 * d * tf * 2 + 2 * tm * d * 4 + 2 * tm * tf * 4
    return pl.pallas_call(
        functools.partial(_ffn_kernel, final=final),
        name=f"ffn_m{m}_l{layer}",
        out_shape=jax.ShapeDtypeStruct((m, d), F32),
        grid=(m // tm, dff // tf),
        in_specs=[pl.BlockSpec((tm, d), lambda i, f: (i, 0)),
                  pl.BlockSpec((1, d), lambda i, f: (0, 0)),
                  _wspec(w1, layer, (d, tf), lambda i, f: (0, f)),
                  _wspec(w2, layer, (tf, d), lambda i, f: (f, 0)),
                  pl.BlockSpec((1, d), lambda i, f: (0, 0))],
        out_specs=pl.BlockSpec((tm, d), lambda i, f: (i, 0)),
        scratch_shapes=[pltpu.VMEM((tm, d), BF16)],
        compiler_params=_cparams(("parallel", "arbitrary"), vmem),
    )(x, g, w1, w2, final_g)


def _conv_tail_kernel(x_ref, u_ref, halo_ref, dww_ref, dwb_ref, lng_ref, lnb_ref, wout_ref,
                      o_ref, uext_ref, y_ref, *, n_taps, rows, ln_rows, lanes):
    tm, c = u_ref.shape
    uext_ref[0:CONV_HALO, :] = halo_ref[0]
    uext_ref[CONV_HALO:CONV_HALO + tm, :] = u_ref[...]
    off = CONV_HALO - (n_taps - 1)

    def row_chunk(r, carry):
        r0 = pl.multiple_of(r * rows, rows)
        for l0 in range(0, c, lanes):
            win = uext_ref[pl.ds(r0, rows + CONV_HALO), l0:l0 + lanes]
            accs = [jnp.zeros((SUBLANES, lanes), F32)] * (rows // SUBLANES)
            for s in range(SUBLANES):
                ws = win if s == 0 else pltpu.roll(win, rows + CONV_HALO - s, axis=0)
                for a in range((CONV_HALO + SUBLANES) // SUBLANES):
                    k = a * SUBLANES + s - off
                    if 0 <= k < n_taps:
                        wk = dww_ref[k * SUBLANES:(k + 1) * SUBLANES, l0:l0 + lanes]
                        accs = [acc + wk * ws[(a + j) * SUBLANES:(a + j + 1) * SUBLANES]
                                for j, acc in enumerate(accs)]
            for j, acc in enumerate(accs):
                uext_ref[pl.ds(r0 + j * SUBLANES, SUBLANES), l0:l0 + lanes] = acc + dwb_ref[:, l0:l0 + lanes]
        return carry

    lax.fori_loop(0, tm // rows, row_chunk, 0)

    def ln_chunk(r, carry):
        r0 = pl.multiple_of(r * ln_rows, ln_rows)
        cv = uext_ref[pl.ds(r0, ln_rows), :]
        mu = jnp.mean(cv, axis=-1, keepdims=True)
        cc = cv - mu
        var = jnp.mean(cc * cc, axis=-1, keepdims=True)
        y = cc * lax.rsqrt(var + LN_EPS) * lng_ref[...] + lnb_ref[...]
        y_ref[pl.ds(r0, ln_rows), :] = (y * jax.nn.sigmoid(y)).astype(BF16)
        return carry

    lax.fori_loop(0, tm // ln_rows, ln_chunk, 0, unroll=min(2, tm // ln_rows))
    o_ref[...] = x_ref[...] + _dot(y_ref[...], wout_ref[...])


def _conv_tail(x, u, halo, dww, dwb, lng, lnb, wout, layer, tm, n_taps):
    m, d = x.shape
    c = u.shape[1]
    rows = _tile(tm, 32)
    ln_rows = _tile(tm, 16)
    lanes = _tile(c, 256)
    vmem = 4 * tm * d * 4 + 2 * tm * c * 4 + 2 * c * d * 2 + (tm + CONV_HALO) * c * 4 + tm * c * 2
    return pl.pallas_call(
        functools.partial(_conv_tail_kernel, n_taps=n_taps, rows=rows, ln_rows=ln_rows, lanes=lanes),
        name=f"conv_tail_m{m}",
        out_shape=jax.ShapeDtypeStruct((m, d), F32),
        grid=(m // tm,),
        in_specs=[pl.BlockSpec((tm, d), lambda i: (i, 0)),
                  pl.BlockSpec((tm, c), lambda i: (i, 0)),
                  pl.BlockSpec((1, CONV_HALO, c), lambda i: (i, 0, 0)),
                  pl.BlockSpec(dww.shape, lambda i: (0, 0)),
                  pl.BlockSpec((1, c), lambda i: (0, 0)),
                  pl.BlockSpec((1, c), lambda i: (0, 0)),
                  pl.BlockSpec((1, c), lambda i: (0, 0)),
                  _wspec(wout, layer, (c, d), lambda i: (0, 0))],
        out_specs=pl.BlockSpec((tm, d), lambda i: (i, 0)),
        scratch_shapes=[pltpu.VMEM((tm + CONV_HALO, c), F32), pltpu.VMEM((tm, c), BF16)],
        compiler_params=_cparams(("parallel",), vmem),
    )(x, u, halo, dww, dwb, lng, lnb, wout)


def _last_rows(past, seq, n):
    if seq.shape[1] >= n:
        return seq[:, seq.shape[1] - n:]
    return jnp.concatenate([past, seq], axis=1)[:, -n:]


def _halos(seq, past, tm, halo_rows):
    b, l, c = seq.shape
    p = past.shape[1]
    first = jnp.concatenate([jnp.zeros((b, halo_rows - p, c), seq.dtype), past], axis=1)[:, None]
    nt = l // tm
    if nt == 1:
        return first.reshape(b, halo_rows, c)
    rest = seq.reshape(b, nt, tm, c)[:, :-1, tm - halo_rows:]
    return jnp.concatenate([first, rest], axis=1).reshape(b * nt, halo_rows, c)


def _lru_tail_kernel(x_ref, ybr_ref, xbr_ref, halo_ref, h0_ref, cw_ref, cb_ref, wa_ref, ba_ref,
                     wx_ref, bx_ref, lam_ref, wout_ref, o_ref, hlast_ref,
                     xext_ref, a_ref, b_ref, hc_ref, *, n_taps, lanes):
    tm, c = xbr_ref.shape
    n_blocks, blk, _ = wa_ref.shape
    t = pl.program_id(1)

    @pl.when(t == 0)
    def _():
        hc_ref[...] = jnp.broadcast_to(h0_ref[0], hc_ref.shape)

    xext_ref[0:LRU_HALO, :] = halo_ref[0]
    xext_ref[LRU_HALO:LRU_HALO + tm, :] = xbr_ref[...]
    off = LRU_HALO - (n_taps - 1)

    sp = jax.nn.softplus(-lam_ref[...])
    for n in range(n_blocks):
        sl = slice(n * blk, (n + 1) * blk)
        xc = jnp.zeros((tm, blk), F32)
        for k in range(n_taps):
            xc = xc + cw_ref[k:k + 1, sl] * xext_ref[off + k:off + k + tm, sl]
        xc = xc + cb_ref[:, sl]
        xcb = xc.astype(BF16)
        r = jax.nn.sigmoid(_dot(xcb, wa_ref[n]) + ba_ref[:, sl])
        gi = jax.nn.sigmoid(_dot(xcb, wx_ref[n]) + bx_ref[:, sl])
        log_a = -LRU_C * r * sp[:, sl]
        a = jnp.exp(log_a)
        a_ref[:, sl] = a
        b_ref[:, sl] = jnp.sqrt(-jnp.tanh(log_a) * (a * a + 1.0)) * (gi * xc)

    row = lax.broadcasted_iota(jnp.int32, (SUBLANES, lanes), 0)

    for l0 in range(0, c, lanes):
        def group(gidx, h):
            r0 = pl.multiple_of(gidx * SUBLANES, SUBLANES)
            av = a_ref[pl.ds(r0, SUBLANES), l0:l0 + lanes]
            bv = b_ref[pl.ds(r0, SUBLANES), l0:l0 + lanes]
            for s in (1, 2, 4):
                a_sh = jnp.where(row >= s, pltpu.roll(av, s, axis=0), 1.0)
                b_sh = jnp.where(row >= s, pltpu.roll(bv, s, axis=0), 0.0)
                bv = av * b_sh + bv
                av = av * a_sh
            hs = av * h + bv
            b_ref[pl.ds(r0, SUBLANES), l0:l0 + lanes] = hs
            return jnp.broadcast_to(hs[SUBLANES - 1:SUBLANES, :], (SUBLANES, lanes))

        h_fin = lax.fori_loop(0, tm // SUBLANES, group, hc_ref[:, l0:l0 + lanes], unroll=2)
        hc_ref[:, l0:l0 + lanes] = h_fin

    gated = (b_ref[...] * ybr_ref[...]).astype(BF16)
    o_ref[...] = x_ref[...] + _dot(gated, wout_ref[...])

    @pl.when(t == pl.num_programs(1) - 1)
    def _():
        hlast_ref[0] = hc_ref[...]


def _lru_tail(x, ybr, xbr, halo, h0, cw, cb, wa, ba, wx, bx, lam, wout, batch, tm, n_taps):
    m, d = x.shape
    c = xbr.shape[1]
    nt = m // batch // tm
    lanes = _tile(c, 512)
    row2 = lambda b, t: (b * nt + t, 0)
    const2 = lambda b, t: (0, 0)
    vec = pl.BlockSpec((1, c), const2)
    vmem = (4 * tm * d * 4 + 4 * tm * c * 4 + 2 * c * d * 2 + 4 * wa.size * 2
            + (tm + LRU_HALO) * c * 4 + 2 * tm * c * 4)
    return pl.pallas_call(
        functools.partial(_lru_tail_kernel, n_taps=n_taps, lanes=lanes),
        name=f"lru_tail_m{m}",
        out_shape=[jax.ShapeDtypeStruct((m, d), F32),
                   jax.ShapeDtypeStruct((batch, SUBLANES, c), F32)],
        grid=(batch, nt),
        in_specs=[pl.BlockSpec((tm, d), row2),
                  pl.BlockSpec((tm, c), row2),
                  pl.BlockSpec((tm, c), row2),
                  pl.BlockSpec((1, LRU_HALO, c), lambda b, t: (b * nt + t, 0, 0)),
                  pl.BlockSpec((1, 1, c), lambda b, t: (b, 0, 0)),
                  pl.BlockSpec(cw.shape, const2),
                  vec,
                  pl.BlockSpec(wa.shape, lambda b, t: (0, 0, 0)),
                  vec,
                  pl.BlockSpec(wx.shape, lambda b, t: (0, 0, 0)),
                  vec, vec,
                  pl.BlockSpec((c, d), const2)],
        out_specs=[pl.BlockSpec((tm, d), row2),
                   pl.BlockSpec((1, SUBLANES, c), lambda b, t: (b, 0, 0))],
        scratch_shapes=[pltpu.VMEM((tm + LRU_HALO, c), F32), pltpu.VMEM((tm, c), F32),
                        pltpu.VMEM((tm, c), F32), pltpu.VMEM((SUBLANES, c), F32)],
        compiler_params=_cparams(("parallel", "arbitrary"), vmem),
    )(x, ybr, xbr, halo, h0, cw, cb, wa, ba, wx, bx, lam, wout)


def _shifted_rows(vec, n_rows):
    return pltpu.roll(jnp.broadcast_to(vec, (n_rows, vec.shape[1])), 0, 1, stride=1, stride_axis=0)


def _block_softmax(s_parts, v_parts):
    m = s_parts[0].max(axis=-1, keepdims=True)
    for s in s_parts[1:]:
        m = jnp.maximum(m, s.max(axis=-1, keepdims=True))
    o, den = None, None
    for s, v in zip(s_parts, v_parts):
        p = jnp.exp(s - m)
        d = p.sum(axis=-1, keepdims=True)
        pv = _dot(p.astype(BF16), v)
        o, den = (pv, d) if o is None else (o + pv, den + d)
    return o, m, den


def _merge(state, o_b, m_b, l_b):
    if state is None:
        return o_b, m_b, l_b
    acc, m_old, l_old = state
    m_new = jnp.maximum(m_old, m_b)
    a_old = jnp.exp(m_old - m_new)
    a_b = jnp.exp(m_b - m_new)
    return acc * a_old + o_b * a_b, m_new, l_old * a_old + l_b * a_b


def _attn_prompt_kernel(*refs, heads, dh):
    n_g = len(GROUPS)
    ins = refs[:5 * n_g]
    bvec_ref, o_ref, acc_ref, m_ref, l_ref = refs[5 * n_g:]
    first_chunk = pl.program_id(1) == 0
    h = pl.program_id(2)
    scale = dh ** -0.5
    qi = lax.broadcasted_iota(jnp.int32, (BLK, 2 * BLK), 0)
    kj = lax.broadcasted_iota(jnp.int32, (BLK, 2 * BLK), 1)
    in_band = (kj - qi >= 0) & (kj - qi <= BLK)
    for g, (_, dil) in enumerate(GROUPS):
        q_ref, k_ref, v_ref, kp_ref, vp_ref = ins[5 * g:5 * g + 5]
        bias = jnp.where(in_band, _shifted_rows(bvec_ref[pl.ds(g * heads + h, 1), :], BLK), NEG_INF)
        bias_p, bias_c = bias[:, :BLK], bias[:, BLK:]
        for r in range(dil):
            prev_rows = pl.ds(r, BLK, stride=dil)
            k_prev = kp_ref[prev_rows, :].astype(BF16)
            v_prev = vp_ref[prev_rows, :].astype(BF16)
            for nb in range(CHUNK // (dil * BLK)):
                rows = pl.ds(r + dil * BLK * nb, BLK, stride=dil)
                q = q_ref[rows, :].astype(BF16)
                k_cur = k_ref[rows, :].astype(BF16)
                v_cur = v_ref[rows, :].astype(BF16)
                s_p = _nt_dot(q, k_prev) * scale + bias_p
                if nb == 0:
                    s_p = jnp.where(first_chunk, NEG_INF, s_p)
                s_c = _nt_dot(q, k_cur) * scale + bias_c
                o_b, m_b, l_b = _block_softmax([s_p, s_c], [v_prev, v_cur])
                m_b = jnp.broadcast_to(m_b, (BLK, dh))
                l_b = jnp.broadcast_to(l_b, (BLK, dh))
                state = None if g == 0 else (acc_ref[rows, :], m_ref[rows, :], l_ref[rows, :])
                acc, m_new, l_new = _merge(state, o_b, m_b, l_b)
                if g == n_g - 1:
                    o_ref[rows, :] = acc / l_new
                else:
                    acc_ref[rows, :] = acc
                    m_ref[rows, :] = m_new
                    l_ref[rows, :] = l_new
                k_prev, v_prev = k_cur, v_cur


def _attn_prompt(qkv, bvec, batch, seq, heads, dh):
    n_g = len(GROUPS)
    nc = seq // CHUNK
    col = lambda comp, g: (lambda b, n, h: (b * nc + n, (comp * n_g + g) * heads + h))
    in_specs, args = [], []
    for g, (win, _) in enumerate(GROUPS):
        per = CHUNK // win
        prev = lambda comp, g=g, per=per: (
            lambda b, n, h: (jnp.maximum((b * nc + n) * per - 1, 0), (comp * n_g + g) * heads + h))
        in_specs += [pl.BlockSpec((CHUNK, dh), col(0, g)), pl.BlockSpec((CHUNK, dh), col(1, g)),
                     pl.BlockSpec((CHUNK, dh), col(2, g)),
                     pl.BlockSpec((win, dh), prev(1)), pl.BlockSpec((win, dh), prev(2))]
        args += [qkv] * 5
    in_specs.append(pl.BlockSpec(bvec.shape, lambda b, n, h: (0, 0)))
    vmem = 2 * (9 * CHUNK + 2 * sum(w for w, _ in GROUPS)) * dh * 4 + 2 * CHUNK * dh * 4 + 3 * CHUNK * dh * 4
    return pl.pallas_call(
        functools.partial(_attn_prompt_kernel, heads=heads, dh=dh),
        name="attn_prompt",
        out_shape=jax.ShapeDtypeStruct((batch * seq, heads * dh), F32),
        grid=(batch, nc, heads),
        in_specs=in_specs,
        out_specs=pl.BlockSpec((CHUNK, dh), lambda b, n, h: (b * nc + n, h)),
        scratch_shapes=[pltpu.VMEM((CHUNK, dh), F32)] * 3,
        compiler_params=_cparams(("parallel", "parallel", "parallel"), vmem),
    )(*args, bvec)


def _attn_step_kernel(*refs, heads, dh, n_bufs):
    n_g = len(GROUPS)
    ins = refs[:6 * n_g]
    o_ref, acc_ref, m_ref, l_ref, knp_ref, vnp_ref = refs[6 * n_g:]
    part = pl.program_id(1)
    t = o_ref.shape[1]
    scale = dh ** -0.5
    row_stride = 2 * heads

    def merge_into(hs, o_b, m_b, l_b):
        state = (acc_ref[:, hs], m_ref[:, hs], l_ref[:, hs])
        acc, m_new, l_new = _merge(state, o_b, jnp.broadcast_to(m_b, (t, dh)), jnp.broadcast_to(l_b, (t, dh)))
        acc_ref[:, hs] = acc
        m_ref[:, hs] = m_new
        l_ref[:, hs] = l_new

    def cache_piece(g, first_row):
        dil = GROUPS[g][1]
        q_ref, _, _, cache_ref, yc_ref, _ = ins[6 * g:6 * g + 6]
        rows = cache_ref.shape[1] // row_stride
        tc = lax.broadcasted_iota(jnp.int32, (t, rows), 0)
        cc = lax.broadcasted_iota(jnp.int32, (t, rows), 1) + first_row
        dist_c = n_bufs[g] + tc - cc
        ok_c = ((dist_c & (dil - 1)) == 0) & (dist_c <= (N_DIL_KEYS - 1) * dil)
        for h in range(heads):
            hs = slice(h * dh, (h + 1) * dh)
            q = q_ref[0, :, hs].astype(BF16)
            kc = cache_ref[0, pl.ds(h, rows, stride=row_stride), :].astype(BF16)
            vc = cache_ref[0, pl.ds(heads + h, rows, stride=row_stride), :].astype(BF16)
            bias = _shifted_rows(yc_ref[0, h:h + 1, :], t)[:, BLK:]
            s_c = jnp.where(ok_c, _nt_dot(q, kc) * scale + bias, NEG_INF)
            merge_into(hs, *_block_softmax([s_c], [vc]))

    @pl.when(part == 0)
    def _():
        acc_ref[...] = jnp.zeros_like(acc_ref)
        l_ref[...] = jnp.zeros_like(l_ref)
        m_ref[...] = jnp.full_like(m_ref, NEG_INF)
        tn = lax.broadcasted_iota(jnp.int32, (t, BLK), 0)
        cn = lax.broadcasted_iota(jnp.int32, (t, BLK), 1)
        dist_n = tn - cn
        for g, (_, dil) in enumerate(GROUPS):
            q_ref, kn_ref, vn_ref, _, _, yn_ref = ins[6 * g:6 * g + 6]
            ok_n = (dist_n >= 0) & ((dist_n & (dil - 1)) == 0) & (dist_n <= (N_DIL_KEYS - 1) * dil)
            knp_ref[...] = jnp.zeros_like(knp_ref)
            vnp_ref[...] = jnp.zeros_like(vnp_ref)
            knp_ref[0:t, :] = kn_ref[0]
            vnp_ref[0:t, :] = vn_ref[0]
            for h in range(heads):
                hs = slice(h * dh, (h + 1) * dh)
                q = q_ref[0, :, hs].astype(BF16)
                bias = _shifted_rows(yn_ref[h:h + 1, :], t)[:, BLK:]
                s_n = jnp.where(ok_n, _nt_dot(q, knp_ref[:, hs].astype(BF16)) * scale + bias, NEG_INF)
                merge_into(hs, *_block_softmax([s_n], [vnp_ref[:, hs].astype(BF16)]))
        for g in range(n_g):
            if ins[6 * g + 3].shape[1] == n_bufs[g] * row_stride:
                cache_piece(g, 0)

    for g in range(n_g):
        rows = ins[6 * g + 3].shape[1] // row_stride
        if rows != n_bufs[g]:
            cache_piece(g, part * rows)

    @pl.when(part == pl.num_programs(1) - 1)
    def _():
        o_ref[0] = acc_ref[...] / l_ref[...]


def _attn_step(qkv, caches, rel_bias, batch, t, heads, dh, n_parts=2):
    n_g = len(GROUPS)
    width = heads * dh
    qkv3 = qkv.reshape(batch, t, qkv.shape[1])
    in_specs, args, n_bufs = [], [], []
    vmem = 8 * t * width * 4
    for g in range(n_g):
        n_buf = caches[g].shape[1]
        parts = n_parts if n_buf % (n_parts * BLK) == 0 else 1
        rows = n_buf // parts
        flat = caches[g].reshape(batch, n_buf * 2 * heads, dh)
        yc, yn = _step_bias_vectors(rel_bias, g, heads, n_buf, parts)
        new = lambda comp, g=g: pl.BlockSpec((1, t, width), lambda b, s: (b, 0, comp * n_g + g))
        part_of = (lambda s: s) if parts == n_parts else (lambda s: 0)
        in_specs += [new(0), new(1), new(2),
                     pl.BlockSpec((1, rows * 2 * heads, dh), lambda b, s, f=part_of: (b, f(s), 0)),
                     pl.BlockSpec((1, heads, rows + BLK), lambda b, s, f=part_of: (f(s), 0, 0)),
                     pl.BlockSpec(yn.shape, lambda b, s: (0, 0))]
        args += [qkv3, qkv3, qkv3, flat, yc, yn]
        n_bufs.append(n_buf)
        vmem += 2 * rows * 2 * width * 4 + 6 * rows * dh * 4
    return pl.pallas_call(
        functools.partial(_attn_step_kernel, heads=heads, dh=dh, n_bufs=tuple(n_bufs)),
        name="attn_step",
        out_shape=jax.ShapeDtypeStruct((batch, t, width), F32),
        grid=(batch, n_parts),
        in_specs=in_specs,
        out_specs=pl.BlockSpec((1, t, width), lambda b, s: (b, 0, 0)),
        scratch_shapes=[pltpu.VMEM((t, width), F32)] * 3 + [pltpu.VMEM((BLK, width), F32)] * 2,
        compiler_params=_cparams(("parallel", "arbitrary"), vmem),
    )(*args).reshape(batch * t, width)


def _attn_out_kernel(x_ref, o_ref, wo_ref, out_ref):
    out_ref[...] = x_ref[...] + _dot(o_ref[...].astype(BF16), wo_ref[...])


def _attn_out(x, o, wo, tm_pref=512):
    m, d = x.shape
    width = wo.shape[0]
    tm = _tile(m, tm_pref)
    row = lambda n: pl.BlockSpec((tm, n), lambda i: (i, 0))
    vmem = 4 * tm * d * 4 + 2 * tm * width * 4 + 2 * width * d * 2
    return pl.pallas_call(
        _attn_out_kernel,
        name=f"attn_out_m{m}",
        out_shape=jax.ShapeDtypeStruct((m, d), F32),
        grid=(m // tm,),
        in_specs=[row(d), row(width), pl.BlockSpec((width, d), lambda i: (0, 0))],
        out_specs=row(d),
        compiler_params=_cparams(("parallel",), vmem),
    )(x, o, wo)


def _t5_bucket(dist):
    max_exact = N_BUCKETS // 2
    d = np.asarray(dist, dtype=np.int32)
    large = max_exact + (np.log(np.maximum(d, max_exact) / max_exact) / np.log(MAX_DISTANCE / max_exact)
                         * (N_BUCKETS - max_exact)).astype(np.int32)
    return np.where(d < max_exact, d, np.minimum(large, N_BUCKETS - 1)).astype(np.int32)


def _group_bias(rel_bias, g, heads):
    buckets = _t5_bucket(np.arange(N_DIL_KEYS) * GROUPS[g][1])
    return rel_bias[:, g * heads:(g + 1) * heads].astype(F32)[buckets]


def _prompt_bias_vectors(rel_bias, heads):
    vecs = []
    for g in range(len(GROUPS)):
        rev = _group_bias(rel_bias, g, heads)[::-1]
        vecs.append(jnp.pad(rev, ((0, 2 * BLK - N_DIL_KEYS), (0, 0))).T)
    return jnp.concatenate(vecs, axis=0)


def _step_bias_vectors(rel_bias, g, heads, n_buf, parts):
    rows = n_buf // parts
    bias = _group_bias(rel_bias, g, heads)
    at = lambda dist: bias[np.clip(dist // GROUPS[g][1], 0, N_DIL_KEYS - 1)]
    first = np.arange(parts)[:, None] * rows + np.arange(rows + BLK)[None, :]
    yc = jnp.transpose(at(n_buf + BLK - first), (0, 2, 1))
    yn = at(BLK - np.arange(2 * BLK)).T
    return yc, yn


def _kv_rows(qkv, g, batch, seq, n_rows, heads, dh):
    n_g = len(GROUPS)
    width = heads * dh
    q3 = qkv.reshape(batch, seq, qkv.shape[1])[:, seq - n_rows:]
    k = q3[:, :, (n_g + g) * width:(n_g + g + 1) * width]
    v = q3[:, :, (2 * n_g + g) * width:(2 * n_g + g + 1) * width]
    return jnp.stack([k, v], axis=2).reshape(batch, n_rows, 2, heads, dh)


def _trunk(x3, st, p, tiles):
    batch, seq, d = x3.shape
    m = batch * seq
    x = x3.reshape(m, d)
    new = []
    depth = p['ffn_w1'].shape[0]
    for i in range(depth):
        kind = i % N_MIXERS
        g_mix = p['norm_mix_g'][i][None]
        if kind == 0:
            j = i // N_MIXERS
            n_taps, c = p['conv_dw_w'].shape[1:]
            buf = jnp.zeros((batch, n_taps - 1, c), F32) if st is None else st[i][0]
            u = _norm_proj(x, g_mix, p['conv_w_glu'], 'glu', tiles['proj_m'], layer=j)
            u3 = u.reshape(batch, seq, c)
            tm = _tile(seq, tiles['conv_m'])
            halo = _halos(u3, buf, tm, CONV_HALO)
            dww = jnp.repeat(p['conv_dw_w'][j], SUBLANES, axis=0)
            x = _conv_tail(x, u, halo, dww, p['conv_dw_b'][j][None], p['conv_ln_g'][j][None],
                           p['conv_ln_b'][j][None], p['conv_w_out'], j, tm, n_taps)
            new.append((_last_rows(buf, u3, n_taps - 1),))
        elif kind == 1:
            n_taps, c = p['lru_conv_w'].shape
            if st is None:
                h0 = jnp.zeros((batch, c), F32)
                cb = jnp.zeros((batch, n_taps - 1, c), F32)
            else:
                h0, cb = st[i]
            ybr, xbr = _norm_proj(x, g_mix, p['lru_w_in'], 'split', tiles['proj_m'])
            xbr3 = xbr.reshape(batch, seq, c)
            tm = _tile(seq, tiles['lru_m'])
            halo = _halos(xbr3, cb, tm, LRU_HALO)
            cw = jnp.pad(p['lru_conv_w'], ((0, SUBLANES - n_taps), (0, 0)))
            x, hl = _lru_tail(x, ybr, xbr, halo, h0[:, None], cw, p['lru_conv_b'][None], p['lru_w_a'],
                              p['lru_b_a'][None], p['lru_w_x'], p['lru_b_x'][None], p['lru_lambda'][None],
                              p['lru_w_out'], batch, tm, n_taps)
            new.append((hl[:, 0], _last_rows(cb, xbr3, n_taps - 1)))
        else:
            rel_bias = p['rel_bias']
            heads = rel_bias.shape[1] // len(GROUPS)
            width = p['att_w_o'].shape[0]
            dh = width // heads
            qkv = _norm_proj(x, g_mix, p['att_w_qkv'], 'plain', tiles['proj_m'])
            if st is None:
                assert seq % CHUNK == 0
                o = _attn_prompt(qkv, _prompt_bias_vectors(rel_bias, heads), batch, seq, heads, dh)
                rows = [_kv_rows(qkv, g, batch, seq, min(win, seq), heads, dh) for g, (win, _) in enumerate(GROUPS)]
            else:
                assert seq <= BLK
                o = _attn_step(qkv, st[i], rel_bias, batch, seq, heads, dh)
                rows = [_kv_rows(qkv, g, batch, seq, seq, heads, dh) for g in range(len(GROUPS))]
            x = _attn_out(x, o, p['att_w_o'], tiles['attn_out_m'])
            new.append(rows)
        x = _ffn(x, p['norm_ffn_g'][i][None], p['ffn_w1'], p['ffn_w2'], i, p['final_norm_g'][None],
                 final=(i == depth - 1), tm_pref=tiles['ffn_m'])
    return x.reshape(batch, seq, d), new


PROMPT_TILES = dict(proj_m=1024, conv_m=256, lru_m=256, attn_out_m=512, ffn_m=512)
SAMPLE_TILES = dict(proj_m=512, conv_m=256, lru_m=256, attn_out_m=512, ffn_m=512)
MATMUL_WEIGHTS = ('ffn_w1', 'ffn_w2', 'conv_w_glu', 'conv_w_out', 'lru_w_in', 'lru_w_a', 'lru_w_x',
                  'lru_w_out', 'att_w_qkv', 'att_w_o')


def kernel(x_prompt, x_sample, state_conv_l0, state_lru_h_l1, state_lru_conv_l1, cache_kv_w128_l2, cache_kv_w512_l2, cache_kv_w2048_l2, state_conv_l3, norm_mix_g, norm_ffn_g, final_norm_g, ffn_w1, ffn_w2, conv_w_glu, conv_dw_w, conv_dw_b, conv_ln_g, conv_ln_b, conv_w_out, lru_w_in, lru_conv_w, lru_conv_b, lru_w_a, lru_b_a, lru_w_x, lru_b_x, lru_lambda, lru_w_out, att_w_qkv, att_w_o, rel_bias):
    p = dict(norm_mix_g=norm_mix_g, norm_ffn_g=norm_ffn_g, final_norm_g=final_norm_g, ffn_w1=ffn_w1,
             ffn_w2=ffn_w2, conv_w_glu=conv_w_glu, conv_dw_w=conv_dw_w, conv_dw_b=conv_dw_b,
             conv_ln_g=conv_ln_g, conv_ln_b=conv_ln_b, conv_w_out=conv_w_out, lru_w_in=lru_w_in,
             lru_conv_w=lru_conv_w, lru_conv_b=lru_conv_b, lru_w_a=lru_w_a, lru_b_a=lru_b_a, lru_w_x=lru_w_x,
             lru_b_x=lru_b_x, lru_lambda=lru_lambda, lru_w_out=lru_w_out, att_w_qkv=att_w_qkv,
             att_w_o=att_w_o, rel_bias=rel_bias)
    for name in MATMUL_WEIGHTS:
        p[name] = p[name].astype(BF16)
    y_prompt, pst = _trunk(x_prompt, None, p, PROMPT_TILES)
    y_sample, sst = _trunk(x_sample, [(state_conv_l0,), (state_lru_h_l1, state_lru_conv_l1),
                                      (cache_kv_w128_l2, cache_kv_w512_l2, cache_kv_w2048_l2),
                                      (state_conv_l3,)], p, SAMPLE_TILES)
    return (y_prompt, y_sample,
            pst[0][0], sst[0][0],
            pst[1][0], sst[1][0],
            pst[1][1], sst[1][1],
            pst[2][0], sst[2][0],
            pst[2][1], sst[2][1],
            pst[2][2], sst[2][2],
            pst[3][0], sst[3][0])
```

```python
import functools

import numpy as np
import jax
import jax.numpy as jnp
from jax import lax
from jax.experimental import pallas as pl
from jax.experimental.pallas import tpu as pltpu

RMS_EPS = 1e-6
LN_EPS = 1e-5
LRU_C = 8.0
NEG_INF = -1e30
N_MIXERS = 3
GROUPS = ((128, 1), (512, 4), (2048, 16))
N_DIL_KEYS = 129
BLK = 128
CHUNK = BLK * max(d for _, d in GROUPS)
N_BUCKETS = 32
MAX_DISTANCE = 2048

V7X_VMEM_BYTES = 64 * 1024 * 1024
SUBLANES = 8
CONV_HALO = 32
LRU_HALO = 8

BF16 = jnp.bfloat16
F32 = jnp.float32


def _cparams(semantics, vmem_bytes):
    limit = int(min(max(vmem_bytes * 3 // 2, 32 * 1024 * 1024), V7X_VMEM_BYTES - 8 * 1024 * 1024))
    return pltpu.CompilerParams(dimension_semantics=semantics, vmem_limit_bytes=limit)


def _tile(n, pref):
    if n <= pref:
        return n
    t = pref
    while n % t:
        t //= 2
    return t


def _rms_to_bf16(x, g):
    ms = jnp.mean(x * x, axis=-1, keepdims=True)
    return (x * lax.rsqrt(ms + RMS_EPS) * g).astype(BF16)


def _dot(a, b):
    return jnp.dot(a, b, preferred_element_type=F32)


def _nt_dot(a, b):
    return lax.dot_general(a, b, (((1,), (1,)), ((), ())), preferred_element_type=F32)


def _norm_proj_kernel(x_ref, g_ref, *refs, mode):
    if mode == 'plain':
        w_ref, o_ref, h_ref = refs
    elif mode == 'glu':
        wa_ref, wg_ref, o_ref, h_ref = refs
    else:
        wa_ref, wg_ref, o_ref, o2_ref, h_ref = refs

    @pl.when(pl.program_id(1) == 0)
    def _():
        h_ref[...] = _rms_to_bf16(x_ref[...], g_ref[...])

    h = h_ref[...]
    if mode == 'plain':
        o_ref[...] = _dot(h, w_ref[...])
    elif mode == 'glu':
        o_ref[...] = _dot(h, wa_ref[...]) * jax.nn.sigmoid(_dot(h, wg_ref[...]))
    else:
        o_ref[...] = jax.nn.gelu(_dot(h, wa_ref[...]), approximate=True)
        o2_ref[...] = _dot(h, wg_ref[...])


def _wspec(w, layer, block, index_map):
    if w.ndim == 2:
        return pl.BlockSpec(block, index_map)
    return pl.BlockSpec((None,) + block, lambda *ids: (layer,) + index_map(*ids))


def _norm_proj(x, g, w, mode, tm_pref=512, tn_pref=512, layer=0):
    m, d = x.shape
    n = w.shape[-1]
    n_out = n if mode == 'plain' else n // 2
    tm = _tile(m, tm_pref)
    tn = _tile(n_out, tn_pref)
    nj = n_out // tn
    in_specs = [pl.BlockSpec((tm, d), lambda i, j: (i, 0)),
                pl.BlockSpec((1, d), lambda i, j: (0, 0)),
                _wspec(w, layer, (d, tn), lambda i, j: (0, j))]
    args = [x, g, w]
    if mode != 'plain':
        in_specs.append(_wspec(w, layer, (d, tn), lambda i, j: (0, j + nj)))
        args.append(w)
    o_spec = pl.BlockSpec((tm, tn), lambda i, j: (i, j))
    o_shape = jax.ShapeDtypeStruct((m, n_out), F32)
    n_o = 2 if mode == 'split' else 1
    vmem = 2 * tm * d * 4 + tm * d * 2 + 2 * len(args[2:]) * d * tn * 2 + 2 * n_o * tm * tn * 4
    return pl.pallas_call(
        functools.partial(_norm_proj_kernel, mode=mode),
        name=f"norm_proj_{mode}_m{m}",
        out_shape=[o_shape] * n_o if n_o == 2 else o_shape,
        grid=(m // tm, nj),
        in_specs=in_specs,
        out_specs=[o_spec] * n_o if n_o == 2 else o_spec,
        scratch_shapes=[pltpu.VMEM((tm, d), BF16)],
        compiler_params=_cparams(("parallel", "arbitrary"), vmem),
    )(*args)


def _ffn_kernel(x_ref, g_ref, w1_ref, w2_ref, fg_ref, o_ref, h_ref, *, final):
    f = pl.program_id(1)

    @pl.when(f == 0)
    def _():
        x = x_ref[...]
        h_ref[...] = _rms_to_bf16(x, g_ref[...])
        o_ref[...] = x

    hid = jnp.square(jnp.maximum(_dot(h_ref[...], w1_ref[...]), 0.0)).astype(BF16)
    o_ref[...] += _dot(hid, w2_ref[...])

    if final:
        @pl.when(f == pl.num_programs(1) - 1)
        def _():
            y = o_ref[...]
            ms = jnp.mean(y * y, axis=-1, keepdims=True)
            o_ref[...] = y * lax.rsqrt(ms + RMS_EPS) * fg_ref[...]


def _ffn(x, g, w1, w2, layer, final_g, final, tm_pref=512, tf_pref=1024):
    m, d = x.shape
    dff = w1.shape[-1]
    tm = _tile(m, tm_pref)
    tf = _tile(dff, tf_pref)
    vmem = 2 * tm * d * 4 + tm * d * 2 + 2 * 2 * d * tf * 2 + 2 * tm * d * 4 + 2 * tm * tf * 4
    return pl.pallas_call(
        functools.partial(_ffn_kernel, final=final),
        name=f"ffn_m{m}_l{layer}",
        out_shape=jax.ShapeDtypeStruct((m, d), F32),
        grid=(m // tm, dff // tf),
        in_specs=[pl.BlockSpec((tm, d), lambda i, f: (i, 0)),
                  pl.BlockSpec((1, d), lambda i, f: (0, 0)),
                  _wspec(w1, layer, (d, tf), lambda i, f: (0, f)),
                  _wspec(w2, layer, (tf, d), lambda i, f: (f, 0)),
                  pl.BlockSpec((1, d), lambda i, f: (0, 0))],
        out_specs=pl.BlockSpec((tm, d), lambda i, f: (i, 0)),
        scratch_shapes=[pltpu.VMEM((tm, d), BF16)],
        compiler_params=_cparams(("parallel", "arbitrary"), vmem),
    )(x, g, w1, w2, final_g)


def _conv_tail_kernel(x_ref, u_ref, halo_ref, dww_ref, dwb_ref, lng_ref, lnb_ref, wout_ref,
                      o_ref, uext_ref, y_ref, *, n_taps, rows, ln_rows, lanes):
    tm, c = u_ref.shape
    uext_ref[0:CONV_HALO, :] = halo_ref[0]
    uext_ref[CONV_HALO:CONV_HALO + tm, :] = u_ref[...]
    off = CONV_HALO - (n_taps - 1)

    def row_chunk(r, carry):
        r0 = pl.multiple_of(r * rows, rows)
        for l0 in range(0, c, lanes):
            win = uext_ref[pl.ds(r0, rows + CONV_HALO), l0:l0 + lanes]
            accs = [jnp.zeros((SUBLANES, lanes), F32)] * (rows // SUBLANES)
            for s in range(SUBLANES):
                ws = win if s == 0 else pltpu.roll(win, rows + CONV_HALO - s, axis=0)
                for a in range((CONV_HALO + SUBLANES) // SUBLANES):
                    k = a * SUBLANES + s - off
                    if 0 <= k < n_taps:
                        wk = dww_ref[k * SUBLANES:(k + 1) * SUBLANES, l0:l0 + lanes]
                        accs = [acc + wk * ws[(a + j) * SUBLANES:(a + j + 1) * SUBLANES]
                                for j, acc in enumerate(accs)]
            for j, acc in enumerate(accs):
                uext_ref[pl.ds(r0 + j * SUBLANES, SUBLANES), l0:l0 + lanes] = acc + dwb_ref[:, l0:l0 + lanes]
        return carry

    lax.fori_loop(0, tm // rows, row_chunk, 0)

    def ln_chunk(r, carry):
        r0 = pl.multiple_of(r * ln_rows, ln_rows)
        cv = uext_ref[pl.ds(r0, ln_rows), :]
        mu = jnp.mean(cv, axis=-1, keepdims=True)
        cc = cv - mu
        var = jnp.mean(cc * cc, axis=-1, keepdims=True)
        y = cc * lax.rsqrt(var + LN_EPS) * lng_ref[...] + lnb_ref[...]
        y_ref[pl.ds(r0, ln_rows), :] = (y * jax.nn.sigmoid(y)).astype(BF16)
        return carry

    lax.fori_loop(0, tm // ln_rows, ln_chunk, 0, unroll=min(2, tm // ln_rows))
    o_ref[...] = x_ref[...] + _dot(y_ref[...], wout_ref[...])


def _conv_tail(x, u, halo, dww, dwb, lng, lnb, wout, layer, tm, n_taps):
    m, d = x.shape
    c = u.shape[1]
    rows = _tile(tm, 32)
    ln_rows = _tile(tm, 16)
    lanes = _tile(c, 256)
    vmem = 4 * tm * d * 4 + 2 * tm * c * 4 + 2 * c * d * 2 + (tm + CONV_HALO) * c * 4 + tm * c * 2
    return pl.pallas_call(
        functools.partial(_conv_tail_kernel, n_taps=n_taps, rows=rows, ln_rows=ln_rows, lanes=lanes),
        name=f"conv_tail_m{m}",
        out_shape=jax.ShapeDtypeStruct((m, d), F32),
        grid=(m // tm,),
        in_specs=[pl.BlockSpec((tm, d), lambda i: (i, 0)),
                  pl.BlockSpec((tm, c), lambda i: (i, 0)),
                  pl.BlockSpec((1, CONV_HALO, c), lambda i: (i, 0, 0)),
                  pl.BlockSpec(dww.shape, lambda i: (0, 0)),
                  pl.BlockSpec((1, c), lambda i: (0, 0)),
                  pl.BlockSpec((1, c), lambda i: (0, 0)),
                  pl.BlockSpec((1, c), lambda i: (0, 0)),
                  _wspec(wout, layer, (c, d), lambda i: (0, 0))],
        out_specs=pl.BlockSpec((tm, d), lambda i: (i, 0)),
        scratch_shapes=[pltpu.VMEM((tm + CONV_HALO, c), F32), pltpu.VMEM((tm, c), BF16)],
        compiler_params=_cparams(("parallel",), vmem),
    )(x, u, halo, dww, dwb, lng, lnb, wout)


def _last_rows(past, seq, n):
    if seq.shape[1] >= n:
        return seq[:, seq.shape[1] - n:]
    return jnp.concatenate([past, seq], axis=1)[:, -n:]


def _halos(seq, past, tm, halo_rows):
    b, l, c = seq.shape
    p = past.shape[1]
    first = jnp.concatenate([jnp.zeros((b, halo_rows - p, c), seq.dtype), past], axis=1)[:, None]
    nt = l // tm
    if nt == 1:
        return first.reshape(b, halo_rows, c)
    rest = seq.reshape(b, nt, tm, c)[:, :-1, tm - halo_rows:]
    return jnp.concatenate([first, rest], axis=1).reshape(b * nt, halo_rows, c)


def _lru_tail_kernel(x_ref, ybr_ref, xbr_ref, halo_ref, h0_ref, cw_ref, cb_ref, wa_ref, ba_ref,
                     wx_ref, bx_ref, lam_ref, wout_ref, o_ref, hlast_ref,
                     xext_ref, a_ref, b_ref, hc_ref, *, n_taps, lanes):
    tm, c = xbr_ref.shape
    n_blocks, blk, _ = wa_ref.shape
    t = pl.program_id(1)

    @pl.when(t == 0)
    def _():
        hc_ref[...] = jnp.broadcast_to(h0_ref[0], hc_ref.shape)

    xext_ref[0:LRU_HALO, :] = halo_ref[0]
    xext_ref[LRU_HALO:LRU_HALO + tm, :] = xbr_ref[...]
    off = LRU_HALO - (n_taps - 1)

    sp = jax.nn.softplus(-lam_ref[...])
    for n in range(n_blocks):
        sl = slice(n * blk, (n + 1) * blk)
        xc = jnp.zeros((tm, blk), F32)
        for k in range(n_taps):
            xc = xc + cw_ref[k:k + 1, sl] * xext_ref[off + k:off + k + tm, sl]
        xc = xc + cb_ref[:, sl]
        xcb = xc.astype(BF16)
        r = jax.nn.sigmoid(_dot(xcb, wa_ref[n]) + ba_ref[:, sl])
        gi = jax.nn.sigmoid(_dot(xcb, wx_ref[n]) + bx_ref[:, sl])
        log_a = -LRU_C * r * sp[:, sl]
        a = jnp.exp(log_a)
        a_ref[:, sl] = a
        b_ref[:, sl] = jnp.sqrt(-jnp.tanh(log_a) * (a * a + 1.0)) * (gi * xc)

    row = lax.broadcasted_iota(jnp.int32, (SUBLANES, lanes), 0)

    for l0 in range(0, c, lanes):
        def group(gidx, h):
            r0 = pl.multiple_of(gidx * SUBLANES, SUBLANES)
            av = a_ref[pl.ds(r0, SUBLANES), l0:l0 + lanes]
            bv = b_ref[pl.ds(r0, SUBLANES), l0:l0 + lanes]
            for s in (1, 2, 4):
                a_sh = jnp.where(row >= s, pltpu.roll(av, s, axis=0), 1.0)
                b_sh = jnp.where(row >= s, pltpu.roll(bv, s, axis=0), 0.0)
                bv = av * b_sh + bv
                av = av * a_sh
            hs = av * h + bv
            b_ref[pl.ds(r0, SUBLANES), l0:l0 + lanes] = hs
            return jnp.broadcast_to(hs[SUBLANES - 1:SUBLANES, :], (SUBLANES, lanes))

        h_fin = lax.fori_loop(0, tm // SUBLANES, group, hc_ref[:, l0:l0 + lanes], unroll=2)
        hc_ref[:, l0:l0 + lanes] = h_fin

    gated = (b_ref[...] * ybr_ref[...]).astype(BF16)
    o_ref[...] = x_ref[...] + _dot(gated, wout_ref[...])

    @pl.when(t == pl.num_programs(1) - 1)
    def _():
        hlast_ref[0] = hc_ref[...]


def _lru_tail(x, ybr, xbr, halo, h0, cw, cb, wa, ba, wx, bx, lam, wout, batch, tm, n_taps):
    m, d = x.shape
    c = xbr.shape[1]
    nt = m // batch // tm
    lanes = _tile(c, 512)
    row2 = lambda b, t: (b * nt + t, 0)
    const2 = lambda b, t: (0, 0)
    vec = pl.BlockSpec((1, c), const2)
    vmem = (4 * tm * d * 4 + 4 * tm * c * 4 + 2 * c * d * 2 + 4 * wa.size * 2
            + (tm + LRU_HALO) * c * 4 + 2 * tm * c * 4)
    return pl.pallas_call(
        functools.partial(_lru_tail_kernel, n_taps=n_taps, lanes=lanes),
        name=f"lru_tail_m{m}",
        out_shape=[jax.ShapeDtypeStruct((m, d), F32),
                   jax.ShapeDtypeStruct((batch, SUBLANES, c), F32)],
        grid=(batch, nt),
        in_specs=[pl.BlockSpec((tm, d), row2),
                  pl.BlockSpec((tm, c), row2),
                  pl.BlockSpec((tm, c), row2),
                  pl.BlockSpec((1, LRU_HALO, c), lambda b, t: (b * nt + t, 0, 0)),
                  pl.BlockSpec((1, 1, c), lambda b, t: (b, 0, 0)),
                  pl.BlockSpec(cw.shape, const2),
                  vec,
                  pl.BlockSpec(wa.shape, lambda b, t: (0, 0, 0)),
                  vec,
                  pl.BlockSpec(wx.shape, lambda b, t: (0, 0, 0)),
                  vec, vec,
                  pl.BlockSpec((c, d), const2)],
        out_specs=[pl.BlockSpec((tm, d), row2),
                   pl.BlockSpec((1, SUBLANES, c), lambda b, t: (b, 0, 0))],
        scratch_shapes=[pltpu.VMEM((tm + LRU_HALO, c), F32), pltpu.VMEM((tm, c), F32),
                        pltpu.VMEM((tm, c), F32), pltpu.VMEM((SUBLANES, c), F32)],
        compiler_params=_cparams(("parallel", "arbitrary"), vmem),
    )(x, ybr, xbr, halo, h0, cw, cb, wa, ba, wx, bx, lam, wout)


def _shifted_rows(vec, n_rows):
    return pltpu.roll(jnp.broadcast_to(vec, (n_rows, vec.shape[1])), 0, 1, stride=1, stride_axis=0)


def _block_softmax(s_parts, v_parts):
    m = s_parts[0].max(axis=-1, keepdims=True)
    for s in s_parts[1:]:
        m = jnp.maximum(m, s.max(axis=-1, keepdims=True))
    o, den = None, None
    for s, v in zip(s_parts, v_parts):
        p = jnp.exp(s - m)
        d = p.sum(axis=-1, keepdims=True)
        pv = _dot(p.astype(BF16), v)
        o, den = (pv, d) if o is None else (o + pv, den + d)
    return o, m, den


def _block_softmax_mxu_sum(s_parts, v_parts):
    m = s_parts[0].max(axis=-1, keepdims=True)
    for s in s_parts[1:]:
        m = jnp.maximum(m, s.max(axis=-1, keepdims=True))
    acc = None
    for s, v in zip(s_parts, v_parts):
        pv = _dot(jnp.exp(s - m).astype(BF16), v)
        acc = pv if acc is None else acc + pv
    dh = acc.shape[1] // 2
    return acc[:, :dh], m, acc[:, dh:]


def _merge(state, o_b, m_b, l_b):
    if state is None:
        return o_b, m_b, l_b
    acc, m_old, l_old = state
    m_new = jnp.maximum(m_old, m_b)
    a_old = jnp.exp(m_old - m_new)
    a_b = jnp.exp(m_b - m_new)
    return acc * a_old + o_b * a_b, m_new, l_old * a_old + l_b * a_b


def _attn_prompt_kernel(*refs, heads, dh):
    n_g = len(GROUPS)
    ins = refs[:5 * n_g]
    bvec_ref, o_ref, acc_ref, m_ref, l_ref = refs[5 * n_g:]
    first_chunk = pl.program_id(1) == 0
    h = pl.program_id(2)
    scale = dh ** -0.5
    qi = lax.broadcasted_iota(jnp.int32, (BLK, 2 * BLK), 0)
    kj = lax.broadcasted_iota(jnp.int32, (BLK, 2 * BLK), 1)
    in_band = (kj - qi >= 0) & (kj - qi <= BLK)
    ones = jnp.ones((BLK, dh), BF16)
    for g, (_, dil) in enumerate(GROUPS):
        q_ref, k_ref, v_ref, kp_ref, vp_ref = ins[5 * g:5 * g + 5]
        bias = jnp.where(in_band, _shifted_rows(bvec_ref[pl.ds(g * heads + h, 1), :], BLK), NEG_INF)
        bias_p, bias_c = bias[:, :BLK], bias[:, BLK:]
        for r in range(dil):
            prev_rows = pl.ds(r, BLK, stride=dil)
            k_prev = kp_ref[prev_rows, :].astype(BF16)
            v_prev = jnp.concatenate([vp_ref[prev_rows, :].astype(BF16), ones], axis=1)
            for nb in range(CHUNK // (dil * BLK)):
                rows = pl.ds(r + dil * BLK * nb, BLK, stride=dil)
                q = q_ref[rows, :].astype(BF16)
                k_cur = k_ref[rows, :].astype(BF16)
                v_cur = jnp.concatenate([v_ref[rows, :].astype(BF16), ones], axis=1)
                s_p = _nt_dot(q, k_prev) * scale + bias_p
                if nb == 0:
                    s_p = jnp.where(first_chunk, NEG_INF, s_p)
                s_c = _nt_dot(q, k_cur) * scale + bias_c
                o_b, m_b, l_b = _block_softmax_mxu_sum([s_p, s_c], [v_prev, v_cur])
                m_b = jnp.broadcast_to(m_b, (BLK, dh))
                state = None if g == 0 else (acc_ref[rows, :], m_ref[rows, :], l_ref[rows, :])
                acc, m_new, l_new = _merge(state, o_b, m_b, l_b)
                if g == n_g - 1:
                    o_ref[rows, :] = acc / l_new
                else:
                    acc_ref[rows, :] = acc
                    m_ref[rows, :] = m_new
                    l_ref[rows, :] = l_new
                k_prev, v_prev = k_cur, v_cur


def _attn_prompt(qkv, bvec, batch, seq, heads, dh):
    n_g = len(GROUPS)
    nc = seq // CHUNK
    col = lambda comp, g: (lambda b, n, h: (b * nc + n, (comp * n_g + g) * heads + h))
    in_specs, args = [], []
    for g, (win, _) in enumerate(GROUPS):
        per = CHUNK // win
        prev = lambda comp, g=g, per=per: (
            lambda b, n, h: (jnp.maximum((b * nc + n) * per - 1, 0), (comp * n_g + g) * heads + h))
        in_specs += [pl.BlockSpec((CHUNK, dh), col(0, g)), pl.BlockSpec((CHUNK, dh), col(1, g)),
                     pl.BlockSpec((CHUNK, dh), col(2, g)),
                     pl.BlockSpec((win, dh), prev(1)), pl.BlockSpec((win, dh), prev(2))]
        args += [qkv] * 5
    in_specs.append(pl.BlockSpec(bvec.shape, lambda b, n, h: (0, 0)))
    vmem = 2 * (9 * CHUNK + 2 * sum(w for w, _ in GROUPS)) * dh * 4 + 2 * CHUNK * dh * 4 + 3 * CHUNK * dh * 4
    return pl.pallas_call(
        functools.partial(_attn_prompt_kernel, heads=heads, dh=dh),
        name="attn_prompt",
        out_shape=jax.ShapeDtypeStruct((batch * seq, heads * dh), F32),
        grid=(batch, nc, heads),
        in_specs=in_specs,
        out_specs=pl.BlockSpec((CHUNK, dh), lambda b, n, h: (b * nc + n, h)),
        scratch_shapes=[pltpu.VMEM((CHUNK, dh), F32)] * 3,
        compiler_params=_cparams(("parallel", "parallel", "parallel"), vmem),
    )(*args, bvec)


def _attn_step_kernel(*refs, heads, dh, n_bufs):
    n_g = len(GROUPS)
    ins = refs[:6 * n_g]
    o_ref, acc_ref, m_ref, l_ref, knp_ref, vnp_ref = refs[6 * n_g:]
    part = pl.program_id(1)
    t = o_ref.shape[1]
    scale = dh ** -0.5
    row_stride = 2 * heads

    def merge_into(hs, o_b, m_b, l_b):
        state = (acc_ref[:, hs], m_ref[:, hs], l_ref[:, hs])
        acc, m_new, l_new = _merge(state, o_b, jnp.broadcast_to(m_b, (t, dh)), jnp.broadcast_to(l_b, (t, dh)))
        acc_ref[:, hs] = acc
        m_ref[:, hs] = m_new
        l_ref[:, hs] = l_new

    def cache_piece(g, first_row):
        dil = GROUPS[g][1]
        q_ref, _, _, cache_ref, yc_ref, _ = ins[6 * g:6 * g + 6]
        rows = cache_ref.shape[1] // row_stride
        tc = lax.broadcasted_iota(jnp.int32, (t, rows), 0)
        cc = lax.broadcasted_iota(jnp.int32, (t, rows), 1) + first_row
        dist_c = n_bufs[g] + tc - cc
        ok_c = ((dist_c & (dil - 1)) == 0) & (dist_c <= (N_DIL_KEYS - 1) * dil)
        for h in range(heads):
            hs = slice(h * dh, (h + 1) * dh)
            q = q_ref[0, :, hs].astype(BF16)
            kc = cache_ref[0, pl.ds(h, rows, stride=row_stride), :].astype(BF16)
            vc = cache_ref[0, pl.ds(heads + h, rows, stride=row_stride), :].astype(BF16)
            bias = _shifted_rows(yc_ref[0, h:h + 1, :], t)[:, BLK:]
            s_c = jnp.where(ok_c, _nt_dot(q, kc) * scale + bias, NEG_INF)
            merge_into(hs, *_block_softmax([s_c], [vc]))

    @pl.when(part == 0)
    def _():
        acc_ref[...] = jnp.zeros_like(acc_ref)
        l_ref[...] = jnp.zeros_like(l_ref)
        m_ref[...] = jnp.full_like(m_ref, NEG_INF)
        tn = lax.broadcasted_iota(jnp.int32, (t, BLK), 0)
        cn = lax.broadcasted_iota(jnp.int32, (t, BLK), 1)
        dist_n = tn - cn
        for g, (_, dil) in enumerate(GROUPS):
            q_ref, kn_ref, vn_ref, _, _, yn_ref = ins[6 * g:6 * g + 6]
            ok_n = (dist_n >= 0) & ((dist_n & (dil - 1)) == 0) & (dist_n <= (N_DIL_KEYS - 1) * dil)
            knp_ref[...] = jnp.zeros_like(knp_ref)
            vnp_ref[...] = jnp.zeros_like(vnp_ref)
            knp_ref[0:t, :] = kn_ref[0]
            vnp_ref[0:t, :] = vn_ref[0]
            for h in range(heads):
                hs = slice(h * dh, (h + 1) * dh)
                q = q_ref[0, :, hs].astype(BF16)
                bias = _shifted_rows(yn_ref[h:h + 1, :], t)[:, BLK:]
                s_n = jnp.where(ok_n, _nt_dot(q, knp_ref[:, hs].astype(BF16)) * scale + bias, NEG_INF)
                merge_into(hs, *_block_softmax([s_n], [vnp_ref[:, hs].astype(BF16)]))
        for g in range(n_g):
            if ins[6 * g + 3].shape[1] == n_bufs[g] * row_stride:
                cache_piece(g, 0)

    for g in range(n_g):
        rows = ins[6 * g + 3].shape[1] // row_stride
        if rows != n_bufs[g]:
            cache_piece(g, part * rows)

    @pl.when(part == pl.num_programs(1) - 1)
    def _():
        o_ref[0] = acc_ref[...] / l_ref[...]


def _attn_step(qkv, caches, rel_bias, batch, t, heads, dh, n_parts=2):
    n_g = len(GROUPS)
    width = heads * dh
    qkv3 = qkv.reshape(batch, t, qkv.shape[1])
    in_specs, args, n_bufs = [], [], []
    vmem = 8 * t * width * 4
    for g in range(n_g):
        n_buf = caches[g].shape[1]
        parts = n_parts if n_buf % (n_parts * BLK) == 0 else 1
        rows = n_buf // parts
        flat = caches[g].reshape(batch, n_buf * 2 * heads, dh)
        yc, yn = _step_bias_vectors(rel_bias, g, heads, n_buf, parts)
        new = lambda comp, g=g: pl.BlockSpec((1, t, width), lambda b, s: (b, 0, comp * n_g + g))
        part_of = (lambda s: s) if parts == n_parts else (lambda s: 0)
        in_specs += [new(0), new(1), new(2),
                     pl.BlockSpec((1, rows * 2 * heads, dh), lambda b, s, f=part_of: (b, f(s), 0)),
                     pl.BlockSpec((1, heads, rows + BLK), lambda b, s, f=part_of: (f(s), 0, 0)),
                     pl.BlockSpec(yn.shape, lambda b, s: (0, 0))]
        args += [qkv3, qkv3, qkv3, flat, yc, yn]
        n_bufs.append(n_buf)
        vmem += 2 * rows * 2 * width * 4 + 6 * rows * dh * 4
    return pl.pallas_call(
        functools.partial(_attn_step_kernel, heads=heads, dh=dh, n_bufs=tuple(n_bufs)),
        name="attn_step",
        out_shape=jax.ShapeDtypeStruct((batch, t, width), F32),
        grid=(batch, n_parts),
        in_specs=in_specs,
        out_specs=pl.BlockSpec((1, t, width), lambda b, s: (b, 0, 0)),
        scratch_shapes=[pltpu.VMEM((t, width), F32)] * 3 + [pltpu.VMEM((BLK, width), F32)] * 2,
        compiler_params=_cparams(("parallel", "arbitrary"), vmem),
    )(*args).reshape(batch * t, width)


def _attn_out_kernel(x_ref, o_ref, wo_ref, out_ref):
    out_ref[...] = x_ref[...] + _dot(o_ref[...].astype(BF16), wo_ref[...])


def _attn_out(x, o, wo, tm_pref=512):
    m, d = x.shape
    width = wo.shape[0]
    tm = _tile(m, tm_pref)
    row = lambda n: pl.BlockSpec((tm, n), lambda i: (i, 0))
    vmem = 4 * tm * d * 4 + 2 * tm * width * 4 + 2 * width * d * 2
    return pl.pallas_call(
        _attn_out_kernel,
        name=f"attn_out_m{m}",
        out_shape=jax.ShapeDtypeStruct((m, d), F32),
        grid=(m // tm,),
        in_specs=[row(d), row(width), pl.BlockSpec((width, d), lambda i: (0, 0))],
        out_specs=row(d),
        compiler_params=_cparams(("parallel",), vmem),
    )(x, o, wo)


def _t5_bucket(dist):
    max_exact = N_BUCKETS // 2
    d = np.asarray(dist, dtype=np.int32)
    large = max_exact + (np.log(np.maximum(d, max_exact) / max_exact) / np.log(MAX_DISTANCE / max_exact)
                         * (N_BUCKETS - max_exact)).astype(np.int32)
    return np.where(d < max_exact, d, np.minimum(large, N_BUCKETS - 1)).astype(np.int32)


def _group_bias(rel_bias, g, heads):
    buckets = _t5_bucket(np.arange(N_DIL_KEYS) * GROUPS[g][1])
    return rel_bias[:, g * heads:(g + 1) * heads].astype(F32)[buckets]


def _prompt_bias_vectors(rel_bias, heads):
    vecs = []
    for g in range(len(GROUPS)):
        rev = _group_bias(rel_bias, g, heads)[::-1]
        vecs.append(jnp.pad(rev, ((0, 2 * BLK - N_DIL_KEYS), (0, 0))).T)
    return jnp.concatenate(vecs, axis=0)


def _step_bias_vectors(rel_bias, g, heads, n_buf, parts):
    rows = n_buf // parts
    bias = _group_bias(rel_bias, g, heads)
    at = lambda dist: bias[np.clip(dist // GROUPS[g][1], 0, N_DIL_KEYS - 1)]
    first = np.arange(parts)[:, None] * rows + np.arange(rows + BLK)[None, :]
    yc = jnp.transpose(at(n_buf + BLK - first), (0, 2, 1))
    yn = at(BLK - np.arange(2 * BLK)).T
    return yc, yn


def _kv_rows(qkv, g, batch, seq, n_rows, heads, dh):
    n_g = len(GROUPS)
    width = heads * dh
    q3 = qkv.reshape(batch, seq, qkv.shape[1])[:, seq - n_rows:]
    k = q3[:, :, (n_g + g) * width:(n_g + g + 1) * width]
    v = q3[:, :, (2 * n_g + g) * width:(2 * n_g + g + 1) * width]
    return jnp.stack([k, v], axis=2).reshape(batch, n_rows, 2, heads, dh)


def _trunk(x3, st, p, tiles):
    batch, seq, d = x3.shape
    m = batch * seq
    x = x3.reshape(m, d)
    new = []
    depth = p['ffn_w1'].shape[0]
    for i in range(depth):
        kind = i % N_MIXERS
        g_mix = p['norm_mix_g'][i][None]
        if kind == 0:
            j = i // N_MIXERS
            n_taps, c = p['conv_dw_w'].shape[1:]
            buf = jnp.zeros((batch, n_taps - 1, c), F32) if st is None else st[i][0]
            u = _norm_proj(x, g_mix, p['conv_w_glu'], 'glu', tiles['proj_m'], layer=j)
            u3 = u.reshape(batch, seq, c)
            tm = _tile(seq, tiles['conv_m'])
            halo = _halos(u3, buf, tm, CONV_HALO)
            dww = jnp.repeat(p['conv_dw_w'][j], SUBLANES, axis=0)
            x = _conv_tail(x, u, halo, dww, p['conv_dw_b'][j][None], p['conv_ln_g'][j][None],
                           p['conv_ln_b'][j][None], p['conv_w_out'], j, tm, n_taps)
            new.append((_last_rows(buf, u3, n_taps - 1),))
        elif kind == 1:
            n_taps, c = p['lru_conv_w'].shape
            if st is None:
                h0 = jnp.zeros((batch, c), F32)
                cb = jnp.zeros((batch, n_taps - 1, c), F32)
            else:
                h0, cb = st[i]
            ybr, xbr = _norm_proj(x, g_mix, p['lru_w_in'], 'split', tiles['proj_m'])
            xbr3 = xbr.reshape(batch, seq, c)
            tm = _tile(seq, tiles['lru_m'])
            halo = _halos(xbr3, cb, tm, LRU_HALO)
            cw = jnp.pad(p['lru_conv_w'], ((0, SUBLANES - n_taps), (0, 0)))
            x, hl = _lru_tail(x, ybr, xbr, halo, h0[:, None], cw, p['lru_conv_b'][None], p['lru_w_a'],
                              p['lru_b_a'][None], p['lru_w_x'], p['lru_b_x'][None], p['lru_lambda'][None],
                              p['lru_w_out'], batch, tm, n_taps)
            new.append((hl[:, 0], _last_rows(cb, xbr3, n_taps - 1)))
        else:
            rel_bias = p['rel_bias']
            heads = rel_bias.shape[1] // len(GROUPS)
            width = p['att_w_o'].shape[0]
            dh = width // heads
            qkv = _norm_proj(x, g_mix, p['att_w_qkv'], 'plain', tiles['proj_m'], tn_pref=1024)
            if st is None:
                assert seq % CHUNK == 0
                o = _attn_prompt(qkv, _prompt_bias_vectors(rel_bias, heads), batch, seq, heads, dh)
                rows = [_kv_rows(qkv, g, batch, seq, min(win, seq), heads, dh) for g, (win, _) in enumerate(GROUPS)]
            else:
                assert seq <= BLK
                o = _attn_step(qkv, st[i], rel_bias, batch, seq, heads, dh)
                rows = [_kv_rows(qkv, g, batch, seq, seq, heads, dh) for g in range(len(GROUPS))]
            x = _attn_out(x, o, p['att_w_o'], tiles['attn_out_m'])
            new.append(rows)
        x = _ffn(x, p['norm_ffn_g'][i][None], p['ffn_w1'], p['ffn_w2'], i, p['final_norm_g'][None],
                 final=(i == depth - 1), tm_pref=tiles['ffn_m'])
    return x.reshape(batch, seq, d), new


PROMPT_TILES = dict(proj_m=1024, conv_m=256, lru_m=256, attn_out_m=512, ffn_m=512)
SAMPLE_TILES = dict(proj_m=512, conv_m=256, lru_m=256, attn_out_m=512, ffn_m=512)
MATMUL_WEIGHTS = ('ffn_w1', 'ffn_w2', 'conv_w_glu', 'conv_w_out', 'lru_w_in', 'lru_w_a', 'lru_w_x',
                  'lru_w_out', 'att_w_qkv', 'att_w_o')


def kernel(x_prompt, x_sample, state_conv_l0, state_lru_h_l1, state_lru_conv_l1, cache_kv_w128_l2, cache_kv_w512_l2, cache_kv_w2048_l2, state_conv_l3, norm_mix_g, norm_ffn_g, final_norm_g, ffn_w1, ffn_w2, conv_w_glu, conv_dw_w, conv_dw_b, conv_ln_g, conv_ln_b, conv_w_out, lru_w_in, lru_conv_w, lru_conv_b, lru_w_a, lru_b_a, lru_w_x, lru_b_x, lru_lambda, lru_w_out, att_w_qkv, att_w_o, rel_bias):
    p = dict(norm_mix_g=norm_mix_g, norm_ffn_g=norm_ffn_g, final_norm_g=final_norm_g, ffn_w1=ffn_w1,
             ffn_w2=ffn_w2, conv_w_glu=conv_w_glu, conv_dw_w=conv_dw_w, conv_dw_b=conv_dw_b,
             conv_ln_g=conv_ln_g, conv_ln_b=conv_ln_b, conv_w_out=conv_w_out, lru_w_in=lru_w_in,
             lru_conv_w=lru_conv_w, lru_conv_b=lru_conv_b, lru_w_a=lru_w_a, lru_b_a=lru_b_a, lru_w_x=lru_w_x,
             lru_b_x=lru_b_x, lru_lambda=lru_lambda, lru_w_out=lru_w_out, att_w_qkv=att_w_qkv,
             att_w_o=att_w_o, rel_bias=rel_bias)
    for name in MATMUL_WEIGHTS:
        p[name] = p[name].astype(BF16)
    y_prompt, pst = _trunk(x_prompt, None, p, PROMPT_TILES)
    y_sample, sst = _trunk(x_sample, [(state_conv_l0,), (state_lru_h_l1, state_lru_conv_l1),
                                      (cache_kv_w128_l2, cache_kv_w512_l2, cache_kv_w2048_l2),
                                      (state_conv_l3,)], p, SAMPLE_TILES)
    return (y_prompt, y_sample,
            pst[0][0], sst[0][0],
            pst[1][0], sst[1][0],
            pst[1][1], sst[1][1],
            pst[2][0], sst[2][0],
            pst[2][1], sst[2][1],
            pst[2][2], sst[2][2],
            pst[3][0], sst[3][0])
```

```python
import functools

import numpy as np
import jax
import jax.numpy as jnp
from jax import lax
from jax.experimental import pallas as pl
from jax.experimental.pallas import tpu as pltpu

RMS_EPS = 1e-6
LN_EPS = 1e-5
LRU_C = 8.0
NEG_INF = -1e30
N_MIXERS = 3
GROUPS = ((128, 1), (512, 4), (2048, 16))
N_DIL_KEYS = 129
BLK = 128
CHUNK = BLK * max(d for _, d in GROUPS)
N_BUCKETS = 32
MAX_DISTANCE = 2048

V7X_VMEM_BYTES = 64 * 1024 * 1024
SUBLANES = 8
CONV_HALO = 32
LRU_HALO = 8

BF16 = jnp.bfloat16
F32 = jnp.float32


def _cparams(semantics, vmem_bytes):
    limit = int(min(max(vmem_bytes * 3 // 2, 32 * 1024 * 1024), V7X_VMEM_BYTES - 8 * 1024 * 1024))
    return pltpu.CompilerParams(dimension_semantics=semantics, vmem_limit_bytes=limit)


def _tile(n, pref):
    if n <= pref:
        return n
    t = pref
    while n % t:
        t //= 2
    return t


def _rms_to_bf16(x, g):
    ms = jnp.mean(x * x, axis=-1, keepdims=True)
    return (x * lax.rsqrt(ms + RMS_EPS) * g).astype(BF16)


def _dot(a, b):
    return jnp.dot(a, b, preferred_element_type=F32)


def _nt_dot(a, b):
    return lax.dot_general(a, b, (((1,), (1,)), ((), ())), preferred_element_type=F32)


def _norm_proj_kernel(x_ref, g_ref, *refs, mode):
    if mode == 'plain':
        w_ref, o_ref, h_ref = refs
    elif mode == 'glu':
        wa_ref, wg_ref, o_ref, h_ref = refs
    else:
        wa_ref, wg_ref, o_ref, o2_ref, h_ref = refs

    @pl.when(pl.program_id(1) == 0)
    def _():
        h_ref[...] = _rms_to_bf16(x_ref[...], g_ref[...])

    h = h_ref[...]
    if mode == 'plain':
        o_ref[...] = _dot(h, w_ref[...])
    elif mode == 'glu':
        o_ref[...] = _dot(h, wa_ref[...]) * jax.nn.sigmoid(_dot(h, wg_ref[...]))
    else:
        o_ref[...] = jax.nn.gelu(_dot(h, wa_ref[...]), approximate=True)
        o2_ref[...] = _dot(h, wg_ref[...])


def _wspec(w, layer, block, index_map):
    if w.ndim == 2:
        return pl.BlockSpec(block, index_map)
    return pl.BlockSpec((None,) + block, lambda *ids: (layer,) + index_map(*ids))


def _norm_proj(x, g, w, mode, tm_pref=512, tn_pref=512, layer=0):
    m, d = x.shape
    n = w.shape[-1]
    n_out = n if mode == 'plain' else n // 2
    tm = _tile(m, tm_pref)
    tn = _tile(n_out, tn_pref)
    nj = n_out // tn
    in_specs = [pl.BlockSpec((tm, d), lambda i, j: (i, 0)),
                pl.BlockSpec((1, d), lambda i, j: (0, 0)),
                _wspec(w, layer, (d, tn), lambda i, j: (0, j))]
    args = [x, g, w]
    if mode != 'plain':
        in_specs.append(_wspec(w, layer, (d, tn), lambda i, j: (0, j + nj)))
        args.append(w)
    o_spec = pl.BlockSpec((tm, tn), lambda i, j: (i, j))
    o_shape = jax.ShapeDtypeStruct((m, n_out), F32)
    n_o = 2 if mode == 'split' else 1
    vmem = 2 * tm * d * 4 + tm * d * 2 + 2 * len(args[2:]) * d * tn * 2 + 2 * n_o * tm * tn * 4
    return pl.pallas_call(
        functools.partial(_norm_proj_kernel, mode=mode),
        name=f"norm_proj_{mode}_m{m}",
        out_shape=[o_shape] * n_o if n_o == 2 else o_shape,
        grid=(m // tm, nj),
        in_specs=in_specs,
        out_specs=[o_spec] * n_o if n_o == 2 else o_spec,
        scratch_shapes=[pltpu.VMEM((tm, d), BF16)],
        compiler_params=_cparams(("parallel", "arbitrary"), vmem),
    )(*args)


def _ffn_kernel(x_ref, g_ref, w1_ref, w2_ref, fg_ref, o_ref, *rest, final, emit):
    if emit:
        w1b_ref, w2b_ref, h_ref = rest
    else:
        (h_ref,) = rest
    f = pl.program_id(1)

    @pl.when(f == 0)
    def _():
        x = x_ref[...]
        h_ref[...] = _rms_to_bf16(x, g_ref[...])
        o_ref[...] = x

    w1 = w1_ref[...].astype(BF16)
    w2 = w2_ref[...].astype(BF16)
    if emit:
        w1b_ref[...] = w1
        w2b_ref[...] = w2
    hid = jnp.square(jnp.maximum(_dot(h_ref[...], w1), 0.0)).astype(BF16)
    o_ref[...] += _dot(hid, w2)

    if final:
        @pl.when(f == pl.num_programs(1) - 1)
        def _():
            y = o_ref[...]
            ms = jnp.mean(y * y, axis=-1, keepdims=True)
            o_ref[...] = y * lax.rsqrt(ms + RMS_EPS) * fg_ref[...]


def _ffn(x, g, w1, w2, layer, final_g, final, tm_pref=512, tf_pref=1024, emit=False):
    m, d = x.shape
    dff = w1.shape[-1]
    tm = _tile(m, tm_pref)
    tf = _tile(dff, tf_pref)
    wbytes = w1.dtype.itemsize
    vmem = (2 * tm * d * 4 + tm * d * 2 + 2 * 2 * d * tf * wbytes + 2 * tm * d * 4 + 2 * tm * tf * 4
            + (6 * d * tf * 2 if emit else 0))
    o_shape = jax.ShapeDtypeStruct((m, d), F32)
    o_spec = pl.BlockSpec((tm, d), lambda i, f: (i, 0))
    if emit:
        assert m == tm, "each weight block must be visited once"
        o_shape = [o_shape, jax.ShapeDtypeStruct((d, dff), BF16), jax.ShapeDtypeStruct((dff, d), BF16)]
        o_spec = [o_spec, pl.BlockSpec((d, tf), lambda i, f: (0, f)), pl.BlockSpec((tf, d), lambda i, f: (f, 0))]
    return pl.pallas_call(
        functools.partial(_ffn_kernel, final=final, emit=emit),
        name=f"ffn_m{m}_l{layer}",
        out_shape=o_shape,
        grid=(m // tm, dff // tf),
        in_specs=[pl.BlockSpec((tm, d), lambda i, f: (i, 0)),
                  pl.BlockSpec((1, d), lambda i, f: (0, 0)),
                  _wspec(w1, layer, (d, tf), lambda i, f: (0, f)),
                  _wspec(w2, layer, (tf, d), lambda i, f: (f, 0)),
                  pl.BlockSpec((1, d), lambda i, f: (0, 0))],
        out_specs=o_spec,
        scratch_shapes=[pltpu.VMEM((tm, d), BF16)],
        compiler_params=_cparams(("parallel", "arbitrary"), vmem),
    )(x, g, w1, w2, final_g)


def _conv_tail_kernel(x_ref, u_ref, halo_ref, dww_ref, dwb_ref, lng_ref, lnb_ref, wout_ref,
                      o_ref, uext_ref, y_ref, *, n_taps, rows, ln_rows, lanes):
    tm, c = u_ref.shape
    uext_ref[0:CONV_HALO, :] = halo_ref[0]
    uext_ref[CONV_HALO:CONV_HALO + tm, :] = u_ref[...]
    off = CONV_HALO - (n_taps - 1)

    def row_chunk(r, carry):
        r0 = pl.multiple_of(r * rows, rows)
        for l0 in range(0, c, lanes):
            win = uext_ref[pl.ds(r0, rows + CONV_HALO), l0:l0 + lanes]
            accs = [jnp.zeros((SUBLANES, lanes), F32)] * (rows // SUBLANES)
            for s in range(SUBLANES):
                ws = win if s == 0 else pltpu.roll(win, rows + CONV_HALO - s, axis=0)
                for a in range((CONV_HALO + SUBLANES) // SUBLANES):
                    k = a * SUBLANES + s - off
                    if 0 <= k < n_taps:
                        wk = dww_ref[k * SUBLANES:(k + 1) * SUBLANES, l0:l0 + lanes]
                        accs = [acc + wk * ws[(a + j) * SUBLANES:(a + j + 1) * SUBLANES]
                                for j, acc in enumerate(accs)]
            for j, acc in enumerate(accs):
                uext_ref[pl.ds(r0 + j * SUBLANES, SUBLANES), l0:l0 + lanes] = acc + dwb_ref[:, l0:l0 + lanes]
        return carry

    lax.fori_loop(0, tm // rows, row_chunk, 0)

    def ln_chunk(r, carry):
        r0 = pl.multiple_of(r * ln_rows, ln_rows)
        cv = uext_ref[pl.ds(r0, ln_rows), :]
        mu = jnp.mean(cv, axis=-1, keepdims=True)
        cc = cv - mu
        var = jnp.mean(cc * cc, axis=-1, keepdims=True)
        y = cc * lax.rsqrt(var + LN_EPS) * lng_ref[...] + lnb_ref[...]
        y_ref[pl.ds(r0, ln_rows), :] = (y * jax.nn.sigmoid(y)).astype(BF16)
        return carry

    lax.fori_loop(0, tm // ln_rows, ln_chunk, 0, unroll=min(4, tm // ln_rows))
    o_ref[...] = x_ref[...] + _dot(y_ref[...], wout_ref[...])


def _conv_tail(x, u, halo, dww, dwb, lng, lnb, wout, layer, tm, n_taps):
    m, d = x.shape
    c = u.shape[1]
    rows = _tile(tm, 32)
    ln_rows = _tile(tm, 16)
    lanes = _tile(c, 256)
    vmem = 4 * tm * d * 4 + 2 * tm * c * 4 + 2 * c * d * 2 + (tm + CONV_HALO) * c * 4 + tm * c * 2
    return pl.pallas_call(
        functools.partial(_conv_tail_kernel, n_taps=n_taps, rows=rows, ln_rows=ln_rows, lanes=lanes),
        name=f"conv_tail_m{m}",
        out_shape=jax.ShapeDtypeStruct((m, d), F32),
        grid=(m // tm,),
        in_specs=[pl.BlockSpec((tm, d), lambda i: (i, 0)),
                  pl.BlockSpec((tm, c), lambda i: (i, 0)),
                  pl.BlockSpec((1, CONV_HALO, c), lambda i: (i, 0, 0)),
                  pl.BlockSpec(dww.shape, lambda i: (0, 0)),
                  pl.BlockSpec((1, c), lambda i: (0, 0)),
                  pl.BlockSpec((1, c), lambda i: (0, 0)),
                  pl.BlockSpec((1, c), lambda i: (0, 0)),
                  _wspec(wout, layer, (c, d), lambda i: (0, 0))],
        out_specs=pl.BlockSpec((tm, d), lambda i: (i, 0)),
        scratch_shapes=[pltpu.VMEM((tm + CONV_HALO, c), F32), pltpu.VMEM((tm, c), BF16)],
        compiler_params=_cparams(("parallel",), vmem),
    )(x, u, halo, dww, dwb, lng, lnb, wout)


def _last_rows(past, seq, n):
    if seq.shape[1] >= n:
        return seq[:, seq.shape[1] - n:]
    return jnp.concatenate([past, seq], axis=1)[:, -n:]


def _halos(seq, past, tm, halo_rows):
    b, l, c = seq.shape
    p = past.shape[1]
    first = jnp.concatenate([jnp.zeros((b, halo_rows - p, c), seq.dtype), past], axis=1)[:, None]
    nt = l // tm
    if nt == 1:
        return first.reshape(b, halo_rows, c)
    rest = seq.reshape(b, nt, tm, c)[:, :-1, tm - halo_rows:]
    return jnp.concatenate([first, rest], axis=1).reshape(b * nt, halo_rows, c)


def _lru_tail_kernel(x_ref, ybr_ref, xbr_ref, halo_ref, h0_ref, cw_ref, cb_ref, wa_ref, ba_ref,
                     wx_ref, bx_ref, lam_ref, wout_ref, o_ref, hlast_ref,
                     xext_ref, a_ref, b_ref, hc_ref, *, n_taps, lanes):
    tm, c = xbr_ref.shape
    n_blocks, blk, _ = wa_ref.shape
    t = pl.program_id(1)

    @pl.when(t == 0)
    def _():
        hc_ref[...] = jnp.broadcast_to(h0_ref[0], hc_ref.shape)

    xext_ref[0:LRU_HALO, :] = halo_ref[0]
    xext_ref[LRU_HALO:LRU_HALO + tm, :] = xbr_ref[...]
    off = LRU_HALO - (n_taps - 1)

    sp = jax.nn.softplus(-lam_ref[...])
    for n in range(n_blocks):
        sl = slice(n * blk, (n + 1) * blk)
        xc = jnp.zeros((tm, blk), F32)
        for k in range(n_taps):
            xc = xc + cw_ref[k:k + 1, sl] * xext_ref[off + k:off + k + tm, sl]
        xc = xc + cb_ref[:, sl]
        xcb = xc.astype(BF16)
        r = jax.nn.sigmoid(_dot(xcb, wa_ref[n]) + ba_ref[:, sl])
        gi = jax.nn.sigmoid(_dot(xcb, wx_ref[n]) + bx_ref[:, sl])
        log_a = -LRU_C * r * sp[:, sl]
        a = jnp.exp(log_a)
        a_ref[:, sl] = a
        b_ref[:, sl] = jnp.sqrt(-jnp.tanh(log_a) * (a * a + 1.0)) * (gi * xc)

    row = lax.broadcasted_iota(jnp.int32, (SUBLANES, lanes), 0)

    for l0 in range(0, c, lanes):
        def group(gidx, h):
            r0 = pl.multiple_of(gidx * SUBLANES, SUBLANES)
            av = a_ref[pl.ds(r0, SUBLANES), l0:l0 + lanes]
            bv = b_ref[pl.ds(r0, SUBLANES), l0:l0 + lanes]
            for s in (1, 2, 4):
                a_sh = jnp.where(row >= s, pltpu.roll(av, s, axis=0), 1.0)
                b_sh = jnp.where(row >= s, pltpu.roll(bv, s, axis=0), 0.0)
                bv = av * b_sh + bv
                av = av * a_sh
            hs = av * h + bv
            b_ref[pl.ds(r0, SUBLANES), l0:l0 + lanes] = hs
            return jnp.broadcast_to(hs[SUBLANES - 1:SUBLANES, :], (SUBLANES, lanes))

        h_fin = lax.fori_loop(0, tm // SUBLANES, group, hc_ref[:, l0:l0 + lanes], unroll=2)
        hc_ref[:, l0:l0 + lanes] = h_fin

    gated = (b_ref[...] * ybr_ref[...]).astype(BF16)
    o_ref[...] = x_ref[...] + _dot(gated, wout_ref[...])

    @pl.when(t == pl.num_programs(1) - 1)
    def _():
        hlast_ref[0] = hc_ref[...]


def _lru_tail(x, ybr, xbr, halo, h0, cw, cb, wa, ba, wx, bx, lam, wout, batch, tm, n_taps):
    m, d = x.shape
    c = xbr.shape[1]
    nt = m // batch // tm
    lanes = _tile(c, 512)
    row2 = lambda b, t: (b * nt + t, 0)
    const2 = lambda b, t: (0, 0)
    vec = pl.BlockSpec((1, c), const2)
    vmem = (4 * tm * d * 4 + 4 * tm * c * 4 + 2 * c * d * 2 + 4 * wa.size * 2
            + (tm + LRU_HALO) * c * 4 + 2 * tm * c * 4)
    return pl.pallas_call(
        functools.partial(_lru_tail_kernel, n_taps=n_taps, lanes=lanes),
        name=f"lru_tail_m{m}",
        out_shape=[jax.ShapeDtypeStruct((m, d), F32),
                   jax.ShapeDtypeStruct((batch, SUBLANES, c), F32)],
        grid=(batch, nt),
        in_specs=[pl.BlockSpec((tm, d), row2),
                  pl.BlockSpec((tm, c), row2),
                  pl.BlockSpec((tm, c), row2),
                  pl.BlockSpec((1, LRU_HALO, c), lambda b, t: (b * nt + t, 0, 0)),
                  pl.BlockSpec((1, 1, c), lambda b, t: (b, 0, 0)),
                  pl.BlockSpec(cw.shape, const2),
                  vec,
                  pl.BlockSpec(wa.shape, lambda b, t: (0, 0, 0)),
                  vec,
                  pl.BlockSpec(wx.shape, lambda b, t: (0, 0, 0)),
                  vec, vec,
                  pl.BlockSpec((c, d), const2)],
        out_specs=[pl.BlockSpec((tm, d), row2),
                   pl.BlockSpec((1, SUBLANES, c), lambda b, t: (b, 0, 0))],
        scratch_shapes=[pltpu.VMEM((tm + LRU_HALO, c), F32), pltpu.VMEM((tm, c), F32),
                        pltpu.VMEM((tm, c), F32), pltpu.VMEM((SUBLANES, c), F32)],
        compiler_params=_cparams(("parallel", "arbitrary"), vmem),
    )(x, ybr, xbr, halo, h0, cw, cb, wa, ba, wx, bx, lam, wout)


def _shifted_rows(vec, n_rows):
    return pltpu.roll(jnp.broadcast_to(vec, (n_rows, vec.shape[1])), 0, 1, stride=1, stride_axis=0)


def _block_softmax(s_parts, v_parts):
    m = s_parts[0].max(axis=-1, keepdims=True)
    for s in s_parts[1:]:
        m = jnp.maximum(m, s.max(axis=-1, keepdims=True))
    o, den = None, None
    for s, v in zip(s_parts, v_parts):
        p = jnp.exp(s - m)
        d = p.sum(axis=-1, keepdims=True)
        pv = _dot(p.astype(BF16), v)
        o, den = (pv, d) if o is None else (o + pv, den + d)
    return o, m, den


def _block_softmax_mxu_sum(s_parts, v_parts):
    m = s_parts[0].max(axis=-1, keepdims=True)
    for s in s_parts[1:]:
        m = jnp.maximum(m, s.max(axis=-1, keepdims=True))
    acc = None
    for s, v in zip(s_parts, v_parts):
        pv = _dot(jnp.exp(s - m).astype(BF16), v)
        acc = pv if acc is None else acc + pv
    dh = acc.shape[1] // 2
    return acc[:, :dh], m, acc[:, dh:]


def _merge(state, o_b, m_b, l_b):
    if state is None:
        return o_b, m_b, l_b
    acc, m_old, l_old = state
    m_new = jnp.maximum(m_old, m_b)
    a_old = jnp.exp(m_old - m_new)
    a_b = jnp.exp(m_b - m_new)
    return acc * a_old + o_b * a_b, m_new, l_old * a_old + l_b * a_b


def _attn_prompt_kernel(*refs, heads, dh):
    n_g = len(GROUPS)
    ins = refs[:5 * n_g]
    bvec_ref, o_ref, acc_ref, m_ref, l_ref = refs[5 * n_g:]
    first_chunk = pl.program_id(1) == 0
    h = pl.program_id(2)
    scale = dh ** -0.5
    qi = lax.broadcasted_iota(jnp.int32, (BLK, 2 * BLK), 0)
    kj = lax.broadcasted_iota(jnp.int32, (BLK, 2 * BLK), 1)
    in_band = (kj - qi >= 0) & (kj - qi <= BLK)
    ones = jnp.ones((BLK, dh), BF16)
    for g, (_, dil) in enumerate(GROUPS):
        q_ref, k_ref, v_ref, kp_ref, vp_ref = ins[5 * g:5 * g + 5]
        bias = jnp.where(in_band, _shifted_rows(bvec_ref[pl.ds(g * heads + h, 1), :], BLK), NEG_INF)
        bias_p, bias_c = bias[:, :BLK], bias[:, BLK:]
        for r in range(dil):
            prev_rows = pl.ds(r, BLK, stride=dil)
            k_prev = kp_ref[prev_rows, :].astype(BF16)
            v_prev = jnp.concatenate([vp_ref[prev_rows, :].astype(BF16), ones], axis=1)
            for nb in range(CHUNK // (dil * BLK)):
                rows = pl.ds(r + dil * BLK * nb, BLK, stride=dil)
                q = q_ref[rows, :].astype(BF16)
                k_cur = k_ref[rows, :].astype(BF16)
                v_cur = jnp.concatenate([v_ref[rows, :].astype(BF16), ones], axis=1)
                s_p = _nt_dot(q, k_prev) * scale + bias_p
                if nb == 0:
                    s_p = jnp.where(first_chunk, NEG_INF, s_p)
                s_c = _nt_dot(q, k_cur) * scale + bias_c
                o_b, m_b, l_b = _block_softmax_mxu_sum([s_p, s_c], [v_prev, v_cur])
                m_b = jnp.broadcast_to(m_b, (BLK, dh))
                state = None if g == 0 else (acc_ref[rows, :], m_ref[rows, :], l_ref[rows, :])
                acc, m_new, l_new = _merge(state, o_b, m_b, l_b)
                if g == n_g - 1:
                    o_ref[rows, :] = acc / l_new
                else:
                    acc_ref[rows, :] = acc
                    m_ref[rows, :] = m_new
                    l_ref[rows, :] = l_new
                k_prev, v_prev = k_cur, v_cur


def _attn_prompt(qkv, bvec, batch, seq, heads, dh):
    n_g = len(GROUPS)
    nc = seq // CHUNK
    col = lambda comp, g: (lambda b, n, h: (b * nc + n, (comp * n_g + g) * heads + h))
    in_specs, args = [], []
    for g, (win, _) in enumerate(GROUPS):
        per = CHUNK // win
        prev = lambda comp, g=g, per=per: (
            lambda b, n, h: (jnp.maximum((b * nc + n) * per - 1, 0), (comp * n_g + g) * heads + h))
        in_specs += [pl.BlockSpec((CHUNK, dh), col(0, g)), pl.BlockSpec((CHUNK, dh), col(1, g)),
                     pl.BlockSpec((CHUNK, dh), col(2, g)),
                     pl.BlockSpec((win, dh), prev(1)), pl.BlockSpec((win, dh), prev(2))]
        args += [qkv] * 5
    in_specs.append(pl.BlockSpec(bvec.shape, lambda b, n, h: (0, 0)))
    vmem = 2 * (9 * CHUNK + 2 * sum(w for w, _ in GROUPS)) * dh * 4 + 2 * CHUNK * dh * 4 + 3 * CHUNK * dh * 4
    return pl.pallas_call(
        functools.partial(_attn_prompt_kernel, heads=heads, dh=dh),
        name="attn_prompt",
        out_shape=jax.ShapeDtypeStruct((batch * seq, heads * dh), F32),
        grid=(batch, nc, heads),
        in_specs=in_specs,
        out_specs=pl.BlockSpec((CHUNK, dh), lambda b, n, h: (b * nc + n, h)),
        scratch_shapes=[pltpu.VMEM((CHUNK, dh), F32)] * 3,
        compiler_params=_cparams(("parallel", "parallel", "parallel"), vmem),
    )(*args, bvec)


def _attn_step_kernel(*refs, heads, dh, n_bufs):
    n_g = len(GROUPS)
    ins = refs[:6 * n_g]
    o_ref, acc_ref, m_ref, l_ref, knp_ref, vnp_ref = refs[6 * n_g:]
    part = pl.program_id(1)
    t = o_ref.shape[1]
    scale = dh ** -0.5
    row_stride = 2 * heads

    def merge_into(hs, o_b, m_b, l_b):
        state = (acc_ref[:, hs], m_ref[:, hs], l_ref[:, hs])
        acc, m_new, l_new = _merge(state, o_b, jnp.broadcast_to(m_b, (t, dh)), jnp.broadcast_to(l_b, (t, dh)))
        acc_ref[:, hs] = acc
        m_ref[:, hs] = m_new
        l_ref[:, hs] = l_new

    def cache_piece(g, first_row):
        dil = GROUPS[g][1]
        q_ref, _, _, cache_ref, yc_ref, _ = ins[6 * g:6 * g + 6]
        rows = cache_ref.shape[1] // row_stride
        tc = lax.broadcasted_iota(jnp.int32, (t, rows), 0)
        cc = lax.broadcasted_iota(jnp.int32, (t, rows), 1) + first_row
        dist_c = n_bufs[g] + tc - cc
        ok_c = ((dist_c & (dil - 1)) == 0) & (dist_c <= (N_DIL_KEYS - 1) * dil)
        for h in range(heads):
            hs = slice(h * dh, (h + 1) * dh)
            q = q_ref[0, :, hs].astype(BF16)
            kc = cache_ref[0, pl.ds(h, rows, stride=row_stride), :].astype(BF16)
            vc = cache_ref[0, pl.ds(heads + h, rows, stride=row_stride), :].astype(BF16)
            bias = _shifted_rows(yc_ref[0, h:h + 1, :], t)[:, BLK:]
            s_c = jnp.where(ok_c, _nt_dot(q, kc) * scale + bias, NEG_INF)
            merge_into(hs, *_block_softmax([s_c], [vc]))

    @pl.when(part == 0)
    def _():
        acc_ref[...] = jnp.zeros_like(acc_ref)
        l_ref[...] = jnp.zeros_like(l_ref)
        m_ref[...] = jnp.full_like(m_ref, NEG_INF)
        tn = lax.broadcasted_iota(jnp.int32, (t, BLK), 0)
        cn = lax.broadcasted_iota(jnp.int32, (t, BLK), 1)
        dist_n = tn - cn
        for g, (_, dil) in enumerate(GROUPS):
            q_ref, kn_ref, vn_ref, _, _, yn_ref = ins[6 * g:6 * g + 6]
            ok_n = (dist_n >= 0) & ((dist_n & (dil - 1)) == 0) & (dist_n <= (N_DIL_KEYS - 1) * dil)
            knp_ref[...] = jnp.zeros_like(knp_ref)
            vnp_ref[...] = jnp.zeros_like(vnp_ref)
            knp_ref[0:t, :] = kn_ref[0]
            vnp_ref[0:t, :] = vn_ref[0]
            for h in range(heads):
                hs = slice(h * dh, (h + 1) * dh)
                q = q_ref[0, :, hs].astype(BF16)
                bias = _shifted_rows(yn_ref[h:h + 1, :], t)[:, BLK:]
                s_n = jnp.where(ok_n, _nt_dot(q, knp_ref[:, hs].astype(BF16)) * scale + bias, NEG_INF)
                merge_into(hs, *_block_softmax([s_n], [vnp_ref[:, hs].astype(BF16)]))
        for g in range(n_g):
            if ins[6 * g + 3].shape[1] == n_bufs[g] * row_stride:
                cache_piece(g, 0)

    for g in range(n_g):
        rows = ins[6 * g + 3].shape[1] // row_stride
        if rows != n_bufs[g]:
            cache_piece(g, part * rows)

    @pl.when(part == pl.num_programs(1) - 1)
    def _():
        o_ref[0] = acc_ref[...] / l_ref[...]


def _attn_step(qkv, caches, rel_bias, batch, t, heads, dh, n_parts=2):
    n_g = len(GROUPS)
    width = heads * dh
    qkv3 = qkv.reshape(batch, t, qkv.shape[1])
    in_specs, args, n_bufs = [], [], []
    vmem = 8 * t * width * 4
    for g in range(n_g):
        n_buf = caches[g].shape[1]
        parts = n_parts if n_buf % (n_parts * BLK) == 0 else 1
        rows = n_buf // parts
        flat = caches[g].reshape(batch, n_buf * 2 * heads, dh)
        yc, yn = _step_bias_vectors(rel_bias, g, heads, n_buf, parts)
        new = lambda comp, g=g: pl.BlockSpec((1, t, width), lambda b, s: (b, 0, comp * n_g + g))
        part_of = (lambda s: s) if parts == n_parts else (lambda s: 0)
        in_specs += [new(0), new(1), new(2),
                     pl.BlockSpec((1, rows * 2 * heads, dh), lambda b, s, f=part_of: (b, f(s), 0)),
                     pl.BlockSpec((1, heads, rows + BLK), lambda b, s, f=part_of: (f(s), 0, 0)),
                     pl.BlockSpec(yn.shape, lambda b, s: (0, 0))]
        args += [qkv3, qkv3, qkv3, flat, yc, yn]
        n_bufs.append(n_buf)
        vmem += 2 * rows * 2 * width * 4 + 6 * rows * dh * 4
    return pl.pallas_call(
        functools.partial(_attn_step_kernel, heads=heads, dh=dh, n_bufs=tuple(n_bufs)),
        name="attn_step",
        out_shape=jax.ShapeDtypeStruct((batch, t, width), F32),
        grid=(batch, n_parts),
        in_specs=in_specs,
        out_specs=pl.BlockSpec((1, t, width), lambda b, s: (b, 0, 0)),
        scratch_shapes=[pltpu.VMEM((t, width), F32)] * 3 + [pltpu.VMEM((BLK, width), F32)] * 2,
        compiler_params=_cparams(("parallel", "arbitrary"), vmem),
    )(*args).reshape(batch * t, width)


def _attn_out_kernel(x_ref, o_ref, wo_ref, out_ref):
    out_ref[...] = x_ref[...] + _dot(o_ref[...].astype(BF16), wo_ref[...])


def _attn_out(x, o, wo, tm_pref=512):
    m, d = x.shape
    width = wo.shape[0]
    tm = _tile(m, tm_pref)
    row = lambda n: pl.BlockSpec((tm, n), lambda i: (i, 0))
    vmem = 4 * tm * d * 4 + 2 * tm * width * 4 + 2 * width * d * 2
    return pl.pallas_call(
        _attn_out_kernel,
        name=f"attn_out_m{m}",
        out_shape=jax.ShapeDtypeStruct((m, d), F32),
        grid=(m // tm,),
        in_specs=[row(d), row(width), pl.BlockSpec((width, d), lambda i: (0, 0))],
        out_specs=row(d),
        compiler_params=_cparams(("parallel",), vmem),
    )(x, o, wo)


def _t5_bucket(dist):
    max_exact = N_BUCKETS // 2
    d = np.asarray(dist, dtype=np.int32)
    large = max_exact + (np.log(np.maximum(d, max_exact) / max_exact) / np.log(MAX_DISTANCE / max_exact)
                         * (N_BUCKETS - max_exact)).astype(np.int32)
    return np.where(d < max_exact, d, np.minimum(large, N_BUCKETS - 1)).astype(np.int32)


def _group_bias(rel_bias, g, heads):
    buckets = _t5_bucket(np.arange(N_DIL_KEYS) * GROUPS[g][1])
    return rel_bias[:, g * heads:(g + 1) * heads].astype(F32)[buckets]


def _prompt_bias_vectors(rel_bias, heads):
    vecs = []
    for g in range(len(GROUPS)):
        rev = _group_bias(rel_bias, g, heads)[::-1]
        vecs.append(jnp.pad(rev, ((0, 2 * BLK - N_DIL_KEYS), (0, 0))).T)
    return jnp.concatenate(vecs, axis=0)


def _step_bias_vectors(rel_bias, g, heads, n_buf, parts):
    rows = n_buf // parts
    bias = _group_bias(rel_bias, g, heads)
    at = lambda dist: bias[np.clip(dist // GROUPS[g][1], 0, N_DIL_KEYS - 1)]
    first = np.arange(parts)[:, None] * rows + np.arange(rows + BLK)[None, :]
    yc = jnp.transpose(at(n_buf + BLK - first), (0, 2, 1))
    yn = at(BLK - np.arange(2 * BLK)).T
    return yc, yn


def _kv_rows(qkv, g, batch, seq, n_rows, heads, dh):
    n_g = len(GROUPS)
    width = heads * dh
    q3 = qkv.reshape(batch, seq, qkv.shape[1])[:, seq - n_rows:]
    k = q3[:, :, (n_g + g) * width:(n_g + g + 1) * width]
    v = q3[:, :, (2 * n_g + g) * width:(2 * n_g + g + 1) * width]
    return jnp.stack([k, v], axis=2).reshape(batch, n_rows, 2, heads, dh)


def _trunk(x3, st, p, tiles, ffn_bf16, emit_ffn):
    batch, seq, d = x3.shape
    m = batch * seq
    x = x3.reshape(m, d)
    new = []
    depth = p['ffn_w1'].shape[0]
    for i in range(depth):
        kind = i % N_MIXERS
        g_mix = p['norm_mix_g'][i][None]
        if kind == 0:
            j = i // N_MIXERS
            n_taps, c = p['conv_dw_w'].shape[1:]
            buf = jnp.zeros((batch, n_taps - 1, c), F32) if st is None else st[i][0]
            u = _norm_proj(x, g_mix, p['conv_w_glu'], 'glu', tiles['proj_m'], layer=j)
            u3 = u.reshape(batch, seq, c)
            tm = _tile(seq, tiles['conv_m'])
            halo = _halos(u3, buf, tm, CONV_HALO)
            dww = jnp.repeat(p['conv_dw_w'][j], SUBLANES, axis=0)
            x = _conv_tail(x, u, halo, dww, p['conv_dw_b'][j][None], p['conv_ln_g'][j][None],
                           p['conv_ln_b'][j][None], p['conv_w_out'], j, tm, n_taps)
            new.append((_last_rows(buf, u3, n_taps - 1),))
        elif kind == 1:
            n_taps, c = p['lru_conv_w'].shape
            if st is None:
                h0 = jnp.zeros((batch, c), F32)
                cb = jnp.zeros((batch, n_taps - 1, c), F32)
            else:
                h0, cb = st[i]
            ybr, xbr = _norm_proj(x, g_mix, p['lru_w_in'], 'split', tiles['proj_m'])
            xbr3 = xbr.reshape(batch, seq, c)
            tm = _tile(seq, tiles['lru_m'])
            halo = _halos(xbr3, cb, tm, LRU_HALO)
            cw = jnp.pad(p['lru_conv_w'], ((0, SUBLANES - n_taps), (0, 0)))
            x, hl = _lru_tail(x, ybr, xbr, halo, h0[:, None], cw, p['lru_conv_b'][None], p['lru_w_a'],
                              p['lru_b_a'][None], p['lru_w_x'], p['lru_b_x'][None], p['lru_lambda'][None],
                              p['lru_w_out'], batch, tm, n_taps)
            new.append((hl[:, 0], _last_rows(cb, xbr3, n_taps - 1)))
        else:
            rel_bias = p['rel_bias']
            heads = rel_bias.shape[1] // len(GROUPS)
            width = p['att_w_o'].shape[0]
            dh = width // heads
            qkv = _norm_proj(x, g_mix, p['att_w_qkv'], 'plain', tiles['proj_m'], tn_pref=1024)
            if st is None:
                assert seq % CHUNK == 0
                o = _attn_prompt(qkv, _prompt_bias_vectors(rel_bias, heads), batch, seq, heads, dh)
                rows = [_kv_rows(qkv, g, batch, seq, min(win, seq), heads, dh) for g, (win, _) in enumerate(GROUPS)]
            else:
                assert seq <= BLK
                o = _attn_step(qkv, st[i], rel_bias, batch, seq, heads, dh)
                rows = [_kv_rows(qkv, g, batch, seq, seq, heads, dh) for g in range(len(GROUPS))]
            x = _attn_out(x, o, p['att_w_o'], tiles['attn_out_m'])
            new.append(rows)
        g_ffn, g_fin, final = p['norm_ffn_g'][i][None], p['final_norm_g'][None], i == depth - 1
        if emit_ffn:
            x, w1b, w2b = _ffn(x, g_ffn, p['ffn_w1'], p['ffn_w2'], i, g_fin, final, tm_pref=tiles['ffn_m'],
                               tf_pref=tiles['ffn_f'], emit=True)
            ffn_bf16.append((w1b, w2b))
        else:
            x = _ffn(x, g_ffn, *ffn_bf16[i], i, g_fin, final, tm_pref=tiles['ffn_m'], tf_pref=tiles['ffn_f'])
    return x.reshape(batch, seq, d), new


PROMPT_TILES = dict(proj_m=1024, conv_m=256, lru_m=256, attn_out_m=512, ffn_m=512, ffn_f=1024)
SAMPLE_TILES = dict(proj_m=512, conv_m=256, lru_m=256, attn_out_m=512, ffn_m=512, ffn_f=512)
MATMUL_WEIGHTS = ('conv_w_glu', 'conv_w_out', 'lru_w_in', 'lru_w_a', 'lru_w_x', 'lru_w_out', 'att_w_qkv', 'att_w_o')


def kernel(x_prompt, x_sample, state_conv_l0, state_lru_h_l1, state_lru_conv_l1, cache_kv_w128_l2, cache_kv_w512_l2, cache_kv_w2048_l2, state_conv_l3, norm_mix_g, norm_ffn_g, final_norm_g, ffn_w1, ffn_w2, conv_w_glu, conv_dw_w, conv_dw_b, conv_ln_g, conv_ln_b, conv_w_out, lru_w_in, lru_conv_w, lru_conv_b, lru_w_a, lru_b_a, lru_w_x, lru_b_x, lru_lambda, lru_w_out, att_w_qkv, att_w_o, rel_bias):
    p = dict(norm_mix_g=norm_mix_g, norm_ffn_g=norm_ffn_g, final_norm_g=final_norm_g, ffn_w1=ffn_w1,
             ffn_w2=ffn_w2, conv_w_glu=conv_w_glu, conv_dw_w=conv_dw_w, conv_dw_b=conv_dw_b,
             conv_ln_g=conv_ln_g, conv_ln_b=conv_ln_b, conv_w_out=conv_w_out, lru_w_in=lru_w_in,
             lru_conv_w=lru_conv_w, lru_conv_b=lru_conv_b, lru_w_a=lru_w_a, lru_b_a=lru_b_a, lru_w_x=lru_w_x,
             lru_b_x=lru_b_x, lru_lambda=lru_lambda, lru_w_out=lru_w_out, att_w_qkv=att_w_qkv,
             att_w_o=att_w_o, rel_bias=rel_bias)
    for name in MATMUL_WEIGHTS:
        p[name] = p[name].astype(BF16)
    ffn_bf16 = []
    y_sample, sst = _trunk(x_sample, [(state_conv_l0,), (state_lru_h_l1, state_lru_conv_l1),
                                      (cache_kv_w128_l2, cache_kv_w512_l2, cache_kv_w2048_l2),
                                      (state_conv_l3,)], p, SAMPLE_TILES, ffn_bf16, emit_ffn=True)
    y_prompt, pst = _trunk(x_prompt, None, p, PROMPT_TILES, ffn_bf16, emit_ffn=False)
    return (y_prompt, y_sample,
            pst[0][0], sst[0][0],
            pst[1][0], sst[1][0],
            pst[1][1], sst[1][1],
            pst[2][0], sst[2][0],
            pst[2][1], sst[2][1],
            pst[2][2], sst[2][2],
            pst[3][0], sst[3][0])
```

```python
import functools

import numpy as np
import jax
import jax.numpy as jnp
from jax import lax
from jax.experimental import pallas as pl
from jax.experimental.pallas import tpu as pltpu

RMS_EPS = 1e-6
LN_EPS = 1e-5
LRU_C = 8.0
NEG_INF = -1e30
N_MIXERS = 3
GROUPS = ((128, 1), (512, 4), (2048, 16))
N_DIL_KEYS = 129
BLK = 128
CHUNK = BLK * max(d for _, d in GROUPS)
N_BUCKETS = 32
MAX_DISTANCE = 2048

V7X_VMEM_BYTES = 64 * 1024 * 1024
SUBLANES = 8
CONV_HALO = 32
LRU_HALO = 8

BF16 = jnp.bfloat16
F32 = jnp.float32


def _cparams(semantics, vmem_bytes):
    limit = int(min(max(vmem_bytes * 3 // 2, 32 * 1024 * 1024), V7X_VMEM_BYTES - 8 * 1024 * 1024))
    return pltpu.CompilerParams(dimension_semantics=semantics, vmem_limit_bytes=limit)


def _tile(n, pref):
    if n <= pref:
        return n
    t = pref
    while n % t:
        t //= 2
    return t


def _rms_to_bf16(x, g):
    ms = jnp.mean(x * x, axis=-1, keepdims=True)
    return (x * lax.rsqrt(ms + RMS_EPS) * g).astype(BF16)


def _dot(a, b):
    return jnp.dot(a, b, preferred_element_type=F32)


def _nt_dot(a, b):
    return lax.dot_general(a, b, (((1,), (1,)), ((), ())), preferred_element_type=F32)


def _norm_proj_kernel(x_ref, g_ref, *refs, mode):
    if mode == 'plain':
        w_ref, o_ref, h_ref = refs
    elif mode == 'glu':
        wa_ref, wg_ref, o_ref, h_ref = refs
    else:
        wa_ref, wg_ref, o_ref, o2_ref, h_ref = refs

    @pl.when(pl.program_id(1) == 0)
    def _():
        h_ref[...] = _rms_to_bf16(x_ref[...], g_ref[...])

    h = h_ref[...]
    if mode == 'plain':
        o_ref[...] = _dot(h, w_ref[...])
    elif mode == 'glu':
        o_ref[...] = _dot(h, wa_ref[...]) * jax.nn.sigmoid(_dot(h, wg_ref[...]))
    else:
        o_ref[...] = jax.nn.gelu(_dot(h, wa_ref[...]), approximate=True)
        o2_ref[...] = _dot(h, wg_ref[...])


def _wspec(w, layer, block, index_map):
    if w.ndim == 2:
        return pl.BlockSpec(block, index_map)
    return pl.BlockSpec((None,) + block, lambda *ids: (layer,) + index_map(*ids))


def _norm_proj(x, g, w, mode, tm_pref=512, tn_pref=512, layer=0):
    m, d = x.shape
    n = w.shape[-1]
    n_out = n if mode == 'plain' else n // 2
    tm = _tile(m, tm_pref)
    tn = _tile(n_out, tn_pref)
    nj = n_out // tn
    in_specs = [pl.BlockSpec((tm, d), lambda i, j: (i, 0)),
                pl.BlockSpec((1, d), lambda i, j: (0, 0)),
                _wspec(w, layer, (d, tn), lambda i, j: (0, j))]
    args = [x, g, w]
    if mode != 'plain':
        in_specs.append(_wspec(w, layer, (d, tn), lambda i, j: (0, j + nj)))
        args.append(w)
    o_spec = pl.BlockSpec((tm, tn), lambda i, j: (i, j))
    o_shape = jax.ShapeDtypeStruct((m, n_out), F32)
    n_o = 2 if mode == 'split' else 1
    vmem = 2 * tm * d * 4 + tm * d * 2 + 2 * len(args[2:]) * d * tn * 2 + 2 * n_o * tm * tn * 4
    return pl.pallas_call(
        functools.partial(_norm_proj_kernel, mode=mode),
        name=f"norm_proj_{mode}_m{m}",
        out_shape=[o_shape] * n_o if n_o == 2 else o_shape,
        grid=(m // tm, nj),
        in_specs=in_specs,
        out_specs=[o_spec] * n_o if n_o == 2 else o_spec,
        scratch_shapes=[pltpu.VMEM((tm, d), BF16)],
        compiler_params=_cparams(("parallel", "arbitrary"), vmem),
    )(*args)


def _ffn_kernel(x_ref, g_ref, w1_ref, w2_ref, fg_ref, o_ref, *rest, final, emit):
    if emit:
        w1b_ref, w2b_ref, h_ref = rest
    else:
        (h_ref,) = rest
    f = pl.program_id(1)

    @pl.when(f == 0)
    def _():
        x = x_ref[...]
        h_ref[...] = _rms_to_bf16(x, g_ref[...])
        o_ref[...] = x

    w1 = w1_ref[...].astype(BF16)
    w2 = w2_ref[...].astype(BF16)
    if emit:
        w1b_ref[...] = w1
        w2b_ref[...] = w2
    hid = jnp.square(jnp.maximum(_dot(h_ref[...], w1), 0.0)).astype(BF16)
    o_ref[...] += _dot(hid, w2)

    if final:
        @pl.when(f == pl.num_programs(1) - 1)
        def _():
            y = o_ref[...]
            ms = jnp.mean(y * y, axis=-1, keepdims=True)
            o_ref[...] = y * lax.rsqrt(ms + RMS_EPS) * fg_ref[...]


def _ffn(x, g, w1, w2, layer, final_g, final, tm_pref=512, tf_pref=1024, emit=False):
    m, d = x.shape
    dff = w1.shape[-1]
    tm = _tile(m, tm_pref)
    tf = _tile(dff, tf_pref)
    wbytes = w1.dtype.itemsize
    vmem = (2 * tm * d * 4 + tm * d * 2 + 2 * 2 * d * tf * wbytes + 2 * tm * d * 4 + 2 * tm * tf * 4
            + (6 * d * tf * 2 if emit else 0))
    o_shape = jax.ShapeDtypeStruct((m, d), F32)
    o_spec = pl.BlockSpec((tm, d), lambda i, f: (i, 0))
    if emit:
        assert m == tm, "each weight block must be visited once"
        o_shape = [o_shape, jax.ShapeDtypeStruct((d, dff), BF16), jax.ShapeDtypeStruct((dff, d), BF16)]
        o_spec = [o_spec, pl.BlockSpec((d, tf), lambda i, f: (0, f)), pl.BlockSpec((tf, d), lambda i, f: (f, 0))]
    return pl.pallas_call(
        functools.partial(_ffn_kernel, final=final, emit=emit),
        name=f"ffn_m{m}_l{layer}",
        out_shape=o_shape,
        grid=(m // tm, dff // tf),
        in_specs=[pl.BlockSpec((tm, d), lambda i, f: (i, 0)),
                  pl.BlockSpec((1, d), lambda i, f: (0, 0)),
                  _wspec(w1, layer, (d, tf), lambda i, f: (0, f)),
                  _wspec(w2, layer, (tf, d), lambda i, f: (f, 0)),
                  pl.BlockSpec((1, d), lambda i, f: (0, 0))],
        out_specs=o_spec,
        scratch_shapes=[pltpu.VMEM((tm, d), BF16)],
        compiler_params=_cparams(("parallel", "arbitrary"), vmem),
    )(x, g, w1, w2, final_g)


def _conv_tail_kernel(x_ref, u_ref, halo_ref, dww_ref, dwb_ref, lng_ref, lnb_ref, wout_ref,
                      o_ref, uext_ref, y_ref, *, n_taps, rows, ln_rows, lanes):
    i = pl.program_id(0)
    tm, c = u_ref.shape
    cur = lax.rem(i, 2)

    @pl.when(i == 0)
    def _():
        y_ref[1] = jnp.zeros((tm, c), BF16)

    uext_ref[0:CONV_HALO, :] = halo_ref[0]
    uext_ref[CONV_HALO:CONV_HALO + tm, :] = u_ref[...]
    off = CONV_HALO - (n_taps - 1)

    def row_chunk(r, carry):
        r0 = pl.multiple_of(r * rows, rows)
        for l0 in range(0, c, lanes):
            win = uext_ref[pl.ds(r0, rows + CONV_HALO), l0:l0 + lanes]
            accs = [jnp.zeros((SUBLANES, lanes), F32)] * (rows // SUBLANES)
            for s in range(SUBLANES):
                ws = win if s == 0 else pltpu.roll(win, rows + CONV_HALO - s, axis=0)
                for a in range((CONV_HALO + SUBLANES) // SUBLANES):
                    k = a * SUBLANES + s - off
                    if 0 <= k < n_taps:
                        wk = dww_ref[k * SUBLANES:(k + 1) * SUBLANES, l0:l0 + lanes]
                        accs = [acc + wk * ws[(a + j) * SUBLANES:(a + j + 1) * SUBLANES]
                                for j, acc in enumerate(accs)]
            for j, acc in enumerate(accs):
                uext_ref[pl.ds(r0 + j * SUBLANES, SUBLANES), l0:l0 + lanes] = acc + dwb_ref[:, l0:l0 + lanes]
        return carry

    @pl.when(i < pl.num_programs(0) - 1)
    def _():
        lax.fori_loop(0, tm // rows, row_chunk, 0)

    o_ref[...] = x_ref[...] + _dot(y_ref[1 - cur], wout_ref[...])
    for r0 in range(0, tm, ln_rows):
        cv = uext_ref[r0:r0 + ln_rows, :]
        mu = jnp.mean(cv, axis=-1, keepdims=True)
        cc = cv - mu
        var = jnp.mean(cc * cc, axis=-1, keepdims=True)
        y = cc * lax.rsqrt(var + LN_EPS) * lng_ref[...] + lnb_ref[...]
        y_ref[cur, r0:r0 + ln_rows, :] = (y * jax.nn.sigmoid(y)).astype(BF16)


def _conv_tail(x, u, halo, dww, dwb, lng, lnb, wout, layer, tm, n_taps):
    m, d = x.shape
    c = u.shape[1]
    rows = _tile(tm, 32)
    ln_rows = _tile(tm, 16)
    lanes = _tile(c, 256)
    nt = m // tm
    prev = lambda i: (jnp.maximum(i - 1, 0), 0)
    this = lambda i: (jnp.minimum(i, nt - 1), 0)
    vmem = 4 * tm * d * 4 + 2 * tm * c * 4 + 2 * c * d * 2 + (tm + CONV_HALO) * c * 4 + 2 * tm * c * 2
    return pl.pallas_call(
        functools.partial(_conv_tail_kernel, n_taps=n_taps, rows=rows, ln_rows=ln_rows, lanes=lanes),
        name=f"conv_tail_m{m}",
        out_shape=jax.ShapeDtypeStruct((m, d), F32),
        grid=(nt + 1,),
        in_specs=[pl.BlockSpec((tm, d), prev),
                  pl.BlockSpec((tm, c), this),
                  pl.BlockSpec((1, CONV_HALO, c), lambda i: (jnp.minimum(i, nt - 1), 0, 0)),
                  pl.BlockSpec(dww.shape, lambda i: (0, 0)),
                  pl.BlockSpec((1, c), lambda i: (0, 0)),
                  pl.BlockSpec((1, c), lambda i: (0, 0)),
                  pl.BlockSpec((1, c), lambda i: (0, 0)),
                  _wspec(wout, layer, (c, d), lambda i: (0, 0))],
        out_specs=pl.BlockSpec((tm, d), prev),
        scratch_shapes=[pltpu.VMEM((tm + CONV_HALO, c), F32), pltpu.VMEM((2, tm, c), BF16)],
        compiler_params=_cparams(("arbitrary",), vmem),
    )(x, u, halo, dww, dwb, lng, lnb, wout)


def _last_rows(past, seq, n):
    if seq.shape[1] >= n:
        return seq[:, seq.shape[1] - n:]
    return jnp.concatenate([past, seq], axis=1)[:, -n:]


def _halos(seq, past, tm, halo_rows):
    b, l, c = seq.shape
    p = past.shape[1]
    first = jnp.concatenate([jnp.zeros((b, halo_rows - p, c), seq.dtype), past], axis=1)[:, None]
    nt = l // tm
    if nt == 1:
        return first.reshape(b, halo_rows, c)
    rest = seq.reshape(b, nt, tm, c)[:, :-1, tm - halo_rows:]
    return jnp.concatenate([first, rest], axis=1).reshape(b * nt, halo_rows, c)


def _lru_tail_kernel(x_ref, ybr_ref, xbr_ref, halo_ref, h0_ref, cw_ref, cb_ref, wa_ref, ba_ref,
                     wx_ref, bx_ref, lam_ref, wout_ref, o_ref, hlast_ref,
                     xext_ref, a_ref, b_ref, hc_ref, *, n_taps, lanes):
    tm, c = xbr_ref.shape
    n_blocks, blk, _ = wa_ref.shape
    t = pl.program_id(1)

    @pl.when(t == 0)
    def _():
        hc_ref[...] = jnp.broadcast_to(h0_ref[0], hc_ref.shape)

    xext_ref[0:LRU_HALO, :] = halo_ref[0]
    xext_ref[LRU_HALO:LRU_HALO + tm, :] = xbr_ref[...]
    off = LRU_HALO - (n_taps - 1)

    sp = jax.nn.softplus(-lam_ref[...])
    for n in range(n_blocks):
        sl = slice(n * blk, (n + 1) * blk)
        xc = jnp.zeros((tm, blk), F32)
        for k in range(n_taps):
            xc = xc + cw_ref[k:k + 1, sl] * xext_ref[off + k:off + k + tm, sl]
        xc = xc + cb_ref[:, sl]
        xcb = xc.astype(BF16)
        r = jax.nn.sigmoid(_dot(xcb, wa_ref[n]) + ba_ref[:, sl])
        gi = jax.nn.sigmoid(_dot(xcb, wx_ref[n]) + bx_ref[:, sl])
        log_a = -LRU_C * r * sp[:, sl]
        a = jnp.exp(log_a)
        a_ref[:, sl] = a
        b_ref[:, sl] = jnp.sqrt(-jnp.tanh(log_a) * (a * a + 1.0)) * (gi * xc)

    row = lax.broadcasted_iota(jnp.int32, (SUBLANES, lanes), 0)

    for l0 in range(0, c, lanes):
        def group(gidx, h):
            r0 = pl.multiple_of(gidx * SUBLANES, SUBLANES)
            av = a_ref[pl.ds(r0, SUBLANES), l0:l0 + lanes]
            bv = b_ref[pl.ds(r0, SUBLANES), l0:l0 + lanes]
            for s in (1, 2, 4):
                a_sh = jnp.where(row >= s, pltpu.roll(av, s, axis=0), 1.0)
                b_sh = jnp.where(row >= s, pltpu.roll(bv, s, axis=0), 0.0)
                bv = av * b_sh + bv
                av = av * a_sh
            hs = av * h + bv
            b_ref[pl.ds(r0, SUBLANES), l0:l0 + lanes] = hs
            return jnp.broadcast_to(hs[SUBLANES - 1:SUBLANES, :], (SUBLANES, lanes))

        h_fin = lax.fori_loop(0, tm // SUBLANES, group, hc_ref[:, l0:l0 + lanes], unroll=2)
        hc_ref[:, l0:l0 + lanes] = h_fin

    gated = (b_ref[...] * ybr_ref[...]).astype(BF16)
    o_ref[...] = x_ref[...] + _dot(gated, wout_ref[...])

    @pl.when(t == pl.num_programs(1) - 1)
    def _():
        hlast_ref[0] = hc_ref[...]


def _lru_tail(x, ybr, xbr, halo, h0, cw, cb, wa, ba, wx, bx, lam, wout, batch, tm, n_taps):
    m, d = x.shape
    c = xbr.shape[1]
    nt = m // batch // tm
    lanes = _tile(c, 512)
    row2 = lambda b, t: (b * nt + t, 0)
    const2 = lambda b, t: (0, 0)
    vec = pl.BlockSpec((1, c), const2)
    vmem = (4 * tm * d * 4 + 4 * tm * c * 4 + 2 * c * d * 2 + 4 * wa.size * 2
            + (tm + LRU_HALO) * c * 4 + 2 * tm * c * 4)
    return pl.pallas_call(
        functools.partial(_lru_tail_kernel, n_taps=n_taps, lanes=lanes),
        name=f"lru_tail_m{m}",
        out_shape=[jax.ShapeDtypeStruct((m, d), F32),
                   jax.ShapeDtypeStruct((batch, SUBLANES, c), F32)],
        grid=(batch, nt),
        in_specs=[pl.BlockSpec((tm, d), row2),
                  pl.BlockSpec((tm, c), row2),
                  pl.BlockSpec((tm, c), row2),
                  pl.BlockSpec((1, LRU_HALO, c), lambda b, t: (b * nt + t, 0, 0)),
                  pl.BlockSpec((1, 1, c), lambda b, t: (b, 0, 0)),
                  pl.BlockSpec(cw.shape, const2),
                  vec,
                  pl.BlockSpec(wa.shape, lambda b, t: (0, 0, 0)),
                  vec,
                  pl.BlockSpec(wx.shape, lambda b, t: (0, 0, 0)),
                  vec, vec,
                  pl.BlockSpec((c, d), const2)],
        out_specs=[pl.BlockSpec((tm, d), row2),
                   pl.BlockSpec((1, SUBLANES, c), lambda b, t: (b, 0, 0))],
        scratch_shapes=[pltpu.VMEM((tm + LRU_HALO, c), F32), pltpu.VMEM((tm, c), F32),
                        pltpu.VMEM((tm, c), F32), pltpu.VMEM((SUBLANES, c), F32)],
        compiler_params=_cparams(("parallel", "arbitrary"), vmem),
    )(x, ybr, xbr, halo, h0, cw, cb, wa, ba, wx, bx, lam, wout)


def _shifted_rows(vec, n_rows):
    return pltpu.roll(jnp.broadcast_to(vec, (n_rows, vec.shape[1])), 0, 1, stride=1, stride_axis=0)


def _block_softmax(s_parts, v_parts):
    m = s_parts[0].max(axis=-1, keepdims=True)
    for s in s_parts[1:]:
        m = jnp.maximum(m, s.max(axis=-1, keepdims=True))
    o, den = None, None
    for s, v in zip(s_parts, v_parts):
        p = jnp.exp(s - m)
        d = p.sum(axis=-1, keepdims=True)
        pv = _dot(p.astype(BF16), v)
        o, den = (pv, d) if o is None else (o + pv, den + d)
    return o, m, den


def _block_softmax_mxu_sum(s_parts, v_parts):
    m = s_parts[0].max(axis=-1, keepdims=True)
    for s in s_parts[1:]:
        m = jnp.maximum(m, s.max(axis=-1, keepdims=True))
    acc = None
    for s, v in zip(s_parts, v_parts):
        pv = _dot(jnp.exp(s - m).astype(BF16), v)
        acc = pv if acc is None else acc + pv
    dh = acc.shape[1] // 2
    return acc[:, :dh], m, acc[:, dh:]


def _merge(state, o_b, m_b, l_b):
    if state is None:
        return o_b, m_b, l_b
    acc, m_old, l_old = state
    m_new = jnp.maximum(m_old, m_b)
    a_old = jnp.exp(m_old - m_new)
    a_b = jnp.exp(m_b - m_new)
    return acc * a_old + o_b * a_b, m_new, l_old * a_old + l_b * a_b


def _attn_prompt_kernel(*refs, heads, dh):
    n_g = len(GROUPS)
    ins = refs[:5 * n_g]
    bvec_ref, o_ref, acc_ref, m_ref, l_ref = refs[5 * n_g:]
    first_chunk = pl.program_id(1) == 0
    h = pl.program_id(2)
    scale = dh ** -0.5
    qi = lax.broadcasted_iota(jnp.int32, (BLK, 2 * BLK), 0)
    kj = lax.broadcasted_iota(jnp.int32, (BLK, 2 * BLK), 1)
    in_band = (kj - qi >= 0) & (kj - qi <= BLK)
    ones = jnp.ones((BLK, dh), BF16)
    for g, (_, dil) in enumerate(GROUPS):
        q_ref, k_ref, v_ref, kp_ref, vp_ref = ins[5 * g:5 * g + 5]
        bias = jnp.where(in_band, _shifted_rows(bvec_ref[pl.ds(g * heads + h, 1), :], BLK), NEG_INF)
        bias_p, bias_c = bias[:, :BLK], bias[:, BLK:]
        for r in range(dil):
            prev_rows = pl.ds(r, BLK, stride=dil)
            k_prev = kp_ref[prev_rows, :].astype(BF16)
            v_prev = jnp.concatenate([vp_ref[prev_rows, :].astype(BF16), ones], axis=1)
            for nb in range(CHUNK // (dil * BLK)):
                rows = pl.ds(r + dil * BLK * nb, BLK, stride=dil)
                q = q_ref[rows, :].astype(BF16)
                k_cur = k_ref[rows, :].astype(BF16)
                v_cur = jnp.concatenate([v_ref[rows, :].astype(BF16), ones], axis=1)
                s_p = _nt_dot(q, k_prev) * scale + bias_p
                if nb == 0:
                    s_p = jnp.where(first_chunk, NEG_INF, s_p)
                s_c = _nt_dot(q, k_cur) * scale + bias_c
                o_b, m_b, l_b = _block_softmax_mxu_sum([s_p, s_c], [v_prev, v_cur])
                m_b = jnp.broadcast_to(m_b, (BLK, dh))
                state = None if g == 0 else (acc_ref[rows, :], m_ref[rows, :], l_ref[rows, :])
                acc, m_new, l_new = _merge(state, o_b, m_b, l_b)
                if g == n_g - 1:
                    o_ref[rows, :] = acc / l_new
                else:
                    acc_ref[rows, :] = acc
                    m_ref[rows, :] = m_new
                    l_ref[rows, :] = l_new
                k_prev, v_prev = k_cur, v_cur


def _attn_prompt(qkv, bvec, batch, seq, heads, dh):
    n_g = len(GROUPS)
    nc = seq // CHUNK
    col = lambda comp, g: (lambda b, n, h: (b * nc + n, (comp * n_g + g) * heads + h))
    in_specs, args = [], []
    for g, (win, _) in enumerate(GROUPS):
        per = CHUNK // win
        prev = lambda comp, g=g, per=per: (
            lambda b, n, h: (jnp.maximum((b * nc + n) * per - 1, 0), (comp * n_g + g) * heads + h))
        in_specs += [pl.BlockSpec((CHUNK, dh), col(0, g)), pl.BlockSpec((CHUNK, dh), col(1, g)),
                     pl.BlockSpec((CHUNK, dh), col(2, g)),
                     pl.BlockSpec((win, dh), prev(1)), pl.BlockSpec((win, dh), prev(2))]
        args += [qkv] * 5
    in_specs.append(pl.BlockSpec(bvec.shape, lambda b, n, h: (0, 0)))
    vmem = 2 * (9 * CHUNK + 2 * sum(w for w, _ in GROUPS)) * dh * 4 + 2 * CHUNK * dh * 4 + 3 * CHUNK * dh * 4
    return pl.pallas_call(
        functools.partial(_attn_prompt_kernel, heads=heads, dh=dh),
        name="attn_prompt",
        out_shape=jax.ShapeDtypeStruct((batch * seq, heads * dh), F32),
        grid=(batch, nc, heads),
        in_specs=in_specs,
        out_specs=pl.BlockSpec((CHUNK, dh), lambda b, n, h: (b * nc + n, h)),
        scratch_shapes=[pltpu.VMEM((CHUNK, dh), F32)] * 3,
        compiler_params=_cparams(("parallel", "parallel", "parallel"), vmem),
    )(*args, bvec)


def _attn_step_kernel(*refs, heads, dh, n_bufs):
    n_g = len(GROUPS)
    ins = refs[:6 * n_g]
    o_ref, acc_ref, m_ref, l_ref, knp_ref, vnp_ref = refs[6 * n_g:]
    part = pl.program_id(1)
    t = o_ref.shape[1]
    scale = dh ** -0.5
    row_stride = 2 * heads

    def merge_into(hs, o_b, m_b, l_b):
        state = (acc_ref[:, hs], m_ref[:, hs], l_ref[:, hs])
        acc, m_new, l_new = _merge(state, o_b, jnp.broadcast_to(m_b, (t, dh)), jnp.broadcast_to(l_b, (t, dh)))
        acc_ref[:, hs] = acc
        m_ref[:, hs] = m_new
        l_ref[:, hs] = l_new

    def cache_piece(g, first_row):
        dil = GROUPS[g][1]
        q_ref, _, _, cache_ref, yc_ref, _ = ins[6 * g:6 * g + 6]
        rows = cache_ref.shape[1] // row_stride
        tc = lax.broadcasted_iota(jnp.int32, (t, rows), 0)
        cc = lax.broadcasted_iota(jnp.int32, (t, rows), 1) + first_row
        dist_c = n_bufs[g] + tc - cc
        ok_c = ((dist_c & (dil - 1)) == 0) & (dist_c <= (N_DIL_KEYS - 1) * dil)
        for h in range(heads):
            hs = slice(h * dh, (h + 1) * dh)
            q = q_ref[0, :, hs].astype(BF16)
            kc = cache_ref[0, pl.ds(h, rows, stride=row_stride), :].astype(BF16)
            vc = cache_ref[0, pl.ds(heads + h, rows, stride=row_stride), :].astype(BF16)
            bias = _shifted_rows(yc_ref[0, h:h + 1, :], t)[:, BLK:]
            s_c = jnp.where(ok_c, _nt_dot(q, kc) * scale + bias, NEG_INF)
            merge_into(hs, *_block_softmax([s_c], [vc]))

    @pl.when(part == 0)
    def _():
        acc_ref[...] = jnp.zeros_like(acc_ref)
        l_ref[...] = jnp.zeros_like(l_ref)
        m_ref[...] = jnp.full_like(m_ref, NEG_INF)
        tn = lax.broadcasted_iota(jnp.int32, (t, BLK), 0)
        cn = lax.broadcasted_iota(jnp.int32, (t, BLK), 1)
        dist_n = tn - cn
        for g, (_, dil) in enumerate(GROUPS):
            q_ref, kn_ref, vn_ref, _, _, yn_ref = ins[6 * g:6 * g + 6]
            ok_n = (dist_n >= 0) & ((dist_n & (dil - 1)) == 0) & (dist_n <= (N_DIL_KEYS - 1) * dil)
            knp_ref[...] = jnp.zeros_like(knp_ref)
            vnp_ref[...] = jnp.zeros_like(vnp_ref)
            knp_ref[0:t, :] = kn_ref[0]
            vnp_ref[0:t, :] = vn_ref[0]
            for h in range(heads):
                hs = slice(h * dh, (h + 1) * dh)
                q = q_ref[0, :, hs].astype(BF16)
                bias = _shifted_rows(yn_ref[h:h + 1, :], t)[:, BLK:]
                s_n = jnp.where(ok_n, _nt_dot(q, knp_ref[:, hs].astype(BF16)) * scale + bias, NEG_INF)
                merge_into(hs, *_block_softmax([s_n], [vnp_ref[:, hs].astype(BF16)]))
        for g in range(n_g):
            if ins[6 * g + 3].shape[1] == n_bufs[g] * row_stride:
                cache_piece(g, 0)

    for g in range(n_g):
        rows = ins[6 * g + 3].shape[1] // row_stride
        if rows != n_bufs[g]:
            cache_piece(g, part * rows)

    @pl.when(part == pl.num_programs(1) - 1)
    def _():
        o_ref[0] = acc_ref[...] / l_ref[...]


def _attn_step(qkv, caches, rel_bias, batch, t, heads, dh, n_parts=2):
    n_g = len(GROUPS)
    width = heads * dh
    qkv3 = qkv.reshape(batch, t, qkv.shape[1])
    in_specs, args, n_bufs = [], [], []
    vmem = 8 * t * width * 4
    for g in range(n_g):
        n_buf = caches[g].shape[1]
        parts = n_parts if n_buf % (n_parts * BLK) == 0 else 1
        rows = n_buf // parts
        flat = caches[g].reshape(batch, n_buf * 2 * heads, dh)
        yc, yn = _step_bias_vectors(rel_bias, g, heads, n_buf, parts)
        new = lambda comp, g=g: pl.BlockSpec((1, t, width), lambda b, s: (b, 0, comp * n_g + g))
        part_of = (lambda s: s) if parts == n_parts else (lambda s: 0)
        in_specs += [new(0), new(1), new(2),
                     pl.BlockSpec((1, rows * 2 * heads, dh), lambda b, s, f=part_of: (b, f(s), 0)),
                     pl.BlockSpec((1, heads, rows + BLK), lambda b, s, f=part_of: (f(s), 0, 0)),
                     pl.BlockSpec(yn.shape, lambda b, s: (0, 0))]
        args += [qkv3, qkv3, qkv3, flat, yc, yn]
        n_bufs.append(n_buf)
        vmem += 2 * rows * 2 * width * 4 + 6 * rows * dh * 4
    return pl.pallas_call(
        functools.partial(_attn_step_kernel, heads=heads, dh=dh, n_bufs=tuple(n_bufs)),
        name="attn_step",
        out_shape=jax.ShapeDtypeStruct((batch, t, width), F32),
        grid=(batch, n_parts),
        in_specs=in_specs,
        out_specs=pl.BlockSpec((1, t, width), lambda b, s: (b, 0, 0)),
        scratch_shapes=[pltpu.VMEM((t, width), F32)] * 3 + [pltpu.VMEM((BLK, width), F32)] * 2,
        compiler_params=_cparams(("parallel", "arbitrary"), vmem),
    )(*args).reshape(batch * t, width)


def _attn_out_kernel(x_ref, o_ref, wo_ref, out_ref):
    out_ref[...] = x_ref[...] + _dot(o_ref[...].astype(BF16), wo_ref[...])


def _attn_out(x, o, wo, tm_pref=512):
    m, d = x.shape
    width = wo.shape[0]
    tm = _tile(m, tm_pref)
    row = lambda n: pl.BlockSpec((tm, n), lambda i: (i, 0))
    vmem = 4 * tm * d * 4 + 2 * tm * width * 4 + 2 * width * d * 2
    return pl.pallas_call(
        _attn_out_kernel,
        name=f"attn_out_m{m}",
        out_shape=jax.ShapeDtypeStruct((m, d), F32),
        grid=(m // tm,),
        in_specs=[row(d), row(width), pl.BlockSpec((width, d), lambda i: (0, 0))],
        out_specs=row(d),
        compiler_params=_cparams(("parallel",), vmem),
    )(x, o, wo)


def _t5_bucket(dist):
    max_exact = N_BUCKETS // 2
    d = np.asarray(dist, dtype=np.int32)
    large = max_exact + (np.log(np.maximum(d, max_exact) / max_exact) / np.log(MAX_DISTANCE / max_exact)
                         * (N_BUCKETS - max_exact)).astype(np.int32)
    return np.where(d < max_exact, d, np.minimum(large, N_BUCKETS - 1)).astype(np.int32)


def _group_bias(rel_bias, g, heads):
    buckets = _t5_bucket(np.arange(N_DIL_KEYS) * GROUPS[g][1])
    return rel_bias[:, g * heads:(g + 1) * heads].astype(F32)[buckets]


def _prompt_bias_vectors(rel_bias, heads):
    vecs = []
    for g in range(len(GROUPS)):
        rev = _group_bias(rel_bias, g, heads)[::-1]
        vecs.append(jnp.pad(rev, ((0, 2 * BLK - N_DIL_KEYS), (0, 0))).T)
    return jnp.concatenate(vecs, axis=0)


def _step_bias_vectors(rel_bias, g, heads, n_buf, parts):
    rows = n_buf // parts
    bias = _group_bias(rel_bias, g, heads)
    at = lambda dist: bias[np.clip(dist // GROUPS[g][1], 0, N_DIL_KEYS - 1)]
    first = np.arange(parts)[:, None] * rows + np.arange(rows + BLK)[None, :]
    yc = jnp.transpose(at(n_buf + BLK - first), (0, 2, 1))
    yn = at(BLK - np.arange(2 * BLK)).T
    return yc, yn


def _kv_rows(qkv, g, batch, seq, n_rows, heads, dh):
    n_g = len(GROUPS)
    width = heads * dh
    q3 = qkv.reshape(batch, seq, qkv.shape[1])[:, seq - n_rows:]
    k = q3[:, :, (n_g + g) * width:(n_g + g + 1) * width]
    v = q3[:, :, (2 * n_g + g) * width:(2 * n_g + g + 1) * width]
    return jnp.stack([k, v], axis=2).reshape(batch, n_rows, 2, heads, dh)


def _trunk(x3, st, p, tiles, ffn_bf16, emit_ffn):
    batch, seq, d = x3.shape
    m = batch * seq
    x = x3.reshape(m, d)
    new = []
    depth = p['ffn_w1'].shape[0]
    for i in range(depth):
        kind = i % N_MIXERS
        g_mix = p['norm_mix_g'][i][None]
        if kind == 0:
            j = i // N_MIXERS
            n_taps, c = p['conv_dw_w'].shape[1:]
            buf = jnp.zeros((batch, n_taps - 1, c), F32) if st is None else st[i][0]
            u = _norm_proj(x, g_mix, p['conv_w_glu'], 'glu', tiles['proj_m'], layer=j)
            u3 = u.reshape(batch, seq, c)
            tm = _tile(seq, tiles['conv_m'])
            halo = _halos(u3, buf, tm, CONV_HALO)
            dww = jnp.repeat(p['conv_dw_w'][j], SUBLANES, axis=0)
            x = _conv_tail(x, u, halo, dww, p['conv_dw_b'][j][None], p['conv_ln_g'][j][None],
                           p['conv_ln_b'][j][None], p['conv_w_out'], j, tm, n_taps)
            new.append((_last_rows(buf, u3, n_taps - 1),))
        elif kind == 1:
            n_taps, c = p['lru_conv_w'].shape
            if st is None:
                h0 = jnp.zeros((batch, c), F32)
                cb = jnp.zeros((batch, n_taps - 1, c), F32)
            else:
                h0, cb = st[i]
            ybr, xbr = _norm_proj(x, g_mix, p['lru_w_in'], 'split', tiles['proj_m'])
            xbr3 = xbr.reshape(batch, seq, c)
            tm = _tile(seq, tiles['lru_m'])
            halo = _halos(xbr3, cb, tm, LRU_HALO)
            cw = jnp.pad(p['lru_conv_w'], ((0, SUBLANES - n_taps), (0, 0)))
            x, hl = _lru_tail(x, ybr, xbr, halo, h0[:, None], cw, p['lru_conv_b'][None], p['lru_w_a'],
                              p['lru_b_a'][None], p['lru_w_x'], p['lru_b_x'][None], p['lru_lambda'][None],
                              p['lru_w_out'], batch, tm, n_taps)
            new.append((hl[:, 0], _last_rows(cb, xbr3, n_taps - 1)))
        else:
            rel_bias = p['rel_bias']
            heads = rel_bias.shape[1] // len(GROUPS)
            width = p['att_w_o'].shape[0]
            dh = width // heads
            qkv = _norm_proj(x, g_mix, p['att_w_qkv'], 'plain', tiles['proj_m'], tn_pref=1024)
            if st is None:
                assert seq % CHUNK == 0
                o = _attn_prompt(qkv, _prompt_bias_vectors(rel_bias, heads), batch, seq, heads, dh)
                rows = [_kv_rows(qkv, g, batch, seq, min(win, seq), heads, dh) for g, (win, _) in enumerate(GROUPS)]
            else:
                assert seq <= BLK
                o = _attn_step(qkv, st[i], rel_bias, batch, seq, heads, dh)
                rows = [_kv_rows(qkv, g, batch, seq, seq, heads, dh) for g in range(len(GROUPS))]
            x = _attn_out(x, o, p['att_w_o'], tiles['attn_out_m'])
            new.append(rows)
        g_ffn, g_fin, final = p['norm_ffn_g'][i][None], p['final_norm_g'][None], i == depth - 1
        if emit_ffn:
            x, w1b, w2b = _ffn(x, g_ffn, p['ffn_w1'], p['ffn_w2'], i, g_fin, final, tm_pref=tiles['ffn_m'],
                               tf_pref=tiles['ffn_f'], emit=True)
            ffn_bf16.append((w1b, w2b))
        else:
            x = _ffn(x, g_ffn, *ffn_bf16[i], i, g_fin, final, tm_pref=tiles['ffn_m'], tf_pref=tiles['ffn_f'])
    return x.reshape(batch, seq, d), new


PROMPT_TILES = dict(proj_m=1024, conv_m=256, lru_m=256, attn_out_m=512, ffn_m=512, ffn_f=1024)
SAMPLE_TILES = dict(proj_m=512, conv_m=256, lru_m=256, attn_out_m=512, ffn_m=512, ffn_f=512)
MATMUL_WEIGHTS = ('conv_w_glu', 'conv_w_out', 'lru_w_in', 'lru_w_a', 'lru_w_x', 'lru_w_out', 'att_w_qkv', 'att_w_o')


def kernel(x_prompt, x_sample, state_conv_l0, state_lru_h_l1, state_lru_conv_l1, cache_kv_w128_l2, cache_kv_w512_l2, cache_kv_w2048_l2, state_conv_l3, norm_mix_g, norm_ffn_g, final_norm_g, ffn_w1, ffn_w2, conv_w_glu, conv_dw_w, conv_dw_b, conv_ln_g, conv_ln_b, conv_w_out, lru_w_in, lru_conv_w, lru_conv_b, lru_w_a, lru_b_a, lru_w_x, lru_b_x, lru_lambda, lru_w_out, att_w_qkv, att_w_o, rel_bias):
    p = dict(norm_mix_g=norm_mix_g, norm_ffn_g=norm_ffn_g, final_norm_g=final_norm_g, ffn_w1=ffn_w1,
             ffn_w2=ffn_w2, conv_w_glu=conv_w_glu, conv_dw_w=conv_dw_w, conv_dw_b=conv_dw_b,
             conv_ln_g=conv_ln_g, conv_ln_b=conv_ln_b, conv_w_out=conv_w_out, lru_w_in=lru_w_in,
             lru_conv_w=lru_conv_w, lru_conv_b=lru_conv_b, lru_w_a=lru_w_a, lru_b_a=lru_b_a, lru_w_x=lru_w_x,
             lru_b_x=lru_b_x, lru_lambda=lru_lambda, lru_w_out=lru_w_out, att_w_qkv=att_w_qkv,
             att_w_o=att_w_o, rel_bias=rel_bias)
    for name in MATMUL_WEIGHTS:
        p[name] = p[name].astype(BF16)
    ffn_bf16 = []
    y_sample, sst = _trunk(x_sample, [(state_conv_l0,), (state_lru_h_l1, state_lru_conv_l1),
                                      (cache_kv_w128_l2, cache_kv_w512_l2, cache_kv_w2048_l2),
                                      (state_conv_l3,)], p, SAMPLE_TILES, ffn_bf16, emit_ffn=True)
    y_prompt, pst = _trunk(x_prompt, None, p, PROMPT_TILES, ffn_bf16, emit_ffn=False)
    return (y_prompt, y_sample,
            pst[0][0], sst[0][0],
            pst[1][0], sst[1][0],
            pst[1][1], sst[1][1],
            pst[2][0], sst[2][0],
            pst[2][1], sst[2][1],
            pst[2][2], sst[2][2],
            pst[3][0], sst[3][0])
```

```python
import functools

import numpy as np
import jax
import jax.numpy as jnp
from jax import lax
from jax.experimental import pallas as pl
from jax.experimental.pallas import tpu as pltpu

RMS_EPS = 1e-6
LN_EPS = 1e-5
LRU_C = 8.0
NEG_INF = -1e30
N_MIXERS = 3
GROUPS = ((128, 1), (512, 4), (2048, 16))
N_DIL_KEYS = 129
BLK = 128
CHUNK = BLK * max(d for _, d in GROUPS)
N_BUCKETS = 32
MAX_DISTANCE = 2048

V7X_VMEM_BYTES = 64 * 1024 * 1024
SUBLANES = 8
CONV_HALO = 32
LRU_HALO = 8

BF16 = jnp.bfloat16
F32 = jnp.float32


def _cparams(semantics, vmem_bytes):
    limit = int(min(max(vmem_bytes * 3 // 2, 32 * 1024 * 1024), V7X_VMEM_BYTES - 8 * 1024 * 1024))
    return pltpu.CompilerParams(dimension_semantics=semantics, vmem_limit_bytes=limit)


def _tile(n, pref):
    if n <= pref:
        return n
    t = pref
    while n % t:
        t //= 2
    return t


def _rms_to_bf16(x, g):
    ms = jnp.mean(x * x, axis=-1, keepdims=True)
    return (x * lax.rsqrt(ms + RMS_EPS) * g).astype(BF16)


def _dot(a, b):
    return jnp.dot(a, b, preferred_element_type=F32)


def _nt_dot(a, b):
    return lax.dot_general(a, b, (((1,), (1,)), ((), ())), preferred_element_type=F32)


def _norm_proj_kernel(x_ref, g_ref, *refs, mode):
    if mode == 'plain':
        w_ref, o_ref, h_ref = refs
    elif mode == 'glu':
        wa_ref, wg_ref, o_ref, h_ref = refs
    else:
        wa_ref, wg_ref, o_ref, o2_ref, h_ref = refs

    @pl.when(pl.program_id(1) == 0)
    def _():
        h_ref[...] = _rms_to_bf16(x_ref[...], g_ref[...])

    h = h_ref[...]
    if mode == 'plain':
        o_ref[...] = _dot(h, w_ref[...])
    elif mode == 'glu':
        o_ref[...] = _dot(h, wa_ref[...]) * jax.nn.sigmoid(_dot(h, wg_ref[...]))
    else:
        o_ref[...] = jax.nn.gelu(_dot(h, wa_ref[...]), approximate=True)
        o2_ref[...] = _dot(h, wg_ref[...])


def _wspec(w, layer, block, index_map):
    if w.ndim == 2:
        return pl.BlockSpec(block, index_map)
    return pl.BlockSpec((None,) + block, lambda *ids: (layer,) + index_map(*ids))


def _norm_proj(x, g, w, mode, tm_pref=512, tn_pref=512, layer=0):
    m, d = x.shape
    n = w.shape[-1]
    n_out = n if mode == 'plain' else n // 2
    tm = _tile(m, tm_pref)
    tn = _tile(n_out, tn_pref)
    nj = n_out // tn
    in_specs = [pl.BlockSpec((tm, d), lambda i, j: (i, 0)),
                pl.BlockSpec((1, d), lambda i, j: (0, 0)),
                _wspec(w, layer, (d, tn), lambda i, j: (0, j))]
    args = [x, g, w]
    if mode != 'plain':
        in_specs.append(_wspec(w, layer, (d, tn), lambda i, j: (0, j + nj)))
        args.append(w)
    o_spec = pl.BlockSpec((tm, tn), lambda i, j: (i, j))
    o_shape = jax.ShapeDtypeStruct((m, n_out), F32)
    n_o = 2 if mode == 'split' else 1
    vmem = 2 * tm * d * 4 + tm * d * 2 + 2 * len(args[2:]) * d * tn * 2 + 2 * n_o * tm * tn * 4
    return pl.pallas_call(
        functools.partial(_norm_proj_kernel, mode=mode),
        name=f"norm_proj_{mode}_m{m}",
        out_shape=[o_shape] * n_o if n_o == 2 else o_shape,
        grid=(m // tm, nj),
        in_specs=in_specs,
        out_specs=[o_spec] * n_o if n_o == 2 else o_spec,
        scratch_shapes=[pltpu.VMEM((tm, d), BF16)],
        compiler_params=_cparams(("parallel", "arbitrary"), vmem),
    )(*args)


def _ffn_kernel(x_ref, g_ref, w1_ref, w2_ref, fg_ref, o_ref, *rest, final, emit):
    if emit:
        w1b_ref, w2b_ref, h_ref = rest
    else:
        (h_ref,) = rest
    f = pl.program_id(1)

    @pl.when(f == 0)
    def _():
        x = x_ref[...]
        h_ref[...] = _rms_to_bf16(x, g_ref[...])
        o_ref[...] = x

    w1 = w1_ref[...].astype(BF16)
    w2 = w2_ref[...].astype(BF16)
    if emit:
        w1b_ref[...] = w1
        w2b_ref[...] = w2
    hid = jnp.square(jnp.maximum(_dot(h_ref[...], w1), 0.0)).astype(BF16)
    o_ref[...] += _dot(hid, w2)

    if final:
        @pl.when(f == pl.num_programs(1) - 1)
        def _():
            y = o_ref[...]
            ms = jnp.mean(y * y, axis=-1, keepdims=True)
            o_ref[...] = y * lax.rsqrt(ms + RMS_EPS) * fg_ref[...]


def _ffn(x, g, w1, w2, layer, final_g, final, tm_pref=512, tf_pref=1024, emit=False, rows=None):
    m, d = (x.shape[0] if rows is None else rows), x.shape[1]
    dff = w1.shape[-1]
    tm = _tile(m, tm_pref)
    tf = _tile(dff, tf_pref)
    wbytes = w1.dtype.itemsize
    vmem = (2 * tm * d * 4 + tm * d * 2 + 2 * 2 * d * tf * wbytes + 2 * tm * d * 4 + 2 * tm * tf * 4
            + (6 * d * tf * 2 if emit else 0))
    o_shape = jax.ShapeDtypeStruct((m, d), F32)
    o_spec = pl.BlockSpec((tm, d), lambda i, f: (i, 0))
    if emit:
        assert m == tm, "each weight block must be visited once"
        o_shape = [o_shape, jax.ShapeDtypeStruct((d, dff), BF16), jax.ShapeDtypeStruct((dff, d), BF16)]
        o_spec = [o_spec, pl.BlockSpec((d, tf), lambda i, f: (0, f)), pl.BlockSpec((tf, d), lambda i, f: (f, 0))]
    return pl.pallas_call(
        functools.partial(_ffn_kernel, final=final, emit=emit),
        name=f"ffn_m{m}_l{layer}",
        out_shape=o_shape,
        grid=(m // tm, dff // tf),
        in_specs=[pl.BlockSpec((tm, d), lambda i, f: (i, 0)),
                  pl.BlockSpec((1, d), lambda i, f: (0, 0)),
                  _wspec(w1, layer, (d, tf), lambda i, f: (0, f)),
                  _wspec(w2, layer, (tf, d), lambda i, f: (f, 0)),
                  pl.BlockSpec((1, d), lambda i, f: (0, 0))],
        out_specs=o_spec,
        scratch_shapes=[pltpu.VMEM((tm, d), BF16)],
        compiler_params=_cparams(("parallel", "arbitrary"), vmem),
    )(x, g, w1, w2, final_g)


def _conv_tail_kernel(x_ref, u_ref, halo_ref, dww_ref, dwb_ref, lng_ref, lnb_ref, wout_ref,
                      o_ref, uext_ref, y_ref, *, n_taps, rows, ln_rows, lanes):
    i = pl.program_id(0)
    tm, c = u_ref.shape
    cur = lax.rem(i, 2)

    @pl.when(i == 0)
    def _():
        y_ref[1] = jnp.zeros((tm, c), BF16)

    uext_ref[0:CONV_HALO, :] = halo_ref[0]
    uext_ref[CONV_HALO:CONV_HALO + tm, :] = u_ref[...]
    off = CONV_HALO - (n_taps - 1)

    def row_chunk(r, carry):
        r0 = pl.multiple_of(r * rows, rows)
        for l0 in range(0, c, lanes):
            win = uext_ref[pl.ds(r0, rows + CONV_HALO), l0:l0 + lanes]
            accs = [jnp.zeros((SUBLANES, lanes), F32)] * (rows // SUBLANES)
            for s in range(SUBLANES):
                ws = win if s == 0 else pltpu.roll(win, rows + CONV_HALO - s, axis=0)
                for a in range((CONV_HALO + SUBLANES) // SUBLANES):
                    k = a * SUBLANES + s - off
                    if 0 <= k < n_taps:
                        wk = dww_ref[k * SUBLANES:(k + 1) * SUBLANES, l0:l0 + lanes]
                        accs = [acc + wk * ws[(a + j) * SUBLANES:(a + j + 1) * SUBLANES]
                                for j, acc in enumerate(accs)]
            for j, acc in enumerate(accs):
                uext_ref[pl.ds(r0 + j * SUBLANES, SUBLANES), l0:l0 + lanes] = acc + dwb_ref[:, l0:l0 + lanes]
        return carry

    @pl.when(i < pl.num_programs(0) - 1)
    def _():
        lax.fori_loop(0, tm // rows, row_chunk, 0)

    o_ref[...] = x_ref[...] + _dot(y_ref[1 - cur], wout_ref[...])
    for r0 in range(0, tm, ln_rows):
        cv = uext_ref[r0:r0 + ln_rows, :]
        mu = jnp.mean(cv, axis=-1, keepdims=True)
        cc = cv - mu
        var = jnp.mean(cc * cc, axis=-1, keepdims=True)
        y = cc * lax.rsqrt(var + LN_EPS) * lng_ref[...] + lnb_ref[...]
        y_ref[cur, r0:r0 + ln_rows, :] = (y * jax.nn.sigmoid(y)).astype(BF16)


def _conv_tail(x, u, halo, dww, dwb, lng, lnb, wout, layer, tm, n_taps):
    m, d = x.shape
    c = u.shape[1]
    rows = _tile(tm, 32)
    ln_rows = _tile(tm, 16)
    lanes = _tile(c, 256)
    nt = m // tm
    prev = lambda i: (jnp.maximum(i - 1, 0), 0)
    this = lambda i: (jnp.minimum(i, nt - 1), 0)
    vmem = 4 * tm * d * 4 + 2 * tm * c * 4 + 2 * c * d * 2 + (tm + CONV_HALO) * c * 4 + 2 * tm * c * 2
    return pl.pallas_call(
        functools.partial(_conv_tail_kernel, n_taps=n_taps, rows=rows, ln_rows=ln_rows, lanes=lanes),
        name=f"conv_tail_m{m}",
        out_shape=jax.ShapeDtypeStruct((m, d), F32),
        grid=(nt + 1,),
        in_specs=[pl.BlockSpec((tm, d), prev),
                  pl.BlockSpec((tm, c), this),
                  pl.BlockSpec((1, CONV_HALO, c), lambda i: (jnp.minimum(i, nt - 1), 0, 0)),
                  pl.BlockSpec(dww.shape, lambda i: (0, 0)),
                  pl.BlockSpec((1, c), lambda i: (0, 0)),
                  pl.BlockSpec((1, c), lambda i: (0, 0)),
                  pl.BlockSpec((1, c), lambda i: (0, 0)),
                  _wspec(wout, layer, (c, d), lambda i: (0, 0))],
        out_specs=pl.BlockSpec((tm, d), prev),
        scratch_shapes=[pltpu.VMEM((tm + CONV_HALO, c), F32), pltpu.VMEM((2, tm, c), BF16)],
        compiler_params=_cparams(("arbitrary",), vmem),
    )(x, u, halo, dww, dwb, lng, lnb, wout)


def _conv_block_kernel(xp_ref, xo_ref, g_ref, wa_ref, wg_ref, wout_ref, halo0_ref, dww_ref, dwb_ref, lng_ref,
                       lnb_ref, o_ref, utail_ref, h_ref, uext_ref, y_ref, *, n_taps, rows, tiles_per_seq):
    t, j = pl.program_id(0), pl.program_id(1)
    tm = xp_ref.shape[0]
    n_lb, _, lanes = uext_ref.shape[1:]
    cur = lax.rem(t, 2)
    prev = 1 - cur
    off = CONV_HALO - (n_taps - 1)

    @pl.when(j == 0)
    def _():
        @pl.when(t == 0)
        def _():
            uext_ref[...] = jnp.zeros_like(uext_ref)
            y_ref[...] = jnp.zeros_like(y_ref)

        h_ref[...] = _rms_to_bf16(xp_ref[...], g_ref[...])
        first = lax.rem(t, tiles_per_seq) == 0
        uext_ref[cur, :, 0:CONV_HALO, :] = jnp.where(first, halo0_ref[0], uext_ref[prev, :, tm:tm + CONV_HALO, :])

    o_ref[...] = xo_ref[...] + _dot(y_ref[cur], wout_ref[...])

    r0 = pl.multiple_of(j * rows, rows)
    for lb in range(n_lb):
        win = uext_ref[prev, lb, pl.ds(r0, rows + CONV_HALO), :]
        accs = [jnp.zeros((SUBLANES, lanes), F32)] * (rows // SUBLANES)
        for s in range(SUBLANES):
            ws = win if s == 0 else pltpu.roll(win, rows + CONV_HALO - s, axis=0)
            for a in range((CONV_HALO + SUBLANES) // SUBLANES):
                k = a * SUBLANES + s - off
                if 0 <= k < n_taps:
                    wk = dww_ref[k * SUBLANES:(k + 1) * SUBLANES, lb * lanes:(lb + 1) * lanes]
                    accs = [acc + wk * ws[(a + q) * SUBLANES:(a + q + 1) * SUBLANES] for q, acc in enumerate(accs)]
        for q, acc in enumerate(accs):
            uext_ref[prev, lb, pl.ds(r0 + q * SUBLANES, SUBLANES), :] = acc + dwb_ref[:, lb * lanes:(lb + 1) * lanes]
    ln_rows = 2 * SUBLANES
    for q in range(rows // ln_rows):
        rs = pl.ds(r0 + q * ln_rows, ln_rows)
        cv = jnp.concatenate([uext_ref[prev, lb, rs, :] for lb in range(n_lb)], axis=1)
        mu = jnp.mean(cv, axis=-1, keepdims=True)
        cc = cv - mu
        var = jnp.mean(cc * cc, axis=-1, keepdims=True)
        yv = cc * lax.rsqrt(var + LN_EPS) * lng_ref[...] + lnb_ref[...]
        y_ref[prev, rs, :] = (yv * jax.nn.sigmoid(yv)).astype(BF16)

    h = h_ref[...]
    u = _dot(h, wa_ref[...]) * jax.nn.sigmoid(_dot(h, wg_ref[...]))
    uext_ref[cur, j, CONV_HALO:CONV_HALO + tm, :] = u
    utail_ref[0, 0] = u[tm - CONV_HALO:, :]


def _conv_block(x, g, w_glu, w_out, layer, buf, dww, dwb, lng, lnb, seq, tm, n_taps):
    m, d = x.shape
    c = w_out.shape[-2]
    batch = m // seq
    rows = 32
    nj = tm // rows
    lanes, cols = c // nj, d // nj
    assert seq % tm == 0 and lanes % BLK == 0 and cols % BLK == 0 and w_glu.shape[-1] == 2 * c
    n_tiles = m // tm
    tps = seq // tm
    halo0 = jnp.pad(buf, ((0, 0), (CONV_HALO - buf.shape[1], 0), (0, 0)))
    halo0 = jnp.transpose(halo0.reshape(batch, CONV_HALO, nj, lanes), (0, 2, 1, 3))
    p_tile = lambda t: jnp.minimum(t, n_tiles - 1)
    o_tile = lambda t: jnp.maximum(t - 2, 0)
    o_dest = lambda t: jnp.where(t >= 2, t - 2, n_tiles + t)
    vec = pl.BlockSpec((1, c), lambda t, j: (0, 0))
    vmem = (2 * tm * d * 4 + 4 * tm * cols * 4 + 6 * d * lanes * 2 + 2 * dww.size * 4 + tm * d * 2
            + 2 * nj * (CONV_HALO + tm) * lanes * 4 + 2 * tm * c * 2 + 8 * tm * lanes * 4)
    out, utail = pl.pallas_call(
        functools.partial(_conv_block_kernel, n_taps=n_taps, rows=rows, tiles_per_seq=tps),
        name=f"conv_block_l{layer}",
        out_shape=[jax.ShapeDtypeStruct((m + 2 * tm, d), F32),
                   jax.ShapeDtypeStruct((n_tiles + 2, nj, CONV_HALO, lanes), F32)],
        grid=(n_tiles + 2, nj),
        in_specs=[pl.BlockSpec((tm, d), lambda t, j: (p_tile(t), 0)),
                  pl.BlockSpec((tm, cols), lambda t, j: (o_tile(t), j)),
                  pl.BlockSpec((1, d), lambda t, j: (0, 0)),
                  _wspec(w_glu, layer, (d, lanes), lambda t, j: (0, j)),
                  _wspec(w_glu, layer, (d, lanes), lambda t, j: (0, j + nj)),
                  _wspec(w_out, layer, (c, cols), lambda t, j: (0, j)),
                  pl.BlockSpec((1, nj, CONV_HALO, lanes), lambda t, j: (p_tile(t) // tps, 0, 0, 0)),
                  pl.BlockSpec(dww.shape, lambda t, j: (0, 0)),
                  vec, vec, vec],
        out_specs=[pl.BlockSpec((tm, cols), lambda t, j: (o_dest(t), j)),
                   pl.BlockSpec((1, 1, CONV_HALO, lanes), lambda t, j: (t, j, 0, 0))],
        scratch_shapes=[pltpu.VMEM((tm, d), BF16),
                        pltpu.VMEM((2, nj, CONV_HALO + tm, lanes), F32),
                        pltpu.VMEM((2, tm, c), BF16)],
        compiler_params=_cparams(("arbitrary", "arbitrary"), vmem),
    )(x, x, g, w_glu, w_glu, w_out, halo0, dww, dwb, lng, lnb)
    last = utail[:n_tiles].reshape(batch, tps, nj, CONV_HALO, lanes)[:, tps - 1]
    state = jnp.transpose(last, (0, 2, 1, 3)).reshape(batch, CONV_HALO, c)[:, CONV_HALO - (n_taps - 1):]
    return out, state


def _last_rows(past, seq, n):
    if seq.shape[1] >= n:
        return seq[:, seq.shape[1] - n:]
    return jnp.concatenate([past, seq], axis=1)[:, -n:]


def _halos(seq, past, tm, halo_rows):
    b, l, c = seq.shape
    p = past.shape[1]
    first = jnp.concatenate([jnp.zeros((b, halo_rows - p, c), seq.dtype), past], axis=1)[:, None]
    nt = l // tm
    if nt == 1:
        return first.reshape(b, halo_rows, c)
    rest = seq.reshape(b, nt, tm, c)[:, :-1, tm - halo_rows:]
    return jnp.concatenate([first, rest], axis=1).reshape(b * nt, halo_rows, c)


def _lru_tail_kernel(x_ref, ybr_ref, xbr_ref, halo_ref, h0_ref, cw_ref, cb_ref, wa_ref, ba_ref,
                     wx_ref, bx_ref, lam_ref, wout_ref, o_ref, hlast_ref,
                     xext_ref, a_ref, b_ref, hc_ref, *, n_taps, lanes):
    tm, c = xbr_ref.shape
    n_blocks, blk, _ = wa_ref.shape
    t = pl.program_id(1)

    @pl.when(t == 0)
    def _():
        hc_ref[...] = jnp.broadcast_to(h0_ref[0], hc_ref.shape)

    xext_ref[0:LRU_HALO, :] = halo_ref[0]
    xext_ref[LRU_HALO:LRU_HALO + tm, :] = xbr_ref[...]
    off = LRU_HALO - (n_taps - 1)

    sp = jax.nn.softplus(-lam_ref[...])
    for n in range(n_blocks):
        sl = slice(n * blk, (n + 1) * blk)
        xc = jnp.zeros((tm, blk), F32)
        for k in range(n_taps):
            xc = xc + cw_ref[k:k + 1, sl] * xext_ref[off + k:off + k + tm, sl]
        xc = xc + cb_ref[:, sl]
        xcb = xc.astype(BF16)
        r = jax.nn.sigmoid(_dot(xcb, wa_ref[n]) + ba_ref[:, sl])
        gi = jax.nn.sigmoid(_dot(xcb, wx_ref[n]) + bx_ref[:, sl])
        log_a = -LRU_C * r * sp[:, sl]
        a = jnp.exp(log_a)
        a_ref[:, sl] = a
        b_ref[:, sl] = jnp.sqrt(-jnp.tanh(log_a) * (a * a + 1.0)) * (gi * xc)

    row = lax.broadcasted_iota(jnp.int32, (SUBLANES, lanes), 0)

    for l0 in range(0, c, lanes):
        def group(gidx, h):
            r0 = pl.multiple_of(gidx * SUBLANES, SUBLANES)
            av = a_ref[pl.ds(r0, SUBLANES), l0:l0 + lanes]
            bv = b_ref[pl.ds(r0, SUBLANES), l0:l0 + lanes]
            for s in (1, 2, 4):
                a_sh = jnp.where(row >= s, pltpu.roll(av, s, axis=0), 1.0)
                b_sh = jnp.where(row >= s, pltpu.roll(bv, s, axis=0), 0.0)
                bv = av * b_sh + bv
                av = av * a_sh
            hs = av * h + bv
            b_ref[pl.ds(r0, SUBLANES), l0:l0 + lanes] = hs
            return jnp.broadcast_to(hs[SUBLANES - 1:SUBLANES, :], (SUBLANES, lanes))

        h_fin = lax.fori_loop(0, tm // SUBLANES, group, hc_ref[:, l0:l0 + lanes], unroll=2)
        hc_ref[:, l0:l0 + lanes] = h_fin

    gated = (b_ref[...] * ybr_ref[...]).astype(BF16)
    o_ref[...] = x_ref[...] + _dot(gated, wout_ref[...])

    @pl.when(t == pl.num_programs(1) - 1)
    def _():
        hlast_ref[0] = hc_ref[...]


def _lru_tail(x, ybr, xbr, halo, h0, cw, cb, wa, ba, wx, bx, lam, wout, batch, tm, n_taps):
    m, d = x.shape
    c = xbr.shape[1]
    nt = m // batch // tm
    lanes = _tile(c, 512)
    row2 = lambda b, t: (b * nt + t, 0)
    const2 = lambda b, t: (0, 0)
    vec = pl.BlockSpec((1, c), const2)
    vmem = (4 * tm * d * 4 + 4 * tm * c * 4 + 2 * c * d * 2 + 4 * wa.size * 2
            + (tm + LRU_HALO) * c * 4 + 2 * tm * c * 4)
    return pl.pallas_call(
        functools.partial(_lru_tail_kernel, n_taps=n_taps, lanes=lanes),
        name=f"lru_tail_m{m}",
        out_shape=[jax.ShapeDtypeStruct((m, d), F32),
                   jax.ShapeDtypeStruct((batch, SUBLANES, c), F32)],
        grid=(batch, nt),
        in_specs=[pl.BlockSpec((tm, d), row2),
                  pl.BlockSpec((tm, c), row2),
                  pl.BlockSpec((tm, c), row2),
                  pl.BlockSpec((1, LRU_HALO, c), lambda b, t: (b * nt + t, 0, 0)),
                  pl.BlockSpec((1, 1, c), lambda b, t: (b, 0, 0)),
                  pl.BlockSpec(cw.shape, const2),
                  vec,
                  pl.BlockSpec(wa.shape, lambda b, t: (0, 0, 0)),
                  vec,
                  pl.BlockSpec(wx.shape, lambda b, t: (0, 0, 0)),
                  vec, vec,
                  pl.BlockSpec((c, d), const2)],
        out_specs=[pl.BlockSpec((tm, d), row2),
                   pl.BlockSpec((1, SUBLANES, c), lambda b, t: (b, 0, 0))],
        scratch_shapes=[pltpu.VMEM((tm + LRU_HALO, c), F32), pltpu.VMEM((tm, c), F32),
                        pltpu.VMEM((tm, c), F32), pltpu.VMEM((SUBLANES, c), F32)],
        compiler_params=_cparams(("parallel", "arbitrary"), vmem),
    )(x, ybr, xbr, halo, h0, cw, cb, wa, ba, wx, bx, lam, wout)


def _shifted_rows(vec, n_rows):
    return pltpu.roll(jnp.broadcast_to(vec, (n_rows, vec.shape[1])), 0, 1, stride=1, stride_axis=0)


def _block_softmax(s_parts, v_parts):
    m = s_parts[0].max(axis=-1, keepdims=True)
    for s in s_parts[1:]:
        m = jnp.maximum(m, s.max(axis=-1, keepdims=True))
    o, den = None, None
    for s, v in zip(s_parts, v_parts):
        p = jnp.exp(s - m)
        d = p.sum(axis=-1, keepdims=True)
        pv = _dot(p.astype(BF16), v)
        o, den = (pv, d) if o is None else (o + pv, den + d)
    return o, m, den


def _block_softmax_mxu_sum(s_parts, v_parts):
    m = s_parts[0].max(axis=-1, keepdims=True)
    for s in s_parts[1:]:
        m = jnp.maximum(m, s.max(axis=-1, keepdims=True))
    acc = None
    for s, v in zip(s_parts, v_parts):
        pv = _dot(jnp.exp(s - m).astype(BF16), v)
        acc = pv if acc is None else acc + pv
    dh = acc.shape[1] // 2
    return acc[:, :dh], m, acc[:, dh:]


def _merge(state, o_b, m_b, l_b):
    if state is None:
        return o_b, m_b, l_b
    acc, m_old, l_old = state
    m_new = jnp.maximum(m_old, m_b)
    a_old = jnp.exp(m_old - m_new)
    a_b = jnp.exp(m_b - m_new)
    return acc * a_old + o_b * a_b, m_new, l_old * a_old + l_b * a_b


def _attn_prompt_kernel(*refs, heads, dh):
    n_g = len(GROUPS)
    ins = refs[:5 * n_g]
    bvec_ref, o_ref, acc_ref, m_ref, l_ref = refs[5 * n_g:]
    first_chunk = pl.program_id(1) == 0
    h = pl.program_id(2)
    scale = dh ** -0.5
    qi = lax.broadcasted_iota(jnp.int32, (BLK, 2 * BLK), 0)
    kj = lax.broadcasted_iota(jnp.int32, (BLK, 2 * BLK), 1)
    in_band = (kj - qi >= 0) & (kj - qi <= BLK)
    ones = jnp.ones((BLK, dh), BF16)
    for g, (_, dil) in enumerate(GROUPS):
        q_ref, k_ref, v_ref, kp_ref, vp_ref = ins[5 * g:5 * g + 5]
        bias = jnp.where(in_band, _shifted_rows(bvec_ref[pl.ds(g * heads + h, 1), :], BLK), NEG_INF)
        bias_p, bias_c = bias[:, :BLK], bias[:, BLK:]
        for r in range(dil):
            prev_rows = pl.ds(r, BLK, stride=dil)
            k_prev = kp_ref[prev_rows, :].astype(BF16)
            v_prev = jnp.concatenate([vp_ref[prev_rows, :].astype(BF16), ones], axis=1)
            for nb in range(CHUNK // (dil * BLK)):
                rows = pl.ds(r + dil * BLK * nb, BLK, stride=dil)
                q = q_ref[rows, :].astype(BF16)
                k_cur = k_ref[rows, :].astype(BF16)
                v_cur = jnp.concatenate([v_ref[rows, :].astype(BF16), ones], axis=1)
                s_p = _nt_dot(q, k_prev) * scale + bias_p
                if nb == 0:
                    s_p = jnp.where(first_chunk, NEG_INF, s_p)
                s_c = _nt_dot(q, k_cur) * scale + bias_c
                o_b, m_b, l_b = _block_softmax_mxu_sum([s_p, s_c], [v_prev, v_cur])
                m_b = jnp.broadcast_to(m_b, (BLK, dh))
                state = None if g == 0 else (acc_ref[rows, :], m_ref[rows, :], l_ref[rows, :])
                acc, m_new, l_new = _merge(state, o_b, m_b, l_b)
                if g == n_g - 1:
                    o_ref[rows, :] = acc / l_new
                else:
                    acc_ref[rows, :] = acc
                    m_ref[rows, :] = m_new
                    l_ref[rows, :] = l_new
                k_prev, v_prev = k_cur, v_cur


def _attn_prompt(qkv, bvec, batch, seq, heads, dh):
    n_g = len(GROUPS)
    nc = seq // CHUNK
    col = lambda comp, g: (lambda b, n, h: (b * nc + n, (comp * n_g + g) * heads + h))
    in_specs, args = [], []
    for g, (win, _) in enumerate(GROUPS):
        per = CHUNK // win
        prev = lambda comp, g=g, per=per: (
            lambda b, n, h: (jnp.maximum((b * nc + n) * per - 1, 0), (comp * n_g + g) * heads + h))
        in_specs += [pl.BlockSpec((CHUNK, dh), col(0, g)), pl.BlockSpec((CHUNK, dh), col(1, g)),
                     pl.BlockSpec((CHUNK, dh), col(2, g)),
                     pl.BlockSpec((win, dh), prev(1)), pl.BlockSpec((win, dh), prev(2))]
        args += [qkv] * 5
    in_specs.append(pl.BlockSpec(bvec.shape, lambda b, n, h: (0, 0)))
    vmem = 2 * (9 * CHUNK + 2 * sum(w for w, _ in GROUPS)) * dh * 4 + 2 * CHUNK * dh * 4 + 3 * CHUNK * dh * 4
    return pl.pallas_call(
        functools.partial(_attn_prompt_kernel, heads=heads, dh=dh),
        name="attn_prompt",
        out_shape=jax.ShapeDtypeStruct((batch * seq, heads * dh), F32),
        grid=(batch, nc, heads),
        in_specs=in_specs,
        out_specs=pl.BlockSpec((CHUNK, dh), lambda b, n, h: (b * nc + n, h)),
        scratch_shapes=[pltpu.VMEM((CHUNK, dh), F32)] * 3,
        compiler_params=_cparams(("parallel", "parallel", "parallel"), vmem),
    )(*args, bvec)


def _attn_step_kernel(*refs, heads, dh, n_bufs):
    n_g = len(GROUPS)
    ins = refs[:6 * n_g]
    o_ref, acc_ref, m_ref, l_ref, knp_ref, vnp_ref = refs[6 * n_g:]
    part = pl.program_id(1)
    t = o_ref.shape[1]
    scale = dh ** -0.5
    row_stride = 2 * heads

    def merge_into(hs, o_b, m_b, l_b):
        state = (acc_ref[:, hs], m_ref[:, hs], l_ref[:, hs])
        acc, m_new, l_new = _merge(state, o_b, jnp.broadcast_to(m_b, (t, dh)), jnp.broadcast_to(l_b, (t, dh)))
        acc_ref[:, hs] = acc
        m_ref[:, hs] = m_new
        l_ref[:, hs] = l_new

    def cache_piece(g, first_row):
        dil = GROUPS[g][1]
        q_ref, _, _, cache_ref, yc_ref, _ = ins[6 * g:6 * g + 6]
        rows = cache_ref.shape[1] // row_stride
        tc = lax.broadcasted_iota(jnp.int32, (t, rows), 0)
        cc = lax.broadcasted_iota(jnp.int32, (t, rows), 1) + first_row
        dist_c = n_bufs[g] + tc - cc
        ok_c = ((dist_c & (dil - 1)) == 0) & (dist_c <= (N_DIL_KEYS - 1) * dil)
        for h in range(heads):
            hs = slice(h * dh, (h + 1) * dh)
            q = q_ref[0, :, hs].astype(BF16)
            kc = cache_ref[0, pl.ds(h, rows, stride=row_stride), :].astype(BF16)
            vc = cache_ref[0, pl.ds(heads + h, rows, stride=row_stride), :].astype(BF16)
            bias = _shifted_rows(yc_ref[0, h:h + 1, :], t)[:, BLK:]
            s_c = jnp.where(ok_c, _nt_dot(q, kc) * scale + bias, NEG_INF)
            merge_into(hs, *_block_softmax([s_c], [vc]))

    @pl.when(part == 0)
    def _():
        acc_ref[...] = jnp.zeros_like(acc_ref)
        l_ref[...] = jnp.zeros_like(l_ref)
        m_ref[...] = jnp.full_like(m_ref, NEG_INF)
        tn = lax.broadcasted_iota(jnp.int32, (t, BLK), 0)
        cn = lax.broadcasted_iota(jnp.int32, (t, BLK), 1)
        dist_n = tn - cn
        for g, (_, dil) in enumerate(GROUPS):
            q_ref, kn_ref, vn_ref, _, _, yn_ref = ins[6 * g:6 * g + 6]
            ok_n = (dist_n >= 0) & ((dist_n & (dil - 1)) == 0) & (dist_n <= (N_DIL_KEYS - 1) * dil)
            knp_ref[...] = jnp.zeros_like(knp_ref)
            vnp_ref[...] = jnp.zeros_like(vnp_ref)
            knp_ref[0:t, :] = kn_ref[0]
            vnp_ref[0:t, :] = vn_ref[0]
            for h in range(heads):
                hs = slice(h * dh, (h + 1) * dh)
                q = q_ref[0, :, hs].astype(BF16)
                bias = _shifted_rows(yn_ref[h:h + 1, :], t)[:, BLK:]
                s_n = jnp.where(ok_n, _nt_dot(q, knp_ref[:, hs].astype(BF16)) * scale + bias, NEG_INF)
                merge_into(hs, *_block_softmax([s_n], [vnp_ref[:, hs].astype(BF16)]))
        for g in range(n_g):
            if ins[6 * g + 3].shape[1] == n_bufs[g] * row_stride:
                cache_piece(g, 0)

    for g in range(n_g):
        rows = ins[6 * g + 3].shape[1] // row_stride
        if rows != n_bufs[g]:
            cache_piece(g, part * rows)

    @pl.when(part == pl.num_programs(1) - 1)
    def _():
        o_ref[0] = acc_ref[...] / l_ref[...]


def _attn_step(qkv, caches, rel_bias, batch, t, heads, dh, n_parts=2):
    n_g = len(GROUPS)
    width = heads * dh
    qkv3 = qkv.reshape(batch, t, qkv.shape[1])
    in_specs, args, n_bufs = [], [], []
    vmem = 8 * t * width * 4
    for g in range(n_g):
        n_buf = caches[g].shape[1]
        parts = n_parts if n_buf % (n_parts * BLK) == 0 else 1
        rows = n_buf // parts
        flat = caches[g].reshape(batch, n_buf * 2 * heads, dh)
        yc, yn = _step_bias_vectors(rel_bias, g, heads, n_buf, parts)
        new = lambda comp, g=g: pl.BlockSpec((1, t, width), lambda b, s: (b, 0, comp * n_g + g))
        part_of = (lambda s: s) if parts == n_parts else (lambda s: 0)
        in_specs += [new(0), new(1), new(2),
                     pl.BlockSpec((1, rows * 2 * heads, dh), lambda b, s, f=part_of: (b, f(s), 0)),
                     pl.BlockSpec((1, heads, rows + BLK), lambda b, s, f=part_of: (f(s), 0, 0)),
                     pl.BlockSpec(yn.shape, lambda b, s: (0, 0))]
        args += [qkv3, qkv3, qkv3, flat, yc, yn]
        n_bufs.append(n_buf)
        vmem += 2 * rows * 2 * width * 4 + 6 * rows * dh * 4
    return pl.pallas_call(
        functools.partial(_attn_step_kernel, heads=heads, dh=dh, n_bufs=tuple(n_bufs)),
        name="attn_step",
        out_shape=jax.ShapeDtypeStruct((batch, t, width), F32),
        grid=(batch, n_parts),
        in_specs=in_specs,
        out_specs=pl.BlockSpec((1, t, width), lambda b, s: (b, 0, 0)),
        scratch_shapes=[pltpu.VMEM((t, width), F32)] * 3 + [pltpu.VMEM((BLK, width), F32)] * 2,
        compiler_params=_cparams(("parallel", "arbitrary"), vmem),
    )(*args).reshape(batch * t, width)


def _attn_out_kernel(x_ref, o_ref, wo_ref, out_ref):
    out_ref[...] = x_ref[...] + _dot(o_ref[...].astype(BF16), wo_ref[...])


def _attn_out(x, o, wo, tm_pref=512):
    m, d = x.shape
    width = wo.shape[0]
    tm = _tile(m, tm_pref)
    row = lambda n: pl.BlockSpec((tm, n), lambda i: (i, 0))
    vmem = 4 * tm * d * 4 + 2 * tm * width * 4 + 2 * width * d * 2
    return pl.pallas_call(
        _attn_out_kernel,
        name=f"attn_out_m{m}",
        out_shape=jax.ShapeDtypeStruct((m, d), F32),
        grid=(m // tm,),
        in_specs=[row(d), row(width), pl.BlockSpec((width, d), lambda i: (0, 0))],
        out_specs=row(d),
        compiler_params=_cparams(("parallel",), vmem),
    )(x, o, wo)


def _t5_bucket(dist):
    max_exact = N_BUCKETS // 2
    d = np.asarray(dist, dtype=np.int32)
    large = max_exact + (np.log(np.maximum(d, max_exact) / max_exact) / np.log(MAX_DISTANCE / max_exact)
                         * (N_BUCKETS - max_exact)).astype(np.int32)
    return np.where(d < max_exact, d, np.minimum(large, N_BUCKETS - 1)).astype(np.int32)


def _group_bias(rel_bias, g, heads):
    buckets = _t5_bucket(np.arange(N_DIL_KEYS) * GROUPS[g][1])
    return rel_bias[:, g * heads:(g + 1) * heads].astype(F32)[buckets]


def _prompt_bias_vectors(rel_bias, heads):
    vecs = []
    for g in range(len(GROUPS)):
        rev = _group_bias(rel_bias, g, heads)[::-1]
        vecs.append(jnp.pad(rev, ((0, 2 * BLK - N_DIL_KEYS), (0, 0))).T)
    return jnp.concatenate(vecs, axis=0)


def _step_bias_vectors(rel_bias, g, heads, n_buf, parts):
    rows = n_buf // parts
    bias = _group_bias(rel_bias, g, heads)
    at = lambda dist: bias[np.clip(dist // GROUPS[g][1], 0, N_DIL_KEYS - 1)]
    first = np.arange(parts)[:, None] * rows + np.arange(rows + BLK)[None, :]
    yc = jnp.transpose(at(n_buf + BLK - first), (0, 2, 1))
    yn = at(BLK - np.arange(2 * BLK)).T
    return yc, yn


def _kv_rows(qkv, g, batch, seq, n_rows, heads, dh):
    n_g = len(GROUPS)
    width = heads * dh
    q3 = qkv.reshape(batch, seq, qkv.shape[1])[:, seq - n_rows:]
    k = q3[:, :, (n_g + g) * width:(n_g + g + 1) * width]
    v = q3[:, :, (2 * n_g + g) * width:(2 * n_g + g + 1) * width]
    return jnp.stack([k, v], axis=2).reshape(batch, n_rows, 2, heads, dh)


def _trunk(x3, st, p, tiles, ffn_bf16, emit_ffn):
    batch, seq, d = x3.shape
    m = batch * seq
    x = x3.reshape(m, d)
    new = []
    depth = p['ffn_w1'].shape[0]
    for i in range(depth):
        kind = i % N_MIXERS
        g_mix = p['norm_mix_g'][i][None]
        if kind == 0:
            j = i // N_MIXERS
            n_taps, c = p['conv_dw_w'].shape[1:]
            buf = jnp.zeros((batch, n_taps - 1, c), F32) if st is None else st[i][0]
            tm = _tile(seq, tiles['conv_m'])
            dww = jnp.repeat(p['conv_dw_w'][j], SUBLANES, axis=0)
            conv_vecs = (p['conv_dw_b'][j][None], p['conv_ln_g'][j][None], p['conv_ln_b'][j][None])
            if seq % tm == 0 and seq // tm > 1:
                x, state = _conv_block(x, g_mix, p['conv_w_glu'], p['conv_w_out'], j, buf, dww, *conv_vecs,
                                       seq, tm, n_taps)
            else:
                u = _norm_proj(x, g_mix, p['conv_w_glu'], 'glu', tiles['proj_m'], layer=j)
                u3 = u.reshape(batch, seq, c)
                halo = _halos(u3, buf, tm, CONV_HALO)
                x = _conv_tail(x, u, halo, dww, *conv_vecs, p['conv_w_out'], j, tm, n_taps)
                state = _last_rows(buf, u3, n_taps - 1)
            new.append((state,))
        elif kind == 1:
            n_taps, c = p['lru_conv_w'].shape
            if st is None:
                h0 = jnp.zeros((batch, c), F32)
                cb = jnp.zeros((batch, n_taps - 1, c), F32)
            else:
                h0, cb = st[i]
            ybr, xbr = _norm_proj(x, g_mix, p['lru_w_in'], 'split', tiles['proj_m'])
            xbr3 = xbr.reshape(batch, seq, c)
            tm = _tile(seq, tiles['lru_m'])
            halo = _halos(xbr3, cb, tm, LRU_HALO)
            cw = jnp.pad(p['lru_conv_w'], ((0, SUBLANES - n_taps), (0, 0)))
            x, hl = _lru_tail(x, ybr, xbr, halo, h0[:, None], cw, p['lru_conv_b'][None], p['lru_w_a'],
                              p['lru_b_a'][None], p['lru_w_x'], p['lru_b_x'][None], p['lru_lambda'][None],
                              p['lru_w_out'], batch, tm, n_taps)
            new.append((hl[:, 0], _last_rows(cb, xbr3, n_taps - 1)))
        else:
            rel_bias = p['rel_bias']
            heads = rel_bias.shape[1] // len(GROUPS)
            width = p['att_w_o'].shape[0]
            dh = width // heads
            qkv = _norm_proj(x, g_mix, p['att_w_qkv'], 'plain', tiles['proj_m'], tn_pref=1024)
            if st is None:
                assert seq % CHUNK == 0
                o = _attn_prompt(qkv, _prompt_bias_vectors(rel_bias, heads), batch, seq, heads, dh)
                rows = [_kv_rows(qkv, g, batch, seq, min(win, seq), heads, dh) for g, (win, _) in enumerate(GROUPS)]
            else:
                assert seq <= BLK
                o = _attn_step(qkv, st[i], rel_bias, batch, seq, heads, dh)
                rows = [_kv_rows(qkv, g, batch, seq, seq, heads, dh) for g in range(len(GROUPS))]
            x = _attn_out(x, o, p['att_w_o'], tiles['attn_out_m'])
            new.append(rows)
        g_ffn, g_fin, final = p['norm_ffn_g'][i][None], p['final_norm_g'][None], i == depth - 1
        if emit_ffn:
            x, w1b, w2b = _ffn(x, g_ffn, p['ffn_w1'], p['ffn_w2'], i, g_fin, final, tm_pref=tiles['ffn_m'],
                               tf_pref=tiles['ffn_f'], emit=True, rows=m)
            ffn_bf16.append((w1b, w2b))
        else:
            x = _ffn(x, g_ffn, *ffn_bf16[i], i, g_fin, final, tm_pref=tiles['ffn_m'], tf_pref=tiles['ffn_f'],
                     rows=m)
    return x.reshape(batch, seq, d), new


PROMPT_TILES = dict(proj_m=1024, conv_m=256, lru_m=256, attn_out_m=512, ffn_m=512, ffn_f=1024)
SAMPLE_TILES = dict(proj_m=512, conv_m=256, lru_m=256, attn_out_m=512, ffn_m=512, ffn_f=512)
MATMUL_WEIGHTS = ('conv_w_glu', 'conv_w_out', 'lru_w_in', 'lru_w_a', 'lru_w_x', 'lru_w_out', 'att_w_qkv', 'att_w_o')


def kernel(x_prompt, x_sample, state_conv_l0, state_lru_h_l1, state_lru_conv_l1, cache_kv_w128_l2, cache_kv_w512_l2, cache_kv_w2048_l2, state_conv_l3, norm_mix_g, norm_ffn_g, final_norm_g, ffn_w1, ffn_w2, conv_w_glu, conv_dw_w, conv_dw_b, conv_ln_g, conv_ln_b, conv_w_out, lru_w_in, lru_conv_w, lru_conv_b, lru_w_a, lru_b_a, lru_w_x, lru_b_x, lru_lambda, lru_w_out, att_w_qkv, att_w_o, rel_bias):
    p = dict(norm_mix_g=norm_mix_g, norm_ffn_g=norm_ffn_g, final_norm_g=final_norm_g, ffn_w1=ffn_w1,
             ffn_w2=ffn_w2, conv_w_glu=conv_w_glu, conv_dw_w=conv_dw_w, conv_dw_b=conv_dw_b,
             conv_ln_g=conv_ln_g, conv_ln_b=conv_ln_b, conv_w_out=conv_w_out, lru_w_in=lru_w_in,
             lru_conv_w=lru_conv_w, lru_conv_b=lru_conv_b, lru_w_a=lru_w_a, lru_b_a=lru_b_a, lru_w_x=lru_w_x,
             lru_b_x=lru_b_x, lru_lambda=lru_lambda, lru_w_out=lru_w_out, att_w_qkv=att_w_qkv,
             att_w_o=att_w_o, rel_bias=rel_bias)
    for name in MATMUL_WEIGHTS:
        p[name] = p[name].astype(BF16)
    ffn_bf16 = []
    y_sample, sst = _trunk(x_sample, [(state_conv_l0,), (state_lru_h_l1, state_lru_conv_l1),
                                      (cache_kv_w128_l2, cache_kv_w512_l2, cache_kv_w2048_l2),
                                      (state_conv_l3,)], p, SAMPLE_TILES, ffn_bf16, emit_ffn=True)
    y_prompt, pst = _trunk(x_prompt, None, p, PROMPT_TILES, ffn_bf16, emit_ffn=False)
    return (y_prompt, y_sample,
            pst[0][0], sst[0][0],
            pst[1][0], sst[1][0],
            pst[1][1], sst[1][1],
            pst[2][0], sst[2][0],
            pst[2][1], sst[2][1],
            pst[2][2], sst[2][2],
            pst[3][0], sst[3][0])
```

```python
import functools

import numpy as np
import jax
import jax.numpy as jnp
from jax import lax
from jax.experimental import pallas as pl
from jax.experimental.pallas import tpu as pltpu

RMS_EPS = 1e-6
LN_EPS = 1e-5
LRU_C = 8.0
NEG_INF = -1e30
N_MIXERS = 3
GROUPS = ((128, 1), (512, 4), (2048, 16))
N_DIL_KEYS = 129
BLK = 128
CHUNK = BLK * max(d for _, d in GROUPS)
N_BUCKETS = 32
MAX_DISTANCE = 2048

V7X_VMEM_BYTES = 64 * 1024 * 1024
SUBLANES = 8
CONV_HALO = 32
LRU_HALO = 8

BF16 = jnp.bfloat16
F32 = jnp.float32


def _cparams(semantics, vmem_bytes):
    limit = int(min(max(vmem_bytes * 3 // 2, 32 * 1024 * 1024), V7X_VMEM_BYTES - 8 * 1024 * 1024))
    return pltpu.CompilerParams(dimension_semantics=semantics, vmem_limit_bytes=limit)


def _tile(n, pref):
    if n <= pref:
        return n
    t = pref
    while n % t:
        t //= 2
    return t


def _rms_to_bf16(x, g):
    ms = jnp.mean(x * x, axis=-1, keepdims=True)
    return (x * lax.rsqrt(ms + RMS_EPS) * g).astype(BF16)


def _dot(a, b):
    return jnp.dot(a, b, preferred_element_type=F32)


def _nt_dot(a, b):
    return lax.dot_general(a, b, (((1,), (1,)), ((), ())), preferred_element_type=F32)


def _norm_proj_kernel(x_ref, g_ref, *refs, mode):
    if mode == 'plain':
        w_ref, o_ref, h_ref = refs
    elif mode == 'glu':
        wa_ref, wg_ref, o_ref, h_ref = refs
    else:
        wa_ref, wg_ref, o_ref, o2_ref, h_ref = refs

    @pl.when(pl.program_id(1) == 0)
    def _():
        h_ref[...] = _rms_to_bf16(x_ref[...], g_ref[...])

    h = h_ref[...]
    if mode == 'plain':
        o_ref[...] = _dot(h, w_ref[...])
    elif mode == 'glu':
        o_ref[...] = _dot(h, wa_ref[...]) * jax.nn.sigmoid(_dot(h, wg_ref[...]))
    else:
        o_ref[...] = jax.nn.gelu(_dot(h, wa_ref[...]), approximate=True)
        o2_ref[...] = _dot(h, wg_ref[...])


def _wspec(w, layer, block, index_map):
    if w.ndim == 2:
        return pl.BlockSpec(block, index_map)
    return pl.BlockSpec((None,) + block, lambda *ids: (layer,) + index_map(*ids))


def _norm_proj(x, g, w, mode, tm_pref=512, tn_pref=512, layer=0):
    m, d = x.shape
    n = w.shape[-1]
    n_out = n if mode == 'plain' else n // 2
    tm = _tile(m, tm_pref)
    tn = _tile(n_out, tn_pref)
    nj = n_out // tn
    in_specs = [pl.BlockSpec((tm, d), lambda i, j: (i, 0)),
                pl.BlockSpec((1, d), lambda i, j: (0, 0)),
                _wspec(w, layer, (d, tn), lambda i, j: (0, j))]
    args = [x, g, w]
    if mode != 'plain':
        in_specs.append(_wspec(w, layer, (d, tn), lambda i, j: (0, j + nj)))
        args.append(w)
    o_spec = pl.BlockSpec((tm, tn), lambda i, j: (i, j))
    o_shape = jax.ShapeDtypeStruct((m, n_out), F32)
    n_o = 2 if mode == 'split' else 1
    vmem = 2 * tm * d * 4 + tm * d * 2 + 2 * len(args[2:]) * d * tn * 2 + 2 * n_o * tm * tn * 4
    return pl.pallas_call(
        functools.partial(_norm_proj_kernel, mode=mode),
        name=f"norm_proj_{mode}_m{m}",
        out_shape=[o_shape] * n_o if n_o == 2 else o_shape,
        grid=(m // tm, nj),
        in_specs=in_specs,
        out_specs=[o_spec] * n_o if n_o == 2 else o_spec,
        scratch_shapes=[pltpu.VMEM((tm, d), BF16)],
        compiler_params=_cparams(("parallel", "arbitrary"), vmem),
    )(*args)


def _ffn_kernel(x_ref, g_ref, w1_ref, w2_ref, fg_ref, o_ref, *rest, final, emit):
    if emit:
        w1b_ref, w2b_ref, h_ref = rest
    else:
        (h_ref,) = rest
    f = pl.program_id(1)

    @pl.when(f == 0)
    def _():
        x = x_ref[...]
        h_ref[...] = _rms_to_bf16(x, g_ref[...])
        o_ref[...] = x

    w1 = w1_ref[...].astype(BF16)
    w2 = w2_ref[...].astype(BF16)
    if emit:
        w1b_ref[...] = w1
        w2b_ref[...] = w2
    hid = jnp.square(jnp.maximum(_dot(h_ref[...], w1), 0.0)).astype(BF16)
    o_ref[...] += _dot(hid, w2)

    if final:
        @pl.when(f == pl.num_programs(1) - 1)
        def _():
            y = o_ref[...]
            ms = jnp.mean(y * y, axis=-1, keepdims=True)
            o_ref[...] = y * lax.rsqrt(ms + RMS_EPS) * fg_ref[...]


def _ffn(x, g, w1, w2, layer, final_g, final, tm_pref=512, tf_pref=1024, emit=False, rows=None):
    m, d = (x.shape[0] if rows is None else rows), x.shape[1]
    dff = w1.shape[-1]
    tm = _tile(m, tm_pref)
    tf = _tile(dff, tf_pref)
    wbytes = w1.dtype.itemsize
    vmem = (2 * tm * d * 4 + tm * d * 2 + 2 * 2 * d * tf * wbytes + 2 * tm * d * 4 + 2 * tm * tf * 4
            + (6 * d * tf * 2 if emit else 0))
    o_shape = jax.ShapeDtypeStruct((m, d), F32)
    o_spec = pl.BlockSpec((tm, d), lambda i, f: (i, 0))
    if emit:
        assert m == tm, "each weight block must be visited once"
        o_shape = [o_shape, jax.ShapeDtypeStruct((d, dff), BF16), jax.ShapeDtypeStruct((dff, d), BF16)]
        o_spec = [o_spec, pl.BlockSpec((d, tf), lambda i, f: (0, f)), pl.BlockSpec((tf, d), lambda i, f: (f, 0))]
    return pl.pallas_call(
        functools.partial(_ffn_kernel, final=final, emit=emit),
        name=f"ffn_m{m}_l{layer}",
        out_shape=o_shape,
        grid=(m // tm, dff // tf),
        in_specs=[pl.BlockSpec((tm, d), lambda i, f: (i, 0)),
                  pl.BlockSpec((1, d), lambda i, f: (0, 0)),
                  _wspec(w1, layer, (d, tf), lambda i, f: (0, f)),
                  _wspec(w2, layer, (tf, d), lambda i, f: (f, 0)),
                  pl.BlockSpec((1, d), lambda i, f: (0, 0))],
        out_specs=o_spec,
        scratch_shapes=[pltpu.VMEM((tm, d), BF16)],
        compiler_params=_cparams(("parallel", "arbitrary"), vmem),
    )(x, g, w1, w2, final_g)


def _conv_tail_kernel(x_ref, u_ref, halo_ref, dww_ref, dwb_ref, lng_ref, lnb_ref, wout_ref,
                      o_ref, uext_ref, y_ref, *, n_taps, rows, ln_rows, lanes):
    i = pl.program_id(0)
    tm, c = u_ref.shape
    cur = lax.rem(i, 2)

    @pl.when(i == 0)
    def _():
        y_ref[1] = jnp.zeros((tm, c), BF16)

    uext_ref[0:CONV_HALO, :] = halo_ref[0]
    uext_ref[CONV_HALO:CONV_HALO + tm, :] = u_ref[...]
    off = CONV_HALO - (n_taps - 1)

    def row_chunk(r, carry):
        r0 = pl.multiple_of(r * rows, rows)
        for l0 in range(0, c, lanes):
            win = uext_ref[pl.ds(r0, rows + CONV_HALO), l0:l0 + lanes]
            accs = [jnp.zeros((SUBLANES, lanes), F32)] * (rows // SUBLANES)
            for s in range(SUBLANES):
                ws = win if s == 0 else pltpu.roll(win, rows + CONV_HALO - s, axis=0)
                for a in range((CONV_HALO + SUBLANES) // SUBLANES):
                    k = a * SUBLANES + s - off
                    if 0 <= k < n_taps:
                        wk = dww_ref[k * SUBLANES:(k + 1) * SUBLANES, l0:l0 + lanes]
                        accs = [acc + wk * ws[(a + j) * SUBLANES:(a + j + 1) * SUBLANES]
                                for j, acc in enumerate(accs)]
            for j, acc in enumerate(accs):
                uext_ref[pl.ds(r0 + j * SUBLANES, SUBLANES), l0:l0 + lanes] = acc + dwb_ref[:, l0:l0 + lanes]
        return carry

    @pl.when(i < pl.num_programs(0) - 1)
    def _():
        lax.fori_loop(0, tm // rows, row_chunk, 0)

    o_ref[...] = x_ref[...] + _dot(y_ref[1 - cur], wout_ref[...])
    for r0 in range(0, tm, ln_rows):
        cv = uext_ref[r0:r0 + ln_rows, :]
        mu = jnp.mean(cv, axis=-1, keepdims=True)
        cc = cv - mu
        var = jnp.mean(cc * cc, axis=-1, keepdims=True)
        y = cc * lax.rsqrt(var + LN_EPS) * lng_ref[...] + lnb_ref[...]
        y_ref[cur, r0:r0 + ln_rows, :] = (y * jax.nn.sigmoid(y)).astype(BF16)


def _conv_tail(x, u, halo, dww, dwb, lng, lnb, wout, layer, tm, n_taps):
    m, d = x.shape
    c = u.shape[1]
    rows = _tile(tm, 32)
    ln_rows = _tile(tm, 16)
    lanes = _tile(c, 256)
    nt = m // tm
    prev = lambda i: (jnp.maximum(i - 1, 0), 0)
    this = lambda i: (jnp.minimum(i, nt - 1), 0)
    vmem = 4 * tm * d * 4 + 2 * tm * c * 4 + 2 * c * d * 2 + (tm + CONV_HALO) * c * 4 + 2 * tm * c * 2
    return pl.pallas_call(
        functools.partial(_conv_tail_kernel, n_taps=n_taps, rows=rows, ln_rows=ln_rows, lanes=lanes),
        name=f"conv_tail_m{m}",
        out_shape=jax.ShapeDtypeStruct((m, d), F32),
        grid=(nt + 1,),
        in_specs=[pl.BlockSpec((tm, d), prev),
                  pl.BlockSpec((tm, c), this),
                  pl.BlockSpec((1, CONV_HALO, c), lambda i: (jnp.minimum(i, nt - 1), 0, 0)),
                  pl.BlockSpec(dww.shape, lambda i: (0, 0)),
                  pl.BlockSpec((1, c), lambda i: (0, 0)),
                  pl.BlockSpec((1, c), lambda i: (0, 0)),
                  pl.BlockSpec((1, c), lambda i: (0, 0)),
                  _wspec(wout, layer, (c, d), lambda i: (0, 0))],
        out_specs=pl.BlockSpec((tm, d), prev),
        scratch_shapes=[pltpu.VMEM((tm + CONV_HALO, c), F32), pltpu.VMEM((2, tm, c), BF16)],
        compiler_params=_cparams(("arbitrary",), vmem),
    )(x, u, halo, dww, dwb, lng, lnb, wout)


def _conv_block_kernel(xp_ref, xo_ref, g_ref, wglu_ref, wout_ref, halo0_ref, dww_ref, dwb_ref, lng_ref,
                       lnb_ref, o_ref, utail_ref, h_ref, uext_ref, y_ref, *, n_taps, rows, tiles_per_seq):
    t, j = pl.program_id(0), pl.program_id(1)
    tm = xp_ref.shape[0]
    n_lb, _, lanes = uext_ref.shape[1:]
    cur = lax.rem(t, 2)
    prev = 1 - cur
    off = CONV_HALO - (n_taps - 1)

    @pl.when(j == 0)
    def _():
        @pl.when(t == 0)
        def _():
            uext_ref[...] = jnp.zeros_like(uext_ref)
            y_ref[...] = jnp.zeros_like(y_ref)

        h_ref[...] = _rms_to_bf16(xp_ref[...], g_ref[...])
        first = lax.rem(t, tiles_per_seq) == 0
        uext_ref[cur, :, 0:CONV_HALO, :] = jnp.where(first, halo0_ref[0], uext_ref[prev, :, tm:tm + CONV_HALO, :])

    o_ref[...] = xo_ref[...] + _dot(y_ref[cur], wout_ref[j])

    r0 = pl.multiple_of(j * rows, rows)
    for lb in range(n_lb):
        win = uext_ref[prev, lb, pl.ds(r0, rows + CONV_HALO), :]
        accs = [jnp.zeros((SUBLANES, lanes), F32)] * (rows // SUBLANES)
        for s in range(SUBLANES):
            ws = win if s == 0 else pltpu.roll(win, rows + CONV_HALO - s, axis=0)
            for a in range((CONV_HALO + SUBLANES) // SUBLANES):
                k = a * SUBLANES + s - off
                if 0 <= k < n_taps:
                    wk = dww_ref[k * SUBLANES:(k + 1) * SUBLANES, lb * lanes:(lb + 1) * lanes]
                    accs = [acc + wk * ws[(a + q) * SUBLANES:(a + q + 1) * SUBLANES] for q, acc in enumerate(accs)]
        for q, acc in enumerate(accs):
            uext_ref[prev, lb, pl.ds(r0 + q * SUBLANES, SUBLANES), :] = acc + dwb_ref[:, lb * lanes:(lb + 1) * lanes]
    ln_rows = 2 * SUBLANES
    for q in range(rows // ln_rows):
        rs = pl.ds(r0 + q * ln_rows, ln_rows)
        cv = jnp.concatenate([uext_ref[prev, lb, rs, :] for lb in range(n_lb)], axis=1)
        mu = jnp.mean(cv, axis=-1, keepdims=True)
        cc = cv - mu
        var = jnp.mean(cc * cc, axis=-1, keepdims=True)
        yv = cc * lax.rsqrt(var + LN_EPS) * lng_ref[...] + lnb_ref[...]
        y_ref[prev, rs, :] = (yv * jax.nn.sigmoid(yv)).astype(BF16)

    h = h_ref[...]
    u = _dot(h, wglu_ref[j]) * jax.nn.sigmoid(_dot(h, wglu_ref[j + n_lb]))
    uext_ref[cur, j, CONV_HALO:CONV_HALO + tm, :] = u
    utail_ref[0, 0] = u[tm - CONV_HALO:, :]


def _conv_block(x, g, w_glu, w_out, layer, buf, dww, dwb, lng, lnb, seq, tm, n_taps):
    m, d = x.shape
    c = w_out.shape[-2]
    batch = m // seq
    rows = 32
    nj = tm // rows
    lanes, cols = c // nj, d // nj
    assert seq % tm == 0 and lanes % BLK == 0 and cols % BLK == 0 and w_glu.shape[-1] == 2 * c
    n_tiles = m // tm
    tps = seq // tm
    halo0 = jnp.pad(buf, ((0, 0), (CONV_HALO - buf.shape[1], 0), (0, 0)))
    halo0 = jnp.transpose(halo0.reshape(batch, CONV_HALO, nj, lanes), (0, 2, 1, 3))
    p_tile = lambda t: jnp.minimum(t, n_tiles - 1)
    o_tile = lambda t: jnp.maximum(t - 2, 0)
    o_dest = lambda t: jnp.where(t >= 2, t - 2, n_tiles + t)
    vec = pl.BlockSpec((1, c), lambda t, j: (0, 0))
    w_glu = w_glu if w_glu.ndim == 2 else w_glu[layer]
    w_out = w_out if w_out.ndim == 2 else w_out[layer]
    w_glu = jnp.transpose(w_glu.reshape(d, 2 * nj, lanes), (1, 0, 2))
    w_out = jnp.transpose(w_out.reshape(c, nj, cols), (1, 0, 2))
    resident = pl.BlockSpec(memory_space=pltpu.VMEM)
    vmem = (2 * tm * d * 4 + 4 * tm * cols * 4 + (w_glu.size + w_out.size) * 2 + 2 * dww.size * 4 + tm * d * 2
            + 2 * nj * (CONV_HALO + tm) * lanes * 4 + 2 * tm * c * 2 + 8 * tm * lanes * 4)
    out, utail = pl.pallas_call(
        functools.partial(_conv_block_kernel, n_taps=n_taps, rows=rows, tiles_per_seq=tps),
        name=f"conv_block_l{layer}",
        out_shape=[jax.ShapeDtypeStruct((m + 2 * tm, d), F32),
                   jax.ShapeDtypeStruct((n_tiles + 2, nj, CONV_HALO, lanes), F32)],
        grid=(n_tiles + 2, nj),
        in_specs=[pl.BlockSpec((tm, d), lambda t, j: (p_tile(t), 0)),
                  pl.BlockSpec((tm, cols), lambda t, j: (o_tile(t), j)),
                  pl.BlockSpec((1, d), lambda t, j: (0, 0)),
                  resident, resident,
                  pl.BlockSpec((1, nj, CONV_HALO, lanes), lambda t, j: (p_tile(t) // tps, 0, 0, 0)),
                  pl.BlockSpec(dww.shape, lambda t, j: (0, 0)),
                  vec, vec, vec],
        out_specs=[pl.BlockSpec((tm, cols), lambda t, j: (o_dest(t), j)),
                   pl.BlockSpec((1, 1, CONV_HALO, lanes), lambda t, j: (t, j, 0, 0))],
        scratch_shapes=[pltpu.VMEM((tm, d), BF16),
                        pltpu.VMEM((2, nj, CONV_HALO + tm, lanes), F32),
                        pltpu.VMEM((2, tm, c), BF16)],
        compiler_params=_cparams(("arbitrary", "arbitrary"), vmem),
    )(x, x, g, w_glu, w_out, halo0, dww, dwb, lng, lnb)
    last = utail[:n_tiles].reshape(batch, tps, nj, CONV_HALO, lanes)[:, tps - 1]
    state = jnp.transpose(last, (0, 2, 1, 3)).reshape(batch, CONV_HALO, c)[:, CONV_HALO - (n_taps - 1):]
    return out, state


def _last_rows(past, seq, n):
    if seq.shape[1] >= n:
        return seq[:, seq.shape[1] - n:]
    return jnp.concatenate([past, seq], axis=1)[:, -n:]


def _halos(seq, past, tm, halo_rows):
    b, l, c = seq.shape
    p = past.shape[1]
    first = jnp.concatenate([jnp.zeros((b, halo_rows - p, c), seq.dtype), past], axis=1)[:, None]
    nt = l // tm
    if nt == 1:
        return first.reshape(b, halo_rows, c)
    rest = seq.reshape(b, nt, tm, c)[:, :-1, tm - halo_rows:]
    return jnp.concatenate([first, rest], axis=1).reshape(b * nt, halo_rows, c)


def _lru_tail_kernel(x_ref, ybr_ref, xbr_ref, halo_ref, h0_ref, cw_ref, cb_ref, wa_ref, ba_ref,
                     wx_ref, bx_ref, lam_ref, wout_ref, o_ref, hlast_ref,
                     xext_ref, a_ref, b_ref, hc_ref, *, n_taps, lanes):
    tm, c = xbr_ref.shape
    n_blocks, blk, _ = wa_ref.shape
    t = pl.program_id(1)

    @pl.when(t == 0)
    def _():
        hc_ref[...] = jnp.broadcast_to(h0_ref[0], hc_ref.shape)

    xext_ref[0:LRU_HALO, :] = halo_ref[0]
    xext_ref[LRU_HALO:LRU_HALO + tm, :] = xbr_ref[...]
    off = LRU_HALO - (n_taps - 1)

    sp = jax.nn.softplus(-lam_ref[...])
    for n in range(n_blocks):
        sl = slice(n * blk, (n + 1) * blk)
        xc = jnp.zeros((tm, blk), F32)
        for k in range(n_taps):
            xc = xc + cw_ref[k:k + 1, sl] * xext_ref[off + k:off + k + tm, sl]
        xc = xc + cb_ref[:, sl]
        xcb = xc.astype(BF16)
        r = jax.nn.sigmoid(_dot(xcb, wa_ref[n]) + ba_ref[:, sl])
        gi = jax.nn.sigmoid(_dot(xcb, wx_ref[n]) + bx_ref[:, sl])
        log_a = -LRU_C * r * sp[:, sl]
        a = jnp.exp(log_a)
        a_ref[:, sl] = a
        b_ref[:, sl] = jnp.sqrt(-jnp.tanh(log_a) * (a * a + 1.0)) * (gi * xc)

    row = lax.broadcasted_iota(jnp.int32, (SUBLANES, lanes), 0)

    for l0 in range(0, c, lanes):
        def group(gidx, h):
            r0 = pl.multiple_of(gidx * SUBLANES, SUBLANES)
            av = a_ref[pl.ds(r0, SUBLANES), l0:l0 + lanes]
            bv = b_ref[pl.ds(r0, SUBLANES), l0:l0 + lanes]
            for s in (1, 2, 4):
                a_sh = jnp.where(row >= s, pltpu.roll(av, s, axis=0), 1.0)
                b_sh = jnp.where(row >= s, pltpu.roll(bv, s, axis=0), 0.0)
                bv = av * b_sh + bv
                av = av * a_sh
            hs = av * h + bv
            b_ref[pl.ds(r0, SUBLANES), l0:l0 + lanes] = hs
            return jnp.broadcast_to(hs[SUBLANES - 1:SUBLANES, :], (SUBLANES, lanes))

        h_fin = lax.fori_loop(0, tm // SUBLANES, group, hc_ref[:, l0:l0 + lanes], unroll=2)
        hc_ref[:, l0:l0 + lanes] = h_fin

    gated = (b_ref[...] * ybr_ref[...]).astype(BF16)
    o_ref[...] = x_ref[...] + _dot(gated, wout_ref[...])

    @pl.when(t == pl.num_programs(1) - 1)
    def _():
        hlast_ref[0] = hc_ref[...]


def _lru_tail(x, ybr, xbr, halo, h0, cw, cb, wa, ba, wx, bx, lam, wout, batch, tm, n_taps):
    m, d = x.shape
    c = xbr.shape[1]
    nt = m // batch // tm
    lanes = _tile(c, 512)
    row2 = lambda b, t: (b * nt + t, 0)
    const2 = lambda b, t: (0, 0)
    vec = pl.BlockSpec((1, c), const2)
    vmem = (4 * tm * d * 4 + 4 * tm * c * 4 + 2 * c * d * 2 + 4 * wa.size * 2
            + (tm + LRU_HALO) * c * 4 + 2 * tm * c * 4)
    return pl.pallas_call(
        functools.partial(_lru_tail_kernel, n_taps=n_taps, lanes=lanes),
        name=f"lru_tail_m{m}",
        out_shape=[jax.ShapeDtypeStruct((m, d), F32),
                   jax.ShapeDtypeStruct((batch, SUBLANES, c), F32)],
        grid=(batch, nt),
        in_specs=[pl.BlockSpec((tm, d), row2),
                  pl.BlockSpec((tm, c), row2),
                  pl.BlockSpec((tm, c), row2),
                  pl.BlockSpec((1, LRU_HALO, c), lambda b, t: (b * nt + t, 0, 0)),
                  pl.BlockSpec((1, 1, c), lambda b, t: (b, 0, 0)),
                  pl.BlockSpec(cw.shape, const2),
                  vec,
                  pl.BlockSpec(wa.shape, lambda b, t: (0, 0, 0)),
                  vec,
                  pl.BlockSpec(wx.shape, lambda b, t: (0, 0, 0)),
                  vec, vec,
                  pl.BlockSpec((c, d), const2)],
        out_specs=[pl.BlockSpec((tm, d), row2),
                   pl.BlockSpec((1, SUBLANES, c), lambda b, t: (b, 0, 0))],
        scratch_shapes=[pltpu.VMEM((tm + LRU_HALO, c), F32), pltpu.VMEM((tm, c), F32),
                        pltpu.VMEM((tm, c), F32), pltpu.VMEM((SUBLANES, c), F32)],
        compiler_params=_cparams(("parallel", "arbitrary"), vmem),
    )(x, ybr, xbr, halo, h0, cw, cb, wa, ba, wx, bx, lam, wout)


def _shifted_rows(vec, n_rows):
    return pltpu.roll(jnp.broadcast_to(vec, (n_rows, vec.shape[1])), 0, 1, stride=1, stride_axis=0)


def _block_softmax(s_parts, v_parts):
    m = s_parts[0].max(axis=-1, keepdims=True)
    for s in s_parts[1:]:
        m = jnp.maximum(m, s.max(axis=-1, keepdims=True))
    o, den = None, None
    for s, v in zip(s_parts, v_parts):
        p = jnp.exp(s - m)
        d = p.sum(axis=-1, keepdims=True)
        pv = _dot(p.astype(BF16), v)
        o, den = (pv, d) if o is None else (o + pv, den + d)
    return o, m, den


def _block_softmax_mxu_sum(s_parts, v_parts):
    m = s_parts[0].max(axis=-1, keepdims=True)
    for s in s_parts[1:]:
        m = jnp.maximum(m, s.max(axis=-1, keepdims=True))
    acc = None
    for s, v in zip(s_parts, v_parts):
        pv = _dot(jnp.exp(s - m).astype(BF16), v)
        acc = pv if acc is None else acc + pv
    dh = acc.shape[1] // 2
    return acc[:, :dh], m, acc[:, dh:]


def _merge(state, o_b, m_b, l_b):
    if state is None:
        return o_b, m_b, l_b
    acc, m_old, l_old = state
    m_new = jnp.maximum(m_old, m_b)
    a_old = jnp.exp(m_old - m_new)
    a_b = jnp.exp(m_b - m_new)
    return acc * a_old + o_b * a_b, m_new, l_old * a_old + l_b * a_b


def _attn_prompt_kernel(*refs, heads, dh):
    n_g = len(GROUPS)
    ins = refs[:5 * n_g]
    bvec_ref, o_ref, acc_ref, m_ref, l_ref = refs[5 * n_g:]
    first_chunk = pl.program_id(1) == 0
    h = pl.program_id(2)
    scale = dh ** -0.5
    qi = lax.broadcasted_iota(jnp.int32, (BLK, 2 * BLK), 0)
    kj = lax.broadcasted_iota(jnp.int32, (BLK, 2 * BLK), 1)
    in_band = (kj - qi >= 0) & (kj - qi <= BLK)
    ones = jnp.ones((BLK, dh), BF16)
    for g, (_, dil) in enumerate(GROUPS):
        q_ref, k_ref, v_ref, kp_ref, vp_ref = ins[5 * g:5 * g + 5]
        bias = jnp.where(in_band, _shifted_rows(bvec_ref[pl.ds(g * heads + h, 1), :], BLK), NEG_INF)
        bias_p, bias_c = bias[:, :BLK], bias[:, BLK:]
        for r in range(dil):
            prev_rows = pl.ds(r, BLK, stride=dil)
            k_prev = kp_ref[prev_rows, :].astype(BF16)
            v_prev = jnp.concatenate([vp_ref[prev_rows, :].astype(BF16), ones], axis=1)
            for nb in range(CHUNK // (dil * BLK)):
                rows = pl.ds(r + dil * BLK * nb, BLK, stride=dil)
                q = q_ref[rows, :].astype(BF16)
                k_cur = k_ref[rows, :].astype(BF16)
                v_cur = jnp.concatenate([v_ref[rows, :].astype(BF16), ones], axis=1)
                s_p = _nt_dot(q, k_prev) * scale + bias_p
                if nb == 0:
                    s_p = jnp.where(first_chunk, NEG_INF, s_p)
                s_c = _nt_dot(q, k_cur) * scale + bias_c
                o_b, m_b, l_b = _block_softmax_mxu_sum([s_p, s_c], [v_prev, v_cur])
                m_b = jnp.broadcast_to(m_b, (BLK, dh))
                state = None if g == 0 else (acc_ref[rows, :], m_ref[rows, :], l_ref[rows, :])
                acc, m_new, l_new = _merge(state, o_b, m_b, l_b)
                if g == n_g - 1:
                    o_ref[rows, :] = acc / l_new
                else:
                    acc_ref[rows, :] = acc
                    m_ref[rows, :] = m_new
                    l_ref[rows, :] = l_new
                k_prev, v_prev = k_cur, v_cur


def _attn_prompt(qkv, bvec, batch, seq, heads, dh):
    n_g = len(GROUPS)
    nc = seq // CHUNK
    col = lambda comp, g: (lambda b, n, h: (b * nc + n, (comp * n_g + g) * heads + h))
    in_specs, args = [], []
    for g, (win, _) in enumerate(GROUPS):
        per = CHUNK // win
        prev = lambda comp, g=g, per=per: (
            lambda b, n, h: (jnp.maximum((b * nc + n) * per - 1, 0), (comp * n_g + g) * heads + h))
        in_specs += [pl.BlockSpec((CHUNK, dh), col(0, g)), pl.BlockSpec((CHUNK, dh), col(1, g)),
                     pl.BlockSpec((CHUNK, dh), col(2, g)),
                     pl.BlockSpec((win, dh), prev(1)), pl.BlockSpec((win, dh), prev(2))]
        args += [qkv] * 5
    in_specs.append(pl.BlockSpec(bvec.shape, lambda b, n, h: (0, 0)))
    vmem = 2 * (9 * CHUNK + 2 * sum(w for w, _ in GROUPS)) * dh * 4 + 2 * CHUNK * dh * 4 + 3 * CHUNK * dh * 4
    return pl.pallas_call(
        functools.partial(_attn_prompt_kernel, heads=heads, dh=dh),
        name="attn_prompt",
        out_shape=jax.ShapeDtypeStruct((batch * seq, heads * dh), F32),
        grid=(batch, nc, heads),
        in_specs=in_specs,
        out_specs=pl.BlockSpec((CHUNK, dh), lambda b, n, h: (b * nc + n, h)),
        scratch_shapes=[pltpu.VMEM((CHUNK, dh), F32)] * 3,
        compiler_params=_cparams(("parallel", "parallel", "parallel"), vmem),
    )(*args, bvec)


def _attn_step_kernel(*refs, heads, dh, n_bufs):
    n_g = len(GROUPS)
    ins = refs[:6 * n_g]
    o_ref, acc_ref, m_ref, l_ref, knp_ref, vnp_ref = refs[6 * n_g:]
    part = pl.program_id(1)
    t = o_ref.shape[1]
    scale = dh ** -0.5
    row_stride = 2 * heads

    def merge_into(hs, o_b, m_b, l_b):
        state = (acc_ref[:, hs], m_ref[:, hs], l_ref[:, hs])
        acc, m_new, l_new = _merge(state, o_b, jnp.broadcast_to(m_b, (t, dh)), jnp.broadcast_to(l_b, (t, dh)))
        acc_ref[:, hs] = acc
        m_ref[:, hs] = m_new
        l_ref[:, hs] = l_new

    def cache_piece(g, first_row):
        dil = GROUPS[g][1]
        q_ref, _, _, cache_ref, yc_ref, _ = ins[6 * g:6 * g + 6]
        rows = cache_ref.shape[1] // row_stride
        tc = lax.broadcasted_iota(jnp.int32, (t, rows), 0)
        cc = lax.broadcasted_iota(jnp.int32, (t, rows), 1) + first_row
        dist_c = n_bufs[g] + tc - cc
        ok_c = ((dist_c & (dil - 1)) == 0) & (dist_c <= (N_DIL_KEYS - 1) * dil)
        for h in range(heads):
            hs = slice(h * dh, (h + 1) * dh)
            q = q_ref[0, :, hs].astype(BF16)
            kc = cache_ref[0, pl.ds(h, rows, stride=row_stride), :].astype(BF16)
            vc = cache_ref[0, pl.ds(heads + h, rows, stride=row_stride), :].astype(BF16)
            bias = _shifted_rows(yc_ref[0, h:h + 1, :], t)[:, BLK:]
            s_c = jnp.where(ok_c, _nt_dot(q, kc) * scale + bias, NEG_INF)
            merge_into(hs, *_block_softmax([s_c], [vc]))

    @pl.when(part == 0)
    def _():
        acc_ref[...] = jnp.zeros_like(acc_ref)
        l_ref[...] = jnp.zeros_like(l_ref)
        m_ref[...] = jnp.full_like(m_ref, NEG_INF)
        tn = lax.broadcasted_iota(jnp.int32, (t, BLK), 0)
        cn = lax.broadcasted_iota(jnp.int32, (t, BLK), 1)
        dist_n = tn - cn
        for g, (_, dil) in enumerate(GROUPS):
            q_ref, kn_ref, vn_ref, _, _, yn_ref = ins[6 * g:6 * g + 6]
            ok_n = (dist_n >= 0) & ((dist_n & (dil - 1)) == 0) & (dist_n <= (N_DIL_KEYS - 1) * dil)
            knp_ref[...] = jnp.zeros_like(knp_ref)
            vnp_ref[...] = jnp.zeros_like(vnp_ref)
            knp_ref[0:t, :] = kn_ref[0]
            vnp_ref[0:t, :] = vn_ref[0]
            for h in range(heads):
                hs = slice(h * dh, (h + 1) * dh)
                q = q_ref[0, :, hs].astype(BF16)
                bias = _shifted_rows(yn_ref[h:h + 1, :], t)[:, BLK:]
                s_n = jnp.where(ok_n, _nt_dot(q, knp_ref[:, hs].astype(BF16)) * scale + bias, NEG_INF)
                merge_into(hs, *_block_softmax([s_n], [vnp_ref[:, hs].astype(BF16)]))
        for g in range(n_g):
            if ins[6 * g + 3].shape[1] == n_bufs[g] * row_stride:
                cache_piece(g, 0)

    for g in range(n_g):
        rows = ins[6 * g + 3].shape[1] // row_stride
        if rows != n_bufs[g]:
            cache_piece(g, part * rows)

    @pl.when(part == pl.num_programs(1) - 1)
    def _():
        o_ref[0] = acc_ref[...] / l_ref[...]


def _attn_step(qkv, caches, rel_bias, batch, t, heads, dh, n_parts=2):
    n_g = len(GROUPS)
    width = heads * dh
    qkv3 = qkv.reshape(batch, t, qkv.shape[1])
    in_specs, args, n_bufs = [], [], []
    vmem = 8 * t * width * 4
    for g in range(n_g):
        n_buf = caches[g].shape[1]
        parts = n_parts if n_buf % (n_parts * BLK) == 0 else 1
        rows = n_buf // parts
        flat = caches[g].reshape(batch, n_buf * 2 * heads, dh)
        yc, yn = _step_bias_vectors(rel_bias, g, heads, n_buf, parts)
        new = lambda comp, g=g: pl.BlockSpec((1, t, width), lambda b, s: (b, 0, comp * n_g + g))
        part_of = (lambda s: s) if parts == n_parts else (lambda s: 0)
        in_specs += [new(0), new(1), new(2),
                     pl.BlockSpec((1, rows * 2 * heads, dh), lambda b, s, f=part_of: (b, f(s), 0)),
                     pl.BlockSpec((1, heads, rows + BLK), lambda b, s, f=part_of: (f(s), 0, 0)),
                     pl.BlockSpec(yn.shape, lambda b, s: (0, 0))]
        args += [qkv3, qkv3, qkv3, flat, yc, yn]
        n_bufs.append(n_buf)
        vmem += 2 * rows * 2 * width * 4 + 6 * rows * dh * 4
    return pl.pallas_call(
        functools.partial(_attn_step_kernel, heads=heads, dh=dh, n_bufs=tuple(n_bufs)),
        name="attn_step",
        out_shape=jax.ShapeDtypeStruct((batch, t, width), F32),
        grid=(batch, n_parts),
        in_specs=in_specs,
        out_specs=pl.BlockSpec((1, t, width), lambda b, s: (b, 0, 0)),
        scratch_shapes=[pltpu.VMEM((t, width), F32)] * 3 + [pltpu.VMEM((BLK, width), F32)] * 2,
        compiler_params=_cparams(("parallel", "arbitrary"), vmem),
    )(*args).reshape(batch * t, width)


def _attn_out_kernel(x_ref, o_ref, wo_ref, out_ref):
    out_ref[...] = x_ref[...] + _dot(o_ref[...].astype(BF16), wo_ref[...])


def _attn_out(x, o, wo, tm_pref=512):
    m, d = x.shape
    width = wo.shape[0]
    tm = _tile(m, tm_pref)
    row = lambda n: pl.BlockSpec((tm, n), lambda i: (i, 0))
    vmem = 4 * tm * d * 4 + 2 * tm * width * 4 + 2 * width * d * 2
    return pl.pallas_call(
        _attn_out_kernel,
        name=f"attn_out_m{m}",
        out_shape=jax.ShapeDtypeStruct((m, d), F32),
        grid=(m // tm,),
        in_specs=[row(d), row(width), pl.BlockSpec((width, d), lambda i: (0, 0))],
        out_specs=row(d),
        compiler_params=_cparams(("parallel",), vmem),
    )(x, o, wo)


def _t5_bucket(dist):
    max_exact = N_BUCKETS // 2
    d = np.asarray(dist, dtype=np.int32)
    large = max_exact + (np.log(np.maximum(d, max_exact) / max_exact) / np.log(MAX_DISTANCE / max_exact)
                         * (N_BUCKETS - max_exact)).astype(np.int32)
    return np.where(d < max_exact, d, np.minimum(large, N_BUCKETS - 1)).astype(np.int32)


def _group_bias(rel_bias, g, heads):
    buckets = _t5_bucket(np.arange(N_DIL_KEYS) * GROUPS[g][1])
    return rel_bias[:, g * heads:(g + 1) * heads].astype(F32)[buckets]


def _prompt_bias_vectors(rel_bias, heads):
    vecs = []
    for g in range(len(GROUPS)):
        rev = _group_bias(rel_bias, g, heads)[::-1]
        vecs.append(jnp.pad(rev, ((0, 2 * BLK - N_DIL_KEYS), (0, 0))).T)
    return jnp.concatenate(vecs, axis=0)


def _step_bias_vectors(rel_bias, g, heads, n_buf, parts):
    rows = n_buf // parts
    bias = _group_bias(rel_bias, g, heads)
    at = lambda dist: bias[np.clip(dist // GROUPS[g][1], 0, N_DIL_KEYS - 1)]
    first = np.arange(parts)[:, None] * rows + np.arange(rows + BLK)[None, :]
    yc = jnp.transpose(at(n_buf + BLK - first), (0, 2, 1))
    yn = at(BLK - np.arange(2 * BLK)).T
    return yc, yn


def _kv_rows(qkv, g, batch, seq, n_rows, heads, dh):
    n_g = len(GROUPS)
    width = heads * dh
    q3 = qkv.reshape(batch, seq, qkv.shape[1])[:, seq - n_rows:]
    k = q3[:, :, (n_g + g) * width:(n_g + g + 1) * width]
    v = q3[:, :, (2 * n_g + g) * width:(2 * n_g + g + 1) * width]
    return jnp.stack([k, v], axis=2).reshape(batch, n_rows, 2, heads, dh)


def _trunk(x3, st, p, tiles, ffn_bf16, emit_ffn):
    batch, seq, d = x3.shape
    m = batch * seq
    x = x3.reshape(m, d)
    new = []
    depth = p['ffn_w1'].shape[0]
    for i in range(depth):
        kind = i % N_MIXERS
        g_mix = p['norm_mix_g'][i][None]
        if kind == 0:
            j = i // N_MIXERS
            n_taps, c = p['conv_dw_w'].shape[1:]
            buf = jnp.zeros((batch, n_taps - 1, c), F32) if st is None else st[i][0]
            tm = _tile(seq, tiles['conv_m'])
            dww = jnp.repeat(p['conv_dw_w'][j], SUBLANES, axis=0)
            conv_vecs = (p['conv_dw_b'][j][None], p['conv_ln_g'][j][None], p['conv_ln_b'][j][None])
            if seq % tm == 0 and seq // tm > 1:
                x, state = _conv_block(x, g_mix, p['conv_w_glu'], p['conv_w_out'], j, buf, dww, *conv_vecs,
                                       seq, tm, n_taps)
            else:
                u = _norm_proj(x, g_mix, p['conv_w_glu'], 'glu', tiles['proj_m'], layer=j)
                u3 = u.reshape(batch, seq, c)
                halo = _halos(u3, buf, tm, CONV_HALO)
                x = _conv_tail(x, u, halo, dww, *conv_vecs, p['conv_w_out'], j, tm, n_taps)
                state = _last_rows(buf, u3, n_taps - 1)
            new.append((state,))
        elif kind == 1:
            n_taps, c = p['lru_conv_w'].shape
            if st is None:
                h0 = jnp.zeros((batch, c), F32)
                cb = jnp.zeros((batch, n_taps - 1, c), F32)
            else:
                h0, cb = st[i]
            ybr, xbr = _norm_proj(x, g_mix, p['lru_w_in'], 'split', tiles['proj_m'])
            xbr3 = xbr.reshape(batch, seq, c)
            tm = _tile(seq, tiles['lru_m'])
            halo = _halos(xbr3, cb, tm, LRU_HALO)
            cw = jnp.pad(p['lru_conv_w'], ((0, SUBLANES - n_taps), (0, 0)))
            x, hl = _lru_tail(x, ybr, xbr, halo, h0[:, None], cw, p['lru_conv_b'][None], p['lru_w_a'],
                              p['lru_b_a'][None], p['lru_w_x'], p['lru_b_x'][None], p['lru_lambda'][None],
                              p['lru_w_out'], batch, tm, n_taps)
            new.append((hl[:, 0], _last_rows(cb, xbr3, n_taps - 1)))
        else:
            rel_bias = p['rel_bias']
            heads = rel_bias.shape[1] // len(GROUPS)
            width = p['att_w_o'].shape[0]
            dh = width // heads
            qkv = _norm_proj(x, g_mix, p['att_w_qkv'], 'plain', tiles['proj_m'], tn_pref=1024)
            if st is None:
                assert seq % CHUNK == 0
                o = _attn_prompt(qkv, _prompt_bias_vectors(rel_bias, heads), batch, seq, heads, dh)
                rows = [_kv_rows(qkv, g, batch, seq, min(win, seq), heads, dh) for g, (win, _) in enumerate(GROUPS)]
            else:
                assert seq <= BLK
                o = _attn_step(qkv, st[i], rel_bias, batch, seq, heads, dh)
                rows = [_kv_rows(qkv, g, batch, seq, seq, heads, dh) for g in range(len(GROUPS))]
            x = _attn_out(x, o, p['att_w_o'], tiles['attn_out_m'])
            new.append(rows)
        g_ffn, g_fin, final = p['norm_ffn_g'][i][None], p['final_norm_g'][None], i == depth - 1
        if emit_ffn:
            x, w1b, w2b = _ffn(x, g_ffn, p['ffn_w1'], p['ffn_w2'], i, g_fin, final, tm_pref=tiles['ffn_m'],
                               tf_pref=tiles['ffn_f'], emit=True, rows=m)
            ffn_bf16.append((w1b, w2b))
        else:
            x = _ffn(x, g_ffn, *ffn_bf16[i], i, g_fin, final, tm_pref=tiles['ffn_m'], tf_pref=tiles['ffn_f'],
                     rows=m)
    return x.reshape(batch, seq, d), new


PROMPT_TILES = dict(proj_m=1024, conv_m=256, lru_m=256, attn_out_m=512, ffn_m=512, ffn_f=1024)
SAMPLE_TILES = dict(proj_m=512, conv_m=256, lru_m=256, attn_out_m=512, ffn_m=512, ffn_f=512)
MATMUL_WEIGHTS = ('conv_w_glu', 'conv_w_out', 'lru_w_in', 'lru_w_a', 'lru_w_x', 'lru_w_out', 'att_w_qkv', 'att_w_o')


def kernel(x_prompt, x_sample, state_conv_l0, state_lru_h_l1, state_lru_conv_l1, cache_kv_w128_l2, cache_kv_w512_l2, cache_kv_w2048_l2, state_conv_l3, norm_mix_g, norm_ffn_g, final_norm_g, ffn_w1, ffn_w2, conv_w_glu, conv_dw_w, conv_dw_b, conv_ln_g, conv_ln_b, conv_w_out, lru_w_in, lru_conv_w, lru_conv_b, lru_w_a, lru_b_a, lru_w_x, lru_b_x, lru_lambda, lru_w_out, att_w_qkv, att_w_o, rel_bias):
    p = dict(norm_mix_g=norm_mix_g, norm_ffn_g=norm_ffn_g, final_norm_g=final_norm_g, ffn_w1=ffn_w1,
             ffn_w2=ffn_w2, conv_w_glu=conv_w_glu, conv_dw_w=conv_dw_w, conv_dw_b=conv_dw_b,
             conv_ln_g=conv_ln_g, conv_ln_b=conv_ln_b, conv_w_out=conv_w_out, lru_w_in=lru_w_in,
             lru_conv_w=lru_conv_w, lru_conv_b=lru_conv_b, lru_w_a=lru_w_a, lru_b_a=lru_b_a, lru_w_x=lru_w_x,
             lru_b_x=lru_b_x, lru_lambda=lru_lambda, lru_w_out=lru_w_out, att_w_qkv=att_w_qkv,
             att_w_o=att_w_o, rel_bias=rel_bias)
    for name in MATMUL_WEIGHTS:
        p[name] = p[name].astype(BF16)
    ffn_bf16 = []
    y_sample, sst = _trunk(x_sample, [(state_conv_l0,), (state_lru_h_l1, state_lru_conv_l1),
                                      (cache_kv_w128_l2, cache_kv_w512_l2, cache_kv_w2048_l2),
                                      (state_conv_l3,)], p, SAMPLE_TILES, ffn_bf16, emit_ffn=True)
    y_prompt, pst = _trunk(x_prompt, None, p, PROMPT_TILES, ffn_bf16, emit_ffn=False)
    return (y_prompt, y_sample,
            pst[0][0], sst[0][0],
            pst[1][0], sst[1][0],
            pst[1][1], sst[1][1],
            pst[2][0], sst[2][0],
            pst[2][1], sst[2][1],
            pst[2][2], sst[2][2],
            pst[3][0], sst[3][0])
```

```python
import functools

import numpy as np
import jax
import jax.numpy as jnp
from jax import lax
from jax.experimental import pallas as pl
from jax.experimental.pallas import tpu as pltpu

RMS_EPS = 1e-6
LN_EPS = 1e-5
LRU_C = 8.0
NEG_INF = -1e30
N_MIXERS = 3
GROUPS = ((128, 1), (512, 4), (2048, 16))
N_DIL_KEYS = 129
BLK = 128
CHUNK = BLK * max(d for _, d in GROUPS)
N_BUCKETS = 32
MAX_DISTANCE = 2048

V7X_VMEM_BYTES = 64 * 1024 * 1024
SUBLANES = 8
CONV_HALO = 32
LRU_HALO = 8

BF16 = jnp.bfloat16
F32 = jnp.float32


def _cparams(semantics, vmem_bytes):
    limit = int(min(max(vmem_bytes * 3 // 2, 32 * 1024 * 1024), V7X_VMEM_BYTES - 8 * 1024 * 1024))
    return pltpu.CompilerParams(dimension_semantics=semantics, vmem_limit_bytes=limit)


def _tile(n, pref):
    if n <= pref:
        return n
    t = pref
    while n % t:
        t //= 2
    return t


def _rms_to_bf16(x, g):
    ms = jnp.mean(x * x, axis=-1, keepdims=True)
    return (x * lax.rsqrt(ms + RMS_EPS) * g).astype(BF16)


def _dot(a, b):
    return jnp.dot(a, b, preferred_element_type=F32)


def _nt_dot(a, b):
    return lax.dot_general(a, b, (((1,), (1,)), ((), ())), preferred_element_type=F32)


def _norm_proj_kernel(x_ref, g_ref, *refs, mode, emit):
    n_w = 1 if mode == 'plain' else 2
    n_o = 2 if mode == 'split' else 1
    w_refs, o_refs = refs[:n_w], refs[n_w:n_w + n_o]
    wb_refs, h_ref = refs[n_w + n_o:-1], refs[-1]

    @pl.when(pl.program_id(1) == 0)
    def _():
        h_ref[...] = _rms_to_bf16(x_ref[...], g_ref[...])

    ws = [w_ref[...].astype(BF16) for w_ref in w_refs]
    if emit:
        for wb_ref, w in zip(wb_refs, ws):
            wb_ref[...] = w
    h = h_ref[...]
    if mode == 'plain':
        o_refs[0][...] = _dot(h, ws[0])
    elif mode == 'glu':
        o_refs[0][...] = _dot(h, ws[0]) * jax.nn.sigmoid(_dot(h, ws[1]))
    else:
        o_refs[0][...] = jax.nn.gelu(_dot(h, ws[0]), approximate=True)
        o_refs[1][...] = _dot(h, ws[1])


def _wspec(w, layer, block, index_map):
    if w.ndim == 2:
        return pl.BlockSpec(block, index_map)
    return pl.BlockSpec((None,) + block, lambda *ids: (layer,) + index_map(*ids))


def _norm_proj(x, g, w, mode, tm_pref=512, tn_pref=512, layer=0, emit=False):
    m, d = x.shape
    halves = mode != 'plain'
    pair = isinstance(w, tuple)
    n_out = w[0].shape[-1] if pair else (w.shape[-1] // 2 if halves else w.shape[-1])
    tm = _tile(m, tm_pref)
    tn = _tile(n_out, tn_pref)
    nj = n_out // tn
    w_args = list(w) if pair else ([w, w] if halves else [w])
    shift = [0, 0] if pair else [0, nj]
    in_specs = [pl.BlockSpec((tm, d), lambda i, j: (i, 0)), pl.BlockSpec((1, d), lambda i, j: (0, 0))]
    in_specs += [_wspec(wk, layer, (d, tn), lambda i, j, s=s: (0, j + s)) for wk, s in zip(w_args, shift)]
    n_o = 2 if mode == 'split' else 1
    o_shapes = [jax.ShapeDtypeStruct((m, n_out), F32)] * n_o
    o_specs = [pl.BlockSpec((tm, tn), lambda i, j: (i, j))] * n_o
    if emit:
        assert m == tm, "each weight block must be visited once"
        o_shapes += [jax.ShapeDtypeStruct((d, n_out), BF16)] * len(w_args)
        o_specs += [pl.BlockSpec((d, tn), lambda i, j: (0, j))] * len(w_args)
    wbytes = w_args[0].dtype.itemsize
    vmem = (2 * tm * d * 4 + tm * d * 2 + 2 * len(w_args) * d * tn * wbytes + 2 * n_o * tm * tn * 4
            + (3 * len(w_args) * d * tn * 2 if emit else 0))
    outs = pl.pallas_call(
        functools.partial(_norm_proj_kernel, mode=mode, emit=emit),
        name=f"norm_proj_{mode}_m{m}",
        out_shape=o_shapes,
        grid=(m // tm, nj),
        in_specs=in_specs,
        out_specs=o_specs,
        scratch_shapes=[pltpu.VMEM((tm, d), BF16)],
        compiler_params=_cparams(("parallel", "arbitrary"), vmem),
    )(x, g, *w_args)
    return outs[0] if len(outs) == 1 else tuple(outs)


def _ffn_kernel(x_ref, g_ref, w1_ref, w2_ref, fg_ref, o_ref, *rest, final, emit):
    if emit:
        w1b_ref, w2b_ref, h_ref = rest
    else:
        (h_ref,) = rest
    f = pl.program_id(1)

    @pl.when(f == 0)
    def _():
        x = x_ref[...]
        h_ref[...] = _rms_to_bf16(x, g_ref[...])
        o_ref[...] = x

    w1 = w1_ref[...].astype(BF16)
    w2 = w2_ref[...].astype(BF16)
    if emit:
        w1b_ref[...] = w1
        w2b_ref[...] = w2
    hid = jnp.square(jnp.maximum(_dot(h_ref[...], w1), 0.0)).astype(BF16)
    o_ref[...] += _dot(hid, w2)

    if final:
        @pl.when(f == pl.num_programs(1) - 1)
        def _():
            y = o_ref[...]
            ms = jnp.mean(y * y, axis=-1, keepdims=True)
            o_ref[...] = y * lax.rsqrt(ms + RMS_EPS) * fg_ref[...]


def _ffn(x, g, w1, w2, layer, final_g, final, tm_pref=512, tf_pref=1024, emit=False):
    m, d = x.shape
    dff = w1.shape[-1]
    tm = _tile(m, tm_pref)
    tf = _tile(dff, tf_pref)
    wbytes = w1.dtype.itemsize
    vmem = (2 * tm * d * 4 + tm * d * 2 + 2 * 2 * d * tf * wbytes + 2 * tm * d * 4 + 2 * tm * tf * 4
            + (6 * d * tf * 2 if emit else 0))
    o_shape = jax.ShapeDtypeStruct((m, d), F32)
    o_spec = pl.BlockSpec((tm, d), lambda i, f: (i, 0))
    if emit:
        assert m == tm, "each weight block must be visited once"
        o_shape = [o_shape, jax.ShapeDtypeStruct((d, dff), BF16), jax.ShapeDtypeStruct((dff, d), BF16)]
        o_spec = [o_spec, pl.BlockSpec((d, tf), lambda i, f: (0, f)), pl.BlockSpec((tf, d), lambda i, f: (f, 0))]
    return pl.pallas_call(
        functools.partial(_ffn_kernel, final=final, emit=emit),
        name=f"ffn_m{m}_l{layer}",
        out_shape=o_shape,
        grid=(m // tm, dff // tf),
        in_specs=[pl.BlockSpec((tm, d), lambda i, f: (i, 0)),
                  pl.BlockSpec((1, d), lambda i, f: (0, 0)),
                  _wspec(w1, layer, (d, tf), lambda i, f: (0, f)),
                  _wspec(w2, layer, (tf, d), lambda i, f: (f, 0)),
                  pl.BlockSpec((1, d), lambda i, f: (0, 0))],
        out_specs=o_spec,
        scratch_shapes=[pltpu.VMEM((tm, d), BF16)],
        compiler_params=_cparams(("parallel", "arbitrary"), vmem),
    )(x, g, w1, w2, final_g)


def _conv_tail_kernel(x_ref, u_ref, halo_ref, dww_ref, dwb_ref, lng_ref, lnb_ref, wout_ref,
                      o_ref, uext_ref, y_ref, *, n_taps, rows, ln_rows, lanes):
    i = pl.program_id(0)
    tm, c = u_ref.shape
    cur = lax.rem(i, 2)

    @pl.when(i == 0)
    def _():
        y_ref[1] = jnp.zeros((tm, c), BF16)

    uext_ref[0:CONV_HALO, :] = halo_ref[0]
    uext_ref[CONV_HALO:CONV_HALO + tm, :] = u_ref[...]
    off = CONV_HALO - (n_taps - 1)

    def row_chunk(r, carry):
        r0 = pl.multiple_of(r * rows, rows)
        for l0 in range(0, c, lanes):
            win = uext_ref[pl.ds(r0, rows + CONV_HALO), l0:l0 + lanes]
            accs = [jnp.zeros((SUBLANES, lanes), F32)] * (rows // SUBLANES)
            for s in range(SUBLANES):
                ws = win if s == 0 else pltpu.roll(win, rows + CONV_HALO - s, axis=0)
                for a in range((CONV_HALO + SUBLANES) // SUBLANES):
                    k = a * SUBLANES + s - off
                    if 0 <= k < n_taps:
                        wk = dww_ref[k * SUBLANES:(k + 1) * SUBLANES, l0:l0 + lanes]
                        accs = [acc + wk * ws[(a + j) * SUBLANES:(a + j + 1) * SUBLANES]
                                for j, acc in enumerate(accs)]
            for j, acc in enumerate(accs):
                uext_ref[pl.ds(r0 + j * SUBLANES, SUBLANES), l0:l0 + lanes] = acc + dwb_ref[:, l0:l0 + lanes]
        return carry

    @pl.when(i < pl.num_programs(0) - 1)
    def _():
        lax.fori_loop(0, tm // rows, row_chunk, 0)

    o_ref[...] = x_ref[...] + _dot(y_ref[1 - cur], wout_ref[...])
    for r0 in range(0, tm, ln_rows):
        cv = uext_ref[r0:r0 + ln_rows, :]
        mu = jnp.mean(cv, axis=-1, keepdims=True)
        cc = cv - mu
        var = jnp.mean(cc * cc, axis=-1, keepdims=True)
        y = cc * lax.rsqrt(var + LN_EPS) * lng_ref[...] + lnb_ref[...]
        y_ref[cur, r0:r0 + ln_rows, :] = (y * jax.nn.sigmoid(y)).astype(BF16)


def _conv_tail(x, u, halo, dww, dwb, lng, lnb, wout, layer, tm, n_taps):
    m, d = x.shape
    c = u.shape[1]
    rows = _tile(tm, 32)
    ln_rows = _tile(tm, 16)
    lanes = _tile(c, 256)
    nt = m // tm
    prev = lambda i: (jnp.maximum(i - 1, 0), 0)
    this = lambda i: (jnp.minimum(i, nt - 1), 0)
    vmem = 4 * tm * d * 4 + 2 * tm * c * 4 + 2 * c * d * 2 + (tm + CONV_HALO) * c * 4 + 2 * tm * c * 2
    return pl.pallas_call(
        functools.partial(_conv_tail_kernel, n_taps=n_taps, rows=rows, ln_rows=ln_rows, lanes=lanes),
        name=f"conv_tail_m{m}",
        out_shape=jax.ShapeDtypeStruct((m, d), F32),
        grid=(nt + 1,),
        in_specs=[pl.BlockSpec((tm, d), prev),
                  pl.BlockSpec((tm, c), this),
                  pl.BlockSpec((1, CONV_HALO, c), lambda i: (jnp.minimum(i, nt - 1), 0, 0)),
                  pl.BlockSpec(dww.shape, lambda i: (0, 0)),
                  pl.BlockSpec((1, c), lambda i: (0, 0)),
                  pl.BlockSpec((1, c), lambda i: (0, 0)),
                  pl.BlockSpec((1, c), lambda i: (0, 0)),
                  _wspec(wout, layer, (c, d), lambda i: (0, 0))],
        out_specs=pl.BlockSpec((tm, d), prev),
        scratch_shapes=[pltpu.VMEM((tm + CONV_HALO, c), F32), pltpu.VMEM((2, tm, c), BF16)],
        compiler_params=_cparams(("arbitrary",), vmem),
    )(x, u, halo, dww, dwb, lng, lnb, wout)


def _last_rows(past, seq, n):
    if seq.shape[1] >= n:
        return seq[:, seq.shape[1] - n:]
    return jnp.concatenate([past, seq], axis=1)[:, -n:]


def _halos(seq, past, tm, halo_rows):
    b, l, c = seq.shape
    p = past.shape[1]
    first = jnp.concatenate([jnp.zeros((b, halo_rows - p, c), seq.dtype), past], axis=1)[:, None]
    nt = l // tm
    if nt == 1:
        return first.reshape(b, halo_rows, c)
    rest = seq.reshape(b, nt, tm, c)[:, :-1, tm - halo_rows:]
    return jnp.concatenate([first, rest], axis=1).reshape(b * nt, halo_rows, c)


def _lru_tail_kernel(x_ref, ybr_ref, xbr_ref, halo_ref, h0_ref, cw_ref, cb_ref, wa_ref, ba_ref,
                     wx_ref, bx_ref, lam_ref, wout_ref, o_ref, hlast_ref,
                     xext_ref, a_ref, b_ref, hc_ref, *, n_taps, lanes):
    tm, c = xbr_ref.shape
    n_blocks, blk, _ = wa_ref.shape
    t = pl.program_id(1)

    @pl.when(t == 0)
    def _():
        hc_ref[...] = jnp.broadcast_to(h0_ref[0], hc_ref.shape)

    xext_ref[0:LRU_HALO, :] = halo_ref[0]
    xext_ref[LRU_HALO:LRU_HALO + tm, :] = xbr_ref[...]
    off = LRU_HALO - (n_taps - 1)

    sp = jax.nn.softplus(-lam_ref[...])
    for n in range(n_blocks):
        sl = slice(n * blk, (n + 1) * blk)
        xc = jnp.zeros((tm, blk), F32)
        for k in range(n_taps):
            xc = xc + cw_ref[k:k + 1, sl] * xext_ref[off + k:off + k + tm, sl]
        xc = xc + cb_ref[:, sl]
        xcb = xc.astype(BF16)
        r = jax.nn.sigmoid(_dot(xcb, wa_ref[n]) + ba_ref[:, sl])
        gi = jax.nn.sigmoid(_dot(xcb, wx_ref[n]) + bx_ref[:, sl])
        log_a = -LRU_C * r * sp[:, sl]
        a = jnp.exp(log_a)
        a_ref[:, sl] = a
        b_ref[:, sl] = jnp.sqrt(-jnp.tanh(log_a) * (a * a + 1.0)) * (gi * xc)

    row = lax.broadcasted_iota(jnp.int32, (SUBLANES, lanes), 0)

    for l0 in range(0, c, lanes):
        def group(gidx, h):
            r0 = pl.multiple_of(gidx * SUBLANES, SUBLANES)
            av = a_ref[pl.ds(r0, SUBLANES), l0:l0 + lanes]
            bv = b_ref[pl.ds(r0, SUBLANES), l0:l0 + lanes]
            for s in (1, 2, 4):
                a_sh = jnp.where(row >= s, pltpu.roll(av, s, axis=0), 1.0)
                b_sh = jnp.where(row >= s, pltpu.roll(bv, s, axis=0), 0.0)
                bv = av * b_sh + bv
                av = av * a_sh
            hs = av * h + bv
            b_ref[pl.ds(r0, SUBLANES), l0:l0 + lanes] = hs
            return jnp.broadcast_to(hs[SUBLANES - 1:SUBLANES, :], (SUBLANES, lanes))

        h_fin = lax.fori_loop(0, tm // SUBLANES, group, hc_ref[:, l0:l0 + lanes], unroll=2)
        hc_ref[:, l0:l0 + lanes] = h_fin

    gated = (b_ref[...] * ybr_ref[...]).astype(BF16)
    o_ref[...] = x_ref[...] + _dot(gated, wout_ref[...])

    @pl.when(t == pl.num_programs(1) - 1)
    def _():
        hlast_ref[0] = hc_ref[...]


def _lru_tail(x, ybr, xbr, halo, h0, cw, cb, wa, ba, wx, bx, lam, wout, batch, tm, n_taps):
    m, d = x.shape
    c = xbr.shape[1]
    nt = m // batch // tm
    lanes = _tile(c, 512)
    row2 = lambda b, t: (b * nt + t, 0)
    const2 = lambda b, t: (0, 0)
    vec = pl.BlockSpec((1, c), const2)
    vmem = (4 * tm * d * 4 + 4 * tm * c * 4 + 2 * c * d * 2 + 4 * wa.size * 2
            + (tm + LRU_HALO) * c * 4 + 2 * tm * c * 4)
    return pl.pallas_call(
        functools.partial(_lru_tail_kernel, n_taps=n_taps, lanes=lanes),
        name=f"lru_tail_m{m}",
        out_shape=[jax.ShapeDtypeStruct((m, d), F32),
                   jax.ShapeDtypeStruct((batch, SUBLANES, c), F32)],
        grid=(batch, nt),
        in_specs=[pl.BlockSpec((tm, d), row2),
                  pl.BlockSpec((tm, c), row2),
                  pl.BlockSpec((tm, c), row2),
                  pl.BlockSpec((1, LRU_HALO, c), lambda b, t: (b * nt + t, 0, 0)),
                  pl.BlockSpec((1, 1, c), lambda b, t: (b, 0, 0)),
                  pl.BlockSpec(cw.shape, const2),
                  vec,
                  pl.BlockSpec(wa.shape, lambda b, t: (0, 0, 0)),
                  vec,
                  pl.BlockSpec(wx.shape, lambda b, t: (0, 0, 0)),
                  vec, vec,
                  pl.BlockSpec((c, d), const2)],
        out_specs=[pl.BlockSpec((tm, d), row2),
                   pl.BlockSpec((1, SUBLANES, c), lambda b, t: (b, 0, 0))],
        scratch_shapes=[pltpu.VMEM((tm + LRU_HALO, c), F32), pltpu.VMEM((tm, c), F32),
                        pltpu.VMEM((tm, c), F32), pltpu.VMEM((SUBLANES, c), F32)],
        compiler_params=_cparams(("parallel", "arbitrary"), vmem),
    )(x, ybr, xbr, halo, h0, cw, cb, wa, ba, wx, bx, lam, wout)


def _shifted_rows(vec, n_rows):
    return pltpu.roll(jnp.broadcast_to(vec, (n_rows, vec.shape[1])), 0, 1, stride=1, stride_axis=0)


def _block_softmax(s_parts, v_parts):
    m = s_parts[0].max(axis=-1, keepdims=True)
    for s in s_parts[1:]:
        m = jnp.maximum(m, s.max(axis=-1, keepdims=True))
    o, den = None, None
    for s, v in zip(s_parts, v_parts):
        p = jnp.exp(s - m)
        d = p.sum(axis=-1, keepdims=True)
        pv = _dot(p.astype(BF16), v)
        o, den = (pv, d) if o is None else (o + pv, den + d)
    return o, m, den


def _block_softmax_mxu_sum(s_parts, v_parts):
    m = s_parts[0].max(axis=-1, keepdims=True)
    for s in s_parts[1:]:
        m = jnp.maximum(m, s.max(axis=-1, keepdims=True))
    acc = None
    for s, v in zip(s_parts, v_parts):
        pv = _dot(jnp.exp(s - m).astype(BF16), v)
        acc = pv if acc is None else acc + pv
    dh = acc.shape[1] // 2
    return acc[:, :dh], m, acc[:, dh:]


def _merge(state, o_b, m_b, l_b):
    if state is None:
        return o_b, m_b, l_b
    acc, m_old, l_old = state
    m_new = jnp.maximum(m_old, m_b)
    a_old = jnp.exp(m_old - m_new)
    a_b = jnp.exp(m_b - m_new)
    return acc * a_old + o_b * a_b, m_new, l_old * a_old + l_b * a_b


def _attn_prompt_kernel(*refs, heads, dh):
    n_g = len(GROUPS)
    ins = refs[:5 * n_g]
    bvec_ref, o_ref, acc_ref, m_ref, l_ref = refs[5 * n_g:]
    first_chunk = pl.program_id(1) == 0
    h = pl.program_id(2)
    scale = dh ** -0.5
    qi = lax.broadcasted_iota(jnp.int32, (BLK, 2 * BLK), 0)
    kj = lax.broadcasted_iota(jnp.int32, (BLK, 2 * BLK), 1)
    in_band = (kj - qi >= 0) & (kj - qi <= BLK)
    ones = jnp.ones((BLK, dh), BF16)
    for g, (_, dil) in enumerate(GROUPS):
        q_ref, k_ref, v_ref, kp_ref, vp_ref = ins[5 * g:5 * g + 5]
        bias = jnp.where(in_band, _shifted_rows(bvec_ref[pl.ds(g * heads + h, 1), :], BLK), NEG_INF)
        bias_p, bias_c = bias[:, :BLK], bias[:, BLK:]
        for r in range(dil):
            prev_rows = pl.ds(r, BLK, stride=dil)
            k_prev = kp_ref[prev_rows, :].astype(BF16)
            v_prev = jnp.concatenate([vp_ref[prev_rows, :].astype(BF16), ones], axis=1)
            for nb in range(CHUNK // (dil * BLK)):
                rows = pl.ds(r + dil * BLK * nb, BLK, stride=dil)
                q = q_ref[rows, :].astype(BF16)
                k_cur = k_ref[rows, :].astype(BF16)
                v_cur = jnp.concatenate([v_ref[rows, :].astype(BF16), ones], axis=1)
                s_p = _nt_dot(q, k_prev) * scale + bias_p
                if nb == 0:
                    s_p = jnp.where(first_chunk, NEG_INF, s_p)
                s_c = _nt_dot(q, k_cur) * scale + bias_c
                o_b, m_b, l_b = _block_softmax_mxu_sum([s_p, s_c], [v_prev, v_cur])
                m_b = jnp.broadcast_to(m_b, (BLK, dh))
                state = None if g == 0 else (acc_ref[rows, :], m_ref[rows, :], l_ref[rows, :])
                acc, m_new, l_new = _merge(state, o_b, m_b, l_b)
                if g == n_g - 1:
                    o_ref[rows, :] = acc / l_new
                else:
                    acc_ref[rows, :] = acc
                    m_ref[rows, :] = m_new
                    l_ref[rows, :] = l_new
                k_prev, v_prev = k_cur, v_cur


def _attn_prompt(qkv, bvec, batch, seq, heads, dh):
    n_g = len(GROUPS)
    nc = seq // CHUNK
    col = lambda comp, g: (lambda b, n, h: (b * nc + n, (comp * n_g + g) * heads + h))
    in_specs, args = [], []
    for g, (win, _) in enumerate(GROUPS):
        per = CHUNK // win
        prev = lambda comp, g=g, per=per: (
            lambda b, n, h: (jnp.maximum((b * nc + n) * per - 1, 0), (comp * n_g + g) * heads + h))
        in_specs += [pl.BlockSpec((CHUNK, dh), col(0, g)), pl.BlockSpec((CHUNK, dh), col(1, g)),
                     pl.BlockSpec((CHUNK, dh), col(2, g)),
                     pl.BlockSpec((win, dh), prev(1)), pl.BlockSpec((win, dh), prev(2))]
        args += [qkv] * 5
    in_specs.append(pl.BlockSpec(bvec.shape, lambda b, n, h: (0, 0)))
    vmem = 2 * (9 * CHUNK + 2 * sum(w for w, _ in GROUPS)) * dh * 4 + 2 * CHUNK * dh * 4 + 3 * CHUNK * dh * 4
    return pl.pallas_call(
        functools.partial(_attn_prompt_kernel, heads=heads, dh=dh),
        name="attn_prompt",
        out_shape=jax.ShapeDtypeStruct((batch * seq, heads * dh), F32),
        grid=(batch, nc, heads),
        in_specs=in_specs,
        out_specs=pl.BlockSpec((CHUNK, dh), lambda b, n, h: (b * nc + n, h)),
        scratch_shapes=[pltpu.VMEM((CHUNK, dh), F32)] * 3,
        compiler_params=_cparams(("parallel", "parallel", "parallel"), vmem),
    )(*args, bvec)


def _attn_step_kernel(*refs, heads, dh, n_bufs):
    n_g = len(GROUPS)
    ins = refs[:6 * n_g]
    o_ref, acc_ref, m_ref, l_ref, knp_ref, vnp_ref = refs[6 * n_g:]
    part = pl.program_id(1)
    t = o_ref.shape[1]
    scale = dh ** -0.5
    row_stride = 2 * heads

    def merge_into(hs, o_b, m_b, l_b):
        state = (acc_ref[:, hs], m_ref[:, hs], l_ref[:, hs])
        acc, m_new, l_new = _merge(state, o_b, jnp.broadcast_to(m_b, (t, dh)), jnp.broadcast_to(l_b, (t, dh)))
        acc_ref[:, hs] = acc
        m_ref[:, hs] = m_new
        l_ref[:, hs] = l_new

    def cache_piece(g, first_row):
        dil = GROUPS[g][1]
        q_ref, _, _, cache_ref, yc_ref, _ = ins[6 * g:6 * g + 6]
        rows = cache_ref.shape[1] // row_stride
        tc = lax.broadcasted_iota(jnp.int32, (t, rows), 0)
        cc = lax.broadcasted_iota(jnp.int32, (t, rows), 1) + first_row
        dist_c = n_bufs[g] + tc - cc
        ok_c = ((dist_c & (dil - 1)) == 0) & (dist_c <= (N_DIL_KEYS - 1) * dil)
        for h in range(heads):
            hs = slice(h * dh, (h + 1) * dh)
            q = q_ref[0, :, hs].astype(BF16)
            kc = cache_ref[0, pl.ds(h, rows, stride=row_stride), :].astype(BF16)
            vc = cache_ref[0, pl.ds(heads + h, rows, stride=row_stride), :].astype(BF16)
            bias = _shifted_rows(yc_ref[0, h:h + 1, :], t)[:, BLK:]
            s_c = jnp.where(ok_c, _nt_dot(q, kc) * scale + bias, NEG_INF)
            merge_into(hs, *_block_softmax([s_c], [vc]))

    @pl.when(part == 0)
    def _():
        acc_ref[...] = jnp.zeros_like(acc_ref)
        l_ref[...] = jnp.zeros_like(l_ref)
        m_ref[...] = jnp.full_like(m_ref, NEG_INF)
        tn = lax.broadcasted_iota(jnp.int32, (t, BLK), 0)
        cn = lax.broadcasted_iota(jnp.int32, (t, BLK), 1)
        dist_n = tn - cn
        for g, (_, dil) in enumerate(GROUPS):
            q_ref, kn_ref, vn_ref, _, _, yn_ref = ins[6 * g:6 * g + 6]
            ok_n = (dist_n >= 0) & ((dist_n & (dil - 1)) == 0) & (dist_n <= (N_DIL_KEYS - 1) * dil)
            knp_ref[...] = jnp.zeros_like(knp_ref)
            vnp_ref[...] = jnp.zeros_like(vnp_ref)
            knp_ref[0:t, :] = kn_ref[0]
            vnp_ref[0:t, :] = vn_ref[0]
            for h in range(heads):
                hs = slice(h * dh, (h + 1) * dh)
                q = q_ref[0, :, hs].astype(BF16)
                bias = _shifted_rows(yn_ref[h:h + 1, :], t)[:, BLK:]
                s_n = jnp.where(ok_n, _nt_dot(q, knp_ref[:, hs].astype(BF16)) * scale + bias, NEG_INF)
                merge_into(hs, *_block_softmax([s_n], [vnp_ref[:, hs].astype(BF16)]))
        for g in range(n_g):
            if ins[6 * g + 3].shape[1] == n_bufs[g] * row_stride:
                cache_piece(g, 0)

    for g in range(n_g):
        rows = ins[6 * g + 3].shape[1] // row_stride
        if rows != n_bufs[g]:
            cache_piece(g, part * rows)

    @pl.when(part == pl.num_programs(1) - 1)
    def _():
        o_ref[0] = acc_ref[...] / l_ref[...]


def _attn_step(qkv, caches, rel_bias, batch, t, heads, dh, n_parts=2):
    n_g = len(GROUPS)
    width = heads * dh
    qkv3 = qkv.reshape(batch, t, qkv.shape[1])
    in_specs, args, n_bufs = [], [], []
    vmem = 8 * t * width * 4
    for g in range(n_g):
        n_buf = caches[g].shape[1]
        parts = n_parts if n_buf % (n_parts * BLK) == 0 else 1
        rows = n_buf // parts
        flat = caches[g].reshape(batch, n_buf * 2 * heads, dh)
        yc, yn = _step_bias_vectors(rel_bias, g, heads, n_buf, parts)
        new = lambda comp, g=g: pl.BlockSpec((1, t, width), lambda b, s: (b, 0, comp * n_g + g))
        part_of = (lambda s: s) if parts == n_parts else (lambda s: 0)
        in_specs += [new(0), new(1), new(2),
                     pl.BlockSpec((1, rows * 2 * heads, dh), lambda b, s, f=part_of: (b, f(s), 0)),
                     pl.BlockSpec((1, heads, rows + BLK), lambda b, s, f=part_of: (f(s), 0, 0)),
                     pl.BlockSpec(yn.shape, lambda b, s: (0, 0))]
        args += [qkv3, qkv3, qkv3, flat, yc, yn]
        n_bufs.append(n_buf)
        vmem += 2 * rows * 2 * width * 4 + 6 * rows * dh * 4
    return pl.pallas_call(
        functools.partial(_attn_step_kernel, heads=heads, dh=dh, n_bufs=tuple(n_bufs)),
        name="attn_step",
        out_shape=jax.ShapeDtypeStruct((batch, t, width), F32),
        grid=(batch, n_parts),
        in_specs=in_specs,
        out_specs=pl.BlockSpec((1, t, width), lambda b, s: (b, 0, 0)),
        scratch_shapes=[pltpu.VMEM((t, width), F32)] * 3 + [pltpu.VMEM((BLK, width), F32)] * 2,
        compiler_params=_cparams(("parallel", "arbitrary"), vmem),
    )(*args).reshape(batch * t, width)


def _attn_out_kernel(x_ref, o_ref, wo_ref, out_ref):
    out_ref[...] = x_ref[...] + _dot(o_ref[...].astype(BF16), wo_ref[...])


def _attn_out(x, o, wo, tm_pref=512):
    m, d = x.shape
    width = wo.shape[0]
    tm = _tile(m, tm_pref)
    row = lambda n: pl.BlockSpec((tm, n), lambda i: (i, 0))
    vmem = 4 * tm * d * 4 + 2 * tm * width * 4 + 2 * width * d * 2
    return pl.pallas_call(
        _attn_out_kernel,
        name=f"attn_out_m{m}",
        out_shape=jax.ShapeDtypeStruct((m, d), F32),
        grid=(m // tm,),
        in_specs=[row(d), row(width), pl.BlockSpec((width, d), lambda i: (0, 0))],
        out_specs=row(d),
        compiler_params=_cparams(("parallel",), vmem),
    )(x, o, wo)


def _t5_bucket(dist):
    max_exact = N_BUCKETS // 2
    d = np.asarray(dist, dtype=np.int32)
    large = max_exact + (np.log(np.maximum(d, max_exact) / max_exact) / np.log(MAX_DISTANCE / max_exact)
                         * (N_BUCKETS - max_exact)).astype(np.int32)
    return np.where(d < max_exact, d, np.minimum(large, N_BUCKETS - 1)).astype(np.int32)


def _group_bias(rel_bias, g, heads):
    buckets = _t5_bucket(np.arange(N_DIL_KEYS) * GROUPS[g][1])
    return rel_bias[:, g * heads:(g + 1) * heads].astype(F32)[buckets]


def _prompt_bias_vectors(rel_bias, heads):
    vecs = []
    for g in range(len(GROUPS)):
        rev = _group_bias(rel_bias, g, heads)[::-1]
        vecs.append(jnp.pad(rev, ((0, 2 * BLK - N_DIL_KEYS), (0, 0))).T)
    return jnp.concatenate(vecs, axis=0)


def _step_bias_vectors(rel_bias, g, heads, n_buf, parts):
    rows = n_buf // parts
    bias = _group_bias(rel_bias, g, heads)
    at = lambda dist: bias[np.clip(dist // GROUPS[g][1], 0, N_DIL_KEYS - 1)]
    first = np.arange(parts)[:, None] * rows + np.arange(rows + BLK)[None, :]
    yc = jnp.transpose(at(n_buf + BLK - first), (0, 2, 1))
    yn = at(BLK - np.arange(2 * BLK)).T
    return yc, yn


def _kv_rows_kernel(k_ref, v_ref, o_ref, *, heads, dh):
    rows = k_ref.shape[0]
    o_ref[0, :, 0] = k_ref[...].reshape(rows, heads, dh)
    o_ref[0, :, 1] = v_ref[...].reshape(rows, heads, dh)


def _kv_rows(qkv, g, batch, seq, n_rows, heads, dh):
    n_g = len(GROUPS)
    width = heads * dh
    tm = _tile(n_rows, 256)
    assert seq % tm == 0 and n_rows % tm == 0
    first = (seq - n_rows) // tm
    per_seq = seq // tm
    col = lambda comp: pl.BlockSpec((tm, width), lambda b, i: (b * per_seq + first + i, comp * n_g + g))
    return pl.pallas_call(
        functools.partial(_kv_rows_kernel, heads=heads, dh=dh),
        name=f"kv_rows_g{g}_n{n_rows}",
        out_shape=jax.ShapeDtypeStruct((batch, n_rows, 2, heads, dh), qkv.dtype),
        grid=(batch, n_rows // tm),
        in_specs=[col(1), col(2)],
        out_specs=pl.BlockSpec((1, tm, 2, heads, dh), lambda b, i: (b, i, 0, 0, 0)),
        compiler_params=_cparams(("parallel", "parallel"), 8 * tm * width * 4),
    )(qkv, qkv)


def _trunk(x3, st, p, tiles, bank, emit):
    batch, seq, d = x3.shape
    m = batch * seq
    x = x3.reshape(m, d)
    new = []
    depth = p['ffn_w1'].shape[0]
    for i in range(depth):
        kind = i % N_MIXERS
        g_mix = p['norm_mix_g'][i][None]
        if kind == 0:
            j = i // N_MIXERS
            n_taps, c = p['conv_dw_w'].shape[1:]
            buf = jnp.zeros((batch, n_taps - 1, c), F32) if st is None else st[i][0]
            if emit:
                u, *bank['glu', j] = _norm_proj(x, g_mix, p['conv_w_glu'], 'glu', tiles['proj_m'], layer=j, emit=True)
            else:
                u = _norm_proj(x, g_mix, tuple(bank['glu', j]), 'glu', tiles['proj_m'])
            u3 = u.reshape(batch, seq, c)
            tm = _tile(seq, tiles['conv_m'])
            halo = _halos(u3, buf, tm, CONV_HALO)
            dww = jnp.repeat(p['conv_dw_w'][j], SUBLANES, axis=0)
            x = _conv_tail(x, u, halo, dww, p['conv_dw_b'][j][None], p['conv_ln_g'][j][None],
                           p['conv_ln_b'][j][None], p['conv_w_out'], j, tm, n_taps)
            new.append((_last_rows(buf, u3, n_taps - 1),))
        elif kind == 1:
            n_taps, c = p['lru_conv_w'].shape
            if st is None:
                h0 = jnp.zeros((batch, c), F32)
                cb = jnp.zeros((batch, n_taps - 1, c), F32)
            else:
                h0, cb = st[i]
            if emit:
                ybr, xbr, *bank['lru_in'] = _norm_proj(x, g_mix, p['lru_w_in'], 'split', tiles['proj_m'], emit=True)
            else:
                ybr, xbr = _norm_proj(x, g_mix, tuple(bank['lru_in']), 'split', tiles['proj_m'])
            xbr3 = xbr.reshape(batch, seq, c)
            tm = _tile(seq, tiles['lru_m'])
            halo = _halos(xbr3, cb, tm, LRU_HALO)
            cw = jnp.pad(p['lru_conv_w'], ((0, SUBLANES - n_taps), (0, 0)))
            x, hl = _lru_tail(x, ybr, xbr, halo, h0[:, None], cw, p['lru_conv_b'][None], p['lru_w_a'],
                              p['lru_b_a'][None], p['lru_w_x'], p['lru_b_x'][None], p['lru_lambda'][None],
                              p['lru_w_out'], batch, tm, n_taps)
            new.append((hl[:, 0], _last_rows(cb, xbr3, n_taps - 1)))
        else:
            rel_bias = p['rel_bias']
            heads = rel_bias.shape[1] // len(GROUPS)
            width = p['att_w_o'].shape[0]
            dh = width // heads
            if emit:
                qkv, bank['qkv'] = _norm_proj(x, g_mix, p['att_w_qkv'], 'plain', tiles['proj_m'], emit=True)
            else:
                qkv = _norm_proj(x, g_mix, bank['qkv'], 'plain', tiles['proj_m'], tn_pref=1024)
            if st is None:
                assert seq % CHUNK == 0
                o = _attn_prompt(qkv, _prompt_bias_vectors(rel_bias, heads), batch, seq, heads, dh)
                rows = [_kv_rows(qkv, g, batch, seq, min(win, seq), heads, dh) for g, (win, _) in enumerate(GROUPS)]
            else:
                assert seq <= BLK
                o = _attn_step(qkv, st[i], rel_bias, batch, seq, heads, dh)
                rows = [_kv_rows(qkv, g, batch, seq, seq, heads, dh) for g in range(len(GROUPS))]
            x = _attn_out(x, o, p['att_w_o'], tiles['attn_out_m'])
            new.append(rows)
        g_ffn, g_fin, final = p['norm_ffn_g'][i][None], p['final_norm_g'][None], i == depth - 1
        if emit:
            x, w1b, w2b = _ffn(x, g_ffn, p['ffn_w1'], p['ffn_w2'], i, g_fin, final, tm_pref=tiles['ffn_m'],
                               tf_pref=tiles['ffn_f'], emit=True)
            bank['ffn', i] = (w1b, w2b)
        else:
            x = _ffn(x, g_ffn, *bank['ffn', i], i, g_fin, final, tm_pref=tiles['ffn_m'], tf_pref=tiles['ffn_f'])
    return x.reshape(batch, seq, d), new


PROMPT_TILES = dict(proj_m=1024, conv_m=256, lru_m=256, attn_out_m=512, ffn_m=512, ffn_f=1024)
SAMPLE_TILES = dict(proj_m=512, conv_m=256, lru_m=256, attn_out_m=512, ffn_m=512, ffn_f=512)
MATMUL_WEIGHTS = ('conv_w_out', 'lru_w_a', 'lru_w_x', 'lru_w_out', 'att_w_o')


def kernel(x_prompt, x_sample, state_conv_l0, state_lru_h_l1, state_lru_conv_l1, cache_kv_w128_l2, cache_kv_w512_l2, cache_kv_w2048_l2, state_conv_l3, norm_mix_g, norm_ffn_g, final_norm_g, ffn_w1, ffn_w2, conv_w_glu, conv_dw_w, conv_dw_b, conv_ln_g, conv_ln_b, conv_w_out, lru_w_in, lru_conv_w, lru_conv_b, lru_w_a, lru_b_a, lru_w_x, lru_b_x, lru_lambda, lru_w_out, att_w_qkv, att_w_o, rel_bias):
    p = dict(norm_mix_g=norm_mix_g, norm_ffn_g=norm_ffn_g, final_norm_g=final_norm_g, ffn_w1=ffn_w1,
             ffn_w2=ffn_w2, conv_w_glu=conv_w_glu, conv_dw_w=conv_dw_w, conv_dw_b=conv_dw_b,
             conv_ln_g=conv_ln_g, conv_ln_b=conv_ln_b, conv_w_out=conv_w_out, lru_w_in=lru_w_in,
             lru_conv_w=lru_conv_w, lru_conv_b=lru_conv_b, lru_w_a=lru_w_a, lru_b_a=lru_b_a, lru_w_x=lru_w_x,
             lru_b_x=lru_b_x, lru_lambda=lru_lambda, lru_w_out=lru_w_out, att_w_qkv=att_w_qkv,
             att_w_o=att_w_o, rel_bias=rel_bias)
    for name in MATMUL_WEIGHTS:
        p[name] = p[name].astype(BF16)
    bank = {}
    y_sample, sst = _trunk(x_sample, [(state_conv_l0,), (state_lru_h_l1, state_lru_conv_l1),
                                      (cache_kv_w128_l2, cache_kv_w512_l2, cache_kv_w2048_l2),
                                      (state_conv_l3,)], p, SAMPLE_TILES, bank, emit=True)
    y_prompt, pst = _trunk(x_prompt, None, p, PROMPT_TILES, bank, emit=False)
    return (y_prompt, y_sample,
            pst[0][0], sst[0][0],
            pst[1][0], sst[1][0],
            pst[1][1], sst[1][1],
            pst[2][0], sst[2][0],
            pst[2][1], sst[2][1],
            pst[2][2], sst[2][2],
            pst[3][0], sst[3][0])
```

```python
import functools

import numpy as np
import jax
import jax.numpy as jnp
from jax import lax
from jax.experimental import pallas as pl
from jax.experimental.pallas import tpu as pltpu

RMS_EPS = 1e-6
LN_EPS = 1e-5
LRU_C = 8.0
NEG_INF = -1e30
N_MIXERS = 3
GROUPS = ((128, 1), (512, 4), (2048, 16))
N_DIL_KEYS = 129
BLK = 128
CHUNK = BLK * max(d for _, d in GROUPS)
N_BUCKETS = 32
MAX_DISTANCE = 2048

V7X_VMEM_BYTES = 64 * 1024 * 1024
SUBLANES = 8
CONV_HALO = 32
LRU_HALO = 8

BF16 = jnp.bfloat16
F32 = jnp.float32


def _cparams(semantics, vmem_bytes):
    limit = int(min(max(vmem_bytes * 3 // 2, 32 * 1024 * 1024), V7X_VMEM_BYTES - 8 * 1024 * 1024))
    return pltpu.CompilerParams(dimension_semantics=semantics, vmem_limit_bytes=limit)


def _tile(n, pref):
    if n <= pref:
        return n
    t = pref
    while n % t:
        t //= 2
    return t


def _rms_to_bf16(x, g):
    ms = jnp.mean(x * x, axis=-1, keepdims=True)
    return (x * lax.rsqrt(ms + RMS_EPS) * g).astype(BF16)


def _dot(a, b):
    return jnp.dot(a, b, preferred_element_type=F32)


def _nt_dot(a, b):
    return lax.dot_general(a, b, (((1,), (1,)), ((), ())), preferred_element_type=F32)


def _norm_proj_kernel(x_ref, g_ref, *refs, mode, emit):
    n_w = 1 if mode == 'plain' else 2
    n_o = 2 if mode == 'split' else 1
    w_refs, o_refs = refs[:n_w], refs[n_w:n_w + n_o]
    wb_refs, h_ref = refs[n_w + n_o:-1], refs[-1]

    @pl.when(pl.program_id(1) == 0)
    def _():
        h_ref[...] = _rms_to_bf16(x_ref[...], g_ref[...])

    ws = [w_ref[...].astype(BF16) for w_ref in w_refs]
    if emit:
        for wb_ref, w in zip(wb_refs, ws):
            wb_ref[...] = w
    h = h_ref[...]
    if mode == 'plain':
        o_refs[0][...] = _dot(h, ws[0])
    elif mode == 'glu':
        o_refs[0][...] = _dot(h, ws[0]) * jax.nn.sigmoid(_dot(h, ws[1]))
    else:
        o_refs[0][...] = jax.nn.gelu(_dot(h, ws[0]), approximate=True)
        o_refs[1][...] = _dot(h, ws[1])


def _wspec(w, layer, block, index_map):
    if w.ndim == 2:
        return pl.BlockSpec(block, index_map)
    return pl.BlockSpec((None,) + block, lambda *ids: (layer,) + index_map(*ids))


def _norm_proj(x, g, w, mode, tm_pref=512, tn_pref=512, layer=0, emit=False):
    m, d = x.shape
    halves = mode != 'plain'
    pair = isinstance(w, tuple)
    n_out = w[0].shape[-1] if pair else (w.shape[-1] // 2 if halves else w.shape[-1])
    tm = _tile(m, tm_pref)
    tn = _tile(n_out, tn_pref)
    nj = n_out // tn
    w_args = list(w) if pair else ([w, w] if halves else [w])
    shift = [0, 0] if pair else [0, nj]
    in_specs = [pl.BlockSpec((tm, d), lambda i, j: (i, 0)), pl.BlockSpec((1, d), lambda i, j: (0, 0))]
    in_specs += [_wspec(wk, layer, (d, tn), lambda i, j, s=s: (0, j + s)) for wk, s in zip(w_args, shift)]
    n_o = 2 if mode == 'split' else 1
    o_shapes = [jax.ShapeDtypeStruct((m, n_out), F32)] * n_o
    o_specs = [pl.BlockSpec((tm, tn), lambda i, j: (i, j))] * n_o
    if emit:
        assert m == tm, "each weight block must be visited once"
        o_shapes += [jax.ShapeDtypeStruct((d, n_out), BF16)] * len(w_args)
        o_specs += [pl.BlockSpec((d, tn), lambda i, j: (0, j))] * len(w_args)
    wbytes = w_args[0].dtype.itemsize
    vmem = (2 * tm * d * 4 + tm * d * 2 + 2 * len(w_args) * d * tn * wbytes + 2 * n_o * tm * tn * 4
            + (3 * len(w_args) * d * tn * 2 if emit else 0))
    outs = pl.pallas_call(
        functools.partial(_norm_proj_kernel, mode=mode, emit=emit),
        name=f"norm_proj_{mode}_m{m}",
        out_shape=o_shapes,
        grid=(m // tm, nj),
        in_specs=in_specs,
        out_specs=o_specs,
        scratch_shapes=[pltpu.VMEM((tm, d), BF16)],
        compiler_params=_cparams(("parallel", "arbitrary"), vmem),
    )(x, g, *w_args)
    return outs[0] if len(outs) == 1 else tuple(outs)


def _ffn_kernel(x_ref, g_ref, w1_ref, w2_ref, fg_ref, o_ref, *rest, final, emit):
    if emit:
        w1b_ref, w2b_ref, h_ref = rest
    else:
        (h_ref,) = rest
    f = pl.program_id(1)

    @pl.when(f == 0)
    def _():
        x = x_ref[...]
        h_ref[...] = _rms_to_bf16(x, g_ref[...])
        o_ref[...] = x

    w1 = w1_ref[...].astype(BF16)
    w2 = w2_ref[...].astype(BF16)
    if emit:
        w1b_ref[...] = w1
        w2b_ref[...] = w2
    hid = jnp.square(jnp.maximum(_dot(h_ref[...], w1), 0.0)).astype(BF16)
    o_ref[...] += _dot(hid, w2)

    if final:
        @pl.when(f == pl.num_programs(1) - 1)
        def _():
            y = o_ref[...]
            ms = jnp.mean(y * y, axis=-1, keepdims=True)
            o_ref[...] = y * lax.rsqrt(ms + RMS_EPS) * fg_ref[...]


def _ffn(x, g, w1, w2, layer, final_g, final, tm_pref=512, tf_pref=1024, emit=False):
    m, d = x.shape
    dff = w1.shape[-1]
    tm = _tile(m, tm_pref)
    tf = _tile(dff, tf_pref)
    wbytes = w1.dtype.itemsize
    vmem = (2 * tm * d * 4 + tm * d * 2 + 2 * 2 * d * tf * wbytes + 2 * tm * d * 4 + 2 * tm * tf * 4
            + (6 * d * tf * 2 if emit else 0))
    o_shape = jax.ShapeDtypeStruct((m, d), F32)
    o_spec = pl.BlockSpec((tm, d), lambda i, f: (i, 0))
    if emit:
        assert m == tm, "each weight block must be visited once"
        o_shape = [o_shape, jax.ShapeDtypeStruct((d, dff), BF16), jax.ShapeDtypeStruct((dff, d), BF16)]
        o_spec = [o_spec, pl.BlockSpec((d, tf), lambda i, f: (0, f)), pl.BlockSpec((tf, d), lambda i, f: (f, 0))]
    return pl.pallas_call(
        functools.partial(_ffn_kernel, final=final, emit=emit),
        name=f"ffn_m{m}_l{layer}",
        out_shape=o_shape,
        grid=(m // tm, dff // tf),
        in_specs=[pl.BlockSpec((tm, d), lambda i, f: (i, 0)),
                  pl.BlockSpec((1, d), lambda i, f: (0, 0)),
                  _wspec(w1, layer, (d, tf), lambda i, f: (0, f)),
                  _wspec(w2, layer, (tf, d), lambda i, f: (f, 0)),
                  pl.BlockSpec((1, d), lambda i, f: (0, 0))],
        out_specs=o_spec,
        scratch_shapes=[pltpu.VMEM((tm, d), BF16)],
        compiler_params=_cparams(("parallel", "arbitrary"), vmem),
    )(x, g, w1, w2, final_g)


def _conv_tail_kernel(x_ref, u_ref, halo_ref, dww_ref, dwb_ref, lng_ref, lnb_ref, wout_ref,
                      o_ref, uext_ref, y_ref, *, n_taps, rows, ln_rows, lanes):
    i = pl.program_id(0)
    tm, c = u_ref.shape
    cur = lax.rem(i, 2)

    @pl.when(i == 0)
    def _():
        y_ref[1] = jnp.zeros((tm, c), BF16)

    uext_ref[0:CONV_HALO, :] = halo_ref[0]
    uext_ref[CONV_HALO:CONV_HALO + tm, :] = u_ref[...]
    off = CONV_HALO - (n_taps - 1)

    def row_chunk(r, carry):
        r0 = pl.multiple_of(r * rows, rows)
        for l0 in range(0, c, lanes):
            win = uext_ref[pl.ds(r0, rows + CONV_HALO), l0:l0 + lanes]
            accs = [jnp.zeros((SUBLANES, lanes), F32)] * (rows // SUBLANES)
            for s in range(SUBLANES):
                ws = win if s == 0 else pltpu.roll(win, rows + CONV_HALO - s, axis=0)
                for a in range((CONV_HALO + SUBLANES) // SUBLANES):
                    k = a * SUBLANES + s - off
                    if 0 <= k < n_taps:
                        wk = dww_ref[k * SUBLANES:(k + 1) * SUBLANES, l0:l0 + lanes]
                        accs = [acc + wk * ws[(a + j) * SUBLANES:(a + j + 1) * SUBLANES]
                                for j, acc in enumerate(accs)]
            for j, acc in enumerate(accs):
                uext_ref[pl.ds(r0 + j * SUBLANES, SUBLANES), l0:l0 + lanes] = acc + dwb_ref[:, l0:l0 + lanes]
        return carry

    @pl.when(i < pl.num_programs(0) - 1)
    def _():
        lax.fori_loop(0, tm // rows, row_chunk, 0)

    o_ref[...] = x_ref[...] + _dot(y_ref[1 - cur], wout_ref[...])
    for r0 in range(0, tm, ln_rows):
        cv = uext_ref[r0:r0 + ln_rows, :]
        mu = jnp.mean(cv, axis=-1, keepdims=True)
        cc = cv - mu
        var = jnp.mean(cc * cc, axis=-1, keepdims=True)
        y = cc * lax.rsqrt(var + LN_EPS) * lng_ref[...] + lnb_ref[...]
        y_ref[cur, r0:r0 + ln_rows, :] = (y * jax.nn.sigmoid(y)).astype(BF16)


def _conv_tail(x, u, halo, dww, dwb, lng, lnb, wout, layer, tm, n_taps):
    m, d = x.shape
    c = u.shape[1]
    rows = _tile(tm, 128)
    ln_rows = _tile(tm, 16)
    lanes = _tile(c, 128)
    nt = m // tm
    prev = lambda i: (jnp.maximum(i - 1, 0), 0)
    this = lambda i: (jnp.minimum(i, nt - 1), 0)
    vmem = 4 * tm * d * 4 + 2 * tm * c * 4 + 2 * c * d * 2 + (tm + CONV_HALO) * c * 4 + 2 * tm * c * 2
    return pl.pallas_call(
        functools.partial(_conv_tail_kernel, n_taps=n_taps, rows=rows, ln_rows=ln_rows, lanes=lanes),
        name=f"conv_tail_m{m}",
        out_shape=jax.ShapeDtypeStruct((m, d), F32),
        grid=(nt + 1,),
        in_specs=[pl.BlockSpec((tm, d), prev),
                  pl.BlockSpec((tm, c), this),
                  pl.BlockSpec((1, CONV_HALO, c), lambda i: (jnp.minimum(i, nt - 1), 0, 0)),
                  pl.BlockSpec(dww.shape, lambda i: (0, 0)),
                  pl.BlockSpec((1, c), lambda i: (0, 0)),
                  pl.BlockSpec((1, c), lambda i: (0, 0)),
                  pl.BlockSpec((1, c), lambda i: (0, 0)),
                  _wspec(wout, layer, (c, d), lambda i: (0, 0))],
        out_specs=pl.BlockSpec((tm, d), prev),
        scratch_shapes=[pltpu.VMEM((tm + CONV_HALO, c), F32), pltpu.VMEM((2, tm, c), BF16)],
        compiler_params=_cparams(("arbitrary",), vmem),
    )(x, u, halo, dww, dwb, lng, lnb, wout)


def _last_rows(past, seq, n):
    if seq.shape[1] >= n:
        return seq[:, seq.shape[1] - n:]
    return jnp.concatenate([past, seq], axis=1)[:, -n:]


def _halos(seq, past, tm, halo_rows):
    b, l, c = seq.shape
    p = past.shape[1]
    first = jnp.concatenate([jnp.zeros((b, halo_rows - p, c), seq.dtype), past], axis=1)[:, None]
    nt = l // tm
    if nt == 1:
        return first.reshape(b, halo_rows, c)
    rest = seq.reshape(b, nt, tm, c)[:, :-1, tm - halo_rows:]
    return jnp.concatenate([first, rest], axis=1).reshape(b * nt, halo_rows, c)


def _lru_tail_kernel(x_ref, ybr_ref, xbr_ref, halo_ref, h0_ref, cw_ref, cb_ref, wa_ref, ba_ref,
                     wx_ref, bx_ref, lam_ref, wout_ref, o_ref, hlast_ref,
                     xext_ref, a_ref, b_ref, hc_ref, *, n_taps, lanes):
    tm, c = xbr_ref.shape
    n_blocks, blk, _ = wa_ref.shape
    t = pl.program_id(1)

    @pl.when(t == 0)
    def _():
        hc_ref[...] = jnp.broadcast_to(h0_ref[0], hc_ref.shape)

    xext_ref[0:LRU_HALO, :] = halo_ref[0]
    xext_ref[LRU_HALO:LRU_HALO + tm, :] = xbr_ref[...]
    off = LRU_HALO - (n_taps - 1)

    sp = jax.nn.softplus(-lam_ref[...])
    for n in range(n_blocks):
        sl = slice(n * blk, (n + 1) * blk)
        xc = jnp.zeros((tm, blk), F32)
        for k in range(n_taps):
            xc = xc + cw_ref[k:k + 1, sl] * xext_ref[off + k:off + k + tm, sl]
        xc = xc + cb_ref[:, sl]
        xcb = xc.astype(BF16)
        r = jax.nn.sigmoid(_dot(xcb, wa_ref[n]) + ba_ref[:, sl])
        gi = jax.nn.sigmoid(_dot(xcb, wx_ref[n]) + bx_ref[:, sl])
        log_a = -LRU_C * r * sp[:, sl]
        a = jnp.exp(log_a)
        a_ref[:, sl] = a
        b_ref[:, sl] = jnp.sqrt(-jnp.tanh(log_a) * (a * a + 1.0)) * (gi * xc)

    row = lax.broadcasted_iota(jnp.int32, (SUBLANES, lanes), 0)

    for l0 in range(0, c, lanes):
        def group(gidx, h):
            r0 = pl.multiple_of(gidx * SUBLANES, SUBLANES)
            av = a_ref[pl.ds(r0, SUBLANES), l0:l0 + lanes]
            bv = b_ref[pl.ds(r0, SUBLANES), l0:l0 + lanes]
            for s in (1, 2, 4):
                a_sh = jnp.where(row >= s, pltpu.roll(av, s, axis=0), 1.0)
                b_sh = jnp.where(row >= s, pltpu.roll(bv, s, axis=0), 0.0)
                bv = av * b_sh + bv
                av = av * a_sh
            hs = av * h + bv
            b_ref[pl.ds(r0, SUBLANES), l0:l0 + lanes] = hs
            return jnp.broadcast_to(hs[SUBLANES - 1:SUBLANES, :], (SUBLANES, lanes))

        h_fin = lax.fori_loop(0, tm // SUBLANES, group, hc_ref[:, l0:l0 + lanes], unroll=2)
        hc_ref[:, l0:l0 + lanes] = h_fin

    gated = (b_ref[...] * ybr_ref[...]).astype(BF16)
    o_ref[...] = x_ref[...] + _dot(gated, wout_ref[...])

    @pl.when(t == pl.num_programs(1) - 1)
    def _():
        hlast_ref[0] = hc_ref[...]


def _lru_tail(x, ybr, xbr, halo, h0, cw, cb, wa, ba, wx, bx, lam, wout, batch, tm, n_taps):
    m, d = x.shape
    c = xbr.shape[1]
    nt = m // batch // tm
    lanes = _tile(c, 512)
    row2 = lambda b, t: (b * nt + t, 0)
    const2 = lambda b, t: (0, 0)
    vec = pl.BlockSpec((1, c), const2)
    vmem = (4 * tm * d * 4 + 4 * tm * c * 4 + 2 * c * d * 2 + 4 * wa.size * 2
            + (tm + LRU_HALO) * c * 4 + 2 * tm * c * 4)
    return pl.pallas_call(
        functools.partial(_lru_tail_kernel, n_taps=n_taps, lanes=lanes),
        name=f"lru_tail_m{m}",
        out_shape=[jax.ShapeDtypeStruct((m, d), F32),
                   jax.ShapeDtypeStruct((batch, SUBLANES, c), F32)],
        grid=(batch, nt),
        in_specs=[pl.BlockSpec((tm, d), row2),
                  pl.BlockSpec((tm, c), row2),
                  pl.BlockSpec((tm, c), row2),
                  pl.BlockSpec((1, LRU_HALO, c), lambda b, t: (b * nt + t, 0, 0)),
                  pl.BlockSpec((1, 1, c), lambda b, t: (b, 0, 0)),
                  pl.BlockSpec(cw.shape, const2),
                  vec,
                  pl.BlockSpec(wa.shape, lambda b, t: (0, 0, 0)),
                  vec,
                  pl.BlockSpec(wx.shape, lambda b, t: (0, 0, 0)),
                  vec, vec,
                  pl.BlockSpec((c, d), const2)],
        out_specs=[pl.BlockSpec((tm, d), row2),
                   pl.BlockSpec((1, SUBLANES, c), lambda b, t: (b, 0, 0))],
        scratch_shapes=[pltpu.VMEM((tm + LRU_HALO, c), F32), pltpu.VMEM((tm, c), F32),
                        pltpu.VMEM((tm, c), F32), pltpu.VMEM((SUBLANES, c), F32)],
        compiler_params=_cparams(("parallel", "arbitrary"), vmem),
    )(x, ybr, xbr, halo, h0, cw, cb, wa, ba, wx, bx, lam, wout)


def _shifted_rows(vec, n_rows):
    return pltpu.roll(jnp.broadcast_to(vec, (n_rows, vec.shape[1])), 0, 1, stride=1, stride_axis=0)


def _block_softmax(s_parts, v_parts):
    m = s_parts[0].max(axis=-1, keepdims=True)
    for s in s_parts[1:]:
        m = jnp.maximum(m, s.max(axis=-1, keepdims=True))
    o, den = None, None
    for s, v in zip(s_parts, v_parts):
        p = jnp.exp(s - m)
        d = p.sum(axis=-1, keepdims=True)
        pv = _dot(p.astype(BF16), v)
        o, den = (pv, d) if o is None else (o + pv, den + d)
    return o, m, den


def _block_softmax_mxu_sum(s_parts, v_parts, exp=jnp.exp):
    m = s_parts[0].max(axis=-1, keepdims=True)
    for s in s_parts[1:]:
        m = jnp.maximum(m, s.max(axis=-1, keepdims=True))
    acc = None
    for s, v in zip(s_parts, v_parts):
        pv = _dot(exp(s - m).astype(BF16), v)
        acc = pv if acc is None else acc + pv
    dh = acc.shape[1] // 2
    return acc[:, :dh], m, acc[:, dh:]


def _merge(state, o_b, m_b, l_b, exp=jnp.exp):
    if state is None:
        return o_b, m_b, l_b
    acc, m_old, l_old = state
    m_new = jnp.maximum(m_old, m_b)
    a_old = exp(m_old - m_new)
    a_b = exp(m_b - m_new)
    return acc * a_old + o_b * a_b, m_new, l_old * a_old + l_b * a_b


def _attn_prompt_kernel(*refs, heads, dh):
    n_g = len(GROUPS)
    ins = refs[:5 * n_g]
    bvec_ref, o_ref, acc_ref, m_ref, l_ref = refs[5 * n_g:]
    first_chunk = pl.program_id(1) == 0
    h = pl.program_id(2)
    log2e = float(np.log2(np.e))
    scale = dh ** -0.5 * log2e
    qi = lax.broadcasted_iota(jnp.int32, (BLK, 2 * BLK), 0)
    kj = lax.broadcasted_iota(jnp.int32, (BLK, 2 * BLK), 1)
    in_band = (kj - qi >= 0) & (kj - qi <= BLK)
    ones = jnp.ones((BLK, dh), BF16)
    for g, (_, dil) in enumerate(GROUPS):
        q_ref, k_ref, v_ref, kp_ref, vp_ref = ins[5 * g:5 * g + 5]
        bias = jnp.where(in_band, _shifted_rows(bvec_ref[pl.ds(g * heads + h, 1), :] * log2e, BLK), NEG_INF)
        bias_p, bias_c = bias[:, :BLK], bias[:, BLK:]
        for r in range(dil):
            prev_rows = pl.ds(r, BLK, stride=dil)
            k_prev = kp_ref[prev_rows, :].astype(BF16)
            v_prev = jnp.concatenate([vp_ref[prev_rows, :].astype(BF16), ones], axis=1)
            for nb in range(CHUNK // (dil * BLK)):
                rows = pl.ds(r + dil * BLK * nb, BLK, stride=dil)
                q = q_ref[rows, :].astype(BF16)
                k_cur = k_ref[rows, :].astype(BF16)
                v_cur = jnp.concatenate([v_ref[rows, :].astype(BF16), ones], axis=1)
                s_p = _nt_dot(q, k_prev) * scale + bias_p
                if nb == 0:
                    s_p = jnp.where(first_chunk, NEG_INF, s_p)
                s_c = _nt_dot(q, k_cur) * scale + bias_c
                o_b, m_b, l_b = _block_softmax_mxu_sum([s_p, s_c], [v_prev, v_cur], exp=jnp.exp2)
                m_b = jnp.broadcast_to(m_b, (BLK, dh))
                state = None if g == 0 else (acc_ref[rows, :], m_ref[rows, :], l_ref[rows, :])
                acc, m_new, l_new = _merge(state, o_b, m_b, l_b, exp=jnp.exp2)
                if g == n_g - 1:
                    o_ref[rows, :] = acc / l_new
                else:
                    acc_ref[rows, :] = acc
                    m_ref[rows, :] = m_new
                    l_ref[rows, :] = l_new
                k_prev, v_prev = k_cur, v_cur


def _attn_prompt(qkv, bvec, batch, seq, heads, dh):
    n_g = len(GROUPS)
    nc = seq // CHUNK
    col = lambda comp, g: (lambda b, n, h: (b * nc + n, (comp * n_g + g) * heads + h))
    in_specs, args = [], []
    for g, (win, _) in enumerate(GROUPS):
        per = CHUNK // win
        prev = lambda comp, g=g, per=per: (
            lambda b, n, h: (jnp.maximum((b * nc + n) * per - 1, 0), (comp * n_g + g) * heads + h))
        in_specs += [pl.BlockSpec((CHUNK, dh), col(0, g)), pl.BlockSpec((CHUNK, dh), col(1, g)),
                     pl.BlockSpec((CHUNK, dh), col(2, g)),
                     pl.BlockSpec((win, dh), prev(1)), pl.BlockSpec((win, dh), prev(2))]
        args += [qkv] * 5
    in_specs.append(pl.BlockSpec(bvec.shape, lambda b, n, h: (0, 0)))
    vmem = 2 * (9 * CHUNK + 2 * sum(w for w, _ in GROUPS)) * dh * 4 + 2 * CHUNK * dh * 4 + 3 * CHUNK * dh * 4
    return pl.pallas_call(
        functools.partial(_attn_prompt_kernel, heads=heads, dh=dh),
        name="attn_prompt",
        out_shape=jax.ShapeDtypeStruct((batch * seq, heads * dh), F32),
        grid=(batch, nc, heads),
        in_specs=in_specs,
        out_specs=pl.BlockSpec((CHUNK, dh), lambda b, n, h: (b * nc + n, h)),
        scratch_shapes=[pltpu.VMEM((CHUNK, dh), F32)] * 3,
        compiler_params=_cparams(("parallel", "parallel", "parallel"), vmem),
    )(*args, bvec)


def _attn_step_kernel(*refs, heads, dh, n_bufs):
    n_g = len(GROUPS)
    ins = refs[:6 * n_g]
    o_ref, acc_ref, m_ref, l_ref, knp_ref, vnp_ref = refs[6 * n_g:]
    part = pl.program_id(1)
    t = o_ref.shape[1]
    scale = dh ** -0.5
    row_stride = 2 * heads

    def merge_into(hs, o_b, m_b, l_b):
        state = (acc_ref[:, hs], m_ref[:, hs], l_ref[:, hs])
        acc, m_new, l_new = _merge(state, o_b, jnp.broadcast_to(m_b, (t, dh)), jnp.broadcast_to(l_b, (t, dh)))
        acc_ref[:, hs] = acc
        m_ref[:, hs] = m_new
        l_ref[:, hs] = l_new

    def cache_piece(g, first_row):
        dil = GROUPS[g][1]
        q_ref, _, _, cache_ref, yc_ref, _ = ins[6 * g:6 * g + 6]
        rows = cache_ref.shape[1] // row_stride
        tc = lax.broadcasted_iota(jnp.int32, (t, rows), 0)
        cc = lax.broadcasted_iota(jnp.int32, (t, rows), 1) + first_row
        dist_c = n_bufs[g] + tc - cc
        ok_c = ((dist_c & (dil - 1)) == 0) & (dist_c <= (N_DIL_KEYS - 1) * dil)
        for h in range(heads):
            hs = slice(h * dh, (h + 1) * dh)
            q = q_ref[0, :, hs].astype(BF16)
            kc = cache_ref[0, pl.ds(h, rows, stride=row_stride), :].astype(BF16)
            vc = cache_ref[0, pl.ds(heads + h, rows, stride=row_stride), :].astype(BF16)
            bias = _shifted_rows(yc_ref[0, h:h + 1, :], t)[:, BLK:]
            s_c = jnp.where(ok_c, _nt_dot(q, kc) * scale + bias, NEG_INF)
            merge_into(hs, *_block_softmax([s_c], [vc]))

    @pl.when(part == 0)
    def _():
        acc_ref[...] = jnp.zeros_like(acc_ref)
        l_ref[...] = jnp.zeros_like(l_ref)
        m_ref[...] = jnp.full_like(m_ref, NEG_INF)
        tn = lax.broadcasted_iota(jnp.int32, (t, BLK), 0)
        cn = lax.broadcasted_iota(jnp.int32, (t, BLK), 1)
        dist_n = tn - cn
        for g, (_, dil) in enumerate(GROUPS):
            q_ref, kn_ref, vn_ref, _, _, yn_ref = ins[6 * g:6 * g + 6]
            ok_n = (dist_n >= 0) & ((dist_n & (dil - 1)) == 0) & (dist_n <= (N_DIL_KEYS - 1) * dil)
            knp_ref[...] = jnp.zeros_like(knp_ref)
            vnp_ref[...] = jnp.zeros_like(vnp_ref)
            knp_ref[0:t, :] = kn_ref[0]
            vnp_ref[0:t, :] = vn_ref[0]
            for h in range(heads):
                hs = slice(h * dh, (h + 1) * dh)
                q = q_ref[0, :, hs].astype(BF16)
                bias = _shifted_rows(yn_ref[h:h + 1, :], t)[:, BLK:]
                s_n = jnp.where(ok_n, _nt_dot(q, knp_ref[:, hs].astype(BF16)) * scale + bias, NEG_INF)
                merge_into(hs, *_block_softmax([s_n], [vnp_ref[:, hs].astype(BF16)]))
        for g in range(n_g):
            if ins[6 * g + 3].shape[1] == n_bufs[g] * row_stride:
                cache_piece(g, 0)

    for g in range(n_g):
        rows = ins[6 * g + 3].shape[1] // row_stride
        if rows != n_bufs[g]:
            cache_piece(g, part * rows)

    @pl.when(part == pl.num_programs(1) - 1)
    def _():
        o_ref[0] = acc_ref[...] / l_ref[...]


def _attn_step(qkv, caches, rel_bias, batch, t, heads, dh, n_parts=2):
    n_g = len(GROUPS)
    width = heads * dh
    qkv3 = qkv.reshape(batch, t, qkv.shape[1])
    in_specs, args, n_bufs = [], [], []
    vmem = 8 * t * width * 4
    for g in range(n_g):
        n_buf = caches[g].shape[1]
        parts = n_parts if n_buf % (n_parts * BLK) == 0 else 1
        rows = n_buf // parts
        flat = caches[g].reshape(batch, n_buf * 2 * heads, dh)
        yc, yn = _step_bias_vectors(rel_bias, g, heads, n_buf, parts)
        new = lambda comp, g=g: pl.BlockSpec((1, t, width), lambda b, s: (b, 0, comp * n_g + g))
        part_of = (lambda s: s) if parts == n_parts else (lambda s: 0)
        in_specs += [new(0), new(1), new(2),
                     pl.BlockSpec((1, rows * 2 * heads, dh), lambda b, s, f=part_of: (b, f(s), 0)),
                     pl.BlockSpec((1, heads, rows + BLK), lambda b, s, f=part_of: (f(s), 0, 0)),
                     pl.BlockSpec(yn.shape, lambda b, s: (0, 0))]
        args += [qkv3, qkv3, qkv3, flat, yc, yn]
        n_bufs.append(n_buf)
        vmem += 2 * rows * 2 * width * 4 + 6 * rows * dh * 4
    return pl.pallas_call(
        functools.partial(_attn_step_kernel, heads=heads, dh=dh, n_bufs=tuple(n_bufs)),
        name="attn_step",
        out_shape=jax.ShapeDtypeStruct((batch, t, width), F32),
        grid=(batch, n_parts),
        in_specs=in_specs,
        out_specs=pl.BlockSpec((1, t, width), lambda b, s: (b, 0, 0)),
        scratch_shapes=[pltpu.VMEM((t, width), F32)] * 3 + [pltpu.VMEM((BLK, width), F32)] * 2,
        compiler_params=_cparams(("parallel", "arbitrary"), vmem),
    )(*args).reshape(batch * t, width)


def _attn_out_kernel(x_ref, o_ref, wo_ref, out_ref):
    out_ref[...] = x_ref[...] + _dot(o_ref[...].astype(BF16), wo_ref[...])


def _attn_out(x, o, wo, tm_pref=512):
    m, d = x.shape
    width = wo.shape[0]
    tm = _tile(m, tm_pref)
    row = lambda n: pl.BlockSpec((tm, n), lambda i: (i, 0))
    vmem = 4 * tm * d * 4 + 2 * tm * width * 4 + 2 * width * d * 2
    return pl.pallas_call(
        _attn_out_kernel,
        name=f"attn_out_m{m}",
        out_shape=jax.ShapeDtypeStruct((m, d), F32),
        grid=(m // tm,),
        in_specs=[row(d), row(width), pl.BlockSpec((width, d), lambda i: (0, 0))],
        out_specs=row(d),
        compiler_params=_cparams(("parallel",), vmem),
    )(x, o, wo)


def _t5_bucket(dist):
    max_exact = N_BUCKETS // 2
    d = np.asarray(dist, dtype=np.int32)
    large = max_exact + (np.log(np.maximum(d, max_exact) / max_exact) / np.log(MAX_DISTANCE / max_exact)
                         * (N_BUCKETS - max_exact)).astype(np.int32)
    return np.where(d < max_exact, d, np.minimum(large, N_BUCKETS - 1)).astype(np.int32)


def _group_bias(rel_bias, g, heads):
    buckets = _t5_bucket(np.arange(N_DIL_KEYS) * GROUPS[g][1])
    return rel_bias[:, g * heads:(g + 1) * heads].astype(F32)[buckets]


def _prompt_bias_vectors(rel_bias, heads):
    vecs = []
    for g in range(len(GROUPS)):
        rev = _group_bias(rel_bias, g, heads)[::-1]
        vecs.append(jnp.pad(rev, ((0, 2 * BLK - N_DIL_KEYS), (0, 0))).T)
    return jnp.concatenate(vecs, axis=0)


def _step_bias_vectors(rel_bias, g, heads, n_buf, parts):
    rows = n_buf // parts
    bias = _group_bias(rel_bias, g, heads)
    at = lambda dist: bias[np.clip(dist // GROUPS[g][1], 0, N_DIL_KEYS - 1)]
    first = np.arange(parts)[:, None] * rows + np.arange(rows + BLK)[None, :]
    yc = jnp.transpose(at(n_buf + BLK - first), (0, 2, 1))
    yn = at(BLK - np.arange(2 * BLK)).T
    return yc, yn


def _kv_rows_kernel(k_ref, v_ref, o_ref, *, heads, dh):
    rows = k_ref.shape[0]
    o_ref[0, :, 0] = k_ref[...].reshape(rows, heads, dh)
    o_ref[0, :, 1] = v_ref[...].reshape(rows, heads, dh)


def _kv_rows(qkv, g, batch, seq, n_rows, heads, dh):
    n_g = len(GROUPS)
    width = heads * dh
    tm = _tile(n_rows, 256)
    assert seq % tm == 0 and n_rows % tm == 0
    first = (seq - n_rows) // tm
    per_seq = seq // tm
    col = lambda comp: pl.BlockSpec((tm, width), lambda b, i: (b * per_seq + first + i, comp * n_g + g))
    return pl.pallas_call(
        functools.partial(_kv_rows_kernel, heads=heads, dh=dh),
        name=f"kv_rows_g{g}_n{n_rows}",
        out_shape=jax.ShapeDtypeStruct((batch, n_rows, 2, heads, dh), qkv.dtype),
        grid=(batch, n_rows // tm),
        in_specs=[col(1), col(2)],
        out_specs=pl.BlockSpec((1, tm, 2, heads, dh), lambda b, i: (b, i, 0, 0, 0)),
        compiler_params=_cparams(("parallel", "parallel"), 8 * tm * width * 4),
    )(qkv, qkv)


def _trunk(x3, st, p, tiles, bank, emit):
    batch, seq, d = x3.shape
    m = batch * seq
    x = x3.reshape(m, d)
    new = []
    depth = p['ffn_w1'].shape[0]
    for i in range(depth):
        kind = i % N_MIXERS
        g_mix = p['norm_mix_g'][i][None]
        if kind == 0:
            j = i // N_MIXERS
            n_taps, c = p['conv_dw_w'].shape[1:]
            buf = jnp.zeros((batch, n_taps - 1, c), F32) if st is None else st[i][0]
            if emit:
                u, *bank['glu', j] = _norm_proj(x, g_mix, p['conv_w_glu'], 'glu', tiles['proj_m'], layer=j, emit=True)
            else:
                u = _norm_proj(x, g_mix, tuple(bank['glu', j]), 'glu', tiles['proj_m'])
            u3 = u.reshape(batch, seq, c)
            tm = _tile(seq, tiles['conv_m'])
            halo = _halos(u3, buf, tm, CONV_HALO)
            dww = jnp.repeat(p['conv_dw_w'][j], SUBLANES, axis=0)
            x = _conv_tail(x, u, halo, dww, p['conv_dw_b'][j][None], p['conv_ln_g'][j][None],
                           p['conv_ln_b'][j][None], p['conv_w_out'], j, tm, n_taps)
            new.append((_last_rows(buf, u3, n_taps - 1),))
        elif kind == 1:
            n_taps, c = p['lru_conv_w'].shape
            if st is None:
                h0 = jnp.zeros((batch, c), F32)
                cb = jnp.zeros((batch, n_taps - 1, c), F32)
            else:
                h0, cb = st[i]
            if emit:
                ybr, xbr, *bank['lru_in'] = _norm_proj(x, g_mix, p['lru_w_in'], 'split', tiles['proj_m'], emit=True)
            else:
                ybr, xbr = _norm_proj(x, g_mix, tuple(bank['lru_in']), 'split', tiles['proj_m'])
            xbr3 = xbr.reshape(batch, seq, c)
            tm = _tile(seq, tiles['lru_m'])
            halo = _halos(xbr3, cb, tm, LRU_HALO)
            cw = jnp.pad(p['lru_conv_w'], ((0, SUBLANES - n_taps), (0, 0)))
            x, hl = _lru_tail(x, ybr, xbr, halo, h0[:, None], cw, p['lru_conv_b'][None], p['lru_w_a'],
                              p['lru_b_a'][None], p['lru_w_x'], p['lru_b_x'][None], p['lru_lambda'][None],
                              p['lru_w_out'], batch, tm, n_taps)
            new.append((hl[:, 0], _last_rows(cb, xbr3, n_taps - 1)))
        else:
            rel_bias = p['rel_bias']
            heads = rel_bias.shape[1] // len(GROUPS)
            width = p['att_w_o'].shape[0]
            dh = width // heads
            if emit:
                qkv, bank['qkv'] = _norm_proj(x, g_mix, p['att_w_qkv'], 'plain', tiles['proj_m'], emit=True)
            else:
                qkv = _norm_proj(x, g_mix, bank['qkv'], 'plain', tiles['proj_m'], tn_pref=1024)
            if st is None:
                assert seq % CHUNK == 0
                o = _attn_prompt(qkv, _prompt_bias_vectors(rel_bias, heads), batch, seq, heads, dh)
                rows = [_kv_rows(qkv, g, batch, seq, min(win, seq), heads, dh) for g, (win, _) in enumerate(GROUPS)]
            else:
                assert seq <= BLK
                o = _attn_step(qkv, st[i], rel_bias, batch, seq, heads, dh)
                rows = [_kv_rows(qkv, g, batch, seq, seq, heads, dh) for g in range(len(GROUPS))]
            x = _attn_out(x, o, p['att_w_o'], tiles['attn_out_m'])
            new.append(rows)
        g_ffn, g_fin, final = p['norm_ffn_g'][i][None], p['final_norm_g'][None], i == depth - 1
        if emit:
            x, w1b, w2b = _ffn(x, g_ffn, p['ffn_w1'], p['ffn_w2'], i, g_fin, final, tm_pref=tiles['ffn_m'],
                               tf_pref=tiles['ffn_f'], emit=True)
            bank['ffn', i] = (w1b, w2b)
        else:
            x = _ffn(x, g_ffn, *bank['ffn', i], i, g_fin, final, tm_pref=tiles['ffn_m'], tf_pref=tiles['ffn_f'])
    return x.reshape(batch, seq, d), new


PROMPT_TILES = dict(proj_m=1024, conv_m=256, lru_m=256, attn_out_m=512, ffn_m=512, ffn_f=1024)
SAMPLE_TILES = dict(proj_m=512, conv_m=256, lru_m=256, attn_out_m=512, ffn_m=512, ffn_f=512)
MATMUL_WEIGHTS = ('conv_w_out', 'lru_w_a', 'lru_w_x', 'lru_w_out', 'att_w_o')


def kernel(x_prompt, x_sample, state_conv_l0, state_lru_h_l1, state_lru_conv_l1, cache_kv_w128_l2, cache_kv_w512_l2, cache_kv_w2048_l2, state_conv_l3, norm_mix_g, norm_ffn_g, final_norm_g, ffn_w1, ffn_w2, conv_w_glu, conv_dw_w, conv_dw_b, conv_ln_g, conv_ln_b, conv_w_out, lru_w_in, lru_conv_w, lru_conv_b, lru_w_a, lru_b_a, lru_w_x, lru_b_x, lru_lambda, lru_w_out, att_w_qkv, att_w_o, rel_bias):
    p = dict(norm_mix_g=norm_mix_g, norm_ffn_g=norm_ffn_g, final_norm_g=final_norm_g, ffn_w1=ffn_w1,
             ffn_w2=ffn_w2, conv_w_glu=conv_w_glu, conv_dw_w=conv_dw_w, conv_dw_b=conv_dw_b,
             conv_ln_g=conv_ln_g, conv_ln_b=conv_ln_b, conv_w_out=conv_w_out, lru_w_in=lru_w_in,
             lru_conv_w=lru_conv_w, lru_conv_b=lru_conv_b, lru_w_a=lru_w_a, lru_b_a=lru_b_a, lru_w_x=lru_w_x,
             lru_b_x=lru_b_x, lru_lambda=lru_lambda, lru_w_out=lru_w_out, att_w_qkv=att_w_qkv,
             att_w_o=att_w_o, rel_bias=rel_bias)
    for name in MATMUL_WEIGHTS:
        p[name] = p[name].astype(BF16)
    bank = {}
    y_sample, sst = _trunk(x_sample, [(state_conv_l0,), (state_lru_h_l1, state_lru_conv_l1),
                                      (cache_kv_w128_l2, cache_kv_w512_l2, cache_kv_w2048_l2),
                                      (state_conv_l3,)], p, SAMPLE_TILES, bank, emit=True)
    y_prompt, pst = _trunk(x_prompt, None, p, PROMPT_TILES, bank, emit=False)
    return (y_prompt, y_sample,
            pst[0][0], sst[0][0],
            pst[1][0], sst[1][0],
            pst[1][1], sst[1][1],
            pst[2][0], sst[2][0],
            pst[2][1], sst[2][1],
            pst[2][2], sst[2][2],
            pst[3][0], sst[3][0])
```

```python
import functools

import numpy as np
import jax
import jax.numpy as jnp
from jax import lax
from jax.experimental import pallas as pl
from jax.experimental.pallas import tpu as pltpu

RMS_EPS = 1e-6
LN_EPS = 1e-5
LRU_C = 8.0
NEG_INF = -1e30
N_MIXERS = 3
GROUPS = ((128, 1), (512, 4), (2048, 16))
N_DIL_KEYS = 129
BLK = 128
CHUNK = BLK * max(d for _, d in GROUPS)
N_BUCKETS = 32
MAX_DISTANCE = 2048

V7X_VMEM_BYTES = 64 * 1024 * 1024
SUBLANES = 8
BF16_SUBLANES = 16
CONV_HALO = 32
LRU_HALO = 8

BF16 = jnp.bfloat16
F32 = jnp.float32


def _cparams(semantics, vmem_bytes):
    limit = int(min(max(vmem_bytes * 3 // 2, 32 * 1024 * 1024), V7X_VMEM_BYTES - 8 * 1024 * 1024))
    return pltpu.CompilerParams(dimension_semantics=semantics, vmem_limit_bytes=limit)


def _tile(n, pref):
    if n <= pref:
        return n
    t = pref
    while n % t:
        t //= 2
    return t


def _rms_to_bf16(x, g):
    ms = jnp.mean(x * x, axis=-1, keepdims=True)
    return (x * lax.rsqrt(ms + RMS_EPS) * g).astype(BF16)


def _dot(a, b):
    return jnp.dot(a, b, preferred_element_type=F32)


def _nt_dot(a, b):
    return lax.dot_general(a, b, (((1,), (1,)), ((), ())), preferred_element_type=F32)


def _norm_proj_kernel(x_ref, g_ref, *refs, mode, emit):
    n_w = 1 if mode == 'plain' else 2
    n_o = 2 if mode == 'split' else 1
    w_refs, o_refs = refs[:n_w], refs[n_w:n_w + n_o]
    wb_refs, h_ref = refs[n_w + n_o:-1], refs[-1]

    @pl.when(pl.program_id(1) == 0)
    def _():
        h_ref[...] = _rms_to_bf16(x_ref[...], g_ref[...])

    ws = [w_ref[...].astype(BF16) for w_ref in w_refs]
    if emit:
        for wb_ref, w in zip(wb_refs, ws):
            wb_ref[...] = w
    h = h_ref[...]
    if mode == 'plain':
        o_refs[0][...] = _dot(h, ws[0])
    elif mode == 'glu':
        o_refs[0][...] = _dot(h, ws[0]) * jax.nn.sigmoid(_dot(h, ws[1]))
    else:
        o_refs[0][...] = jax.nn.gelu(_dot(h, ws[0]), approximate=True)
        o_refs[1][...] = _dot(h, ws[1])


def _wspec(w, layer, block, index_map):
    if w.ndim == 2:
        return pl.BlockSpec(block, index_map)
    return pl.BlockSpec((None,) + block, lambda *ids: (layer,) + index_map(*ids))


def _norm_proj(x, g, w, mode, tm_pref=512, tn_pref=512, layer=0, emit=False):
    m, d = x.shape
    halves = mode != 'plain'
    pair = isinstance(w, tuple)
    n_out = w[0].shape[-1] if pair else (w.shape[-1] // 2 if halves else w.shape[-1])
    tm = _tile(m, tm_pref)
    tn = _tile(n_out, tn_pref)
    nj = n_out // tn
    w_args = list(w) if pair else ([w, w] if halves else [w])
    shift = [0, 0] if pair else [0, nj]
    in_specs = [pl.BlockSpec((tm, d), lambda i, j: (i, 0)), pl.BlockSpec((1, d), lambda i, j: (0, 0))]
    in_specs += [_wspec(wk, layer, (d, tn), lambda i, j, s=s: (0, j + s)) for wk, s in zip(w_args, shift)]
    n_o = 2 if mode == 'split' else 1
    o_shapes = [jax.ShapeDtypeStruct((m, n_out), F32)] * n_o
    o_specs = [pl.BlockSpec((tm, tn), lambda i, j: (i, j))] * n_o
    if emit:
        assert m == tm, "each weight block must be visited once"
        o_shapes += [jax.ShapeDtypeStruct((d, n_out), BF16)] * len(w_args)
        o_specs += [pl.BlockSpec((d, tn), lambda i, j: (0, j))] * len(w_args)
    wbytes = w_args[0].dtype.itemsize
    vmem = (2 * tm * d * 4 + tm * d * 2 + 2 * len(w_args) * d * tn * wbytes + 2 * n_o * tm * tn * 4
            + (3 * len(w_args) * d * tn * 2 if emit else 0))
    outs = pl.pallas_call(
        functools.partial(_norm_proj_kernel, mode=mode, emit=emit),
        name=f"norm_proj_{mode}_m{m}",
        out_shape=o_shapes,
        grid=(m // tm, nj),
        in_specs=in_specs,
        out_specs=o_specs,
        scratch_shapes=[pltpu.VMEM((tm, d), BF16)],
        compiler_params=_cparams(("parallel", "arbitrary"), vmem),
    )(x, g, *w_args)
    return outs[0] if len(outs) == 1 else tuple(outs)


def _ffn_kernel(x_ref, g_ref, w1_ref, w2_ref, fg_ref, *rest, final, emit, cast_next):
    if cast_next:
        n1_ref, n2_ref, o_ref, n1b_ref, n2b_ref, h_ref = rest
        n1b_ref[...] = n1_ref[...].astype(BF16)
        n2b_ref[...] = n2_ref[...].astype(BF16)
    elif emit:
        o_ref, w1b_ref, w2b_ref, h_ref = rest
    else:
        o_ref, h_ref = rest
    f = pl.program_id(1)

    @pl.when(f == 0)
    def _():
        x = x_ref[...]
        h_ref[...] = _rms_to_bf16(x, g_ref[...])
        o_ref[...] = x

    w1 = w1_ref[...].astype(BF16)
    w2 = w2_ref[...].astype(BF16)
    if emit:
        w1b_ref[...] = w1
        w2b_ref[...] = w2
    hid = jnp.square(jnp.maximum(_dot(h_ref[...], w1), 0.0)).astype(BF16)
    o_ref[...] += _dot(hid, w2)

    if final:
        @pl.when(f == pl.num_programs(1) - 1)
        def _():
            y = o_ref[...]
            ms = jnp.mean(y * y, axis=-1, keepdims=True)
            o_ref[...] = y * lax.rsqrt(ms + RMS_EPS) * fg_ref[...]


def _can_cast_next(m, d, dff, tm_pref, tf_pref):
    steps = (m // _tile(m, tm_pref)) * (dff // _tile(dff, tf_pref))
    return d % (BF16_SUBLANES * steps) == 0 and dff % (BF16_SUBLANES * steps) == 0


def _ffn(x, g, w1, w2, layer, final_g, final, tm_pref=512, tf_pref=1024, emit=False, cast_next=None):
    m, d = x.shape
    dff = w1.shape[-1]
    tm = _tile(m, tm_pref)
    tf = _tile(dff, tf_pref)
    nf = dff // tf
    wbytes = w1.dtype.itemsize
    vmem = (2 * tm * d * 4 + tm * d * 2 + 2 * 2 * d * tf * wbytes + 2 * tm * d * 4 + 2 * tm * tf * 4
            + (6 * d * tf * 2 if emit else 0))
    o_shape = [jax.ShapeDtypeStruct((m, d), F32)]
    o_spec = [pl.BlockSpec((tm, d), lambda i, f: (i, 0))]
    in_specs = [pl.BlockSpec((tm, d), lambda i, f: (i, 0)),
                pl.BlockSpec((1, d), lambda i, f: (0, 0)),
                _wspec(w1, layer, (d, tf), lambda i, f: (0, f)),
                _wspec(w2, layer, (tf, d), lambda i, f: (f, 0)),
                pl.BlockSpec((1, d), lambda i, f: (0, 0))]
    args = [x, g, w1, w2, final_g]
    if emit:
        assert m == tm, "each weight block must be visited once"
        o_shape += [jax.ShapeDtypeStruct((d, dff), BF16), jax.ShapeDtypeStruct((dff, d), BF16)]
        o_spec += [pl.BlockSpec((d, tf), lambda i, f: (0, f)), pl.BlockSpec((tf, d), lambda i, f: (f, 0))]
    if cast_next is not None:
        n1, n2, nxt = cast_next
        steps = (m // tm) * nf
        r1, r2 = d // steps, dff // steps
        slab = lambda i, f: (i * nf + f, 0)
        in_specs += [_wspec(n1, nxt, (r1, dff), slab), _wspec(n2, nxt, (r2, d), slab)]
        args += [n1, n2]
        o_shape += [jax.ShapeDtypeStruct((d, dff), BF16), jax.ShapeDtypeStruct((dff, d), BF16)]
        o_spec += [pl.BlockSpec((r1, dff), slab), pl.BlockSpec((r2, d), slab)]
        vmem += 6 * (r1 * dff + r2 * d) * 4
    outs = pl.pallas_call(
        functools.partial(_ffn_kernel, final=final, emit=emit, cast_next=cast_next is not None),
        name=f"ffn_m{m}_l{layer}",
        out_shape=o_shape,
        grid=(m // tm, nf),
        in_specs=in_specs,
        out_specs=o_spec,
        scratch_shapes=[pltpu.VMEM((tm, d), BF16)],
        compiler_params=_cparams(("arbitrary", "arbitrary"), vmem),
    )(*args)
    return outs[0] if len(outs) == 1 else tuple(outs)


def _conv_tail_kernel(x_ref, u_ref, halo_ref, dww_ref, dwb_ref, lng_ref, lnb_ref, wout_ref,
                      o_ref, uext_ref, y_ref, *, n_taps, rows, ln_rows, lanes):
    i = pl.program_id(0)
    tm, c = u_ref.shape
    cur = lax.rem(i, 2)

    @pl.when(i == 0)
    def _():
        y_ref[1] = jnp.zeros((tm, c), BF16)

    uext_ref[0:CONV_HALO, :] = halo_ref[0]
    uext_ref[CONV_HALO:CONV_HALO + tm, :] = u_ref[...]
    off = CONV_HALO - (n_taps - 1)

    def row_chunk(r, carry):
        r0 = pl.multiple_of(r * rows, rows)
        for l0 in range(0, c, lanes):
            win = uext_ref[pl.ds(r0, rows + CONV_HALO), l0:l0 + lanes]
            accs = [jnp.zeros((SUBLANES, lanes), F32)] * (rows // SUBLANES)
            for s in range(SUBLANES):
                ws = win if s == 0 else pltpu.roll(win, rows + CONV_HALO - s, axis=0)
                for a in range((CONV_HALO + SUBLANES) // SUBLANES):
                    k = a * SUBLANES + s - off
                    if 0 <= k < n_taps:
                        wk = dww_ref[k * SUBLANES:(k + 1) * SUBLANES, l0:l0 + lanes]
                        accs = [acc + wk * ws[(a + j) * SUBLANES:(a + j + 1) * SUBLANES]
                                for j, acc in enumerate(accs)]
            for j, acc in enumerate(accs):
                uext_ref[pl.ds(r0 + j * SUBLANES, SUBLANES), l0:l0 + lanes] = acc + dwb_ref[:, l0:l0 + lanes]
        return carry

    @pl.when(i < pl.num_programs(0) - 1)
    def _():
        lax.fori_loop(0, tm // rows, row_chunk, 0)

    o_ref[...] = x_ref[...] + _dot(y_ref[1 - cur], wout_ref[...])
    for r0 in range(0, tm, ln_rows):
        cv = uext_ref[r0:r0 + ln_rows, :]
        mu = jnp.mean(cv, axis=-1, keepdims=True)
        cc = cv - mu
        var = jnp.mean(cc * cc, axis=-1, keepdims=True)
        y = cc * lax.rsqrt(var + LN_EPS) * lng_ref[...] + lnb_ref[...]
        y_ref[cur, r0:r0 + ln_rows, :] = (y * jax.nn.sigmoid(y)).astype(BF16)


def _conv_tail(x, u, halo, dww, dwb, lng, lnb, wout, layer, tm, n_taps):
    m, d = x.shape
    c = u.shape[1]
    rows = _tile(tm, 128)
    ln_rows = _tile(tm, 16)
    lanes = _tile(c, 128)
    nt = m // tm
    prev = lambda i: (jnp.maximum(i - 1, 0), 0)
    this = lambda i: (jnp.minimum(i, nt - 1), 0)
    vmem = 4 * tm * d * 4 + 2 * tm * c * 4 + 2 * c * d * 2 + (tm + CONV_HALO) * c * 4 + 2 * tm * c * 2
    return pl.pallas_call(
        functools.partial(_conv_tail_kernel, n_taps=n_taps, rows=rows, ln_rows=ln_rows, lanes=lanes),
        name=f"conv_tail_m{m}",
        out_shape=jax.ShapeDtypeStruct((m, d), F32),
        grid=(nt + 1,),
        in_specs=[pl.BlockSpec((tm, d), prev),
                  pl.BlockSpec((tm, c), this),
                  pl.BlockSpec((1, CONV_HALO, c), lambda i: (jnp.minimum(i, nt - 1), 0, 0)),
                  pl.BlockSpec(dww.shape, lambda i: (0, 0)),
                  pl.BlockSpec((1, c), lambda i: (0, 0)),
                  pl.BlockSpec((1, c), lambda i: (0, 0)),
                  pl.BlockSpec((1, c), lambda i: (0, 0)),
                  _wspec(wout, layer, (c, d), lambda i: (0, 0))],
        out_specs=pl.BlockSpec((tm, d), prev),
        scratch_shapes=[pltpu.VMEM((tm + CONV_HALO, c), F32), pltpu.VMEM((2, tm, c), BF16)],
        compiler_params=_cparams(("arbitrary",), vmem),
    )(x, u, halo, dww, dwb, lng, lnb, wout)


def _last_rows(past, seq, n):
    if seq.shape[1] >= n:
        return seq[:, seq.shape[1] - n:]
    return jnp.concatenate([past, seq], axis=1)[:, -n:]


def _halos(seq, past, tm, halo_rows):
    b, l, c = seq.shape
    p = past.shape[1]
    first = jnp.concatenate([jnp.zeros((b, halo_rows - p, c), seq.dtype), past], axis=1)[:, None]
    nt = l // tm
    if nt == 1:
        return first.reshape(b, halo_rows, c)
    rest = seq.reshape(b, nt, tm, c)[:, :-1, tm - halo_rows:]
    return jnp.concatenate([first, rest], axis=1).reshape(b * nt, halo_rows, c)


def _lru_tail_kernel(x_ref, ybr_ref, xbr_ref, halo_ref, h0_ref, cw_ref, cb_ref, wa_ref, ba_ref,
                     wx_ref, bx_ref, lam_ref, wout_ref, o_ref, hlast_ref,
                     xext_ref, a_ref, b_ref, hc_ref, *, n_taps, lanes):
    tm, c = xbr_ref.shape
    n_blocks, blk, _ = wa_ref.shape
    t = pl.program_id(1)

    @pl.when(t == 0)
    def _():
        hc_ref[...] = jnp.broadcast_to(h0_ref[0], hc_ref.shape)

    xext_ref[0:LRU_HALO, :] = halo_ref[0]
    xext_ref[LRU_HALO:LRU_HALO + tm, :] = xbr_ref[...]
    off = LRU_HALO - (n_taps - 1)

    sp = jax.nn.softplus(-lam_ref[...])
    for n in range(n_blocks):
        sl = slice(n * blk, (n + 1) * blk)
        xc = jnp.zeros((tm, blk), F32)
        for k in range(n_taps):
            xc = xc + cw_ref[k:k + 1, sl] * xext_ref[off + k:off + k + tm, sl]
        xc = xc + cb_ref[:, sl]
        xcb = xc.astype(BF16)
        r = jax.nn.sigmoid(_dot(xcb, wa_ref[n]) + ba_ref[:, sl])
        gi = jax.nn.sigmoid(_dot(xcb, wx_ref[n]) + bx_ref[:, sl])
        log_a = -LRU_C * r * sp[:, sl]
        a = jnp.exp(log_a)
        a_ref[:, sl] = a
        b_ref[:, sl] = jnp.sqrt(-jnp.tanh(log_a) * (a * a + 1.0)) * (gi * xc)

    row = lax.broadcasted_iota(jnp.int32, (SUBLANES, lanes), 0)

    for l0 in range(0, c, lanes):
        def group(gidx, h):
            r0 = pl.multiple_of(gidx * SUBLANES, SUBLANES)
            av = a_ref[pl.ds(r0, SUBLANES), l0:l0 + lanes]
            bv = b_ref[pl.ds(r0, SUBLANES), l0:l0 + lanes]
            for s in (1, 2, 4):
                a_sh = jnp.where(row >= s, pltpu.roll(av, s, axis=0), 1.0)
                b_sh = jnp.where(row >= s, pltpu.roll(bv, s, axis=0), 0.0)
                bv = av * b_sh + bv
                av = av * a_sh
            hs = av * h + bv
            b_ref[pl.ds(r0, SUBLANES), l0:l0 + lanes] = hs
            return jnp.broadcast_to(hs[SUBLANES - 1:SUBLANES, :], (SUBLANES, lanes))

        h_fin = lax.fori_loop(0, tm // SUBLANES, group, hc_ref[:, l0:l0 + lanes], unroll=2)
        hc_ref[:, l0:l0 + lanes] = h_fin

    gated = (b_ref[...] * ybr_ref[...]).astype(BF16)
    o_ref[...] = x_ref[...] + _dot(gated, wout_ref[...])

    @pl.when(t == pl.num_programs(1) - 1)
    def _():
        hlast_ref[0] = hc_ref[...]


def _lru_tail(x, ybr, xbr, halo, h0, cw, cb, wa, ba, wx, bx, lam, wout, batch, tm, n_taps):
    m, d = x.shape
    c = xbr.shape[1]
    nt = m // batch // tm
    lanes = _tile(c, 512)
    row2 = lambda b, t: (b * nt + t, 0)
    const2 = lambda b, t: (0, 0)
    vec = pl.BlockSpec((1, c), const2)
    vmem = (4 * tm * d * 4 + 4 * tm * c * 4 + 2 * c * d * 2 + 4 * wa.size * 2
            + (tm + LRU_HALO) * c * 4 + 2 * tm * c * 4)
    return pl.pallas_call(
        functools.partial(_lru_tail_kernel, n_taps=n_taps, lanes=lanes),
        name=f"lru_tail_m{m}",
        out_shape=[jax.ShapeDtypeStruct((m, d), F32),
                   jax.ShapeDtypeStruct((batch, SUBLANES, c), F32)],
        grid=(batch, nt),
        in_specs=[pl.BlockSpec((tm, d), row2),
                  pl.BlockSpec((tm, c), row2),
                  pl.BlockSpec((tm, c), row2),
                  pl.BlockSpec((1, LRU_HALO, c), lambda b, t: (b * nt + t, 0, 0)),
                  pl.BlockSpec((1, 1, c), lambda b, t: (b, 0, 0)),
                  pl.BlockSpec(cw.shape, const2),
                  vec,
                  pl.BlockSpec(wa.shape, lambda b, t: (0, 0, 0)),
                  vec,
                  pl.BlockSpec(wx.shape, lambda b, t: (0, 0, 0)),
                  vec, vec,
                  pl.BlockSpec((c, d), const2)],
        out_specs=[pl.BlockSpec((tm, d), row2),
                   pl.BlockSpec((1, SUBLANES, c), lambda b, t: (b, 0, 0))],
        scratch_shapes=[pltpu.VMEM((tm + LRU_HALO, c), F32), pltpu.VMEM((tm, c), F32),
                        pltpu.VMEM((tm, c), F32), pltpu.VMEM((SUBLANES, c), F32)],
        compiler_params=_cparams(("parallel", "arbitrary"), vmem),
    )(x, ybr, xbr, halo, h0, cw, cb, wa, ba, wx, bx, lam, wout)


def _shifted_rows(vec, n_rows):
    return pltpu.roll(jnp.broadcast_to(vec, (n_rows, vec.shape[1])), 0, 1, stride=1, stride_axis=0)


def _block_softmax(s_parts, v_parts):
    m = s_parts[0].max(axis=-1, keepdims=True)
    for s in s_parts[1:]:
        m = jnp.maximum(m, s.max(axis=-1, keepdims=True))
    o, den = None, None
    for s, v in zip(s_parts, v_parts):
        p = jnp.exp(s - m)
        d = p.sum(axis=-1, keepdims=True)
        pv = _dot(p.astype(BF16), v)
        o, den = (pv, d) if o is None else (o + pv, den + d)
    return o, m, den


def _block_softmax_mxu_sum(s_parts, v_parts, exp=jnp.exp):
    m = s_parts[0].max(axis=-1, keepdims=True)
    for s in s_parts[1:]:
        m = jnp.maximum(m, s.max(axis=-1, keepdims=True))
    acc = None
    for s, v in zip(s_parts, v_parts):
        pv = _dot(exp(s - m).astype(BF16), v)
        acc = pv if acc is None else acc + pv
    dh = acc.shape[1] // 2
    return acc[:, :dh], m, acc[:, dh:]


def _merge(state, o_b, m_b, l_b, exp=jnp.exp):
    if state is None:
        return o_b, m_b, l_b
    acc, m_old, l_old = state
    m_new = jnp.maximum(m_old, m_b)
    a_old = exp(m_old - m_new)
    a_b = exp(m_b - m_new)
    return acc * a_old + o_b * a_b, m_new, l_old * a_old + l_b * a_b


def _attn_prompt_kernel(*refs, heads, dh):
    n_g = len(GROUPS)
    ins = refs[:5 * n_g]
    bvec_ref, o_ref, acc_ref, m_ref, l_ref = refs[5 * n_g:]
    first_chunk = pl.program_id(1) == 0
    h = pl.program_id(2)
    log2e = float(np.log2(np.e))
    scale = dh ** -0.5 * log2e
    qi = lax.broadcasted_iota(jnp.int32, (BLK, 2 * BLK), 0)
    kj = lax.broadcasted_iota(jnp.int32, (BLK, 2 * BLK), 1)
    in_band = (kj - qi >= 0) & (kj - qi <= BLK)
    ones = jnp.ones((BLK, dh), BF16)
    for g, (_, dil) in enumerate(GROUPS):
        q_ref, k_ref, v_ref, kp_ref, vp_ref = ins[5 * g:5 * g + 5]
        bias = jnp.where(in_band, _shifted_rows(bvec_ref[pl.ds(g * heads + h, 1), :] * log2e, BLK), NEG_INF)
        bias_p, bias_c = bias[:, :BLK], bias[:, BLK:]
        for r in range(dil):
            prev_rows = pl.ds(r, BLK, stride=dil)
            k_prev = kp_ref[prev_rows, :].astype(BF16)
            v_prev = jnp.concatenate([vp_ref[prev_rows, :].astype(BF16), ones], axis=1)
            for nb in range(CHUNK // (dil * BLK)):
                rows = pl.ds(r + dil * BLK * nb, BLK, stride=dil)
                q = q_ref[rows, :].astype(BF16)
                k_cur = k_ref[rows, :].astype(BF16)
                v_cur = jnp.concatenate([v_ref[rows, :].astype(BF16), ones], axis=1)
                s_p = _nt_dot(q, k_prev) * scale + bias_p
                if nb == 0:
                    s_p = jnp.where(first_chunk, NEG_INF, s_p)
                s_c = _nt_dot(q, k_cur) * scale + bias_c
                o_b, m_b, l_b = _block_softmax_mxu_sum([s_p, s_c], [v_prev, v_cur], exp=jnp.exp2)
                m_b = jnp.broadcast_to(m_b, (BLK, dh))
                state = None if g == 0 else (acc_ref[rows, :], m_ref[rows, :], l_ref[rows, :])
                acc, m_new, l_new = _merge(state, o_b, m_b, l_b, exp=jnp.exp2)
                if g == n_g - 1:
                    o_ref[rows, :] = acc / l_new
                else:
                    acc_ref[rows, :] = acc
                    m_ref[rows, :] = m_new
                    l_ref[rows, :] = l_new
                k_prev, v_prev = k_cur, v_cur


def _attn_prompt(qkv, bvec, batch, seq, heads, dh):
    n_g = len(GROUPS)
    nc = seq // CHUNK
    col = lambda comp, g: (lambda b, n, h: (b * nc + n, (comp * n_g + g) * heads + h))
    in_specs, args = [], []
    for g, (win, _) in enumerate(GROUPS):
        per = CHUNK // win
        prev = lambda comp, g=g, per=per: (
            lambda b, n, h: (jnp.maximum((b * nc + n) * per - 1, 0), (comp * n_g + g) * heads + h))
        in_specs += [pl.BlockSpec((CHUNK, dh), col(0, g)), pl.BlockSpec((CHUNK, dh), col(1, g)),
                     pl.BlockSpec((CHUNK, dh), col(2, g)),
                     pl.BlockSpec((win, dh), prev(1)), pl.BlockSpec((win, dh), prev(2))]
        args += [qkv] * 5
    in_specs.append(pl.BlockSpec(bvec.shape, lambda b, n, h: (0, 0)))
    vmem = 2 * (9 * CHUNK + 2 * sum(w for w, _ in GROUPS)) * dh * 4 + 2 * CHUNK * dh * 4 + 3 * CHUNK * dh * 4
    return pl.pallas_call(
        functools.partial(_attn_prompt_kernel, heads=heads, dh=dh),
        name="attn_prompt",
        out_shape=jax.ShapeDtypeStruct((batch * seq, heads * dh), F32),
        grid=(batch, nc, heads),
        in_specs=in_specs,
        out_specs=pl.BlockSpec((CHUNK, dh), lambda b, n, h: (b * nc + n, h)),
        scratch_shapes=[pltpu.VMEM((CHUNK, dh), F32)] * 3,
        compiler_params=_cparams(("parallel", "parallel", "parallel"), vmem),
    )(*args, bvec)


def _attn_step_kernel(*refs, heads, dh, n_bufs):
    n_g = len(GROUPS)
    ins = refs[:6 * n_g]
    o_ref, acc_ref, m_ref, l_ref, knp_ref, vnp_ref = refs[6 * n_g:]
    part = pl.program_id(1)
    t = o_ref.shape[1]
    scale = dh ** -0.5
    row_stride = 2 * heads

    def merge_into(hs, o_b, m_b, l_b):
        state = (acc_ref[:, hs], m_ref[:, hs], l_ref[:, hs])
        acc, m_new, l_new = _merge(state, o_b, jnp.broadcast_to(m_b, (t, dh)), jnp.broadcast_to(l_b, (t, dh)))
        acc_ref[:, hs] = acc
        m_ref[:, hs] = m_new
        l_ref[:, hs] = l_new

    def cache_piece(g, first_row):
        dil = GROUPS[g][1]
        q_ref, _, _, cache_ref, yc_ref, _ = ins[6 * g:6 * g + 6]
        rows = cache_ref.shape[1] // row_stride
        tc = lax.broadcasted_iota(jnp.int32, (t, rows), 0)
        cc = lax.broadcasted_iota(jnp.int32, (t, rows), 1) + first_row
        dist_c = n_bufs[g] + tc - cc
        ok_c = ((dist_c & (dil - 1)) == 0) & (dist_c <= (N_DIL_KEYS - 1) * dil)
        for h in range(heads):
            hs = slice(h * dh, (h + 1) * dh)
            q = q_ref[0, :, hs].astype(BF16)
            kc = cache_ref[0, pl.ds(h, rows, stride=row_stride), :].astype(BF16)
            vc = cache_ref[0, pl.ds(heads + h, rows, stride=row_stride), :].astype(BF16)
            bias = _shifted_rows(yc_ref[0, h:h + 1, :], t)[:, BLK:]
            s_c = jnp.where(ok_c, _nt_dot(q, kc) * scale + bias, NEG_INF)
            merge_into(hs, *_block_softmax([s_c], [vc]))

    @pl.when(part == 0)
    def _():
        acc_ref[...] = jnp.zeros_like(acc_ref)
        l_ref[...] = jnp.zeros_like(l_ref)
        m_ref[...] = jnp.full_like(m_ref, NEG_INF)
        tn = lax.broadcasted_iota(jnp.int32, (t, BLK), 0)
        cn = lax.broadcasted_iota(jnp.int32, (t, BLK), 1)
        dist_n = tn - cn
        for g, (_, dil) in enumerate(GROUPS):
            q_ref, kn_ref, vn_ref, _, _, yn_ref = ins[6 * g:6 * g + 6]
            ok_n = (dist_n >= 0) & ((dist_n & (dil - 1)) == 0) & (dist_n <= (N_DIL_KEYS - 1) * dil)
            knp_ref[...] = jnp.zeros_like(knp_ref)
            vnp_ref[...] = jnp.zeros_like(vnp_ref)
            knp_ref[0:t, :] = kn_ref[0]
            vnp_ref[0:t, :] = vn_ref[0]
            for h in range(heads):
                hs = slice(h * dh, (h + 1) * dh)
                q = q_ref[0, :, hs].astype(BF16)
                bias = _shifted_rows(yn_ref[h:h + 1, :], t)[:, BLK:]
                s_n = jnp.where(ok_n, _nt_dot(q, knp_ref[:, hs].astype(BF16)) * scale + bias, NEG_INF)
                merge_into(hs, *_block_softmax([s_n], [vnp_ref[:, hs].astype(BF16)]))
        for g in range(n_g):
            if ins[6 * g + 3].shape[1] == n_bufs[g] * row_stride:
                cache_piece(g, 0)

    for g in range(n_g):
        rows = ins[6 * g + 3].shape[1] // row_stride
        if rows != n_bufs[g]:
            cache_piece(g, part * rows)

    @pl.when(part == pl.num_programs(1) - 1)
    def _():
        o_ref[0] = acc_ref[...] / l_ref[...]


def _attn_step(qkv, caches, rel_bias, batch, t, heads, dh, n_parts=2):
    n_g = len(GROUPS)
    width = heads * dh
    qkv3 = qkv.reshape(batch, t, qkv.shape[1])
    in_specs, args, n_bufs = [], [], []
    vmem = 8 * t * width * 4
    for g in range(n_g):
        n_buf = caches[g].shape[1]
        parts = n_parts if n_buf % (n_parts * BLK) == 0 else 1
        rows = n_buf // parts
        flat = caches[g].reshape(batch, n_buf * 2 * heads, dh)
        yc, yn = _step_bias_vectors(rel_bias, g, heads, n_buf, parts)
        new = lambda comp, g=g: pl.BlockSpec((1, t, width), lambda b, s: (b, 0, comp * n_g + g))
        part_of = (lambda s: s) if parts == n_parts else (lambda s: 0)
        in_specs += [new(0), new(1), new(2),
                     pl.BlockSpec((1, rows * 2 * heads, dh), lambda b, s, f=part_of: (b, f(s), 0)),
                     pl.BlockSpec((1, heads, rows + BLK), lambda b, s, f=part_of: (f(s), 0, 0)),
                     pl.BlockSpec(yn.shape, lambda b, s: (0, 0))]
        args += [qkv3, qkv3, qkv3, flat, yc, yn]
        n_bufs.append(n_buf)
        vmem += 2 * rows * 2 * width * 4 + 6 * rows * dh * 4
    return pl.pallas_call(
        functools.partial(_attn_step_kernel, heads=heads, dh=dh, n_bufs=tuple(n_bufs)),
        name="attn_step",
        out_shape=jax.ShapeDtypeStruct((batch, t, width), F32),
        grid=(batch, n_parts),
        in_specs=in_specs,
        out_specs=pl.BlockSpec((1, t, width), lambda b, s: (b, 0, 0)),
        scratch_shapes=[pltpu.VMEM((t, width), F32)] * 3 + [pltpu.VMEM((BLK, width), F32)] * 2,
        compiler_params=_cparams(("parallel", "arbitrary"), vmem),
    )(*args).reshape(batch * t, width)


def _attn_out_kernel(x_ref, o_ref, wo_ref, out_ref):
    out_ref[...] = x_ref[...] + _dot(o_ref[...].astype(BF16), wo_ref[...])


def _attn_out(x, o, wo, tm_pref=512):
    m, d = x.shape
    width = wo.shape[0]
    tm = _tile(m, tm_pref)
    row = lambda n: pl.BlockSpec((tm, n), lambda i: (i, 0))
    vmem = 4 * tm * d * 4 + 2 * tm * width * 4 + 2 * width * d * 2
    return pl.pallas_call(
        _attn_out_kernel,
        name=f"attn_out_m{m}",
        out_shape=jax.ShapeDtypeStruct((m, d), F32),
        grid=(m // tm,),
        in_specs=[row(d), row(width), pl.BlockSpec((width, d), lambda i: (0, 0))],
        out_specs=row(d),
        compiler_params=_cparams(("parallel",), vmem),
    )(x, o, wo)


def _t5_bucket(dist):
    max_exact = N_BUCKETS // 2
    d = np.asarray(dist, dtype=np.int32)
    large = max_exact + (np.log(np.maximum(d, max_exact) / max_exact) / np.log(MAX_DISTANCE / max_exact)
                         * (N_BUCKETS - max_exact)).astype(np.int32)
    return np.where(d < max_exact, d, np.minimum(large, N_BUCKETS - 1)).astype(np.int32)


def _group_bias(rel_bias, g, heads):
    buckets = _t5_bucket(np.arange(N_DIL_KEYS) * GROUPS[g][1])
    return rel_bias[:, g * heads:(g + 1) * heads].astype(F32)[buckets]


def _prompt_bias_vectors(rel_bias, heads):
    vecs = []
    for g in range(len(GROUPS)):
        rev = _group_bias(rel_bias, g, heads)[::-1]
        vecs.append(jnp.pad(rev, ((0, 2 * BLK - N_DIL_KEYS), (0, 0))).T)
    return jnp.concatenate(vecs, axis=0)


def _step_bias_vectors(rel_bias, g, heads, n_buf, parts):
    rows = n_buf // parts
    bias = _group_bias(rel_bias, g, heads)
    at = lambda dist: bias[np.clip(dist // GROUPS[g][1], 0, N_DIL_KEYS - 1)]
    first = np.arange(parts)[:, None] * rows + np.arange(rows + BLK)[None, :]
    yc = jnp.transpose(at(n_buf + BLK - first), (0, 2, 1))
    yn = at(BLK - np.arange(2 * BLK)).T
    return yc, yn


def _kv_rows_kernel(k_ref, v_ref, o_ref, *, heads, dh):
    rows = k_ref.shape[0]
    o_ref[0, :, 0] = k_ref[...].reshape(rows, heads, dh)
    o_ref[0, :, 1] = v_ref[...].reshape(rows, heads, dh)


def _kv_rows(qkv, g, batch, seq, n_rows, heads, dh):
    n_g = len(GROUPS)
    width = heads * dh
    tm = _tile(n_rows, 256)
    assert seq % tm == 0 and n_rows % tm == 0
    first = (seq - n_rows) // tm
    per_seq = seq // tm
    col = lambda comp: pl.BlockSpec((tm, width), lambda b, i: (b * per_seq + first + i, comp * n_g + g))
    return pl.pallas_call(
        functools.partial(_kv_rows_kernel, heads=heads, dh=dh),
        name=f"kv_rows_g{g}_n{n_rows}",
        out_shape=jax.ShapeDtypeStruct((batch, n_rows, 2, heads, dh), qkv.dtype),
        grid=(batch, n_rows // tm),
        in_specs=[col(1), col(2)],
        out_specs=pl.BlockSpec((1, tm, 2, heads, dh), lambda b, i: (b, i, 0, 0, 0)),
        compiler_params=_cparams(("parallel", "parallel"), 8 * tm * width * 4),
    )(qkv, qkv)


def _trunk(x3, st, p, tiles, bank, emit):
    batch, seq, d = x3.shape
    m = batch * seq
    x = x3.reshape(m, d)
    new = []
    depth = p['ffn_w1'].shape[0]
    for i in range(depth):
        kind = i % N_MIXERS
        g_mix = p['norm_mix_g'][i][None]
        if kind == 0:
            j = i // N_MIXERS
            n_taps, c = p['conv_dw_w'].shape[1:]
            buf = jnp.zeros((batch, n_taps - 1, c), F32) if st is None else st[i][0]
            if emit:
                u, *bank['glu', j] = _norm_proj(x, g_mix, p['conv_w_glu'], 'glu', tiles['proj_m'], layer=j, emit=True)
            else:
                u = _norm_proj(x, g_mix, tuple(bank['glu', j]), 'glu', tiles['proj_m'])
            u3 = u.reshape(batch, seq, c)
            tm = _tile(seq, tiles['conv_m'])
            halo = _halos(u3, buf, tm, CONV_HALO)
            dww = jnp.repeat(p['conv_dw_w'][j], SUBLANES, axis=0)
            x = _conv_tail(x, u, halo, dww, p['conv_dw_b'][j][None], p['conv_ln_g'][j][None],
                           p['conv_ln_b'][j][None], p['conv_w_out'], j, tm, n_taps)
            new.append((_last_rows(buf, u3, n_taps - 1),))
        elif kind == 1:
            n_taps, c = p['lru_conv_w'].shape
            if st is None:
                h0 = jnp.zeros((batch, c), F32)
                cb = jnp.zeros((batch, n_taps - 1, c), F32)
            else:
                h0, cb = st[i]
            if emit:
                ybr, xbr, *bank['lru_in'] = _norm_proj(x, g_mix, p['lru_w_in'], 'split', tiles['proj_m'], emit=True)
            else:
                ybr, xbr = _norm_proj(x, g_mix, tuple(bank['lru_in']), 'split', tiles['proj_m'])
            xbr3 = xbr.reshape(batch, seq, c)
            tm = _tile(seq, tiles['lru_m'])
            halo = _halos(xbr3, cb, tm, LRU_HALO)
            cw = jnp.pad(p['lru_conv_w'], ((0, SUBLANES - n_taps), (0, 0)))
            x, hl = _lru_tail(x, ybr, xbr, halo, h0[:, None], cw, p['lru_conv_b'][None], p['lru_w_a'],
                              p['lru_b_a'][None], p['lru_w_x'], p['lru_b_x'][None], p['lru_lambda'][None],
                              p['lru_w_out'], batch, tm, n_taps)
            new.append((hl[:, 0], _last_rows(cb, xbr3, n_taps - 1)))
        else:
            rel_bias = p['rel_bias']
            heads = rel_bias.shape[1] // len(GROUPS)
            width = p['att_w_o'].shape[0]
            dh = width // heads
            if emit:
                qkv, bank['qkv'] = _norm_proj(x, g_mix, p['att_w_qkv'], 'plain', tiles['proj_m'], emit=True)
            else:
                qkv = _norm_proj(x, g_mix, bank['qkv'], 'plain', tiles['proj_m'], tn_pref=1024)
            if st is None:
                assert seq % CHUNK == 0
                o = _attn_prompt(qkv, _prompt_bias_vectors(rel_bias, heads), batch, seq, heads, dh)
                rows = [_kv_rows(qkv, g, batch, seq, min(win, seq), heads, dh) for g, (win, _) in enumerate(GROUPS)]
            else:
                assert seq <= BLK
                o = _attn_step(qkv, st[i], rel_bias, batch, seq, heads, dh)
                rows = [_kv_rows(qkv, g, batch, seq, seq, heads, dh) for g in range(len(GROUPS))]
            x = _attn_out(x, o, p['att_w_o'], tiles['attn_out_m'])
            new.append(rows)
        g_ffn, g_fin, final = p['norm_ffn_g'][i][None], p['final_norm_g'][None], i == depth - 1
        ffn_tiles = dict(tm_pref=tiles['ffn_m'], tf_pref=tiles['ffn_f'])
        if emit and ('ffn', i) not in bank:
            x, *bank['ffn', i] = _ffn(x, g_ffn, p['ffn_w1'], p['ffn_w2'], i, g_fin, final, emit=True, **ffn_tiles)
        elif emit or i + 1 == depth or not _can_cast_next(m, d, p['ffn_w1'].shape[-1], **ffn_tiles):
            x = _ffn(x, g_ffn, *bank['ffn', i], i, g_fin, final, **ffn_tiles)
        else:
            x, *bank['ffn', i + 1] = _ffn(x, g_ffn, *bank['ffn', i], i, g_fin, final, **ffn_tiles,
                                          cast_next=(p['ffn_w1'], p['ffn_w2'], i + 1))
        yield
    return x.reshape(batch, seq, d), new


def _run_alternating(*gens):
    results = [None] * len(gens)
    live = list(range(len(gens)))
    while live:
        for k in list(live):
            try:
                next(gens[k])
            except StopIteration as done:
                results[k] = done.value
                live.remove(k)
    return results


PROMPT_TILES = dict(proj_m=1024, conv_m=256, lru_m=256, attn_out_m=512, ffn_m=512, ffn_f=1024)
SAMPLE_TILES = dict(proj_m=512, conv_m=256, lru_m=256, attn_out_m=512, ffn_m=512, ffn_f=512)
MATMUL_WEIGHTS = ('conv_w_out', 'lru_w_a', 'lru_w_x', 'lru_w_out', 'att_w_o')


def kernel(x_prompt, x_sample, state_conv_l0, state_lru_h_l1, state_lru_conv_l1, cache_kv_w128_l2, cache_kv_w512_l2, cache_kv_w2048_l2, state_conv_l3, norm_mix_g, norm_ffn_g, final_norm_g, ffn_w1, ffn_w2, conv_w_glu, conv_dw_w, conv_dw_b, conv_ln_g, conv_ln_b, conv_w_out, lru_w_in, lru_conv_w, lru_conv_b, lru_w_a, lru_b_a, lru_w_x, lru_b_x, lru_lambda, lru_w_out, att_w_qkv, att_w_o, rel_bias):
    p = dict(norm_mix_g=norm_mix_g, norm_ffn_g=norm_ffn_g, final_norm_g=final_norm_g, ffn_w1=ffn_w1,
             ffn_w2=ffn_w2, conv_w_glu=conv_w_glu, conv_dw_w=conv_dw_w, conv_dw_b=conv_dw_b,
             conv_ln_g=conv_ln_g, conv_ln_b=conv_ln_b, conv_w_out=conv_w_out, lru_w_in=lru_w_in,
             lru_conv_w=lru_conv_w, lru_conv_b=lru_conv_b, lru_w_a=lru_w_a, lru_b_a=lru_b_a, lru_w_x=lru_w_x,
             lru_b_x=lru_b_x, lru_lambda=lru_lambda, lru_w_out=lru_w_out, att_w_qkv=att_w_qkv,
             att_w_o=att_w_o, rel_bias=rel_bias)
    for name in MATMUL_WEIGHTS:
        p[name] = p[name].astype(BF16)
    bank = {}
    sample = _trunk(x_sample, [(state_conv_l0,), (state_lru_h_l1, state_lru_conv_l1),
                               (cache_kv_w128_l2, cache_kv_w512_l2, cache_kv_w2048_l2),
                               (state_conv_l3,)], p, SAMPLE_TILES, bank, emit=True)
    prompt = _trunk(x_prompt, None, p, PROMPT_TILES, bank, emit=False)
    (y_sample, sst), (y_prompt, pst) = _run_alternating(sample, prompt)
    return (y_prompt, y_sample,
            pst[0][0], sst[0][0],
            pst[1][0], sst[1][0],
            pst[1][1], sst[1][1],
            pst[2][0], sst[2][0],
            pst[2][1], sst[2][1],
            pst[2][2], sst[2][2],
            pst[3][0], sst[3][0])
```

```python
import functools

import numpy as np
import jax
import jax.numpy as jnp
from jax import lax
from jax.experimental import pallas as pl
from jax.experimental.pallas import tpu as pltpu

RMS_EPS = 1e-6
LN_EPS = 1e-5
LRU_C = 8.0
NEG_INF = -1e30
N_MIXERS = 3
GROUPS = ((128, 1), (512, 4), (2048, 16))
N_DIL_KEYS = 129
BLK = 128
CHUNK = BLK * max(d for _, d in GROUPS)
N_BUCKETS = 32
MAX_DISTANCE = 2048

V7X_VMEM_BYTES = 64 * 1024 * 1024
SUBLANES = 8
BF16_SUBLANES = 16
CONV_HALO = 32
LRU_HALO = 8

BF16 = jnp.bfloat16
F32 = jnp.float32


def _cparams(semantics, vmem_bytes):
    limit = int(min(max(vmem_bytes * 3 // 2, 32 * 1024 * 1024), V7X_VMEM_BYTES - 8 * 1024 * 1024))
    return pltpu.CompilerParams(dimension_semantics=semantics, vmem_limit_bytes=limit)


def _tile(n, pref):
    if n <= pref:
        return n
    t = pref
    while n % t:
        t //= 2
    return t


def _rms_to_bf16(x, g):
    ms = jnp.mean(x * x, axis=-1, keepdims=True)
    return (x * lax.rsqrt(ms + RMS_EPS) * g).astype(BF16)


def _dot(a, b):
    return jnp.dot(a, b, preferred_element_type=F32)


def _nt_dot(a, b):
    return lax.dot_general(a, b, (((1,), (1,)), ((), ())), preferred_element_type=F32)


def _norm_proj_kernel(x_ref, g_ref, *refs, mode, emit):
    n_w = 1 if mode == 'plain' else 2
    n_o = 2 if mode == 'split' else 1
    w_refs, o_refs = refs[:n_w], refs[n_w:n_w + n_o]
    wb_refs, h_ref = refs[n_w + n_o:-1], refs[-1]

    @pl.when(pl.program_id(1) == 0)
    def _():
        h_ref[...] = _rms_to_bf16(x_ref[...], g_ref[...])

    ws = [w_ref[...].astype(BF16) for w_ref in w_refs]
    if emit:
        for wb_ref, w in zip(wb_refs, ws):
            wb_ref[...] = w
    h = h_ref[...]
    if mode == 'plain':
        o_refs[0][...] = _dot(h, ws[0])
    elif mode == 'glu':
        o_refs[0][...] = _dot(h, ws[0]) * jax.nn.sigmoid(_dot(h, ws[1]))
    else:
        o_refs[0][...] = jax.nn.gelu(_dot(h, ws[0]), approximate=True)
        o_refs[1][...] = _dot(h, ws[1])


def _wspec(w, layer, block, index_map):
    if w.ndim == 2:
        return pl.BlockSpec(block, index_map)
    return pl.BlockSpec((None,) + block, lambda *ids: (layer,) + index_map(*ids))


def _norm_proj(x, g, w, mode, tm_pref=512, tn_pref=512, layer=0, emit=False):
    m, d = x.shape
    halves = mode != 'plain'
    pair = isinstance(w, tuple)
    n_out = w[0].shape[-1] if pair else (w.shape[-1] // 2 if halves else w.shape[-1])
    tm = _tile(m, tm_pref)
    tn = _tile(n_out, tn_pref)
    nj = n_out // tn
    w_args = list(w) if pair else ([w, w] if halves else [w])
    shift = [0, 0] if pair else [0, nj]
    in_specs = [pl.BlockSpec((tm, d), lambda i, j: (i, 0)), pl.BlockSpec((1, d), lambda i, j: (0, 0))]
    in_specs += [_wspec(wk, layer, (d, tn), lambda i, j, s=s: (0, j + s)) for wk, s in zip(w_args, shift)]
    n_o = 2 if mode == 'split' else 1
    o_shapes = [jax.ShapeDtypeStruct((m, n_out), F32)] * n_o
    o_specs = [pl.BlockSpec((tm, tn), lambda i, j: (i, j))] * n_o
    if emit:
        assert m == tm, "each weight block must be visited once"
        o_shapes += [jax.ShapeDtypeStruct((d, n_out), BF16)] * len(w_args)
        o_specs += [pl.BlockSpec((d, tn), lambda i, j: (0, j))] * len(w_args)
    wbytes = w_args[0].dtype.itemsize
    vmem = (2 * tm * d * 4 + tm * d * 2 + 2 * len(w_args) * d * tn * wbytes + 2 * n_o * tm * tn * 4
            + (3 * len(w_args) * d * tn * 2 if emit else 0))
    outs = pl.pallas_call(
        functools.partial(_norm_proj_kernel, mode=mode, emit=emit),
        name=f"norm_proj_{mode}_m{m}",
        out_shape=o_shapes,
        grid=(m // tm, nj),
        in_specs=in_specs,
        out_specs=o_specs,
        scratch_shapes=[pltpu.VMEM((tm, d), BF16)],
        compiler_params=_cparams(("parallel", "arbitrary"), vmem),
    )(x, g, *w_args)
    return outs[0] if len(outs) == 1 else tuple(outs)


def _ffn_kernel(x_ref, g_ref, w1_ref, w2_ref, fg_ref, *rest, final, emit, n_cast):
    if n_cast:
        o_ref, h_ref = rest[n_cast], rest[-1]
        for src_ref, dst_ref in zip(rest[:n_cast], rest[n_cast + 1:-1]):
            dst_ref[...] = src_ref[...].astype(BF16)
    elif emit:
        o_ref, w1b_ref, w2b_ref, h_ref = rest
    else:
        o_ref, h_ref = rest
    f = pl.program_id(1)

    @pl.when(f == 0)
    def _():
        x = x_ref[...]
        h_ref[...] = _rms_to_bf16(x, g_ref[...])
        o_ref[...] = x

    w1 = w1_ref[...].astype(BF16)
    w2 = w2_ref[...].astype(BF16)
    if emit:
        w1b_ref[...] = w1
        w2b_ref[...] = w2
    hid = jnp.square(jnp.maximum(_dot(h_ref[...], w1), 0.0)).astype(BF16)
    o_ref[...] += _dot(hid, w2)

    if final:
        @pl.when(f == pl.num_programs(1) - 1)
        def _():
            y = o_ref[...]
            ms = jnp.mean(y * y, axis=-1, keepdims=True)
            o_ref[...] = y * lax.rsqrt(ms + RMS_EPS) * fg_ref[...]


def _can_cast(m, dff, row_counts, tm_pref, tf_pref):
    steps = (m // _tile(m, tm_pref)) * (dff // _tile(dff, tf_pref))
    return all(rows % (BF16_SUBLANES * steps) == 0 for rows in row_counts)


def _ffn(x, g, w1, w2, layer, final_g, final, tm_pref=512, tf_pref=1024, emit=False, cast=()):
    m, d = x.shape
    dff = w1.shape[-1]
    tm = _tile(m, tm_pref)
    tf = _tile(dff, tf_pref)
    nf = dff // tf
    wbytes = w1.dtype.itemsize
    vmem = (2 * tm * d * 4 + tm * d * 2 + 2 * 2 * d * tf * wbytes + 2 * tm * d * 4 + 2 * tm * tf * 4
            + (6 * d * tf * 2 if emit else 0))
    o_shape = [jax.ShapeDtypeStruct((m, d), F32)]
    o_spec = [pl.BlockSpec((tm, d), lambda i, f: (i, 0))]
    in_specs = [pl.BlockSpec((tm, d), lambda i, f: (i, 0)),
                pl.BlockSpec((1, d), lambda i, f: (0, 0)),
                _wspec(w1, layer, (d, tf), lambda i, f: (0, f)),
                _wspec(w2, layer, (tf, d), lambda i, f: (f, 0)),
                pl.BlockSpec((1, d), lambda i, f: (0, 0))]
    args = [x, g, w1, w2, final_g]
    if emit:
        assert m == tm, "each weight block must be visited once"
        o_shape += [jax.ShapeDtypeStruct((d, dff), BF16), jax.ShapeDtypeStruct((dff, d), BF16)]
        o_spec += [pl.BlockSpec((d, tf), lambda i, f: (0, f)), pl.BlockSpec((tf, d), lambda i, f: (f, 0))]
    assert not (emit and cast)
    steps = (m // tm) * nf
    slab = lambda i, f: (i * nf + f, 0)
    for mat, mat_layer in cast:
        rows, cols = mat.shape[-2:]
        r = rows // steps
        assert r * steps == rows and r % BF16_SUBLANES == 0
        in_specs.append(_wspec(mat, mat_layer, (r, cols), slab))
        args.append(mat)
        o_shape.append(jax.ShapeDtypeStruct((rows, cols), BF16))
        o_spec.append(pl.BlockSpec((r, cols), slab))
        vmem += 6 * r * cols * 4
    outs = pl.pallas_call(
        functools.partial(_ffn_kernel, final=final, emit=emit, n_cast=len(cast)),
        name=f"ffn_m{m}_l{layer}",
        out_shape=o_shape,
        grid=(m // tm, nf),
        in_specs=in_specs,
        out_specs=o_spec,
        scratch_shapes=[pltpu.VMEM((tm, d), BF16)],
        compiler_params=_cparams(("arbitrary", "arbitrary"), vmem),
    )(*args)
    return outs[0] if len(outs) == 1 else tuple(outs)


def _conv_tail_kernel(x_ref, u_ref, halo_ref, dww_ref, dwb_ref, lng_ref, lnb_ref, wout_ref,
                      o_ref, uext_ref, y_ref, *, n_taps, rows, ln_rows, lanes):
    i = pl.program_id(0)
    tm, c = u_ref.shape
    cur = lax.rem(i, 2)

    @pl.when(i == 0)
    def _():
        y_ref[1] = jnp.zeros((tm, c), BF16)

    uext_ref[0:CONV_HALO, :] = halo_ref[0]
    uext_ref[CONV_HALO:CONV_HALO + tm, :] = u_ref[...]
    off = CONV_HALO - (n_taps - 1)

    def row_chunk(r, carry):
        r0 = pl.multiple_of(r * rows, rows)
        for l0 in range(0, c, lanes):
            win = uext_ref[pl.ds(r0, rows + CONV_HALO), l0:l0 + lanes]
            accs = [jnp.zeros((SUBLANES, lanes), F32)] * (rows // SUBLANES)
            for s in range(SUBLANES):
                ws = win if s == 0 else pltpu.roll(win, rows + CONV_HALO - s, axis=0)
                for a in range((CONV_HALO + SUBLANES) // SUBLANES):
                    k = a * SUBLANES + s - off
                    if 0 <= k < n_taps:
                        wk = dww_ref[k * SUBLANES:(k + 1) * SUBLANES, l0:l0 + lanes]
                        accs = [acc + wk * ws[(a + j) * SUBLANES:(a + j + 1) * SUBLANES]
                                for j, acc in enumerate(accs)]
            for j, acc in enumerate(accs):
                uext_ref[pl.ds(r0 + j * SUBLANES, SUBLANES), l0:l0 + lanes] = acc + dwb_ref[:, l0:l0 + lanes]
        return carry

    @pl.when(i < pl.num_programs(0) - 1)
    def _():
        lax.fori_loop(0, tm // rows, row_chunk, 0)

    o_ref[...] = x_ref[...] + _dot(y_ref[1 - cur], wout_ref[...])
    for r0 in range(0, tm, ln_rows):
        cv = uext_ref[r0:r0 + ln_rows, :]
        mu = jnp.mean(cv, axis=-1, keepdims=True)
        cc = cv - mu
        var = jnp.mean(cc * cc, axis=-1, keepdims=True)
        y = cc * lax.rsqrt(var + LN_EPS) * lng_ref[...] + lnb_ref[...]
        y_ref[cur, r0:r0 + ln_rows, :] = (y * jax.nn.sigmoid(y)).astype(BF16)


def _conv_tail(x, u, halo, dww, dwb, lng, lnb, wout, layer, tm, n_taps):
    m, d = x.shape
    c = u.shape[1]
    rows = _tile(tm, 128)
    ln_rows = _tile(tm, 16)
    lanes = _tile(c, 128)
    nt = m // tm
    prev = lambda i: (jnp.maximum(i - 1, 0), 0)
    this = lambda i: (jnp.minimum(i, nt - 1), 0)
    vmem = 4 * tm * d * 4 + 2 * tm * c * 4 + 2 * c * d * 2 + (tm + CONV_HALO) * c * 4 + 2 * tm * c * 2
    return pl.pallas_call(
        functools.partial(_conv_tail_kernel, n_taps=n_taps, rows=rows, ln_rows=ln_rows, lanes=lanes),
        name=f"conv_tail_m{m}",
        out_shape=jax.ShapeDtypeStruct((m, d), F32),
        grid=(nt + 1,),
        in_specs=[pl.BlockSpec((tm, d), prev),
                  pl.BlockSpec((tm, c), this),
                  pl.BlockSpec((1, CONV_HALO, c), lambda i: (jnp.minimum(i, nt - 1), 0, 0)),
                  pl.BlockSpec(dww.shape, lambda i: (0, 0)),
                  pl.BlockSpec((1, c), lambda i: (0, 0)),
                  pl.BlockSpec((1, c), lambda i: (0, 0)),
                  pl.BlockSpec((1, c), lambda i: (0, 0)),
                  _wspec(wout, layer, (c, d), lambda i: (0, 0))],
        out_specs=pl.BlockSpec((tm, d), prev),
        scratch_shapes=[pltpu.VMEM((tm + CONV_HALO, c), F32), pltpu.VMEM((2, tm, c), BF16)],
        compiler_params=_cparams(("arbitrary",), vmem),
    )(x, u, halo, dww, dwb, lng, lnb, wout)


def _last_rows(past, seq, n):
    if seq.shape[1] >= n:
        return seq[:, seq.shape[1] - n:]
    return jnp.concatenate([past, seq], axis=1)[:, -n:]


def _halos(seq, past, tm, halo_rows):
    b, l, c = seq.shape
    p = past.shape[1]
    first = jnp.concatenate([jnp.zeros((b, halo_rows - p, c), seq.dtype), past], axis=1)[:, None]
    nt = l // tm
    if nt == 1:
        return first.reshape(b, halo_rows, c)
    rest = seq.reshape(b, nt, tm, c)[:, :-1, tm - halo_rows:]
    return jnp.concatenate([first, rest], axis=1).reshape(b * nt, halo_rows, c)


def _lru_tail_kernel(x_ref, ybr_ref, xbr_ref, halo_ref, h0_ref, cw_ref, cb_ref, wa_ref, ba_ref,
                     wx_ref, bx_ref, lam_ref, wout_ref, o_ref, hlast_ref,
                     xext_ref, a_ref, b_ref, hc_ref, *, n_taps, lanes):
    tm, c = xbr_ref.shape
    n_blocks, blk, _ = wa_ref.shape
    t = pl.program_id(1)

    @pl.when(t == 0)
    def _():
        hc_ref[...] = jnp.broadcast_to(h0_ref[0], hc_ref.shape)

    xext_ref[0:LRU_HALO, :] = halo_ref[0]
    xext_ref[LRU_HALO:LRU_HALO + tm, :] = xbr_ref[...]
    off = LRU_HALO - (n_taps - 1)

    sp = jax.nn.softplus(-lam_ref[...])
    for n in range(n_blocks):
        sl = slice(n * blk, (n + 1) * blk)
        xc = jnp.zeros((tm, blk), F32)
        for k in range(n_taps):
            xc = xc + cw_ref[k:k + 1, sl] * xext_ref[off + k:off + k + tm, sl]
        xc = xc + cb_ref[:, sl]
        xcb = xc.astype(BF16)
        r = jax.nn.sigmoid(_dot(xcb, wa_ref[n]) + ba_ref[:, sl])
        gi = jax.nn.sigmoid(_dot(xcb, wx_ref[n]) + bx_ref[:, sl])
        log_a = -LRU_C * r * sp[:, sl]
        a = jnp.exp(log_a)
        a_ref[:, sl] = a
        b_ref[:, sl] = jnp.sqrt(-jnp.tanh(log_a) * (a * a + 1.0)) * (gi * xc)

    row = lax.broadcasted_iota(jnp.int32, (SUBLANES, lanes), 0)

    for l0 in range(0, c, lanes):
        def group(gidx, h):
            r0 = pl.multiple_of(gidx * SUBLANES, SUBLANES)
            av = a_ref[pl.ds(r0, SUBLANES), l0:l0 + lanes]
            bv = b_ref[pl.ds(r0, SUBLANES), l0:l0 + lanes]
            for s in (1, 2, 4):
                a_sh = jnp.where(row >= s, pltpu.roll(av, s, axis=0), 1.0)
                b_sh = jnp.where(row >= s, pltpu.roll(bv, s, axis=0), 0.0)
                bv = av * b_sh + bv
                av = av * a_sh
            hs = av * h + bv
            b_ref[pl.ds(r0, SUBLANES), l0:l0 + lanes] = hs
            return jnp.broadcast_to(hs[SUBLANES - 1:SUBLANES, :], (SUBLANES, lanes))

        h_fin = lax.fori_loop(0, tm // SUBLANES, group, hc_ref[:, l0:l0 + lanes], unroll=2)
        hc_ref[:, l0:l0 + lanes] = h_fin

    gated = (b_ref[...] * ybr_ref[...]).astype(BF16)
    o_ref[...] = x_ref[...] + _dot(gated, wout_ref[...])

    @pl.when(t == pl.num_programs(1) - 1)
    def _():
        hlast_ref[0] = hc_ref[...]


def _lru_tail(x, ybr, xbr, halo, h0, cw, cb, wa, ba, wx, bx, lam, wout, batch, tm, n_taps):
    m, d = x.shape
    c = xbr.shape[1]
    nt = m // batch // tm
    lanes = _tile(c, 512)
    row2 = lambda b, t: (b * nt + t, 0)
    const2 = lambda b, t: (0, 0)
    vec = pl.BlockSpec((1, c), const2)
    vmem = (4 * tm * d * 4 + 4 * tm * c * 4 + 2 * c * d * 2 + 4 * wa.size * 2
            + (tm + LRU_HALO) * c * 4 + 2 * tm * c * 4)
    return pl.pallas_call(
        functools.partial(_lru_tail_kernel, n_taps=n_taps, lanes=lanes),
        name=f"lru_tail_m{m}",
        out_shape=[jax.ShapeDtypeStruct((m, d), F32),
                   jax.ShapeDtypeStruct((batch, SUBLANES, c), F32)],
        grid=(batch, nt),
        in_specs=[pl.BlockSpec((tm, d), row2),
                  pl.BlockSpec((tm, c), row2),
                  pl.BlockSpec((tm, c), row2),
                  pl.BlockSpec((1, LRU_HALO, c), lambda b, t: (b * nt + t, 0, 0)),
                  pl.BlockSpec((1, 1, c), lambda b, t: (b, 0, 0)),
                  pl.BlockSpec(cw.shape, const2),
                  vec,
                  pl.BlockSpec(wa.shape, lambda b, t: (0, 0, 0)),
                  vec,
                  pl.BlockSpec(wx.shape, lambda b, t: (0, 0, 0)),
                  vec, vec,
                  pl.BlockSpec((c, d), const2)],
        out_specs=[pl.BlockSpec((tm, d), row2),
                   pl.BlockSpec((1, SUBLANES, c), lambda b, t: (b, 0, 0))],
        scratch_shapes=[pltpu.VMEM((tm + LRU_HALO, c), F32), pltpu.VMEM((tm, c), F32),
                        pltpu.VMEM((tm, c), F32), pltpu.VMEM((SUBLANES, c), F32)],
        compiler_params=_cparams(("parallel", "arbitrary"), vmem),
    )(x, ybr, xbr, halo, h0, cw, cb, wa, ba, wx, bx, lam, wout)


def _shifted_rows(vec, n_rows):
    return pltpu.roll(jnp.broadcast_to(vec, (n_rows, vec.shape[1])), 0, 1, stride=1, stride_axis=0)


def _block_softmax(s_parts, v_parts):
    m = s_parts[0].max(axis=-1, keepdims=True)
    for s in s_parts[1:]:
        m = jnp.maximum(m, s.max(axis=-1, keepdims=True))
    o, den = None, None
    for s, v in zip(s_parts, v_parts):
        p = jnp.exp(s - m)
        d = p.sum(axis=-1, keepdims=True)
        pv = _dot(p.astype(BF16), v)
        o, den = (pv, d) if o is None else (o + pv, den + d)
    return o, m, den


def _block_softmax_mxu_sum(s_parts, v_parts, exp=jnp.exp):
    m = s_parts[0].max(axis=-1, keepdims=True)
    for s in s_parts[1:]:
        m = jnp.maximum(m, s.max(axis=-1, keepdims=True))
    acc = None
    for s, v in zip(s_parts, v_parts):
        pv = _dot(exp(s - m).astype(BF16), v)
        acc = pv if acc is None else acc + pv
    dh = acc.shape[1] // 2
    return acc[:, :dh], m, acc[:, dh:]


def _merge(state, o_b, m_b, l_b, exp=jnp.exp):
    if state is None:
        return o_b, m_b, l_b
    acc, m_old, l_old = state
    m_new = jnp.maximum(m_old, m_b)
    a_old = exp(m_old - m_new)
    a_b = exp(m_b - m_new)
    return acc * a_old + o_b * a_b, m_new, l_old * a_old + l_b * a_b


def _attn_prompt_kernel(*refs, heads, dh):
    n_g = len(GROUPS)
    ins = refs[:5 * n_g]
    bvec_ref, o_ref, acc_ref, m_ref, l_ref = refs[5 * n_g:]
    first_chunk = pl.program_id(1) == 0
    h = pl.program_id(2)
    log2e = float(np.log2(np.e))
    scale = dh ** -0.5 * log2e
    qi = lax.broadcasted_iota(jnp.int32, (BLK, 2 * BLK), 0)
    kj = lax.broadcasted_iota(jnp.int32, (BLK, 2 * BLK), 1)
    in_band = (kj - qi >= 0) & (kj - qi <= BLK)
    ones = jnp.ones((BLK, dh), BF16)
    for g, (_, dil) in enumerate(GROUPS):
        q_ref, k_ref, v_ref, kp_ref, vp_ref = ins[5 * g:5 * g + 5]
        bias = jnp.where(in_band, _shifted_rows(bvec_ref[pl.ds(g * heads + h, 1), :] * log2e, BLK), NEG_INF)
        bias_p, bias_c = bias[:, :BLK], bias[:, BLK:]
        for r in range(dil):
            prev_rows = pl.ds(r, BLK, stride=dil)
            k_prev = kp_ref[prev_rows, :].astype(BF16)
            v_prev = jnp.concatenate([vp_ref[prev_rows, :].astype(BF16), ones], axis=1)
            for nb in range(CHUNK // (dil * BLK)):
                rows = pl.ds(r + dil * BLK * nb, BLK, stride=dil)
                q = q_ref[rows, :].astype(BF16)
                k_cur = k_ref[rows, :].astype(BF16)
                v_cur = jnp.concatenate([v_ref[rows, :].astype(BF16), ones], axis=1)
                s_p = _nt_dot(q, k_prev) * scale + bias_p
                if nb == 0:
                    s_p = jnp.where(first_chunk, NEG_INF, s_p)
                s_c = _nt_dot(q, k_cur) * scale + bias_c
                o_b, m_b, l_b = _block_softmax_mxu_sum([s_p, s_c], [v_prev, v_cur], exp=jnp.exp2)
                m_b = jnp.broadcast_to(m_b, (BLK, dh))
                state = None if g == 0 else (acc_ref[rows, :], m_ref[rows, :], l_ref[rows, :])
                acc, m_new, l_new = _merge(state, o_b, m_b, l_b, exp=jnp.exp2)
                if g == n_g - 1:
                    o_ref[rows, :] = acc / l_new
                else:
                    acc_ref[rows, :] = acc
                    m_ref[rows, :] = m_new
                    l_ref[rows, :] = l_new
                k_prev, v_prev = k_cur, v_cur


def _attn_prompt(qkv, bvec, batch, seq, heads, dh):
    n_g = len(GROUPS)
    nc = seq // CHUNK
    col = lambda comp, g: (lambda b, n, h: (b * nc + n, (comp * n_g + g) * heads + h))
    in_specs, args = [], []
    for g, (win, _) in enumerate(GROUPS):
        per = CHUNK // win
        prev = lambda comp, g=g, per=per: (
            lambda b, n, h: (jnp.maximum((b * nc + n) * per - 1, 0), (comp * n_g + g) * heads + h))
        in_specs += [pl.BlockSpec((CHUNK, dh), col(0, g)), pl.BlockSpec((CHUNK, dh), col(1, g)),
                     pl.BlockSpec((CHUNK, dh), col(2, g)),
                     pl.BlockSpec((win, dh), prev(1)), pl.BlockSpec((win, dh), prev(2))]
        args += [qkv] * 5
    in_specs.append(pl.BlockSpec(bvec.shape, lambda b, n, h: (0, 0)))
    vmem = 2 * (9 * CHUNK + 2 * sum(w for w, _ in GROUPS)) * dh * 4 + 2 * CHUNK * dh * 4 + 3 * CHUNK * dh * 4
    return pl.pallas_call(
        functools.partial(_attn_prompt_kernel, heads=heads, dh=dh),
        name="attn_prompt",
        out_shape=jax.ShapeDtypeStruct((batch * seq, heads * dh), F32),
        grid=(batch, nc, heads),
        in_specs=in_specs,
        out_specs=pl.BlockSpec((CHUNK, dh), lambda b, n, h: (b * nc + n, h)),
        scratch_shapes=[pltpu.VMEM((CHUNK, dh), F32)] * 3,
        compiler_params=_cparams(("parallel", "parallel", "parallel"), vmem),
    )(*args, bvec)


def _attn_step_kernel(*refs, heads, dh, n_bufs):
    n_g = len(GROUPS)
    ins = refs[:6 * n_g]
    o_ref, acc_ref, m_ref, l_ref, knp_ref, vnp_ref = refs[6 * n_g:]
    part = pl.program_id(1)
    t = o_ref.shape[1]
    scale = dh ** -0.5
    row_stride = 2 * heads

    def merge_into(hs, o_b, m_b, l_b):
        state = (acc_ref[:, hs], m_ref[:, hs], l_ref[:, hs])
        acc, m_new, l_new = _merge(state, o_b, jnp.broadcast_to(m_b, (t, dh)), jnp.broadcast_to(l_b, (t, dh)))
        acc_ref[:, hs] = acc
        m_ref[:, hs] = m_new
        l_ref[:, hs] = l_new

    def cache_piece(g, first_row):
        dil = GROUPS[g][1]
        q_ref, _, _, cache_ref, yc_ref, _ = ins[6 * g:6 * g + 6]
        rows = cache_ref.shape[1] // row_stride
        tc = lax.broadcasted_iota(jnp.int32, (t, rows), 0)
        cc = lax.broadcasted_iota(jnp.int32, (t, rows), 1) + first_row
        dist_c = n_bufs[g] + tc - cc
        ok_c = ((dist_c & (dil - 1)) == 0) & (dist_c <= (N_DIL_KEYS - 1) * dil)
        for h in range(heads):
            hs = slice(h * dh, (h + 1) * dh)
            q = q_ref[0, :, hs].astype(BF16)
            kc = cache_ref[0, pl.ds(h, rows, stride=row_stride), :].astype(BF16)
            vc = cache_ref[0, pl.ds(heads + h, rows, stride=row_stride), :].astype(BF16)
            bias = _shifted_rows(yc_ref[0, h:h + 1, :], t)[:, BLK:]
            s_c = jnp.where(ok_c, _nt_dot(q, kc) * scale + bias, NEG_INF)
            merge_into(hs, *_block_softmax([s_c], [vc]))

    @pl.when(part == 0)
    def _():
        acc_ref[...] = jnp.zeros_like(acc_ref)
        l_ref[...] = jnp.zeros_like(l_ref)
        m_ref[...] = jnp.full_like(m_ref, NEG_INF)
        tn = lax.broadcasted_iota(jnp.int32, (t, BLK), 0)
        cn = lax.broadcasted_iota(jnp.int32, (t, BLK), 1)
        dist_n = tn - cn
        for g, (_, dil) in enumerate(GROUPS):
            q_ref, kn_ref, vn_ref, _, _, yn_ref = ins[6 * g:6 * g + 6]
            ok_n = (dist_n >= 0) & ((dist_n & (dil - 1)) == 0) & (dist_n <= (N_DIL_KEYS - 1) * dil)
            knp_ref[...] = jnp.zeros_like(knp_ref)
            vnp_ref[...] = jnp.zeros_like(vnp_ref)
            knp_ref[0:t, :] = kn_ref[0]
            vnp_ref[0:t, :] = vn_ref[0]
            for h in range(heads):
                hs = slice(h * dh, (h + 1) * dh)
                q = q_ref[0, :, hs].astype(BF16)
                bias = _shifted_rows(yn_ref[h:h + 1, :], t)[:, BLK:]
                s_n = jnp.where(ok_n, _nt_dot(q, knp_ref[:, hs].astype(BF16)) * scale + bias, NEG_INF)
                merge_into(hs, *_block_softmax([s_n], [vnp_ref[:, hs].astype(BF16)]))
        for g in range(n_g):
            if ins[6 * g + 3].shape[1] == n_bufs[g] * row_stride:
                cache_piece(g, 0)

    for g in range(n_g):
        rows = ins[6 * g + 3].shape[1] // row_stride
        if rows != n_bufs[g]:
            cache_piece(g, part * rows)

    @pl.when(part == pl.num_programs(1) - 1)
    def _():
        o_ref[0] = acc_ref[...] / l_ref[...]


def _attn_step(qkv, caches, rel_bias, batch, t, heads, dh, n_parts=2):
    n_g = len(GROUPS)
    width = heads * dh
    qkv3 = qkv.reshape(batch, t, qkv.shape[1])
    in_specs, args, n_bufs = [], [], []
    vmem = 8 * t * width * 4
    for g in range(n_g):
        n_buf = caches[g].shape[1]
        parts = n_parts if n_buf % (n_parts * BLK) == 0 else 1
        rows = n_buf // parts
        flat = caches[g].reshape(batch, n_buf * 2 * heads, dh)
        yc, yn = _step_bias_vectors(rel_bias, g, heads, n_buf, parts)
        new = lambda comp, g=g: pl.BlockSpec((1, t, width), lambda b, s: (b, 0, comp * n_g + g))
        part_of = (lambda s: s) if parts == n_parts else (lambda s: 0)
        in_specs += [new(0), new(1), new(2),
                     pl.BlockSpec((1, rows * 2 * heads, dh), lambda b, s, f=part_of: (b, f(s), 0)),
                     pl.BlockSpec((1, heads, rows + BLK), lambda b, s, f=part_of: (f(s), 0, 0)),
                     pl.BlockSpec(yn.shape, lambda b, s: (0, 0))]
        args += [qkv3, qkv3, qkv3, flat, yc, yn]
        n_bufs.append(n_buf)
        vmem += 2 * rows * 2 * width * 4 + 6 * rows * dh * 4
    return pl.pallas_call(
        functools.partial(_attn_step_kernel, heads=heads, dh=dh, n_bufs=tuple(n_bufs)),
        name="attn_step",
        out_shape=jax.ShapeDtypeStruct((batch, t, width), F32),
        grid=(batch, n_parts),
        in_specs=in_specs,
        out_specs=pl.BlockSpec((1, t, width), lambda b, s: (b, 0, 0)),
        scratch_shapes=[pltpu.VMEM((t, width), F32)] * 3 + [pltpu.VMEM((BLK, width), F32)] * 2,
        compiler_params=_cparams(("parallel", "arbitrary"), vmem),
    )(*args).reshape(batch * t, width)


def _attn_out_kernel(x_ref, o_ref, wo_ref, out_ref):
    out_ref[...] = x_ref[...] + _dot(o_ref[...].astype(BF16), wo_ref[...])


def _attn_out(x, o, wo, tm_pref=512):
    m, d = x.shape
    width = wo.shape[0]
    tm = _tile(m, tm_pref)
    row = lambda n: pl.BlockSpec((tm, n), lambda i: (i, 0))
    vmem = 4 * tm * d * 4 + 2 * tm * width * 4 + 2 * width * d * 2
    return pl.pallas_call(
        _attn_out_kernel,
        name=f"attn_out_m{m}",
        out_shape=jax.ShapeDtypeStruct((m, d), F32),
        grid=(m // tm,),
        in_specs=[row(d), row(width), pl.BlockSpec((width, d), lambda i: (0, 0))],
        out_specs=row(d),
        compiler_params=_cparams(("parallel",), vmem),
    )(x, o, wo)


def _t5_bucket(dist):
    max_exact = N_BUCKETS // 2
    d = np.asarray(dist, dtype=np.int32)
    large = max_exact + (np.log(np.maximum(d, max_exact) / max_exact) / np.log(MAX_DISTANCE / max_exact)
                         * (N_BUCKETS - max_exact)).astype(np.int32)
    return np.where(d < max_exact, d, np.minimum(large, N_BUCKETS - 1)).astype(np.int32)


def _group_bias(rel_bias, g, heads):
    buckets = _t5_bucket(np.arange(N_DIL_KEYS) * GROUPS[g][1])
    return rel_bias[:, g * heads:(g + 1) * heads].astype(F32)[buckets]


def _prompt_bias_vectors(rel_bias, heads):
    vecs = []
    for g in range(len(GROUPS)):
        rev = _group_bias(rel_bias, g, heads)[::-1]
        vecs.append(jnp.pad(rev, ((0, 2 * BLK - N_DIL_KEYS), (0, 0))).T)
    return jnp.concatenate(vecs, axis=0)


def _step_bias_vectors(rel_bias, g, heads, n_buf, parts):
    rows = n_buf // parts
    bias = _group_bias(rel_bias, g, heads)
    at = lambda dist: bias[np.clip(dist // GROUPS[g][1], 0, N_DIL_KEYS - 1)]
    first = np.arange(parts)[:, None] * rows + np.arange(rows + BLK)[None, :]
    yc = jnp.transpose(at(n_buf + BLK - first), (0, 2, 1))
    yn = at(BLK - np.arange(2 * BLK)).T
    return yc, yn


def _kv_rows_kernel(k_ref, v_ref, o_ref, *, heads, dh):
    rows = k_ref.shape[0]
    o_ref[0, :, 0] = k_ref[...].reshape(rows, heads, dh)
    o_ref[0, :, 1] = v_ref[...].reshape(rows, heads, dh)


def _kv_rows(qkv, g, batch, seq, n_rows, heads, dh):
    n_g = len(GROUPS)
    width = heads * dh
    tm = _tile(n_rows, 256)
    assert seq % tm == 0 and n_rows % tm == 0
    first = (seq - n_rows) // tm
    per_seq = seq // tm
    col = lambda comp: pl.BlockSpec((tm, width), lambda b, i: (b * per_seq + first + i, comp * n_g + g))
    return pl.pallas_call(
        functools.partial(_kv_rows_kernel, heads=heads, dh=dh),
        name=f"kv_rows_g{g}_n{n_rows}",
        out_shape=jax.ShapeDtypeStruct((batch, n_rows, 2, heads, dh), qkv.dtype),
        grid=(batch, n_rows // tm),
        in_specs=[col(1), col(2)],
        out_specs=pl.BlockSpec((1, tm, 2, heads, dh), lambda b, i: (b, i, 0, 0, 0)),
        compiler_params=_cparams(("parallel", "parallel"), 8 * tm * width * 4),
    )(qkv, qkv)


def _trunk(x3, st, p, tiles, bank, emit):
    batch, seq, d = x3.shape
    m = batch * seq
    x = x3.reshape(m, d)
    new = []
    depth = p['ffn_w1'].shape[0]
    for i in range(depth):
        kind = i % N_MIXERS
        g_mix = p['norm_mix_g'][i][None]
        if kind == 0:
            j = i // N_MIXERS
            n_taps, c = p['conv_dw_w'].shape[1:]
            buf = jnp.zeros((batch, n_taps - 1, c), F32) if st is None else st[i][0]
            if ('glu', j) in bank:
                u = _norm_proj(x, g_mix, bank['glu', j], 'glu', tiles['proj_m'])
            else:
                u, *halves = _norm_proj(x, g_mix, p['conv_w_glu'], 'glu', tiles['proj_m'], layer=j, emit=True)
                bank['glu', j] = tuple(halves)
            u3 = u.reshape(batch, seq, c)
            tm = _tile(seq, tiles['conv_m'])
            halo = _halos(u3, buf, tm, CONV_HALO)
            dww = jnp.repeat(p['conv_dw_w'][j], SUBLANES, axis=0)
            w_out = bank['conv_out', j] if ('conv_out', j) in bank else p['conv_w_out'][j].astype(BF16)
            x = _conv_tail(x, u, halo, dww, p['conv_dw_b'][j][None], p['conv_ln_g'][j][None],
                           p['conv_ln_b'][j][None], w_out, j, tm, n_taps)
            new.append((_last_rows(buf, u3, n_taps - 1),))
        elif kind == 1:
            n_taps, c = p['lru_conv_w'].shape
            if st is None:
                h0 = jnp.zeros((batch, c), F32)
                cb = jnp.zeros((batch, n_taps - 1, c), F32)
            else:
                h0, cb = st[i]
            if 'lru_in' in bank:
                ybr, xbr = _norm_proj(x, g_mix, bank['lru_in'], 'split', tiles['proj_m'])
            else:
                ybr, xbr, *halves = _norm_proj(x, g_mix, p['lru_w_in'], 'split', tiles['proj_m'], emit=True)
                bank['lru_in'] = tuple(halves)
            w_out = bank['lru_out'] if 'lru_out' in bank else p['lru_w_out'].astype(BF16)
            xbr3 = xbr.reshape(batch, seq, c)
            tm = _tile(seq, tiles['lru_m'])
            halo = _halos(xbr3, cb, tm, LRU_HALO)
            cw = jnp.pad(p['lru_conv_w'], ((0, SUBLANES - n_taps), (0, 0)))
            x, hl = _lru_tail(x, ybr, xbr, halo, h0[:, None], cw, p['lru_conv_b'][None], p['lru_w_a'],
                              p['lru_b_a'][None], p['lru_w_x'], p['lru_b_x'][None], p['lru_lambda'][None],
                              w_out, batch, tm, n_taps)
            new.append((hl[:, 0], _last_rows(cb, xbr3, n_taps - 1)))
        else:
            rel_bias = p['rel_bias']
            heads = rel_bias.shape[1] // len(GROUPS)
            width = p['att_w_o'].shape[0]
            dh = width // heads
            if 'qkv' in bank:
                qkv = _norm_proj(x, g_mix, bank['qkv'], 'plain', tiles['proj_m'], tn_pref=1024)
            else:
                qkv, bank['qkv'] = _norm_proj(x, g_mix, p['att_w_qkv'], 'plain', tiles['proj_m'], emit=True)
            if st is None:
                assert seq % CHUNK == 0
                o = _attn_prompt(qkv, _prompt_bias_vectors(rel_bias, heads), batch, seq, heads, dh)
                rows = [_kv_rows(qkv, g, batch, seq, min(win, seq), heads, dh) for g, (win, _) in enumerate(GROUPS)]
            else:
                assert seq <= BLK
                o = _attn_step(qkv, st[i], rel_bias, batch, seq, heads, dh)
                rows = [_kv_rows(qkv, g, batch, seq, seq, heads, dh) for g in range(len(GROUPS))]
            x = _attn_out(x, o, p['att_w_o'], tiles['attn_out_m'])
            new.append(rows)
        g_ffn, g_fin, final = p['norm_ffn_g'][i][None], p['final_norm_g'][None], i == depth - 1
        ffn_tiles = dict(tm_pref=tiles['ffn_m'], tf_pref=tiles['ffn_f'])
        jobs = _cast_jobs(p, i + 1) if not emit and i + 1 < depth else []
        if ('ffn_w1', i) not in bank:
            x, bank['ffn_w1', i], bank['ffn_w2', i] = _ffn(x, g_ffn, p['ffn_w1'], p['ffn_w2'], i, g_fin, final,
                                                           emit=True, **ffn_tiles)
        elif jobs and _can_cast(m, p['ffn_w1'].shape[-1], [mat.shape[-2] for _, mat, _ in jobs], **ffn_tiles):
            x, *casts = _ffn(x, g_ffn, bank['ffn_w1', i], bank['ffn_w2', i], i, g_fin, final, **ffn_tiles,
                             cast=[(mat, mat_layer) for _, mat, mat_layer in jobs])
            bank.update({key: cast for (key, _, _), cast in zip(jobs, casts)})
        else:
            x = _ffn(x, g_ffn, bank['ffn_w1', i], bank['ffn_w2', i], i, g_fin, final, **ffn_tiles)
        yield
    return x.reshape(batch, seq, d), new


def _cast_jobs(p, layer):
    jobs = [(('ffn_w1', layer), p['ffn_w1'], layer), (('ffn_w2', layer), p['ffn_w2'], layer)]
    kind = layer % N_MIXERS
    if kind == 0:
        j = layer // N_MIXERS
        jobs += [(('glu', j), p['conv_w_glu'], j), (('conv_out', j), p['conv_w_out'], j)]
    elif kind == 1:
        jobs += [('lru_in', p['lru_w_in'], None), ('lru_out', p['lru_w_out'], None)]
    else:
        jobs += [('qkv', p['att_w_qkv'], None)]
    return jobs


def _run_alternating(*gens):
    results = [None] * len(gens)
    live = list(range(len(gens)))
    while live:
        for k in list(live):
            try:
                next(gens[k])
            except StopIteration as done:
                results[k] = done.value
                live.remove(k)
    return results


PROMPT_TILES = dict(proj_m=1024, conv_m=256, lru_m=256, attn_out_m=512, ffn_m=512, ffn_f=1024)
SAMPLE_TILES = dict(proj_m=512, conv_m=256, lru_m=256, attn_out_m=512, ffn_m=512, ffn_f=512)
MATMUL_WEIGHTS = ('lru_w_a', 'lru_w_x', 'att_w_o')


def kernel(x_prompt, x_sample, state_conv_l0, state_lru_h_l1, state_lru_conv_l1, cache_kv_w128_l2, cache_kv_w512_l2, cache_kv_w2048_l2, state_conv_l3, norm_mix_g, norm_ffn_g, final_norm_g, ffn_w1, ffn_w2, conv_w_glu, conv_dw_w, conv_dw_b, conv_ln_g, conv_ln_b, conv_w_out, lru_w_in, lru_conv_w, lru_conv_b, lru_w_a, lru_b_a, lru_w_x, lru_b_x, lru_lambda, lru_w_out, att_w_qkv, att_w_o, rel_bias):
    p = dict(norm_mix_g=norm_mix_g, norm_ffn_g=norm_ffn_g, final_norm_g=final_norm_g, ffn_w1=ffn_w1,
             ffn_w2=ffn_w2, conv_w_glu=conv_w_glu, conv_dw_w=conv_dw_w, conv_dw_b=conv_dw_b,
             conv_ln_g=conv_ln_g, conv_ln_b=conv_ln_b, conv_w_out=conv_w_out, lru_w_in=lru_w_in,
             lru_conv_w=lru_conv_w, lru_conv_b=lru_conv_b, lru_w_a=lru_w_a, lru_b_a=lru_b_a, lru_w_x=lru_w_x,
             lru_b_x=lru_b_x, lru_lambda=lru_lambda, lru_w_out=lru_w_out, att_w_qkv=att_w_qkv,
             att_w_o=att_w_o, rel_bias=rel_bias)
    for name in MATMUL_WEIGHTS:
        p[name] = p[name].astype(BF16)
    bank = {}
    sample = _trunk(x_sample, [(state_conv_l0,), (state_lru_h_l1, state_lru_conv_l1),
                               (cache_kv_w128_l2, cache_kv_w512_l2, cache_kv_w2048_l2),
                               (state_conv_l3,)], p, SAMPLE_TILES, bank, emit=True)
    prompt = _trunk(x_prompt, None, p, PROMPT_TILES, bank, emit=False)
    (y_sample, sst), (y_prompt, pst) = _run_alternating(sample, prompt)
    return (y_prompt, y_sample,
            pst[0][0], sst[0][0],
            pst[1][0], sst[1][0],
            pst[1][1], sst[1][1],
            pst[2][0], sst[2][0],
            pst[2][1], sst[2][1],
            pst[2][2], sst[2][2],
            pst[3][0], sst[3][0])
```

```python
import functools

import numpy as np
import jax
import jax.numpy as jnp
from jax import lax
from jax.experimental import pallas as pl
from jax.experimental.pallas import tpu as pltpu

RMS_EPS = 1e-6
LN_EPS = 1e-5
LRU_C = 8.0
NEG_INF = -1e30
N_MIXERS = 3
GROUPS = ((128, 1), (512, 4), (2048, 16))
N_DIL_KEYS = 129
BLK = 128
CHUNK = BLK * max(d for _, d in GROUPS)
N_BUCKETS = 32
MAX_DISTANCE = 2048

V7X_VMEM_BYTES = 64 * 1024 * 1024
MIN_VMEM_LIMIT_BYTES = 32 * 1024 * 1024
VMEM_RESERVE_BYTES = 8 * 1024 * 1024
SUBLANES = 8
BF16_SUBLANES = 16
CONV_HALO = 32
LRU_HALO = 8

BF16 = jnp.bfloat16
F32 = jnp.float32


def _cparams(semantics, vmem_bytes):
    limit = int(min(max(vmem_bytes * 3 // 2, MIN_VMEM_LIMIT_BYTES), V7X_VMEM_BYTES - VMEM_RESERVE_BYTES))
    return pltpu.CompilerParams(dimension_semantics=semantics, vmem_limit_bytes=limit)


def _tile(n, pref):
    if n <= pref:
        return n
    t = pref
    while n % t:
        t //= 2
    return t


def _rms_to_bf16(x, g):
    ms = jnp.mean(x * x, axis=-1, keepdims=True)
    return (x * lax.rsqrt(ms + RMS_EPS) * g).astype(BF16)


def _dot(a, b):
    return jnp.dot(a, b, preferred_element_type=F32)


def _nt_dot(a, b):
    return lax.dot_general(a, b, (((1,), (1,)), ((), ())), preferred_element_type=F32)


def _norm_proj_kernel(x_ref, g_ref, *refs, mode, emit):
    n_w = 1 if mode == 'plain' else 2
    n_o = 2 if mode == 'split' else 1
    w_refs, o_refs = refs[:n_w], refs[n_w:n_w + n_o]
    wb_refs, h_ref = refs[n_w + n_o:-1], refs[-1]

    @pl.when(pl.program_id(1) == 0)
    def _():
        h_ref[...] = _rms_to_bf16(x_ref[...], g_ref[...])

    ws = [w_ref[...].astype(BF16) for w_ref in w_refs]
    if emit:
        for wb_ref, w in zip(wb_refs, ws):
            wb_ref[...] = w
    h = h_ref[...]
    if mode == 'plain':
        o_refs[0][...] = _dot(h, ws[0])
    elif mode == 'glu':
        o_refs[0][...] = _dot(h, ws[0]) * jax.nn.sigmoid(_dot(h, ws[1]))
    else:
        o_refs[0][...] = jax.nn.gelu(_dot(h, ws[0]), approximate=True)
        o_refs[1][...] = _dot(h, ws[1])


def _wspec(w, layer, block, index_map):
    if w.ndim == 2:
        return pl.BlockSpec(block, index_map)
    return pl.BlockSpec((None,) + block, lambda *ids: (layer,) + index_map(*ids))


def _norm_proj(x, g, w, mode, tm_pref=512, tn_pref=512, layer=0, emit=False):
    m, d = x.shape
    halves = mode != 'plain'
    pair = isinstance(w, tuple)
    n_out = w[0].shape[-1] if pair else (w.shape[-1] // 2 if halves else w.shape[-1])
    tm = _tile(m, tm_pref)
    tn = _tile(n_out, tn_pref)
    nj = n_out // tn
    w_args = list(w) if pair else ([w, w] if halves else [w])
    shift = [0, 0] if pair else [0, nj]
    in_specs = [pl.BlockSpec((tm, d), lambda i, j: (i, 0)), pl.BlockSpec((1, d), lambda i, j: (0, 0))]
    in_specs += [_wspec(wk, layer, (d, tn), lambda i, j, s=s: (0, j + s)) for wk, s in zip(w_args, shift)]
    n_o = 2 if mode == 'split' else 1
    o_shapes = [jax.ShapeDtypeStruct((m, n_out), F32)] * n_o
    o_specs = [pl.BlockSpec((tm, tn), lambda i, j: (i, j))] * n_o
    if emit:
        assert m == tm, "each weight block must be visited once"
        o_shapes += [jax.ShapeDtypeStruct((d, n_out), BF16)] * len(w_args)
        o_specs += [pl.BlockSpec((d, tn), lambda i, j: (0, j))] * len(w_args)
    wbytes = w_args[0].dtype.itemsize
    vmem = (2 * tm * d * 4 + tm * d * 2 + 2 * len(w_args) * d * tn * wbytes + 2 * n_o * tm * tn * 4
            + (3 * len(w_args) * d * tn * 2 if emit else 0))
    outs = pl.pallas_call(
        functools.partial(_norm_proj_kernel, mode=mode, emit=emit),
        name=f"norm_proj_{mode}_m{m}",
        out_shape=o_shapes,
        grid=(m // tm, nj),
        in_specs=in_specs,
        out_specs=o_specs,
        scratch_shapes=[pltpu.VMEM((tm, d), BF16)],
        compiler_params=_cparams(("parallel", "arbitrary"), vmem),
    )(x, g, *w_args)
    return outs[0] if len(outs) == 1 else tuple(outs)


def _ffn_kernel(x_ref, xn_ref, g_ref, w1_ref, w2_ref, fg_ref, *rest, final, emit, n_cast, norm_rows):
    o_ref = rest[n_cast]
    slot_a, slot_b = rest[-2:]
    i, f = pl.program_id(0), pl.program_id(1)

    @pl.when(f == 0)
    def _():
        x = x_ref[...]
        o_ref[...] = x
        if norm_rows:
            @pl.when(i == 0)
            def _():
                slot_a[...] = _rms_to_bf16(x, g_ref[...])
        else:
            slot_a[...] = _rms_to_bf16(x, g_ref[...])

    def step(h_ref, h_next_ref):
        for src_ref, dst_ref in zip(rest[:n_cast], rest[n_cast + 1:-2]):
            dst_ref[...] = src_ref[...].astype(BF16)
        w1 = w1_ref[...].astype(BF16)
        w2 = w2_ref[...].astype(BF16)
        if emit:
            rest[1][...] = w1
            rest[2][...] = w2
        hid = jnp.square(jnp.maximum(_dot(h_ref[...], w1), 0.0)).astype(BF16)
        o_ref[...] += _dot(hid, w2)
        if norm_rows:
            rs = pl.ds(pl.multiple_of(f * norm_rows, norm_rows), norm_rows)
            h_next_ref[rs, :] = _rms_to_bf16(xn_ref[rs, :], g_ref[...])

    if norm_rows:
        even = lax.rem(i, 2) == 0
        pl.when(even)(lambda: step(slot_a, slot_b))
        pl.when(jnp.logical_not(even))(lambda: step(slot_b, slot_a))
    else:
        step(slot_a, slot_b)

    if final:
        @pl.when(f == pl.num_programs(1) - 1)
        def _():
            y = o_ref[...]
            ms = jnp.mean(y * y, axis=-1, keepdims=True)
            o_ref[...] = y * lax.rsqrt(ms + RMS_EPS) * fg_ref[...]


def _can_cast(m, dff, row_counts, tm_pref, tf_pref):
    steps = (m // _tile(m, tm_pref)) * (dff // _tile(dff, tf_pref))
    return all(rows % (BF16_SUBLANES * steps) == 0 for rows in row_counts)


def _ffn(x, g, w1, w2, layer, final_g, final, tm_pref=512, tf_pref=1024, emit=False, cast=()):
    m, d = x.shape
    dff = w1.shape[-1]
    tm = _tile(m, tm_pref)
    tf = _tile(dff, tf_pref)
    nf = dff // tf
    wbytes = w1.dtype.itemsize
    n_i = m // tm
    norm_rows = tm // nf if n_i > 1 and tm % (nf * BF16_SUBLANES) == 0 else 0
    vmem = (4 * tm * d * 4 + 2 * tm * d * 2 + 2 * 2 * d * tf * wbytes + 2 * tm * d * 4 + 2 * tm * tf * 4
            + (6 * d * tf * 2 if emit else 0))
    o_shape = [jax.ShapeDtypeStruct((m, d), F32)]
    o_spec = [pl.BlockSpec((tm, d), lambda i, f: (i, 0))]
    in_specs = [pl.BlockSpec((tm, d), lambda i, f: (i, 0)),
                pl.BlockSpec((tm, d), lambda i, f: (jnp.minimum(i + 1, n_i - 1), 0)),
                pl.BlockSpec((1, d), lambda i, f: (0, 0)),
                _wspec(w1, layer, (d, tf), lambda i, f: (0, f)),
                _wspec(w2, layer, (tf, d), lambda i, f: (f, 0)),
                pl.BlockSpec((1, d), lambda i, f: (0, 0))]
    args = [x, x, g, w1, w2, final_g]
    if emit:
        assert m == tm, "each weight block must be visited once"
        o_shape += [jax.ShapeDtypeStruct((d, dff), BF16), jax.ShapeDtypeStruct((dff, d), BF16)]
        o_spec += [pl.BlockSpec((d, tf), lambda i, f: (0, f)), pl.BlockSpec((tf, d), lambda i, f: (f, 0))]
    assert not (emit and cast)
    steps = (m // tm) * nf
    slab = lambda i, f: (i * nf + f, 0)
    for mat, mat_layer in cast:
        rows, cols = mat.shape[-2:]
        r = rows // steps
        assert r * steps == rows and r % BF16_SUBLANES == 0
        in_specs.append(_wspec(mat, mat_layer, (r, cols), slab))
        args.append(mat)
        o_shape.append(jax.ShapeDtypeStruct((rows, cols), BF16))
        o_spec.append(pl.BlockSpec((r, cols), slab))
        vmem += 6 * r * cols * 4
    outs = pl.pallas_call(
        functools.partial(_ffn_kernel, final=final, emit=emit, n_cast=len(cast), norm_rows=norm_rows),
        name=f"ffn_m{m}_l{layer}",
        out_shape=o_shape,
        grid=(n_i, nf),
        in_specs=in_specs,
        out_specs=o_spec,
        scratch_shapes=[pltpu.VMEM((tm, d), BF16)] * 2,
        compiler_params=_cparams(("arbitrary", "arbitrary"), vmem),
    )(*args)
    return outs[0] if len(outs) == 1 else tuple(outs)


def _conv_tail_kernel(x_ref, u_ref, halo_ref, dww_ref, dwb_ref, lng_ref, lnb_ref, wout_ref,
                      o_ref, uext_ref, y_ref, *, n_taps, rows, ln_rows, lanes):
    i = pl.program_id(0)
    tm, c = u_ref.shape
    cur = lax.rem(i, 2)

    @pl.when(i == 0)
    def _():
        y_ref[1] = jnp.zeros((tm, c), BF16)

    uext_ref[0:CONV_HALO, :] = halo_ref[0]
    uext_ref[CONV_HALO:CONV_HALO + tm, :] = u_ref[...]
    off = CONV_HALO - (n_taps - 1)

    def row_chunk(r, carry):
        r0 = pl.multiple_of(r * rows, rows)
        for l0 in range(0, c, lanes):
            win = uext_ref[pl.ds(r0, rows + CONV_HALO), l0:l0 + lanes]
            accs = [jnp.zeros((SUBLANES, lanes), F32)] * (rows // SUBLANES)
            for s in range(SUBLANES):
                ws = win if s == 0 else pltpu.roll(win, rows + CONV_HALO - s, axis=0)
                for a in range((CONV_HALO + SUBLANES) // SUBLANES):
                    k = a * SUBLANES + s - off
                    if 0 <= k < n_taps:
                        wk = dww_ref[k * SUBLANES:(k + 1) * SUBLANES, l0:l0 + lanes]
                        accs = [acc + wk * ws[(a + j) * SUBLANES:(a + j + 1) * SUBLANES]
                                for j, acc in enumerate(accs)]
            for j, acc in enumerate(accs):
                uext_ref[pl.ds(r0 + j * SUBLANES, SUBLANES), l0:l0 + lanes] = acc + dwb_ref[:, l0:l0 + lanes]
        return carry

    @pl.when(i < pl.num_programs(0) - 1)
    def _():
        lax.fori_loop(0, tm // rows, row_chunk, 0)

    o_ref[...] = x_ref[...] + _dot(y_ref[1 - cur], wout_ref[...])
    for r0 in range(0, tm, ln_rows):
        cv = uext_ref[r0:r0 + ln_rows, :]
        mu = jnp.mean(cv, axis=-1, keepdims=True)
        cc = cv - mu
        var = jnp.mean(cc * cc, axis=-1, keepdims=True)
        y = cc * lax.rsqrt(var + LN_EPS) * lng_ref[...] + lnb_ref[...]
        y_ref[cur, r0:r0 + ln_rows, :] = (y * jax.nn.sigmoid(y)).astype(BF16)


def _conv_tail(x, u, halo, dww, dwb, lng, lnb, wout, layer, tm, n_taps):
    m, d = x.shape
    c = u.shape[1]
    rows = _tile(tm, 128)
    ln_rows = _tile(tm, 16)
    lanes = _tile(c, 128)
    nt = m // tm
    prev = lambda i: (jnp.maximum(i - 1, 0), 0)
    this = lambda i: (jnp.minimum(i, nt - 1), 0)
    vmem = 4 * tm * d * 4 + 2 * tm * c * 4 + 2 * c * d * 2 + (tm + CONV_HALO) * c * 4 + 2 * tm * c * 2
    return pl.pallas_call(
        functools.partial(_conv_tail_kernel, n_taps=n_taps, rows=rows, ln_rows=ln_rows, lanes=lanes),
        name=f"conv_tail_m{m}",
        out_shape=jax.ShapeDtypeStruct((m, d), F32),
        grid=(nt + 1,),
        in_specs=[pl.BlockSpec((tm, d), prev),
                  pl.BlockSpec((tm, c), this),
                  pl.BlockSpec((1, CONV_HALO, c), lambda i: (jnp.minimum(i, nt - 1), 0, 0)),
                  pl.BlockSpec(dww.shape, lambda i: (0, 0)),
                  pl.BlockSpec((1, c), lambda i: (0, 0)),
                  pl.BlockSpec((1, c), lambda i: (0, 0)),
                  pl.BlockSpec((1, c), lambda i: (0, 0)),
                  _wspec(wout, layer, (c, d), lambda i: (0, 0))],
        out_specs=pl.BlockSpec((tm, d), prev),
        scratch_shapes=[pltpu.VMEM((tm + CONV_HALO, c), F32), pltpu.VMEM((2, tm, c), BF16)],
        compiler_params=_cparams(("arbitrary",), vmem),
    )(x, u, halo, dww, dwb, lng, lnb, wout)


def _last_rows(past, seq, n):
    if seq.shape[1] >= n:
        return seq[:, seq.shape[1] - n:]
    return jnp.concatenate([past, seq], axis=1)[:, -n:]


def _halos(seq, past, tm, halo_rows):
    b, l, c = seq.shape
    p = past.shape[1]
    first = jnp.concatenate([jnp.zeros((b, halo_rows - p, c), seq.dtype), past], axis=1)[:, None]
    nt = l // tm
    if nt == 1:
        return first.reshape(b, halo_rows, c)
    rest = seq.reshape(b, nt, tm, c)[:, :-1, tm - halo_rows:]
    return jnp.concatenate([first, rest], axis=1).reshape(b * nt, halo_rows, c)


def _lru_tail_kernel(x_ref, ybr_ref, xbr_ref, halo_ref, h0_ref, cw_ref, cb_ref, wa_ref, ba_ref,
                     wx_ref, bx_ref, lam_ref, wout_ref, o_ref, hlast_ref,
                     xext_ref, a_ref, b_ref, hc_ref, *, n_taps, lanes):
    tm, c = xbr_ref.shape
    n_blocks, blk, _ = wa_ref.shape
    t = pl.program_id(1)

    @pl.when(t == 0)
    def _():
        hc_ref[...] = jnp.broadcast_to(h0_ref[0], hc_ref.shape)

    xext_ref[0:LRU_HALO, :] = halo_ref[0]
    xext_ref[LRU_HALO:LRU_HALO + tm, :] = xbr_ref[...]
    off = LRU_HALO - (n_taps - 1)

    sp = jax.nn.softplus(-lam_ref[...])
    for n in range(n_blocks):
        sl = slice(n * blk, (n + 1) * blk)
        xc = jnp.zeros((tm, blk), F32)
        for k in range(n_taps):
            xc = xc + cw_ref[k:k + 1, sl] * xext_ref[off + k:off + k + tm, sl]
        xc = xc + cb_ref[:, sl]
        xcb = xc.astype(BF16)
        r = jax.nn.sigmoid(_dot(xcb, wa_ref[n]) + ba_ref[:, sl])
        gi = jax.nn.sigmoid(_dot(xcb, wx_ref[n]) + bx_ref[:, sl])
        log_a = -LRU_C * r * sp[:, sl]
        a = jnp.exp(log_a)
        a_ref[:, sl] = a
        b_ref[:, sl] = jnp.sqrt(-jnp.tanh(log_a) * (a * a + 1.0)) * (gi * xc)

    row = lax.broadcasted_iota(jnp.int32, (SUBLANES, lanes), 0)

    for l0 in range(0, c, lanes):
        def group(gidx, h):
            r0 = pl.multiple_of(gidx * SUBLANES, SUBLANES)
            av = a_ref[pl.ds(r0, SUBLANES), l0:l0 + lanes]
            bv = b_ref[pl.ds(r0, SUBLANES), l0:l0 + lanes]
            for s in (1, 2, 4):
                a_sh = jnp.where(row >= s, pltpu.roll(av, s, axis=0), 1.0)
                b_sh = jnp.where(row >= s, pltpu.roll(bv, s, axis=0), 0.0)
                bv = av * b_sh + bv
                av = av * a_sh
            hs = av * h + bv
            b_ref[pl.ds(r0, SUBLANES), l0:l0 + lanes] = hs
            return jnp.broadcast_to(hs[SUBLANES - 1:SUBLANES, :], (SUBLANES, lanes))

        h_fin = lax.fori_loop(0, tm // SUBLANES, group, hc_ref[:, l0:l0 + lanes], unroll=2)
        hc_ref[:, l0:l0 + lanes] = h_fin

    gated = (b_ref[...] * ybr_ref[...]).astype(BF16)
    o_ref[...] = x_ref[...] + _dot(gated, wout_ref[...])

    @pl.when(t == pl.num_programs(1) - 1)
    def _():
        hlast_ref[0] = hc_ref[...]


def _lru_tail(x, ybr, xbr, halo, h0, cw, cb, wa, ba, wx, bx, lam, wout, batch, tm, n_taps):
    m, d = x.shape
    c = xbr.shape[1]
    nt = m // batch // tm
    lanes = _tile(c, 512)
    row2 = lambda b, t: (b * nt + t, 0)
    const2 = lambda b, t: (0, 0)
    vec = pl.BlockSpec((1, c), const2)
    vmem = (4 * tm * d * 4 + 4 * tm * c * 4 + 2 * c * d * 2 + 4 * wa.size * 2
            + (tm + LRU_HALO) * c * 4 + 2 * tm * c * 4)
    return pl.pallas_call(
        functools.partial(_lru_tail_kernel, n_taps=n_taps, lanes=lanes),
        name=f"lru_tail_m{m}",
        out_shape=[jax.ShapeDtypeStruct((m, d), F32),
                   jax.ShapeDtypeStruct((batch, SUBLANES, c), F32)],
        grid=(batch, nt),
        in_specs=[pl.BlockSpec((tm, d), row2),
                  pl.BlockSpec((tm, c), row2),
                  pl.BlockSpec((tm, c), row2),
                  pl.BlockSpec((1, LRU_HALO, c), lambda b, t: (b * nt + t, 0, 0)),
                  pl.BlockSpec((1, 1, c), lambda b, t: (b, 0, 0)),
                  pl.BlockSpec(cw.shape, const2),
                  vec,
                  pl.BlockSpec(wa.shape, lambda b, t: (0, 0, 0)),
                  vec,
                  pl.BlockSpec(wx.shape, lambda b, t: (0, 0, 0)),
                  vec, vec,
                  pl.BlockSpec((c, d), const2)],
        out_specs=[pl.BlockSpec((tm, d), row2),
                   pl.BlockSpec((1, SUBLANES, c), lambda b, t: (b, 0, 0))],
        scratch_shapes=[pltpu.VMEM((tm + LRU_HALO, c), F32), pltpu.VMEM((tm, c), F32),
                        pltpu.VMEM((tm, c), F32), pltpu.VMEM((SUBLANES, c), F32)],
        compiler_params=_cparams(("parallel", "arbitrary"), vmem),
    )(x, ybr, xbr, halo, h0, cw, cb, wa, ba, wx, bx, lam, wout)


def _shifted_rows(vec, n_rows):
    return pltpu.roll(jnp.broadcast_to(vec, (n_rows, vec.shape[1])), 0, 1, stride=1, stride_axis=0)


def _block_softmax(s_parts, v_parts):
    m = s_parts[0].max(axis=-1, keepdims=True)
    for s in s_parts[1:]:
        m = jnp.maximum(m, s.max(axis=-1, keepdims=True))
    o, den = None, None
    for s, v in zip(s_parts, v_parts):
        p = jnp.exp(s - m)
        d = p.sum(axis=-1, keepdims=True)
        pv = _dot(p.astype(BF16), v)
        o, den = (pv, d) if o is None else (o + pv, den + d)
    return o, m, den


def _block_softmax_mxu_sum(s_parts, v_parts, exp=jnp.exp):
    m = s_parts[0].max(axis=-1, keepdims=True)
    for s in s_parts[1:]:
        m = jnp.maximum(m, s.max(axis=-1, keepdims=True))
    acc = None
    for s, v in zip(s_parts, v_parts):
        pv = _dot(exp(s - m).astype(BF16), v)
        acc = pv if acc is None else acc + pv
    dh = acc.shape[1] // 2
    return acc[:, :dh], m, acc[:, dh:]


def _merge(state, o_b, m_b, l_b, exp=jnp.exp):
    if state is None:
        return o_b, m_b, l_b
    acc, m_old, l_old = state
    m_new = jnp.maximum(m_old, m_b)
    a_old = exp(m_old - m_new)
    a_b = exp(m_b - m_new)
    return acc * a_old + o_b * a_b, m_new, l_old * a_old + l_b * a_b


def _attn_prompt_kernel(*refs, heads, dh):
    n_g = len(GROUPS)
    ins = refs[:5 * n_g]
    bvec_ref, o_ref, acc_ref, m_ref, l_ref = refs[5 * n_g:]
    first_chunk = pl.program_id(1) == 0
    h = pl.program_id(2)
    log2e = float(np.log2(np.e))
    scale = dh ** -0.5 * log2e
    qi = lax.broadcasted_iota(jnp.int32, (BLK, 2 * BLK), 0)
    kj = lax.broadcasted_iota(jnp.int32, (BLK, 2 * BLK), 1)
    in_band = (kj - qi >= 0) & (kj - qi <= BLK)
    ones = jnp.ones((BLK, dh), BF16)
    for g, (_, dil) in enumerate(GROUPS):
        q_ref, k_ref, v_ref, kp_ref, vp_ref = ins[5 * g:5 * g + 5]
        bias = jnp.where(in_band, _shifted_rows(bvec_ref[pl.ds(g * heads + h, 1), :] * log2e, BLK), NEG_INF)
        bias_p, bias_c = bias[:, :BLK], bias[:, BLK:]
        for r in range(dil):
            prev_rows = pl.ds(r, BLK, stride=dil)
            k_prev = kp_ref[prev_rows, :].astype(BF16)
            v_prev = jnp.concatenate([vp_ref[prev_rows, :].astype(BF16), ones], axis=1)
            for nb in range(CHUNK // (dil * BLK)):
                rows = pl.ds(r + dil * BLK * nb, BLK, stride=dil)
                q = q_ref[rows, :].astype(BF16)
                k_cur = k_ref[rows, :].astype(BF16)
                v_cur = jnp.concatenate([v_ref[rows, :].astype(BF16), ones], axis=1)
                s_p = _nt_dot(q, k_prev) * scale + bias_p
                if nb == 0:
                    s_p = jnp.where(first_chunk, NEG_INF, s_p)
                s_c = _nt_dot(q, k_cur) * scale + bias_c
                o_b, m_b, l_b = _block_softmax_mxu_sum([s_p, s_c], [v_prev, v_cur], exp=jnp.exp2)
                m_b = jnp.broadcast_to(m_b, (BLK, dh))
                state = None if g == 0 else (acc_ref[rows, :], m_ref[rows, :], l_ref[rows, :])
                acc, m_new, l_new = _merge(state, o_b, m_b, l_b, exp=jnp.exp2)
                if g == n_g - 1:
                    o_ref[rows, :] = acc / l_new
                else:
                    acc_ref[rows, :] = acc
                    m_ref[rows, :] = m_new
                    l_ref[rows, :] = l_new
                k_prev, v_prev = k_cur, v_cur


def _attn_prompt(qkv, bvec, batch, seq, heads, dh):
    n_g = len(GROUPS)
    nc = seq // CHUNK
    col = lambda comp, g: (lambda b, n, h: (b * nc + n, (comp * n_g + g) * heads + h))
    in_specs, args = [], []
    for g, (win, _) in enumerate(GROUPS):
        per = CHUNK // win
        prev = lambda comp, g=g, per=per: (
            lambda b, n, h: (jnp.maximum((b * nc + n) * per - 1, 0), (comp * n_g + g) * heads + h))
        in_specs += [pl.BlockSpec((CHUNK, dh), col(0, g)), pl.BlockSpec((CHUNK, dh), col(1, g)),
                     pl.BlockSpec((CHUNK, dh), col(2, g)),
                     pl.BlockSpec((win, dh), prev(1)), pl.BlockSpec((win, dh), prev(2))]
        args += [qkv] * 5
    in_specs.append(pl.BlockSpec(bvec.shape, lambda b, n, h: (0, 0)))
    vmem = 2 * (9 * CHUNK + 2 * sum(w for w, _ in GROUPS)) * dh * 4 + 2 * CHUNK * dh * 4 + 3 * CHUNK * dh * 4
    return pl.pallas_call(
        functools.partial(_attn_prompt_kernel, heads=heads, dh=dh),
        name="attn_prompt",
        out_shape=jax.ShapeDtypeStruct((batch * seq, heads * dh), F32),
        grid=(batch, nc, heads),
        in_specs=in_specs,
        out_specs=pl.BlockSpec((CHUNK, dh), lambda b, n, h: (b * nc + n, h)),
        scratch_shapes=[pltpu.VMEM((CHUNK, dh), F32)] * 3,
        compiler_params=_cparams(("parallel", "parallel", "parallel"), vmem),
    )(*args, bvec)


def _attn_step_kernel(*refs, heads, dh, n_bufs):
    n_g = len(GROUPS)
    ins = refs[:6 * n_g]
    o_ref, acc_ref, m_ref, l_ref, knp_ref, vnp_ref = refs[6 * n_g:]
    part = pl.program_id(1)
    t = o_ref.shape[1]
    scale = dh ** -0.5
    row_stride = 2 * heads

    def merge_into(hs, o_b, m_b, l_b):
        state = (acc_ref[:, hs], m_ref[:, hs], l_ref[:, hs])
        acc, m_new, l_new = _merge(state, o_b, jnp.broadcast_to(m_b, (t, dh)), jnp.broadcast_to(l_b, (t, dh)))
        acc_ref[:, hs] = acc
        m_ref[:, hs] = m_new
        l_ref[:, hs] = l_new

    def cache_piece(g, first_row):
        dil = GROUPS[g][1]
        q_ref, _, _, cache_ref, yc_ref, _ = ins[6 * g:6 * g + 6]
        rows = cache_ref.shape[1] // row_stride
        tc = lax.broadcasted_iota(jnp.int32, (t, rows), 0)
        cc = lax.broadcasted_iota(jnp.int32, (t, rows), 1) + first_row
        dist_c = n_bufs[g] + tc - cc
        ok_c = ((dist_c & (dil - 1)) == 0) & (dist_c <= (N_DIL_KEYS - 1) * dil)
        for h in range(heads):
            hs = slice(h * dh, (h + 1) * dh)
            q = q_ref[0, :, hs].astype(BF16)
            kc = cache_ref[0, pl.ds(h, rows, stride=row_stride), :].astype(BF16)
            vc = cache_ref[0, pl.ds(heads + h, rows, stride=row_stride), :].astype(BF16)
            bias = _shifted_rows(yc_ref[0, h:h + 1, :], t)[:, BLK:]
            s_c = jnp.where(ok_c, _nt_dot(q, kc) * scale + bias, NEG_INF)
            merge_into(hs, *_block_softmax([s_c], [vc]))

    @pl.when(part == 0)
    def _():
        acc_ref[...] = jnp.zeros_like(acc_ref)
        l_ref[...] = jnp.zeros_like(l_ref)
        m_ref[...] = jnp.full_like(m_ref, NEG_INF)
        tn = lax.broadcasted_iota(jnp.int32, (t, BLK), 0)
        cn = lax.broadcasted_iota(jnp.int32, (t, BLK), 1)
        dist_n = tn - cn
        for g, (_, dil) in enumerate(GROUPS):
            q_ref, kn_ref, vn_ref, _, _, yn_ref = ins[6 * g:6 * g + 6]
            ok_n = (dist_n >= 0) & ((dist_n & (dil - 1)) == 0) & (dist_n <= (N_DIL_KEYS - 1) * dil)
            knp_ref[...] = jnp.zeros_like(knp_ref)
            vnp_ref[...] = jnp.zeros_like(vnp_ref)
            knp_ref[0:t, :] = kn_ref[0]
            vnp_ref[0:t, :] = vn_ref[0]
            for h in range(heads):
                hs = slice(h * dh, (h + 1) * dh)
                q = q_ref[0, :, hs].astype(BF16)
                bias = _shifted_rows(yn_ref[h:h + 1, :], t)[:, BLK:]
                s_n = jnp.where(ok_n, _nt_dot(q, knp_ref[:, hs].astype(BF16)) * scale + bias, NEG_INF)
                merge_into(hs, *_block_softmax([s_n], [vnp_ref[:, hs].astype(BF16)]))
        for g in range(n_g):
            if ins[6 * g + 3].shape[1] == n_bufs[g] * row_stride:
                cache_piece(g, 0)

    for g in range(n_g):
        rows = ins[6 * g + 3].shape[1] // row_stride
        if rows != n_bufs[g]:
            cache_piece(g, part * rows)

    @pl.when(part == pl.num_programs(1) - 1)
    def _():
        o_ref[0] = acc_ref[...] / l_ref[...]


def _attn_step(qkv, caches, rel_bias, batch, t, heads, dh, n_parts=2):
    n_g = len(GROUPS)
    width = heads * dh
    qkv3 = qkv.reshape(batch, t, qkv.shape[1])
    in_specs, args, n_bufs = [], [], []
    vmem = 8 * t * width * 4
    for g in range(n_g):
        n_buf = caches[g].shape[1]
        parts = n_parts if n_buf % (n_parts * BLK) == 0 else 1
        rows = n_buf // parts
        flat = caches[g].reshape(batch, n_buf * 2 * heads, dh)
        yc, yn = _step_bias_vectors(rel_bias, g, heads, n_buf, parts)
        new = lambda comp, g=g: pl.BlockSpec((1, t, width), lambda b, s: (b, 0, comp * n_g + g))
        part_of = (lambda s: s) if parts == n_parts else (lambda s: 0)
        in_specs += [new(0), new(1), new(2),
                     pl.BlockSpec((1, rows * 2 * heads, dh), lambda b, s, f=part_of: (b, f(s), 0)),
                     pl.BlockSpec((1, heads, rows + BLK), lambda b, s, f=part_of: (f(s), 0, 0)),
                     pl.BlockSpec(yn.shape, lambda b, s: (0, 0))]
        args += [qkv3, qkv3, qkv3, flat, yc, yn]
        n_bufs.append(n_buf)
        vmem += 2 * rows * 2 * width * 4 + 6 * rows * dh * 4
    return pl.pallas_call(
        functools.partial(_attn_step_kernel, heads=heads, dh=dh, n_bufs=tuple(n_bufs)),
        name="attn_step",
        out_shape=jax.ShapeDtypeStruct((batch, t, width), F32),
        grid=(batch, n_parts),
        in_specs=in_specs,
        out_specs=pl.BlockSpec((1, t, width), lambda b, s: (b, 0, 0)),
        scratch_shapes=[pltpu.VMEM((t, width), F32)] * 3 + [pltpu.VMEM((BLK, width), F32)] * 2,
        compiler_params=_cparams(("parallel", "arbitrary"), vmem),
    )(*args).reshape(batch * t, width)


def _attn_out_kernel(x_ref, o_ref, wo_ref, out_ref):
    out_ref[...] = x_ref[...] + _dot(o_ref[...].astype(BF16), wo_ref[...])


def _attn_out(x, o, wo, tm_pref=512):
    m, d = x.shape
    width = wo.shape[0]
    tm = _tile(m, tm_pref)
    row = lambda n: pl.BlockSpec((tm, n), lambda i: (i, 0))
    vmem = 4 * tm * d * 4 + 2 * tm * width * 4 + 2 * width * d * 2
    return pl.pallas_call(
        _attn_out_kernel,
        name=f"attn_out_m{m}",
        out_shape=jax.ShapeDtypeStruct((m, d), F32),
        grid=(m // tm,),
        in_specs=[row(d), row(width), pl.BlockSpec((width, d), lambda i: (0, 0))],
        out_specs=row(d),
        compiler_params=_cparams(("parallel",), vmem),
    )(x, o, wo)


def _t5_bucket(dist):
    max_exact = N_BUCKETS // 2
    d = np.asarray(dist, dtype=np.int32)
    large = max_exact + (np.log(np.maximum(d, max_exact) / max_exact) / np.log(MAX_DISTANCE / max_exact)
                         * (N_BUCKETS - max_exact)).astype(np.int32)
    return np.where(d < max_exact, d, np.minimum(large, N_BUCKETS - 1)).astype(np.int32)


def _group_bias(rel_bias, g, heads):
    buckets = _t5_bucket(np.arange(N_DIL_KEYS) * GROUPS[g][1])
    return rel_bias[:, g * heads:(g + 1) * heads].astype(F32)[buckets]


def _prompt_bias_vectors(rel_bias, heads):
    vecs = []
    for g in range(len(GROUPS)):
        rev = _group_bias(rel_bias, g, heads)[::-1]
        vecs.append(jnp.pad(rev, ((0, 2 * BLK - N_DIL_KEYS), (0, 0))).T)
    return jnp.concatenate(vecs, axis=0)


def _step_bias_vectors(rel_bias, g, heads, n_buf, parts):
    rows = n_buf // parts
    bias = _group_bias(rel_bias, g, heads)
    at = lambda dist: bias[np.clip(dist // GROUPS[g][1], 0, N_DIL_KEYS - 1)]
    first = np.arange(parts)[:, None] * rows + np.arange(rows + BLK)[None, :]
    yc = jnp.transpose(at(n_buf + BLK - first), (0, 2, 1))
    yn = at(BLK - np.arange(2 * BLK)).T
    return yc, yn


def _kv_rows_kernel(k_ref, v_ref, o_ref, *, heads, dh):
    rows = k_ref.shape[0]
    o_ref[0, :, 0] = k_ref[...].reshape(rows, heads, dh)
    o_ref[0, :, 1] = v_ref[...].reshape(rows, heads, dh)


def _kv_rows(qkv, g, batch, seq, n_rows, heads, dh):
    n_g = len(GROUPS)
    width = heads * dh
    tm = _tile(n_rows, 256)
    assert seq % tm == 0 and n_rows % tm == 0
    first = (seq - n_rows) // tm
    per_seq = seq // tm
    col = lambda comp: pl.BlockSpec((tm, width), lambda b, i: (b * per_seq + first + i, comp * n_g + g))
    return pl.pallas_call(
        functools.partial(_kv_rows_kernel, heads=heads, dh=dh),
        name=f"kv_rows_g{g}_n{n_rows}",
        out_shape=jax.ShapeDtypeStruct((batch, n_rows, 2, heads, dh), qkv.dtype),
        grid=(batch, n_rows // tm),
        in_specs=[col(1), col(2)],
        out_specs=pl.BlockSpec((1, tm, 2, heads, dh), lambda b, i: (b, i, 0, 0, 0)),
        compiler_params=_cparams(("parallel", "parallel"), 8 * tm * width * 4),
    )(qkv, qkv)


def _trunk(x3, st, p, tiles, bank, emit):
    batch, seq, d = x3.shape
    m = batch * seq
    x = x3.reshape(m, d)
    new = []
    depth = p['ffn_w1'].shape[0]
    for i in range(depth):
        kind = i % N_MIXERS
        g_mix = p['norm_mix_g'][i][None]
        if kind == 0:
            j = i // N_MIXERS
            n_taps, c = p['conv_dw_w'].shape[1:]
            buf = jnp.zeros((batch, n_taps - 1, c), F32) if st is None else st[i][0]
            if ('glu', j) in bank:
                u = _norm_proj(x, g_mix, bank['glu', j], 'glu', tiles['proj_m'])
            else:
                u, *halves = _norm_proj(x, g_mix, p['conv_w_glu'], 'glu', tiles['proj_m'], layer=j, emit=True)
                bank['glu', j] = tuple(halves)
            u3 = u.reshape(batch, seq, c)
            tm = _tile(seq, tiles['conv_m'])
            halo = _halos(u3, buf, tm, CONV_HALO)
            dww = jnp.repeat(p['conv_dw_w'][j], SUBLANES, axis=0)
            w_out = bank['conv_out', j] if ('conv_out', j) in bank else p['conv_w_out'][j].astype(BF16)
            x = _conv_tail(x, u, halo, dww, p['conv_dw_b'][j][None], p['conv_ln_g'][j][None],
                           p['conv_ln_b'][j][None], w_out, j, tm, n_taps)
            new.append((_last_rows(buf, u3, n_taps - 1),))
        elif kind == 1:
            n_taps, c = p['lru_conv_w'].shape
            if st is None:
                h0 = jnp.zeros((batch, c), F32)
                cb = jnp.zeros((batch, n_taps - 1, c), F32)
            else:
                h0, cb = st[i]
            if 'lru_in' in bank:
                ybr, xbr = _norm_proj(x, g_mix, bank['lru_in'], 'split', tiles['proj_m'])
            else:
                ybr, xbr, *halves = _norm_proj(x, g_mix, p['lru_w_in'], 'split', tiles['proj_m'], emit=True)
                bank['lru_in'] = tuple(halves)
            w_out = bank['lru_out'] if 'lru_out' in bank else p['lru_w_out'].astype(BF16)
            xbr3 = xbr.reshape(batch, seq, c)
            tm = _tile(seq, tiles['lru_m'])
            halo = _halos(xbr3, cb, tm, LRU_HALO)
            cw = jnp.pad(p['lru_conv_w'], ((0, SUBLANES - n_taps), (0, 0)))
            x, hl = _lru_tail(x, ybr, xbr, halo, h0[:, None], cw, p['lru_conv_b'][None], p['lru_w_a'],
                              p['lru_b_a'][None], p['lru_w_x'], p['lru_b_x'][None], p['lru_lambda'][None],
                              w_out, batch, tm, n_taps)
            new.append((hl[:, 0], _last_rows(cb, xbr3, n_taps - 1)))
        else:
            rel_bias = p['rel_bias']
            heads = rel_bias.shape[1] // len(GROUPS)
            width = p['att_w_o'].shape[0]
            dh = width // heads
            if 'qkv' in bank:
                qkv = _norm_proj(x, g_mix, bank['qkv'], 'plain', tiles['proj_m'], tn_pref=1024)
            else:
                qkv, bank['qkv'] = _norm_proj(x, g_mix, p['att_w_qkv'], 'plain', tiles['proj_m'], emit=True)
            if st is None:
                assert seq % CHUNK == 0
                o = _attn_prompt(qkv, _prompt_bias_vectors(rel_bias, heads), batch, seq, heads, dh)
                rows = [_kv_rows(qkv, g, batch, seq, min(win, seq), heads, dh) for g, (win, _) in enumerate(GROUPS)]
            else:
                assert seq <= BLK
                o = _attn_step(qkv, st[i], rel_bias, batch, seq, heads, dh)
                rows = [_kv_rows(qkv, g, batch, seq, seq, heads, dh) for g in range(len(GROUPS))]
            x = _attn_out(x, o, p['att_w_o'], tiles['attn_out_m'])
            new.append(rows)
        g_ffn, g_fin, final = p['norm_ffn_g'][i][None], p['final_norm_g'][None], i == depth - 1
        ffn_tiles = dict(tm_pref=tiles['ffn_m'], tf_pref=tiles['ffn_f'])
        jobs = _cast_jobs(p, i + 1) if not emit and i + 1 < depth else []
        if ('ffn_w1', i) not in bank:
            x, bank['ffn_w1', i], bank['ffn_w2', i] = _ffn(x, g_ffn, p['ffn_w1'], p['ffn_w2'], i, g_fin, final,
                                                           emit=True, **ffn_tiles)
        elif jobs and _can_cast(m, p['ffn_w1'].shape[-1], [mat.shape[-2] for _, mat, _ in jobs], **ffn_tiles):
            x, *casts = _ffn(x, g_ffn, bank['ffn_w1', i], bank['ffn_w2', i], i, g_fin, final, **ffn_tiles,
                             cast=[(mat, mat_layer) for _, mat, mat_layer in jobs])
            bank.update({key: cast for (key, _, _), cast in zip(jobs, casts)})
        else:
            x = _ffn(x, g_ffn, bank['ffn_w1', i], bank['ffn_w2', i], i, g_fin, final, **ffn_tiles)
        yield
    return x.reshape(batch, seq, d), new


def _cast_jobs(p, layer):
    jobs = [(('ffn_w1', layer), p['ffn_w1'], layer), (('ffn_w2', layer), p['ffn_w2'], layer)]
    kind = layer % N_MIXERS
    if kind == 0:
        j = layer // N_MIXERS
        jobs += [(('glu', j), p['conv_w_glu'], j), (('conv_out', j), p['conv_w_out'], j)]
    elif kind == 1:
        jobs += [('lru_in', p['lru_w_in'], None), ('lru_out', p['lru_w_out'], None)]
    else:
        jobs += [('qkv', p['att_w_qkv'], None)]
    return jobs


def _run_alternating(*gens):
    results = [None] * len(gens)
    live = list(range(len(gens)))
    while live:
        for k in list(live):
            try:
                next(gens[k])
            except StopIteration as done:
                results[k] = done.value
                live.remove(k)
    return results


PROMPT_TILES = dict(proj_m=1024, conv_m=256, lru_m=256, attn_out_m=512, ffn_m=512, ffn_f=1024)
SAMPLE_TILES = dict(proj_m=512, conv_m=256, lru_m=256, attn_out_m=512, ffn_m=512, ffn_f=512)
MATMUL_WEIGHTS = ('lru_w_a', 'lru_w_x', 'att_w_o')


def kernel(x_prompt, x_sample, state_conv_l0, state_lru_h_l1, state_lru_conv_l1, cache_kv_w128_l2, cache_kv_w512_l2, cache_kv_w2048_l2, state_conv_l3, norm_mix_g, norm_ffn_g, final_norm_g, ffn_w1, ffn_w2, conv_w_glu, conv_dw_w, conv_dw_b, conv_ln_g, conv_ln_b, conv_w_out, lru_w_in, lru_conv_w, lru_conv_b, lru_w_a, lru_b_a, lru_w_x, lru_b_x, lru_lambda, lru_w_out, att_w_qkv, att_w_o, rel_bias):
    p = dict(norm_mix_g=norm_mix_g, norm_ffn_g=norm_ffn_g, final_norm_g=final_norm_g, ffn_w1=ffn_w1,
             ffn_w2=ffn_w2, conv_w_glu=conv_w_glu, conv_dw_w=conv_dw_w, conv_dw_b=conv_dw_b,
             conv_ln_g=conv_ln_g, conv_ln_b=conv_ln_b, conv_w_out=conv_w_out, lru_w_in=lru_w_in,
             lru_conv_w=lru_conv_w, lru_conv_b=lru_conv_b, lru_w_a=lru_w_a, lru_b_a=lru_b_a, lru_w_x=lru_w_x,
             lru_b_x=lru_b_x, lru_lambda=lru_lambda, lru_w_out=lru_w_out, att_w_qkv=att_w_qkv,
             att_w_o=att_w_o, rel_bias=rel_bias)
    for name in MATMUL_WEIGHTS:
        p[name] = p[name].astype(BF16)
    bank = {}
    sample = _trunk(x_sample, [(state_conv_l0,), (state_lru_h_l1, state_lru_conv_l1),
                               (cache_kv_w128_l2, cache_kv_w512_l2, cache_kv_w2048_l2),
                               (state_conv_l3,)], p, SAMPLE_TILES, bank, emit=True)
    prompt = _trunk(x_prompt, None, p, PROMPT_TILES, bank, emit=False)
    (y_sample, sst), (y_prompt, pst) = _run_alternating(sample, prompt)
    return (y_prompt, y_sample,
            pst[0][0], sst[0][0],
            pst[1][0], sst[1][0],
            pst[1][1], sst[1][1],
            pst[2][0], sst[2][0],
            pst[2][1], sst[2][1],
            pst[2][2], sst[2][2],
            pst[3][0], sst[3][0])
```

```python
import functools

import numpy as np
import jax
import jax.numpy as jnp
from jax import lax
from jax.experimental import pallas as pl
from jax.experimental.pallas import tpu as pltpu

RMS_EPS = 1e-6
LN_EPS = 1e-5
LRU_C = 8.0
NEG_INF = -1e30
N_MIXERS = 3
GROUPS = ((128, 1), (512, 4), (2048, 16))
N_DIL_KEYS = 129
BLK = 128
CHUNK = BLK * max(d for _, d in GROUPS)
N_BUCKETS = 32
MAX_DISTANCE = 2048

V7X_VMEM_BYTES = 64 * 1024 * 1024
MIN_VMEM_LIMIT_BYTES = 32 * 1024 * 1024
VMEM_RESERVE_BYTES = 8 * 1024 * 1024
SUBLANES = 8
BF16_SUBLANES = 16
CONV_HALO = 32
LRU_HALO = 8

BF16 = jnp.bfloat16
F32 = jnp.float32


def _cparams(semantics, vmem_bytes):
    limit = int(min(max(vmem_bytes * 3 // 2, MIN_VMEM_LIMIT_BYTES), V7X_VMEM_BYTES - VMEM_RESERVE_BYTES))
    return pltpu.CompilerParams(dimension_semantics=semantics, vmem_limit_bytes=limit)


def _tile(n, pref):
    if n <= pref:
        return n
    t = pref
    while n % t:
        t //= 2
    return t


def _rms_to_bf16(x, g):
    ms = jnp.mean(x * x, axis=-1, keepdims=True)
    return (x * lax.rsqrt(ms + RMS_EPS) * g).astype(BF16)


def _dot(a, b):
    return jnp.dot(a, b, preferred_element_type=F32)


def _nt_dot(a, b):
    return lax.dot_general(a, b, (((1,), (1,)), ((), ())), preferred_element_type=F32)


def _norm_proj_kernel(x_ref, g_ref, *refs, mode, emit):
    n_w = 1 if mode == 'plain' else 2
    n_o = 2 if mode == 'split' else 1
    w_refs, o_refs = refs[:n_w], refs[n_w:n_w + n_o]
    wb_refs, h_ref = refs[n_w + n_o:-1], refs[-1]

    @pl.when(pl.program_id(1) == 0)
    def _():
        h_ref[...] = _rms_to_bf16(x_ref[...], g_ref[...])

    ws = [w_ref[...].astype(BF16) for w_ref in w_refs]
    if emit:
        for wb_ref, w in zip(wb_refs, ws):
            wb_ref[...] = w
    h = h_ref[...]
    if mode == 'plain':
        o_refs[0][...] = _dot(h, ws[0])
    elif mode == 'glu':
        o_refs[0][...] = _dot(h, ws[0]) * jax.nn.sigmoid(_dot(h, ws[1]))
    else:
        o_refs[0][...] = jax.nn.gelu(_dot(h, ws[0]), approximate=True)
        o_refs[1][...] = _dot(h, ws[1])


def _wspec(w, layer, block, index_map):
    if w.ndim == 2:
        return pl.BlockSpec(block, index_map)
    return pl.BlockSpec((None,) + block, lambda *ids: (layer,) + index_map(*ids))


def _norm_proj(x, g, w, mode, tm_pref=512, tn_pref=512, layer=0, emit=False):
    m, d = x.shape
    halves = mode != 'plain'
    pair = isinstance(w, tuple)
    n_out = w[0].shape[-1] if pair else (w.shape[-1] // 2 if halves else w.shape[-1])
    tm = _tile(m, tm_pref)
    tn = _tile(n_out, tn_pref)
    nj = n_out // tn
    w_args = list(w) if pair else ([w, w] if halves else [w])
    shift = [0, 0] if pair else [0, nj]
    in_specs = [pl.BlockSpec((tm, d), lambda i, j: (i, 0)), pl.BlockSpec((1, d), lambda i, j: (0, 0))]
    in_specs += [_wspec(wk, layer, (d, tn), lambda i, j, s=s: (0, j + s)) for wk, s in zip(w_args, shift)]
    n_o = 2 if mode == 'split' else 1
    o_shapes = [jax.ShapeDtypeStruct((m, n_out), F32)] * n_o
    o_specs = [pl.BlockSpec((tm, tn), lambda i, j: (i, j))] * n_o
    if emit:
        assert m == tm, "each weight block must be visited once"
        o_shapes += [jax.ShapeDtypeStruct((d, n_out), BF16)] * len(w_args)
        o_specs += [pl.BlockSpec((d, tn), lambda i, j: (0, j))] * len(w_args)
    wbytes = w_args[0].dtype.itemsize
    vmem = (2 * tm * d * 4 + tm * d * 2 + 2 * len(w_args) * d * tn * wbytes + 2 * n_o * tm * tn * 4
            + (3 * len(w_args) * d * tn * 2 if emit else 0))
    outs = pl.pallas_call(
        functools.partial(_norm_proj_kernel, mode=mode, emit=emit),
        name=f"norm_proj_{mode}_m{m}",
        out_shape=o_shapes,
        grid=(m // tm, nj),
        in_specs=in_specs,
        out_specs=o_specs,
        scratch_shapes=[pltpu.VMEM((tm, d), BF16)],
        compiler_params=_cparams(("parallel", "arbitrary"), vmem),
    )(x, g, *w_args)
    return outs[0] if len(outs) == 1 else tuple(outs)


def _ffn_kernel(x_ref, g_ref, w1_ref, w2_ref, fg_ref, *rest, final, emit, n_cast):
    if n_cast:
        o_ref, h_ref = rest[n_cast], rest[-1]
        for src_ref, dst_ref in zip(rest[:n_cast], rest[n_cast + 1:-1]):
            dst_ref[...] = src_ref[...].astype(BF16)
    elif emit:
        o_ref, w1b_ref, w2b_ref, h_ref = rest
    else:
        o_ref, h_ref = rest
    f = pl.program_id(1)

    @pl.when(f == 0)
    def _():
        x = x_ref[...]
        h_ref[...] = _rms_to_bf16(x, g_ref[...])
        o_ref[...] = x

    w1 = w1_ref[...].astype(BF16)
    w2 = w2_ref[...].astype(BF16)
    if emit:
        w1b_ref[...] = w1
        w2b_ref[...] = w2
    hid = jnp.square(jnp.maximum(_dot(h_ref[...], w1), 0.0)).astype(BF16)
    o_ref[...] += _dot(hid, w2)

    if final:
        @pl.when(f == pl.num_programs(1) - 1)
        def _():
            y = o_ref[...]
            ms = jnp.mean(y * y, axis=-1, keepdims=True)
            o_ref[...] = y * lax.rsqrt(ms + RMS_EPS) * fg_ref[...]


def _can_cast(m, dff, row_counts, tm_pref, tf_pref):
    steps = (m // _tile(m, tm_pref)) * (dff // _tile(dff, tf_pref))
    return all(rows % (BF16_SUBLANES * steps) == 0 for rows in row_counts)


def _ffn(x, g, w1, w2, layer, final_g, final, tm_pref=512, tf_pref=1024, emit=False, cast=()):
    m, d = x.shape
    dff = w1.shape[-1]
    tm = _tile(m, tm_pref)
    tf = _tile(dff, tf_pref)
    nf = dff // tf
    wbytes = w1.dtype.itemsize
    vmem = (2 * tm * d * 4 + tm * d * 2 + 2 * 2 * d * tf * wbytes + 2 * tm * d * 4 + 2 * tm * tf * 4
            + (6 * d * tf * 2 if emit else 0))
    o_shape = [jax.ShapeDtypeStruct((m, d), F32)]
    o_spec = [pl.BlockSpec((tm, d), lambda i, f: (i, 0))]
    in_specs = [pl.BlockSpec((tm, d), lambda i, f: (i, 0)),
                pl.BlockSpec((1, d), lambda i, f: (0, 0)),
                _wspec(w1, layer, (d, tf), lambda i, f: (0, f)),
                _wspec(w2, layer, (tf, d), lambda i, f: (f, 0)),
                pl.BlockSpec((1, d), lambda i, f: (0, 0))]
    args = [x, g, w1, w2, final_g]
    if emit:
        assert m == tm, "each weight block must be visited once"
        o_shape += [jax.ShapeDtypeStruct((d, dff), BF16), jax.ShapeDtypeStruct((dff, d), BF16)]
        o_spec += [pl.BlockSpec((d, tf), lambda i, f: (0, f)), pl.BlockSpec((tf, d), lambda i, f: (f, 0))]
    assert not (emit and cast)
    steps = (m // tm) * nf
    slab = lambda i, f: (i * nf + f, 0)
    for mat, mat_layer in cast:
        rows, cols = mat.shape[-2:]
        r = rows // steps
        assert r * steps == rows and r % BF16_SUBLANES == 0
        in_specs.append(_wspec(mat, mat_layer, (r, cols), slab))
        args.append(mat)
        o_shape.append(jax.ShapeDtypeStruct((rows, cols), BF16))
        o_spec.append(pl.BlockSpec((r, cols), slab))
        vmem += 6 * r * cols * 4
    outs = pl.pallas_call(
        functools.partial(_ffn_kernel, final=final, emit=emit, n_cast=len(cast)),
        name=f"ffn_m{m}_l{layer}",
        out_shape=o_shape,
        grid=(m // tm, nf),
        in_specs=in_specs,
        out_specs=o_spec,
        scratch_shapes=[pltpu.VMEM((tm, d), BF16)],
        compiler_params=_cparams(("arbitrary", "arbitrary"), vmem),
    )(*args)
    return outs[0] if len(outs) == 1 else tuple(outs)


def _conv_tail_kernel(x_ref, u_ref, halo_ref, dww_ref, dwb_ref, lng_ref, lnb_ref, wout_ref,
                      o_ref, uext_ref, y_ref, *, n_taps, rows, ln_rows, lanes):
    i = pl.program_id(0)
    tm, c = u_ref.shape
    cur = lax.rem(i, 2)

    @pl.when(i == 0)
    def _():
        y_ref[1] = jnp.zeros((tm, c), BF16)

    uext_ref[0:CONV_HALO, :] = halo_ref[0]
    uext_ref[CONV_HALO:CONV_HALO + tm, :] = u_ref[...]
    off = CONV_HALO - (n_taps - 1)

    def row_chunk(r, carry):
        r0 = pl.multiple_of(r * rows, rows)
        for l0 in range(0, c, lanes):
            win = uext_ref[pl.ds(r0, rows + CONV_HALO), l0:l0 + lanes]
            accs = [jnp.zeros((SUBLANES, lanes), F32)] * (rows // SUBLANES)
            for s in range(SUBLANES):
                ws = win if s == 0 else pltpu.roll(win, rows + CONV_HALO - s, axis=0)
                for a in range((CONV_HALO + SUBLANES) // SUBLANES):
                    k = a * SUBLANES + s - off
                    if 0 <= k < n_taps:
                        wk = dww_ref[k * SUBLANES:(k + 1) * SUBLANES, l0:l0 + lanes]
                        accs = [acc + wk * ws[(a + j) * SUBLANES:(a + j + 1) * SUBLANES]
                                for j, acc in enumerate(accs)]
            for j, acc in enumerate(accs):
                uext_ref[pl.ds(r0 + j * SUBLANES, SUBLANES), l0:l0 + lanes] = acc + dwb_ref[:, l0:l0 + lanes]
        return carry

    @pl.when(i < pl.num_programs(0) - 1)
    def _():
        lax.fori_loop(0, tm // rows, row_chunk, 0)

    o_ref[...] = x_ref[...] + _dot(y_ref[1 - cur], wout_ref[...])
    for r0 in range(0, tm, ln_rows):
        cv = uext_ref[r0:r0 + ln_rows, :]
        mu = jnp.mean(cv, axis=-1, keepdims=True)
        cc = cv - mu
        var = jnp.mean(cc * cc, axis=-1, keepdims=True)
        y = cc * lax.rsqrt(var + LN_EPS) * lng_ref[...] + lnb_ref[...]
        y_ref[cur, r0:r0 + ln_rows, :] = (y * jax.nn.sigmoid(y)).astype(BF16)


def _conv_tail(x, u, halo, dww, dwb, lng, lnb, wout, layer, tm, n_taps):
    m, d = x.shape
    c = u.shape[1]
    rows = _tile(tm, 128)
    ln_rows = _tile(tm, 16)
    lanes = _tile(c, 128)
    nt = m // tm
    prev = lambda i: (jnp.maximum(i - 1, 0), 0)
    this = lambda i: (jnp.minimum(i, nt - 1), 0)
    vmem = 4 * tm * d * 4 + 2 * tm * c * 4 + 2 * c * d * 2 + (tm + CONV_HALO) * c * 4 + 2 * tm * c * 2
    return pl.pallas_call(
        functools.partial(_conv_tail_kernel, n_taps=n_taps, rows=rows, ln_rows=ln_rows, lanes=lanes),
        name=f"conv_tail_m{m}",
        out_shape=jax.ShapeDtypeStruct((m, d), F32),
        grid=(nt + 1,),
        in_specs=[pl.BlockSpec((tm, d), prev),
                  pl.BlockSpec((tm, c), this),
                  pl.BlockSpec((1, CONV_HALO, c), lambda i: (jnp.minimum(i, nt - 1), 0, 0)),
                  pl.BlockSpec(dww.shape, lambda i: (0, 0)),
                  pl.BlockSpec((1, c), lambda i: (0, 0)),
                  pl.BlockSpec((1, c), lambda i: (0, 0)),
                  pl.BlockSpec((1, c), lambda i: (0, 0)),
                  _wspec(wout, layer, (c, d), lambda i: (0, 0))],
        out_specs=pl.BlockSpec((tm, d), prev),
        scratch_shapes=[pltpu.VMEM((tm + CONV_HALO, c), F32), pltpu.VMEM((2, tm, c), BF16)],
        compiler_params=_cparams(("arbitrary",), vmem),
    )(x, u, halo, dww, dwb, lng, lnb, wout)


def _last_rows(past, seq, n):
    if seq.shape[1] >= n:
        return seq[:, seq.shape[1] - n:]
    return jnp.concatenate([past, seq], axis=1)[:, -n:]


def _halos(seq, past, tm, halo_rows):
    b, l, c = seq.shape
    p = past.shape[1]
    first = jnp.concatenate([jnp.zeros((b, halo_rows - p, c), seq.dtype), past], axis=1)[:, None]
    nt = l // tm
    if nt == 1:
        return first.reshape(b, halo_rows, c)
    rest = seq.reshape(b, nt, tm, c)[:, :-1, tm - halo_rows:]
    return jnp.concatenate([first, rest], axis=1).reshape(b * nt, halo_rows, c)


def _lru_tail_kernel(x_ref, ybr_ref, xbr_ref, halo_ref, h0_ref, cw_ref, cb_ref, wa_ref, ba_ref,
                     wx_ref, bx_ref, lam_ref, wout_ref, o_ref, hlast_ref,
                     xext_ref, a_ref, b_ref, hc_ref, *, n_taps, lanes):
    tm, c = xbr_ref.shape
    n_blocks, blk, _ = wa_ref.shape
    t = pl.program_id(1)

    @pl.when(t == 0)
    def _():
        hc_ref[...] = jnp.broadcast_to(h0_ref[0], hc_ref.shape)

    xext_ref[0:LRU_HALO, :] = halo_ref[0]
    xext_ref[LRU_HALO:LRU_HALO + tm, :] = xbr_ref[...]
    off = LRU_HALO - (n_taps - 1)

    sp = jax.nn.softplus(-lam_ref[...])
    for n in range(n_blocks):
        sl = slice(n * blk, (n + 1) * blk)
        xc = jnp.zeros((tm, blk), F32)
        for k in range(n_taps):
            xc = xc + cw_ref[k:k + 1, sl] * xext_ref[off + k:off + k + tm, sl]
        xc = xc + cb_ref[:, sl]
        xcb = xc.astype(BF16)
        r = jax.nn.sigmoid(_dot(xcb, wa_ref[n]) + ba_ref[:, sl])
        gi = jax.nn.sigmoid(_dot(xcb, wx_ref[n]) + bx_ref[:, sl])
        log_a = -LRU_C * r * sp[:, sl]
        a = jnp.exp(log_a)
        a_ref[:, sl] = a
        b_ref[:, sl] = jnp.sqrt(-jnp.tanh(log_a) * (a * a + 1.0)) * (gi * xc)

    row = lax.broadcasted_iota(jnp.int32, (SUBLANES, lanes), 0)

    for l0 in range(0, c, lanes):
        def group(gidx, h):
            r0 = pl.multiple_of(gidx * SUBLANES, SUBLANES)
            av = a_ref[pl.ds(r0, SUBLANES), l0:l0 + lanes]
            bv = b_ref[pl.ds(r0, SUBLANES), l0:l0 + lanes]
            for s in (1, 2, 4):
                a_sh = jnp.where(row >= s, pltpu.roll(av, s, axis=0), 1.0)
                b_sh = jnp.where(row >= s, pltpu.roll(bv, s, axis=0), 0.0)
                bv = av * b_sh + bv
                av = av * a_sh
            hs = av * h + bv
            b_ref[pl.ds(r0, SUBLANES), l0:l0 + lanes] = hs
            return jnp.broadcast_to(hs[SUBLANES - 1:SUBLANES, :], (SUBLANES, lanes))

        h_fin = lax.fori_loop(0, tm // SUBLANES, group, hc_ref[:, l0:l0 + lanes], unroll=2)
        hc_ref[:, l0:l0 + lanes] = h_fin

    gated = (b_ref[...] * ybr_ref[...]).astype(BF16)
    o_ref[...] = x_ref[...] + _dot(gated, wout_ref[...])

    @pl.when(t == pl.num_programs(1) - 1)
    def _():
        hlast_ref[0] = hc_ref[...]


def _lru_tail(x, ybr, xbr, halo, h0, cw, cb, wa, ba, wx, bx, lam, wout, batch, tm, n_taps):
    m, d = x.shape
    c = xbr.shape[1]
    nt = m // batch // tm
    lanes = _tile(c, 512)
    row2 = lambda b, t: (b * nt + t, 0)
    const2 = lambda b, t: (0, 0)
    vec = pl.BlockSpec((1, c), const2)
    vmem = (4 * tm * d * 4 + 4 * tm * c * 4 + 2 * c * d * 2 + 4 * wa.size * 2
            + (tm + LRU_HALO) * c * 4 + 2 * tm * c * 4)
    return pl.pallas_call(
        functools.partial(_lru_tail_kernel, n_taps=n_taps, lanes=lanes),
        name=f"lru_tail_m{m}",
        out_shape=[jax.ShapeDtypeStruct((m, d), F32),
                   jax.ShapeDtypeStruct((batch, SUBLANES, c), F32)],
        grid=(batch, nt),
        in_specs=[pl.BlockSpec((tm, d), row2),
                  pl.BlockSpec((tm, c), row2),
                  pl.BlockSpec((tm, c), row2),
                  pl.BlockSpec((1, LRU_HALO, c), lambda b, t: (b * nt + t, 0, 0)),
                  pl.BlockSpec((1, 1, c), lambda b, t: (b, 0, 0)),
                  pl.BlockSpec(cw.shape, const2),
                  vec,
                  pl.BlockSpec(wa.shape, lambda b, t: (0, 0, 0)),
                  vec,
                  pl.BlockSpec(wx.shape, lambda b, t: (0, 0, 0)),
                  vec, vec,
                  pl.BlockSpec((c, d), const2)],
        out_specs=[pl.BlockSpec((tm, d), row2),
                   pl.BlockSpec((1, SUBLANES, c), lambda b, t: (b, 0, 0))],
        scratch_shapes=[pltpu.VMEM((tm + LRU_HALO, c), F32), pltpu.VMEM((tm, c), F32),
                        pltpu.VMEM((tm, c), F32), pltpu.VMEM((SUBLANES, c), F32)],
        compiler_params=_cparams(("parallel", "arbitrary"), vmem),
    )(x, ybr, xbr, halo, h0, cw, cb, wa, ba, wx, bx, lam, wout)


def _shifted_rows(vec, n_rows):
    return pltpu.roll(jnp.broadcast_to(vec, (n_rows, vec.shape[1])), 0, 1, stride=1, stride_axis=0)


def _block_softmax(s_parts, v_parts):
    m = s_parts[0].max(axis=-1, keepdims=True)
    for s in s_parts[1:]:
        m = jnp.maximum(m, s.max(axis=-1, keepdims=True))
    o, den = None, None
    for s, v in zip(s_parts, v_parts):
        p = jnp.exp(s - m)
        d = p.sum(axis=-1, keepdims=True)
        pv = _dot(p.astype(BF16), v)
        o, den = (pv, d) if o is None else (o + pv, den + d)
    return o, m, den


def _block_softmax_mxu_sum(s_parts, v_parts, exp=jnp.exp):
    m = s_parts[0].max(axis=-1, keepdims=True)
    for s in s_parts[1:]:
        m = jnp.maximum(m, s.max(axis=-1, keepdims=True))
    acc = None
    for s, v in zip(s_parts, v_parts):
        pv = _dot(exp(s - m).astype(BF16), v)
        acc = pv if acc is None else acc + pv
    dh = acc.shape[1] // 2
    return acc[:, :dh], m, acc[:, dh:]


def _merge(state, o_b, m_b, l_b, exp=jnp.exp):
    if state is None:
        return o_b, m_b, l_b
    acc, m_old, l_old = state
    m_new = jnp.maximum(m_old, m_b)
    a_old = exp(m_old - m_new)
    a_b = exp(m_b - m_new)
    return acc * a_old + o_b * a_b, m_new, l_old * a_old + l_b * a_b


def _attn_prompt_kernel(*refs, heads, dh):
    n_g = len(GROUPS)
    ins = refs[:5 * n_g]
    bvec_ref, o_ref, acc_ref, m_ref, l_ref = refs[5 * n_g:]
    first_chunk = pl.program_id(1) == 0
    h = pl.program_id(2)
    log2e = float(np.log2(np.e))
    scale = dh ** -0.5 * log2e
    qi = lax.broadcasted_iota(jnp.int32, (BLK, 2 * BLK), 0)
    kj = lax.broadcasted_iota(jnp.int32, (BLK, 2 * BLK), 1)
    in_band = (kj - qi >= 0) & (kj - qi <= BLK)
    ones = jnp.ones((BLK, dh), BF16)
    for g, (_, dil) in enumerate(GROUPS):
        q_ref, k_ref, v_ref, kp_ref, vp_ref = ins[5 * g:5 * g + 5]
        bias = jnp.where(in_band, _shifted_rows(bvec_ref[pl.ds(g * heads + h, 1), :] * log2e, BLK), NEG_INF)
        bias_p, bias_c = bias[:, :BLK], bias[:, BLK:]
        for r in range(dil):
            prev_rows = pl.ds(r, BLK, stride=dil)
            k_prev = kp_ref[prev_rows, :].astype(BF16)
            v_prev = jnp.concatenate([vp_ref[prev_rows, :].astype(BF16), ones], axis=1)
            for nb in range(CHUNK // (dil * BLK)):
                rows = pl.ds(r + dil * BLK * nb, BLK, stride=dil)
                q = q_ref[rows, :].astype(BF16)
                k_cur = k_ref[rows, :].astype(BF16)
                v_cur = jnp.concatenate([v_ref[rows, :].astype(BF16), ones], axis=1)
                s_p = _nt_dot(q, k_prev) * scale + bias_p
                if nb == 0:
                    s_p = jnp.where(first_chunk, NEG_INF, s_p)
                s_c = _nt_dot(q, k_cur) * scale + bias_c
                o_b, m_b, l_b = _block_softmax_mxu_sum([s_p, s_c], [v_prev, v_cur], exp=jnp.exp2)
                m_b = jnp.broadcast_to(m_b, (BLK, dh))
                state = None if g == 0 else (acc_ref[rows, :], m_ref[rows, :], l_ref[rows, :])
                acc, m_new, l_new = _merge(state, o_b, m_b, l_b, exp=jnp.exp2)
                if g == n_g - 1:
                    o_ref[rows, :] = acc / l_new
                else:
                    acc_ref[rows, :] = acc
                    m_ref[rows, :] = m_new
                    l_ref[rows, :] = l_new
                k_prev, v_prev = k_cur, v_cur


def _attn_prompt(qkv, bvec, batch, seq, heads, dh):
    n_g = len(GROUPS)
    nc = seq // CHUNK
    col = lambda comp, g: (lambda b, n, h: (b * nc + n, (comp * n_g + g) * heads + h))
    in_specs, args = [], []
    for g, (win, _) in enumerate(GROUPS):
        per = CHUNK // win
        prev = lambda comp, g=g, per=per: (
            lambda b, n, h: (jnp.maximum((b * nc + n) * per - 1, 0), (comp * n_g + g) * heads + h))
        in_specs += [pl.BlockSpec((CHUNK, dh), col(0, g)), pl.BlockSpec((CHUNK, dh), col(1, g)),
                     pl.BlockSpec((CHUNK, dh), col(2, g)),
                     pl.BlockSpec((win, dh), prev(1)), pl.BlockSpec((win, dh), prev(2))]
        args += [qkv] * 5
    in_specs.append(pl.BlockSpec(bvec.shape, lambda b, n, h: (0, 0)))
    vmem = 2 * (9 * CHUNK + 2 * sum(w for w, _ in GROUPS)) * dh * 4 + 2 * CHUNK * dh * 4 + 3 * CHUNK * dh * 4
    return pl.pallas_call(
        functools.partial(_attn_prompt_kernel, heads=heads, dh=dh),
        name="attn_prompt",
        out_shape=jax.ShapeDtypeStruct((batch * seq, heads * dh), F32),
        grid=(batch, nc, heads),
        in_specs=in_specs,
        out_specs=pl.BlockSpec((CHUNK, dh), lambda b, n, h: (b * nc + n, h)),
        scratch_shapes=[pltpu.VMEM((CHUNK, dh), F32)] * 3,
        compiler_params=_cparams(("parallel", "parallel", "parallel"), vmem),
    )(*args, bvec)


def _attn_step_kernel(*refs, heads, dh, n_bufs):
    n_g = len(GROUPS)
    ins = refs[:6 * n_g]
    o_ref, acc_ref, m_ref, l_ref, knp_ref, vnp_ref = refs[6 * n_g:]
    part = pl.program_id(1)
    t = o_ref.shape[1]
    scale = dh ** -0.5
    row_stride = 2 * heads

    def merge_into(hs, o_b, m_b, l_b):
        state = (acc_ref[:, hs], m_ref[:, hs], l_ref[:, hs])
        acc, m_new, l_new = _merge(state, o_b, jnp.broadcast_to(m_b, (t, dh)), jnp.broadcast_to(l_b, (t, dh)))
        acc_ref[:, hs] = acc
        m_ref[:, hs] = m_new
        l_ref[:, hs] = l_new

    def cache_piece(g, first_row):
        dil = GROUPS[g][1]
        q_ref, _, _, cache_ref, yc_ref, _ = ins[6 * g:6 * g + 6]
        rows = cache_ref.shape[1] // row_stride
        tc = lax.broadcasted_iota(jnp.int32, (t, rows), 0)
        cc = lax.broadcasted_iota(jnp.int32, (t, rows), 1) + first_row
        dist_c = n_bufs[g] + tc - cc
        ok_c = ((dist_c & (dil - 1)) == 0) & (dist_c <= (N_DIL_KEYS - 1) * dil)
        for h in range(heads):
            hs = slice(h * dh, (h + 1) * dh)
            q = q_ref[0, :, hs].astype(BF16)
            kc = cache_ref[0, pl.ds(h, rows, stride=row_stride), :].astype(BF16)
            vc = cache_ref[0, pl.ds(heads + h, rows, stride=row_stride), :].astype(BF16)
            bias = _shifted_rows(yc_ref[0, h:h + 1, :], t)[:, BLK:]
            s_c = jnp.where(ok_c, _nt_dot(q, kc) * scale + bias, NEG_INF)
            merge_into(hs, *_block_softmax([s_c], [vc]))

    @pl.when(part == 0)
    def _():
        acc_ref[...] = jnp.zeros_like(acc_ref)
        l_ref[...] = jnp.zeros_like(l_ref)
        m_ref[...] = jnp.full_like(m_ref, NEG_INF)
        tn = lax.broadcasted_iota(jnp.int32, (t, BLK), 0)
        cn = lax.broadcasted_iota(jnp.int32, (t, BLK), 1)
        dist_n = tn - cn
        for g, (_, dil) in enumerate(GROUPS):
            q_ref, kn_ref, vn_ref, _, _, yn_ref = ins[6 * g:6 * g + 6]
            ok_n = (dist_n >= 0) & ((dist_n & (dil - 1)) == 0) & (dist_n <= (N_DIL_KEYS - 1) * dil)
            knp_ref[...] = jnp.zeros_like(knp_ref)
            vnp_ref[...] = jnp.zeros_like(vnp_ref)
            knp_ref[0:t, :] = kn_ref[0]
            vnp_ref[0:t, :] = vn_ref[0]
            for h in range(heads):
                hs = slice(h * dh, (h + 1) * dh)
                q = q_ref[0, :, hs].astype(BF16)
                bias = _shifted_rows(yn_ref[h:h + 1, :], t)[:, BLK:]
                s_n = jnp.where(ok_n, _nt_dot(q, knp_ref[:, hs].astype(BF16)) * scale + bias, NEG_INF)
                merge_into(hs, *_block_softmax([s_n], [vnp_ref[:, hs].astype(BF16)]))
        for g in range(n_g):
            if ins[6 * g + 3].shape[1] == n_bufs[g] * row_stride:
                cache_piece(g, 0)

    for g in range(n_g):
        rows = ins[6 * g + 3].shape[1] // row_stride
        if rows != n_bufs[g]:
            cache_piece(g, part * rows)

    @pl.when(part == pl.num_programs(1) - 1)
    def _():
        o_ref[0] = acc_ref[...] / l_ref[...]


def _attn_step(qkv, caches, rel_bias, batch, t, heads, dh, n_parts=2):
    n_g = len(GROUPS)
    width = heads * dh
    qkv3 = qkv.reshape(batch, t, qkv.shape[1])
    in_specs, args, n_bufs = [], [], []
    vmem = 8 * t * width * 4
    for g in range(n_g):
        n_buf = caches[g].shape[1]
        parts = n_parts if n_buf % (n_parts * BLK) == 0 else 1
        rows = n_buf // parts
        flat = caches[g].reshape(batch, n_buf * 2 * heads, dh)
        yc, yn = _step_bias_vectors(rel_bias, g, heads, n_buf, parts)
        new = lambda comp, g=g: pl.BlockSpec((1, t, width), lambda b, s: (b, 0, comp * n_g + g))
        part_of = (lambda s: s) if parts == n_parts else (lambda s: 0)
        in_specs += [new(0), new(1), new(2),
                     pl.BlockSpec((1, rows * 2 * heads, dh), lambda b, s, f=part_of: (b, f(s), 0)),
                     pl.BlockSpec((1, heads, rows + BLK), lambda b, s, f=part_of: (f(s), 0, 0)),
                     pl.BlockSpec(yn.shape, lambda b, s: (0, 0))]
        args += [qkv3, qkv3, qkv3, flat, yc, yn]
        n_bufs.append(n_buf)
        vmem += 2 * rows * 2 * width * 4 + 6 * rows * dh * 4
    return pl.pallas_call(
        functools.partial(_attn_step_kernel, heads=heads, dh=dh, n_bufs=tuple(n_bufs)),
        name="attn_step",
        out_shape=jax.ShapeDtypeStruct((batch, t, width), F32),
        grid=(batch, n_parts),
        in_specs=in_specs,
        out_specs=pl.BlockSpec((1, t, width), lambda b, s: (b, 0, 0)),
        scratch_shapes=[pltpu.VMEM((t, width), F32)] * 3 + [pltpu.VMEM((BLK, width), F32)] * 2,
        compiler_params=_cparams(("parallel", "arbitrary"), vmem),
    )(*args).reshape(batch * t, width)


def _attn_out_kernel(x_ref, o_ref, wo_ref, out_ref):
    out_ref[...] = x_ref[...] + _dot(o_ref[...].astype(BF16), wo_ref[...])


def _attn_out(x, o, wo, tm_pref=512):
    m, d = x.shape
    width = wo.shape[0]
    tm = _tile(m, tm_pref)
    row = lambda n: pl.BlockSpec((tm, n), lambda i: (i, 0))
    vmem = 4 * tm * d * 4 + 2 * tm * width * 4 + 2 * width * d * 2
    return pl.pallas_call(
        _attn_out_kernel,
        name=f"attn_out_m{m}",
        out_shape=jax.ShapeDtypeStruct((m, d), F32),
        grid=(m // tm,),
        in_specs=[row(d), row(width), pl.BlockSpec((width, d), lambda i: (0, 0))],
        out_specs=row(d),
        compiler_params=_cparams(("parallel",), vmem),
    )(x, o, wo)


def _t5_bucket(dist):
    max_exact = N_BUCKETS // 2
    d = np.asarray(dist, dtype=np.int32)
    large = max_exact + (np.log(np.maximum(d, max_exact) / max_exact) / np.log(MAX_DISTANCE / max_exact)
                         * (N_BUCKETS - max_exact)).astype(np.int32)
    return np.where(d < max_exact, d, np.minimum(large, N_BUCKETS - 1)).astype(np.int32)


def _group_bias(rel_bias, g, heads):
    buckets = _t5_bucket(np.arange(N_DIL_KEYS) * GROUPS[g][1])
    return rel_bias[:, g * heads:(g + 1) * heads].astype(F32)[buckets]


def _prompt_bias_vectors(rel_bias, heads):
    vecs = []
    for g in range(len(GROUPS)):
        rev = _group_bias(rel_bias, g, heads)[::-1]
        vecs.append(jnp.pad(rev, ((0, 2 * BLK - N_DIL_KEYS), (0, 0))).T)
    return jnp.concatenate(vecs, axis=0)


def _step_bias_vectors(rel_bias, g, heads, n_buf, parts):
    rows = n_buf // parts
    bias = _group_bias(rel_bias, g, heads)
    at = lambda dist: bias[np.clip(dist // GROUPS[g][1], 0, N_DIL_KEYS - 1)]
    first = np.arange(parts)[:, None] * rows + np.arange(rows + BLK)[None, :]
    yc = jnp.transpose(at(n_buf + BLK - first), (0, 2, 1))
    yn = at(BLK - np.arange(2 * BLK)).T
    return yc, yn


def _kv_rows_kernel(k_ref, v_ref, o_ref, *, heads, dh):
    rows = k_ref.shape[0]
    o_ref[0, :, 0] = k_ref[...].reshape(rows, heads, dh)
    o_ref[0, :, 1] = v_ref[...].reshape(rows, heads, dh)


def _kv_rows(qkv, g, batch, seq, n_rows, heads, dh):
    n_g = len(GROUPS)
    width = heads * dh
    tm = _tile(n_rows, 256)
    assert seq % tm == 0 and n_rows % tm == 0
    first = (seq - n_rows) // tm
    per_seq = seq // tm
    col = lambda comp: pl.BlockSpec((tm, width), lambda b, i: (b * per_seq + first + i, comp * n_g + g))
    return pl.pallas_call(
        functools.partial(_kv_rows_kernel, heads=heads, dh=dh),
        name=f"kv_rows_g{g}_n{n_rows}",
        out_shape=jax.ShapeDtypeStruct((batch, n_rows, 2, heads, dh), qkv.dtype),
        grid=(batch, n_rows // tm),
        in_specs=[col(1), col(2)],
        out_specs=pl.BlockSpec((1, tm, 2, heads, dh), lambda b, i: (b, i, 0, 0, 0)),
        compiler_params=_cparams(("parallel", "parallel"), 8 * tm * width * 4),
    )(qkv, qkv)


def _trunk(x3, st, p, tiles, bank, emit):
    batch, seq, d = x3.shape
    m = batch * seq
    x = x3.reshape(m, d)
    new = []
    depth = p['ffn_w1'].shape[0]
    for i in range(depth):
        kind = i % N_MIXERS
        g_mix = p['norm_mix_g'][i][None]
        if kind == 0:
            j = i // N_MIXERS
            n_taps, c = p['conv_dw_w'].shape[1:]
            buf = jnp.zeros((batch, n_taps - 1, c), F32) if st is None else st[i][0]
            if ('glu', j) in bank:
                u = _norm_proj(x, g_mix, bank['glu', j], 'glu', tiles['proj_m'])
            else:
                u, *halves = _norm_proj(x, g_mix, p['conv_w_glu'], 'glu', tiles['proj_m'], layer=j, emit=True)
                bank['glu', j] = tuple(halves)
            u3 = u.reshape(batch, seq, c)
            tm = _tile(seq, tiles['conv_m'])
            halo = _halos(u3, buf, tm, CONV_HALO)
            dww = jnp.repeat(p['conv_dw_w'][j], SUBLANES, axis=0)
            w_out = bank['conv_out', j] if ('conv_out', j) in bank else p['conv_w_out'][j].astype(BF16)
            x = _conv_tail(x, u, halo, dww, p['conv_dw_b'][j][None], p['conv_ln_g'][j][None],
                           p['conv_ln_b'][j][None], w_out, j, tm, n_taps)
            new.append((_last_rows(buf, u3, n_taps - 1),))
        elif kind == 1:
            n_taps, c = p['lru_conv_w'].shape
            if st is None:
                h0 = jnp.zeros((batch, c), F32)
                cb = jnp.zeros((batch, n_taps - 1, c), F32)
            else:
                h0, cb = st[i]
            if 'lru_in' in bank:
                ybr, xbr = _norm_proj(x, g_mix, bank['lru_in'], 'split', tiles['proj_m'])
            else:
                ybr, xbr, *halves = _norm_proj(x, g_mix, p['lru_w_in'], 'split', tiles['proj_m'], emit=True)
                bank['lru_in'] = tuple(halves)
            w_out = bank['lru_out'] if 'lru_out' in bank else p['lru_w_out'].astype(BF16)
            xbr3 = xbr.reshape(batch, seq, c)
            tm = _tile(seq, tiles['lru_m'])
            halo = _halos(xbr3, cb, tm, LRU_HALO)
            cw = jnp.pad(p['lru_conv_w'], ((0, SUBLANES - n_taps), (0, 0)))
            x, hl = _lru_tail(x, ybr, xbr, halo, h0[:, None], cw, p['lru_conv_b'][None], p['lru_w_a'],
                              p['lru_b_a'][None], p['lru_w_x'], p['lru_b_x'][None], p['lru_lambda'][None],
                              w_out, batch, tm, n_taps)
            new.append((hl[:, 0], _last_rows(cb, xbr3, n_taps - 1)))
        else:
            rel_bias = p['rel_bias']
            heads = rel_bias.shape[1] // len(GROUPS)
            width = p['att_w_o'].shape[0]
            dh = width // heads
            if 'qkv' in bank:
                qkv = _norm_proj(x, g_mix, bank['qkv'], 'plain', tiles['proj_m'], tn_pref=1024)
            else:
                qkv, bank['qkv'] = _norm_proj(x, g_mix, p['att_w_qkv'], 'plain', tiles['proj_m'], emit=True)
            if st is None:
                assert seq % CHUNK == 0
                o = _attn_prompt(qkv, _prompt_bias_vectors(rel_bias, heads), batch, seq, heads, dh)
                rows = [_kv_rows(qkv, g, batch, seq, min(win, seq), heads, dh) for g, (win, _) in enumerate(GROUPS)]
            else:
                assert seq <= BLK
                o = _attn_step(qkv, st[i], rel_bias, batch, seq, heads, dh)
                rows = [_kv_rows(qkv, g, batch, seq, seq, heads, dh) for g in range(len(GROUPS))]
            x = _attn_out(x, o, p['att_w_o'], tiles['attn_out_m'])
            new.append(rows)
        g_ffn, g_fin, final = p['norm_ffn_g'][i][None], p['final_norm_g'][None], i == depth - 1
        ffn_tiles = dict(tm_pref=tiles['ffn_m'], tf_pref=tiles['ffn_f'])
        jobs = _cast_jobs(p, i + 1) if not emit and i + 1 < depth else []
        if ('ffn_w1', i) not in bank:
            x, bank['ffn_w1', i], bank['ffn_w2', i] = _ffn(x, g_ffn, p['ffn_w1'], p['ffn_w2'], i, g_fin, final,
                                                           emit=True, **ffn_tiles)
        elif jobs and _can_cast(m, p['ffn_w1'].shape[-1], [mat.shape[-2] for _, mat, _ in jobs], **ffn_tiles):
            x, *casts = _ffn(x, g_ffn, bank['ffn_w1', i], bank['ffn_w2', i], i, g_fin, final, **ffn_tiles,
                             cast=[(mat, mat_layer) for _, mat, mat_layer in jobs])
            bank.update({key: cast for (key, _, _), cast in zip(jobs, casts)})
        else:
            x = _ffn(x, g_ffn, bank['ffn_w1', i], bank['ffn_w2', i], i, g_fin, final, **ffn_tiles)
        yield
    return x.reshape(batch, seq, d), new


def _cast_jobs(p, layer):
    jobs = [(('ffn_w1', layer), p['ffn_w1'], layer), (('ffn_w2', layer), p['ffn_w2'], layer)]
    kind = layer % N_MIXERS
    if kind == 0:
        j = layer // N_MIXERS
        jobs += [(('glu', j), p['conv_w_glu'], j), (('conv_out', j), p['conv_w_out'], j)]
    elif kind == 1:
        jobs += [('lru_in', p['lru_w_in'], None), ('lru_out', p['lru_w_out'], None)]
    else:
        jobs += [('qkv', p['att_w_qkv'], None)]
    return jobs


def _run_alternating(*gens):
    results = [None] * len(gens)
    live = list(range(len(gens)))
    while live:
        for k in list(live):
            try:
                next(gens[k])
            except StopIteration as done:
                results[k] = done.value
                live.remove(k)
    return results


PROMPT_TILES = dict(proj_m=1024, conv_m=256, lru_m=256, attn_out_m=512, ffn_m=512, ffn_f=1024)
SAMPLE_TILES = dict(proj_m=512, conv_m=256, lru_m=256, attn_out_m=512, ffn_m=512, ffn_f=512)
MATMUL_WEIGHTS = ('lru_w_a', 'lru_w_x', 'att_w_o')


def kernel(x_prompt, x_sample, state_conv_l0, state_lru_h_l1, state_lru_conv_l1, cache_kv_w128_l2, cache_kv_w512_l2, cache_kv_w2048_l2, state_conv_l3, norm_mix_g, norm_ffn_g, final_norm_g, ffn_w1, ffn_w2, conv_w_glu, conv_dw_w, conv_dw_b, conv_ln_g, conv_ln_b, conv_w_out, lru_w_in, lru_conv_w, lru_conv_b, lru_w_a, lru_b_a, lru_w_x, lru_b_x, lru_lambda, lru_w_out, att_w_qkv, att_w_o, rel_bias):
    p = dict(norm_mix_g=norm_mix_g, norm_ffn_g=norm_ffn_g, final_norm_g=final_norm_g, ffn_w1=ffn_w1,
             ffn_w2=ffn_w2, conv_w_glu=conv_w_glu, conv_dw_w=conv_dw_w, conv_dw_b=conv_dw_b,
             conv_ln_g=conv_ln_g, conv_ln_b=conv_ln_b, conv_w_out=conv_w_out, lru_w_in=lru_w_in,
             lru_conv_w=lru_conv_w, lru_conv_b=lru_conv_b, lru_w_a=lru_w_a, lru_b_a=lru_b_a, lru_w_x=lru_w_x,
             lru_b_x=lru_b_x, lru_lambda=lru_lambda, lru_w_out=lru_w_out, att_w_qkv=att_w_qkv,
             att_w_o=att_w_o, rel_bias=rel_bias)
    for name in MATMUL_WEIGHTS:
        p[name] = p[name].astype(BF16)
    bank = {}
    sample = _trunk(x_sample, [(state_conv_l0,), (state_lru_h_l1, state_lru_conv_l1),
                               (cache_kv_w128_l2, cache_kv_w512_l2, cache_kv_w2048_l2),
                               (state_conv_l3,)], p, SAMPLE_TILES, bank, emit=True)
    prompt = _trunk(x_prompt, None, p, PROMPT_TILES, bank, emit=False)
    (y_sample, sst), (y_prompt, pst) = _run_alternating(sample, prompt)
    return (y_prompt, y_sample,
            pst[0][0], sst[0][0],
            pst[1][0], sst[1][0],
            pst[1][1], sst[1][1],
            pst[2][0], sst[2][0],
            pst[2][1], sst[2][1],
            pst[2][2], sst[2][2],
            pst[3][0], sst[3][0])
```

```python
import functools

import numpy as np
import jax
import jax.numpy as jnp
from jax import lax
from jax.experimental import pallas as pl
from jax.experimental.pallas import tpu as pltpu

RMS_EPS = 1e-6
LN_EPS = 1e-5
LRU_C = 8.0
NEG_INF = -1e30
N_MIXERS = 3
GROUPS = ((128, 1), (512, 4), (2048, 16))
N_DIL_KEYS = 129
BLK = 128
CHUNK = BLK * max(d for _, d in GROUPS)
N_BUCKETS = 32
MAX_DISTANCE = 2048

V7X_VMEM_BYTES = 64 * 1024 * 1024
MIN_VMEM_LIMIT_BYTES = 32 * 1024 * 1024
VMEM_RESERVE_BYTES = 8 * 1024 * 1024
SUBLANES = 8
BF16_SUBLANES = 16
CONV_HALO = 32
LRU_HALO = 8

BF16 = jnp.bfloat16
F32 = jnp.float32


def _cparams(semantics, vmem_bytes):
    limit = int(min(max(vmem_bytes * 3 // 2, MIN_VMEM_LIMIT_BYTES), V7X_VMEM_BYTES - VMEM_RESERVE_BYTES))
    return pltpu.CompilerParams(dimension_semantics=semantics, vmem_limit_bytes=limit)


def _tile(n, pref):
    if n <= pref:
        return n
    t = pref
    while n % t:
        t //= 2
    return t


def _rms_to_bf16(x, g):
    ms = jnp.mean(x * x, axis=-1, keepdims=True)
    return (x * lax.rsqrt(ms + RMS_EPS) * g).astype(BF16)


def _dot(a, b):
    return jnp.dot(a, b, preferred_element_type=F32)


def _nt_dot(a, b):
    return lax.dot_general(a, b, (((1,), (1,)), ((), ())), preferred_element_type=F32)


def _norm_proj_kernel(x_ref, g_ref, *refs, mode, emit):
    n_w = 1 if mode == 'plain' else 2
    n_o = 2 if mode == 'split' else 1
    w_refs, o_refs = refs[:n_w], refs[n_w:n_w + n_o]
    wb_refs, h_ref = refs[n_w + n_o:-1], refs[-1]

    @pl.when(pl.program_id(1) == 0)
    def _():
        h_ref[...] = _rms_to_bf16(x_ref[...], g_ref[...])

    ws = [w_ref[...].astype(BF16) for w_ref in w_refs]
    if emit:
        for wb_ref, w in zip(wb_refs, ws):
            wb_ref[...] = w
    h = h_ref[...]
    if mode == 'plain':
        o_refs[0][...] = _dot(h, ws[0])
    elif mode == 'glu':
        o_refs[0][...] = _dot(h, ws[0]) * jax.nn.sigmoid(_dot(h, ws[1]))
    else:
        o_refs[0][...] = jax.nn.gelu(_dot(h, ws[0]), approximate=True)
        o_refs[1][...] = _dot(h, ws[1])


def _wspec(w, layer, block, index_map):
    if w.ndim == 2:
        return pl.BlockSpec(block, index_map)
    return pl.BlockSpec((None,) + block, lambda *ids: (layer,) + index_map(*ids))


def _norm_proj(x, g, w, mode, tm_pref=512, tn_pref=512, layer=0, emit=False):
    m, d = x.shape
    halves = mode != 'plain'
    pair = isinstance(w, tuple)
    n_out = w[0].shape[-1] if pair else (w.shape[-1] // 2 if halves else w.shape[-1])
    tm = _tile(m, tm_pref)
    tn = _tile(n_out, tn_pref)
    nj = n_out // tn
    w_args = list(w) if pair else ([w, w] if halves else [w])
    shift = [0, 0] if pair else [0, nj]
    in_specs = [pl.BlockSpec((tm, d), lambda i, j: (i, 0)), pl.BlockSpec((1, d), lambda i, j: (0, 0))]
    in_specs += [_wspec(wk, layer, (d, tn), lambda i, j, s=s: (0, j + s)) for wk, s in zip(w_args, shift)]
    n_o = 2 if mode == 'split' else 1
    o_shapes = [jax.ShapeDtypeStruct((m, n_out), F32)] * n_o
    o_specs = [pl.BlockSpec((tm, tn), lambda i, j: (i, j))] * n_o
    if emit:
        assert m == tm, "each weight block must be visited once"
        o_shapes += [jax.ShapeDtypeStruct((d, n_out), BF16)] * len(w_args)
        o_specs += [pl.BlockSpec((d, tn), lambda i, j: (0, j))] * len(w_args)
    wbytes = w_args[0].dtype.itemsize
    vmem = (2 * tm * d * 4 + tm * d * 2 + 2 * len(w_args) * d * tn * wbytes + 2 * n_o * tm * tn * 4
            + (3 * len(w_args) * d * tn * 2 if emit else 0))
    outs = pl.pallas_call(
        functools.partial(_norm_proj_kernel, mode=mode, emit=emit),
        name=f"norm_proj_{mode}_m{m}",
        out_shape=o_shapes,
        grid=(m // tm, nj),
        in_specs=in_specs,
        out_specs=o_specs,
        scratch_shapes=[pltpu.VMEM((tm, d), BF16)],
        compiler_params=_cparams(("parallel", "arbitrary"), vmem),
    )(x, g, *w_args)
    return outs[0] if len(outs) == 1 else tuple(outs)


def _ffn_kernel(x_ref, g_ref, w1_ref, w2_ref, fg_ref, *rest, final, emit, n_cast):
    if n_cast:
        o_ref, h_ref = rest[n_cast], rest[-1]
        for src_ref, dst_ref in zip(rest[:n_cast], rest[n_cast + 1:-1]):
            dst_ref[...] = src_ref[...].astype(BF16)
    elif emit:
        o_ref, w1b_ref, w2b_ref, h_ref = rest
    else:
        o_ref, h_ref = rest
    f = pl.program_id(1)

    @pl.when(f == 0)
    def _():
        x = x_ref[...]
        h_ref[...] = _rms_to_bf16(x, g_ref[...])
        o_ref[...] = x

    w1 = w1_ref[...].astype(BF16)
    w2 = w2_ref[...].astype(BF16)
    if emit:
        w1b_ref[...] = w1
        w2b_ref[...] = w2
    hid = jnp.square(jnp.maximum(_dot(h_ref[...], w1), 0.0)).astype(BF16)
    o_ref[...] += _dot(hid, w2)

    if final:
        @pl.when(f == pl.num_programs(1) - 1)
        def _():
            y = o_ref[...]
            ms = jnp.mean(y * y, axis=-1, keepdims=True)
            o_ref[...] = y * lax.rsqrt(ms + RMS_EPS) * fg_ref[...]


def _can_cast(m, dff, row_counts, tm_pref, tf_pref):
    steps = (m // _tile(m, tm_pref)) * (dff // _tile(dff, tf_pref))
    return all(rows % (BF16_SUBLANES * steps) == 0 for rows in row_counts)


def _ffn(x, g, w1, w2, layer, final_g, final, tm_pref=512, tf_pref=1024, emit=False, cast=()):
    m, d = x.shape
    dff = w1.shape[-1]
    tm = _tile(m, tm_pref)
    tf = _tile(dff, tf_pref)
    nf = dff // tf
    wbytes = w1.dtype.itemsize
    vmem = (2 * tm * d * 4 + tm * d * 2 + 2 * 2 * d * tf * wbytes + 2 * tm * d * 4 + 2 * tm * tf * 4
            + (6 * d * tf * 2 if emit else 0))
    o_shape = [jax.ShapeDtypeStruct((m, d), F32)]
    o_spec = [pl.BlockSpec((tm, d), lambda i, f: (i, 0))]
    in_specs = [pl.BlockSpec((tm, d), lambda i, f: (i, 0)),
                pl.BlockSpec((1, d), lambda i, f: (0, 0)),
                _wspec(w1, layer, (d, tf), lambda i, f: (0, f)),
                _wspec(w2, layer, (tf, d), lambda i, f: (f, 0)),
                pl.BlockSpec((1, d), lambda i, f: (0, 0))]
    args = [x, g, w1, w2, final_g]
    if emit:
        assert m == tm, "each weight block must be visited once"
        o_shape += [jax.ShapeDtypeStruct((d, dff), BF16), jax.ShapeDtypeStruct((dff, d), BF16)]
        o_spec += [pl.BlockSpec((d, tf), lambda i, f: (0, f)), pl.BlockSpec((tf, d), lambda i, f: (f, 0))]
    assert not (emit and cast)
    steps = (m // tm) * nf
    slab = lambda i, f: (i * nf + f, 0)
    for mat, mat_layer in cast:
        rows, cols = mat.shape[-2:]
        r = rows // steps
        assert r * steps == rows and r % BF16_SUBLANES == 0
        in_specs.append(_wspec(mat, mat_layer, (r, cols), slab))
        args.append(mat)
        o_shape.append(jax.ShapeDtypeStruct((rows, cols), BF16))
        o_spec.append(pl.BlockSpec((r, cols), slab))
        vmem += 6 * r * cols * 4
    outs = pl.pallas_call(
        functools.partial(_ffn_kernel, final=final, emit=emit, n_cast=len(cast)),
        name=f"ffn_m{m}_l{layer}",
        out_shape=o_shape,
        grid=(m // tm, nf),
        in_specs=in_specs,
        out_specs=o_spec,
        scratch_shapes=[pltpu.VMEM((tm, d), BF16)],
        compiler_params=_cparams(("arbitrary", "arbitrary"), vmem),
    )(*args)
    return outs[0] if len(outs) == 1 else tuple(outs)


def _conv_tail_kernel(x_ref, u_ref, halo_ref, dww_ref, dwb_ref, lng_ref, lnb_ref, wout_ref,
                      o_ref, uext_ref, y_ref, *, n_taps, rows, ln_rows, lanes):
    i = pl.program_id(0)
    tm, c = u_ref.shape
    cur = lax.rem(i, 2)

    @pl.when(i == 0)
    def _():
        y_ref[1] = jnp.zeros((tm, c), BF16)

    uext_ref[0:CONV_HALO, :] = halo_ref[0]
    uext_ref[CONV_HALO:CONV_HALO + tm, :] = u_ref[...]
    off = CONV_HALO - (n_taps - 1)

    def row_chunk(r, carry):
        r0 = pl.multiple_of(r * rows, rows)
        for l0 in range(0, c, lanes):
            win = uext_ref[pl.ds(r0, rows + CONV_HALO), l0:l0 + lanes]
            accs = [jnp.zeros((SUBLANES, lanes), F32)] * (rows // SUBLANES)
            for s in range(SUBLANES):
                ws = win if s == 0 else pltpu.roll(win, rows + CONV_HALO - s, axis=0)
                for a in range((CONV_HALO + SUBLANES) // SUBLANES):
                    k = a * SUBLANES + s - off
                    if 0 <= k < n_taps:
                        wk = dww_ref[k * SUBLANES:(k + 1) * SUBLANES, l0:l0 + lanes]
                        accs = [acc + wk * ws[(a + j) * SUBLANES:(a + j + 1) * SUBLANES]
                                for j, acc in enumerate(accs)]
            for j, acc in enumerate(accs):
                uext_ref[pl.ds(r0 + j * SUBLANES, SUBLANES), l0:l0 + lanes] = acc + dwb_ref[:, l0:l0 + lanes]
        return carry

    @pl.when(i < pl.num_programs(0) - 1)
    def _():
        lax.fori_loop(0, tm // rows, row_chunk, 0)

    o_ref[...] = x_ref[...] + _dot(y_ref[1 - cur], wout_ref[...])
    for r0 in range(0, tm, ln_rows):
        cv = uext_ref[r0:r0 + ln_rows, :]
        mu = jnp.mean(cv, axis=-1, keepdims=True)
        cc = cv - mu
        var = jnp.mean(cc * cc, axis=-1, keepdims=True)
        y = cc * lax.rsqrt(var + LN_EPS) * lng_ref[...] + lnb_ref[...]
        y_ref[cur, r0:r0 + ln_rows, :] = (y * jax.nn.sigmoid(y)).astype(BF16)


def _conv_tail(x, u, halo, dww, dwb, lng, lnb, wout, layer, tm, n_taps):
    m, d = x.shape
    c = u.shape[1]
    rows = _tile(tm, 128)
    ln_rows = _tile(tm, 16)
    lanes = _tile(c, 128)
    nt = m // tm
    prev = lambda i: (jnp.maximum(i - 1, 0), 0)
    this = lambda i: (jnp.minimum(i, nt - 1), 0)
    vmem = 4 * tm * d * 4 + 2 * tm * c * 4 + 2 * c * d * 2 + (tm + CONV_HALO) * c * 4 + 2 * tm * c * 2
    return pl.pallas_call(
        functools.partial(_conv_tail_kernel, n_taps=n_taps, rows=rows, ln_rows=ln_rows, lanes=lanes),
        name=f"conv_tail_m{m}",
        out_shape=jax.ShapeDtypeStruct((m, d), F32),
        grid=(nt + 1,),
        in_specs=[pl.BlockSpec((tm, d), prev),
                  pl.BlockSpec((tm, c), this),
                  pl.BlockSpec((1, CONV_HALO, c), lambda i: (jnp.minimum(i, nt - 1), 0, 0)),
                  pl.BlockSpec(dww.shape, lambda i: (0, 0)),
                  pl.BlockSpec((1, c), lambda i: (0, 0)),
                  pl.BlockSpec((1, c), lambda i: (0, 0)),
                  pl.BlockSpec((1, c), lambda i: (0, 0)),
                  _wspec(wout, layer, (c, d), lambda i: (0, 0))],
        out_specs=pl.BlockSpec((tm, d), prev),
        scratch_shapes=[pltpu.VMEM((tm + CONV_HALO, c), F32), pltpu.VMEM((2, tm, c), BF16)],
        compiler_params=_cparams(("arbitrary",), vmem),
    )(x, u, halo, dww, dwb, lng, lnb, wout)


def _last_rows(past, seq, n):
    if seq.shape[1] >= n:
        return seq[:, seq.shape[1] - n:]
    return jnp.concatenate([past, seq], axis=1)[:, -n:]


def _halos(seq, past, tm, halo_rows):
    b, l, c = seq.shape
    p = past.shape[1]
    first = jnp.concatenate([jnp.zeros((b, halo_rows - p, c), seq.dtype), past], axis=1)[:, None]
    nt = l // tm
    if nt == 1:
        return first.reshape(b, halo_rows, c)
    rest = seq.reshape(b, nt, tm, c)[:, :-1, tm - halo_rows:]
    return jnp.concatenate([first, rest], axis=1).reshape(b * nt, halo_rows, c)


def _lru_tail_kernel(x_ref, ybr_ref, xbr_ref, halo_ref, h0_ref, cw_ref, cb_ref, wa_ref, ba_ref,
                     wx_ref, bx_ref, lam_ref, wout_ref, o_ref, hlast_ref,
                     xext_ref, a_ref, b_ref, hc_ref, *, n_taps, lanes):
    tm, c = xbr_ref.shape
    n_blocks, blk, _ = wa_ref.shape
    t = pl.program_id(1)

    @pl.when(t == 0)
    def _():
        hc_ref[...] = jnp.broadcast_to(h0_ref[0], hc_ref.shape)

    xext_ref[0:LRU_HALO, :] = halo_ref[0]
    xext_ref[LRU_HALO:LRU_HALO + tm, :] = xbr_ref[...]
    off = LRU_HALO - (n_taps - 1)

    sp = jax.nn.softplus(-lam_ref[...])
    for n in range(n_blocks):
        sl = slice(n * blk, (n + 1) * blk)
        xc = jnp.zeros((tm, blk), F32)
        for k in range(n_taps):
            xc = xc + cw_ref[k:k + 1, sl] * xext_ref[off + k:off + k + tm, sl]
        xc = xc + cb_ref[:, sl]
        xcb = xc.astype(BF16)
        r = jax.nn.sigmoid(_dot(xcb, wa_ref[n]) + ba_ref[:, sl])
        gi = jax.nn.sigmoid(_dot(xcb, wx_ref[n]) + bx_ref[:, sl])
        log_a = -LRU_C * r * sp[:, sl]
        a = jnp.exp(log_a)
        a_ref[:, sl] = a
        b_ref[:, sl] = jnp.sqrt(-jnp.tanh(log_a) * (a * a + 1.0)) * (gi * xc)

    row = lax.broadcasted_iota(jnp.int32, (SUBLANES, lanes), 0)

    for l0 in range(0, c, lanes):
        def group(gidx, h):
            r0 = pl.multiple_of(gidx * SUBLANES, SUBLANES)
            av = a_ref[pl.ds(r0, SUBLANES), l0:l0 + lanes]
            bv = b_ref[pl.ds(r0, SUBLANES), l0:l0 + lanes]
            for s in (1, 2, 4):
                a_sh = jnp.where(row >= s, pltpu.roll(av, s, axis=0), 1.0)
                b_sh = jnp.where(row >= s, pltpu.roll(bv, s, axis=0), 0.0)
                bv = av * b_sh + bv
                av = av * a_sh
            hs = av * h + bv
            b_ref[pl.ds(r0, SUBLANES), l0:l0 + lanes] = hs
            return jnp.broadcast_to(hs[SUBLANES - 1:SUBLANES, :], (SUBLANES, lanes))

        h_fin = lax.fori_loop(0, tm // SUBLANES, group, hc_ref[:, l0:l0 + lanes], unroll=2)
        hc_ref[:, l0:l0 + lanes] = h_fin

    gated = (b_ref[...] * ybr_ref[...]).astype(BF16)
    o_ref[...] = x_ref[...] + _dot(gated, wout_ref[...])

    @pl.when(t == pl.num_programs(1) - 1)
    def _():
        hlast_ref[0] = hc_ref[...]


def _lru_tail(x, ybr, xbr, halo, h0, cw, cb, wa, ba, wx, bx, lam, wout, batch, tm, n_taps):
    m, d = x.shape
    c = xbr.shape[1]
    nt = m // batch // tm
    lanes = _tile(c, 2048)
    row2 = lambda b, t: (b * nt + t, 0)
    const2 = lambda b, t: (0, 0)
    vec = pl.BlockSpec((1, c), const2)
    vmem = (4 * tm * d * 4 + 4 * tm * c * 4 + 2 * c * d * 2 + 4 * wa.size * 2
            + (tm + LRU_HALO) * c * 4 + 2 * tm * c * 4)
    return pl.pallas_call(
        functools.partial(_lru_tail_kernel, n_taps=n_taps, lanes=lanes),
        name=f"lru_tail_m{m}",
        out_shape=[jax.ShapeDtypeStruct((m, d), F32),
                   jax.ShapeDtypeStruct((batch, SUBLANES, c), F32)],
        grid=(batch, nt),
        in_specs=[pl.BlockSpec((tm, d), row2),
                  pl.BlockSpec((tm, c), row2),
                  pl.BlockSpec((tm, c), row2),
                  pl.BlockSpec((1, LRU_HALO, c), lambda b, t: (b * nt + t, 0, 0)),
                  pl.BlockSpec((1, 1, c), lambda b, t: (b, 0, 0)),
                  pl.BlockSpec(cw.shape, const2),
                  vec,
                  pl.BlockSpec(wa.shape, lambda b, t: (0, 0, 0)),
                  vec,
                  pl.BlockSpec(wx.shape, lambda b, t: (0, 0, 0)),
                  vec, vec,
                  pl.BlockSpec((c, d), const2)],
        out_specs=[pl.BlockSpec((tm, d), row2),
                   pl.BlockSpec((1, SUBLANES, c), lambda b, t: (b, 0, 0))],
        scratch_shapes=[pltpu.VMEM((tm + LRU_HALO, c), F32), pltpu.VMEM((tm, c), F32),
                        pltpu.VMEM((tm, c), F32), pltpu.VMEM((SUBLANES, c), F32)],
        compiler_params=_cparams(("parallel", "arbitrary"), vmem),
    )(x, ybr, xbr, halo, h0, cw, cb, wa, ba, wx, bx, lam, wout)


def _shifted_rows(vec, n_rows):
    return pltpu.roll(jnp.broadcast_to(vec, (n_rows, vec.shape[1])), 0, 1, stride=1, stride_axis=0)


def _block_softmax(s_parts, v_parts):
    m = s_parts[0].max(axis=-1, keepdims=True)
    for s in s_parts[1:]:
        m = jnp.maximum(m, s.max(axis=-1, keepdims=True))
    o, den = None, None
    for s, v in zip(s_parts, v_parts):
        p = jnp.exp(s - m)
        d = p.sum(axis=-1, keepdims=True)
        pv = _dot(p.astype(BF16), v)
        o, den = (pv, d) if o is None else (o + pv, den + d)
    return o, m, den


def _block_softmax_mxu_sum(s_parts, v_parts, exp=jnp.exp):
    m = s_parts[0].max(axis=-1, keepdims=True)
    for s in s_parts[1:]:
        m = jnp.maximum(m, s.max(axis=-1, keepdims=True))
    acc = None
    for s, v in zip(s_parts, v_parts):
        pv = _dot(exp(s - m).astype(BF16), v)
        acc = pv if acc is None else acc + pv
    dh = acc.shape[1] // 2
    return acc[:, :dh], m, acc[:, dh:]


def _merge(state, o_b, m_b, l_b, exp=jnp.exp):
    if state is None:
        return o_b, m_b, l_b
    acc, m_old, l_old = state
    m_new = jnp.maximum(m_old, m_b)
    a_old = exp(m_old - m_new)
    a_b = exp(m_b - m_new)
    return acc * a_old + o_b * a_b, m_new, l_old * a_old + l_b * a_b


def _attn_prompt_kernel(*refs, heads, dh):
    n_g = len(GROUPS)
    ins = refs[:5 * n_g]
    bvec_ref, o_ref, acc_ref, m_ref, l_ref = refs[5 * n_g:]
    first_chunk = pl.program_id(1) == 0
    h = pl.program_id(2)
    log2e = float(np.log2(np.e))
    scale = dh ** -0.5 * log2e
    qi = lax.broadcasted_iota(jnp.int32, (BLK, 2 * BLK), 0)
    kj = lax.broadcasted_iota(jnp.int32, (BLK, 2 * BLK), 1)
    in_band = (kj - qi >= 0) & (kj - qi <= BLK)
    ones = jnp.ones((BLK, dh), BF16)
    for g, (_, dil) in enumerate(GROUPS):
        q_ref, k_ref, v_ref, kp_ref, vp_ref = ins[5 * g:5 * g + 5]
        bias = jnp.where(in_band, _shifted_rows(bvec_ref[pl.ds(g * heads + h, 1), :] * log2e, BLK), NEG_INF)
        bias_p, bias_c = bias[:, :BLK], bias[:, BLK:]
        for r in range(dil):
            prev_rows = pl.ds(r, BLK, stride=dil)
            k_prev = kp_ref[prev_rows, :].astype(BF16)
            v_prev = jnp.concatenate([vp_ref[prev_rows, :].astype(BF16), ones], axis=1)
            for nb in range(CHUNK // (dil * BLK)):
                rows = pl.ds(r + dil * BLK * nb, BLK, stride=dil)
                q = q_ref[rows, :].astype(BF16)
                k_cur = k_ref[rows, :].astype(BF16)
                v_cur = jnp.concatenate([v_ref[rows, :].astype(BF16), ones], axis=1)
                s_p = _nt_dot(q, k_prev) * scale + bias_p
                if nb == 0:
                    s_p = jnp.where(first_chunk, NEG_INF, s_p)
                s_c = _nt_dot(q, k_cur) * scale + bias_c
                o_b, m_b, l_b = _block_softmax_mxu_sum([s_p, s_c], [v_prev, v_cur], exp=jnp.exp2)
                m_b = jnp.broadcast_to(m_b, (BLK, dh))
                state = None if g == 0 else (acc_ref[rows, :], m_ref[rows, :], l_ref[rows, :])
                acc, m_new, l_new = _merge(state, o_b, m_b, l_b, exp=jnp.exp2)
                if g == n_g - 1:
                    o_ref[rows, :] = acc / l_new
                else:
                    acc_ref[rows, :] = acc
                    m_ref[rows, :] = m_new
                    l_ref[rows, :] = l_new
                k_prev, v_prev = k_cur, v_cur


def _attn_prompt(qkv, bvec, batch, seq, heads, dh):
    n_g = len(GROUPS)
    nc = seq // CHUNK
    col = lambda comp, g: (lambda b, n, h: (b * nc + n, (comp * n_g + g) * heads + h))
    in_specs, args = [], []
    for g, (win, _) in enumerate(GROUPS):
        per = CHUNK // win
        prev = lambda comp, g=g, per=per: (
            lambda b, n, h: (jnp.maximum((b * nc + n) * per - 1, 0), (comp * n_g + g) * heads + h))
        in_specs += [pl.BlockSpec((CHUNK, dh), col(0, g)), pl.BlockSpec((CHUNK, dh), col(1, g)),
                     pl.BlockSpec((CHUNK, dh), col(2, g)),
                     pl.BlockSpec((win, dh), prev(1)), pl.BlockSpec((win, dh), prev(2))]
        args += [qkv] * 5
    in_specs.append(pl.BlockSpec(bvec.shape, lambda b, n, h: (0, 0)))
    vmem = 2 * (9 * CHUNK + 2 * sum(w for w, _ in GROUPS)) * dh * 4 + 2 * CHUNK * dh * 4 + 3 * CHUNK * dh * 4
    return pl.pallas_call(
        functools.partial(_attn_prompt_kernel, heads=heads, dh=dh),
        name="attn_prompt",
        out_shape=jax.ShapeDtypeStruct((batch * seq, heads * dh), F32),
        grid=(batch, nc, heads),
        in_specs=in_specs,
        out_specs=pl.BlockSpec((CHUNK, dh), lambda b, n, h: (b * nc + n, h)),
        scratch_shapes=[pltpu.VMEM((CHUNK, dh), F32)] * 3,
        compiler_params=_cparams(("parallel", "parallel", "parallel"), vmem),
    )(*args, bvec)


def _attn_step_kernel(*refs, heads, dh, n_bufs):
    n_g = len(GROUPS)
    ins = refs[:6 * n_g]
    o_ref, acc_ref, m_ref, l_ref, knp_ref, vnp_ref = refs[6 * n_g:]
    part = pl.program_id(1)
    t = o_ref.shape[1]
    scale = dh ** -0.5
    row_stride = 2 * heads

    def merge_into(hs, o_b, m_b, l_b):
        state = (acc_ref[:, hs], m_ref[:, hs], l_ref[:, hs])
        acc, m_new, l_new = _merge(state, o_b, jnp.broadcast_to(m_b, (t, dh)), jnp.broadcast_to(l_b, (t, dh)))
        acc_ref[:, hs] = acc
        m_ref[:, hs] = m_new
        l_ref[:, hs] = l_new

    def cache_piece(g, first_row):
        dil = GROUPS[g][1]
        q_ref, _, _, cache_ref, yc_ref, _ = ins[6 * g:6 * g + 6]
        rows = cache_ref.shape[1] // row_stride
        tc = lax.broadcasted_iota(jnp.int32, (t, rows), 0)
        cc = lax.broadcasted_iota(jnp.int32, (t, rows), 1) + first_row
        dist_c = n_bufs[g] + tc - cc
        ok_c = ((dist_c & (dil - 1)) == 0) & (dist_c <= (N_DIL_KEYS - 1) * dil)
        for h in range(heads):
            hs = slice(h * dh, (h + 1) * dh)
            q = q_ref[0, :, hs].astype(BF16)
            kc = cache_ref[0, pl.ds(h, rows, stride=row_stride), :].astype(BF16)
            vc = cache_ref[0, pl.ds(heads + h, rows, stride=row_stride), :].astype(BF16)
            bias = _shifted_rows(yc_ref[0, h:h + 1, :], t)[:, BLK:]
            s_c = jnp.where(ok_c, _nt_dot(q, kc) * scale + bias, NEG_INF)
            merge_into(hs, *_block_softmax([s_c], [vc]))

    @pl.when(part == 0)
    def _():
        acc_ref[...] = jnp.zeros_like(acc_ref)
        l_ref[...] = jnp.zeros_like(l_ref)
        m_ref[...] = jnp.full_like(m_ref, NEG_INF)
        tn = lax.broadcasted_iota(jnp.int32, (t, BLK), 0)
        cn = lax.broadcasted_iota(jnp.int32, (t, BLK), 1)
        dist_n = tn - cn
        for g, (_, dil) in enumerate(GROUPS):
            q_ref, kn_ref, vn_ref, _, _, yn_ref = ins[6 * g:6 * g + 6]
            ok_n = (dist_n >= 0) & ((dist_n & (dil - 1)) == 0) & (dist_n <= (N_DIL_KEYS - 1) * dil)
            knp_ref[...] = jnp.zeros_like(knp_ref)
            vnp_ref[...] = jnp.zeros_like(vnp_ref)
            knp_ref[0:t, :] = kn_ref[0]
            vnp_ref[0:t, :] = vn_ref[0]
            for h in range(heads):
                hs = slice(h * dh, (h + 1) * dh)
                q = q_ref[0, :, hs].astype(BF16)
                bias = _shifted_rows(yn_ref[h:h + 1, :], t)[:, BLK:]
                s_n = jnp.where(ok_n, _nt_dot(q, knp_ref[:, hs].astype(BF16)) * scale + bias, NEG_INF)
                merge_into(hs, *_block_softmax([s_n], [vnp_ref[:, hs].astype(BF16)]))
        for g in range(n_g):
            if ins[6 * g + 3].shape[1] == n_bufs[g] * row_stride:
                cache_piece(g, 0)

    for g in range(n_g):
        rows = ins[6 * g + 3].shape[1] // row_stride
        if rows != n_bufs[g]:
            cache_piece(g, part * rows)

    @pl.when(part == pl.num_programs(1) - 1)
    def _():
        o_ref[0] = acc_ref[...] / l_ref[...]


def _attn_step(qkv, caches, rel_bias, batch, t, heads, dh, n_parts=2):
    n_g = len(GROUPS)
    width = heads * dh
    qkv3 = qkv.reshape(batch, t, qkv.shape[1])
    in_specs, args, n_bufs = [], [], []
    vmem = 8 * t * width * 4
    for g in range(n_g):
        n_buf = caches[g].shape[1]
        parts = n_parts if n_buf % (n_parts * BLK) == 0 else 1
        rows = n_buf // parts
        flat = caches[g].reshape(batch, n_buf * 2 * heads, dh)
        yc, yn = _step_bias_vectors(rel_bias, g, heads, n_buf, parts)
        new = lambda comp, g=g: pl.BlockSpec((1, t, width), lambda b, s: (b, 0, comp * n_g + g))
        part_of = (lambda s: s) if parts == n_parts else (lambda s: 0)
        in_specs += [new(0), new(1), new(2),
                     pl.BlockSpec((1, rows * 2 * heads, dh), lambda b, s, f=part_of: (b, f(s), 0)),
                     pl.BlockSpec((1, heads, rows + BLK), lambda b, s, f=part_of: (f(s), 0, 0)),
                     pl.BlockSpec(yn.shape, lambda b, s: (0, 0))]
        args += [qkv3, qkv3, qkv3, flat, yc, yn]
        n_bufs.append(n_buf)
        vmem += 2 * rows * 2 * width * 4 + 6 * rows * dh * 4
    return pl.pallas_call(
        functools.partial(_attn_step_kernel, heads=heads, dh=dh, n_bufs=tuple(n_bufs)),
        name="attn_step",
        out_shape=jax.ShapeDtypeStruct((batch, t, width), F32),
        grid=(batch, n_parts),
        in_specs=in_specs,
        out_specs=pl.BlockSpec((1, t, width), lambda b, s: (b, 0, 0)),
        scratch_shapes=[pltpu.VMEM((t, width), F32)] * 3 + [pltpu.VMEM((BLK, width), F32)] * 2,
        compiler_params=_cparams(("parallel", "arbitrary"), vmem),
    )(*args).reshape(batch * t, width)


def _attn_out_kernel(x_ref, o_ref, wo_ref, out_ref):
    out_ref[...] = x_ref[...] + _dot(o_ref[...].astype(BF16), wo_ref[...])


def _attn_out(x, o, wo, tm_pref=512):
    m, d = x.shape
    width = wo.shape[0]
    tm = _tile(m, tm_pref)
    row = lambda n: pl.BlockSpec((tm, n), lambda i: (i, 0))
    vmem = 4 * tm * d * 4 + 2 * tm * width * 4 + 2 * width * d * 2
    return pl.pallas_call(
        _attn_out_kernel,
        name=f"attn_out_m{m}",
        out_shape=jax.ShapeDtypeStruct((m, d), F32),
        grid=(m // tm,),
        in_specs=[row(d), row(width), pl.BlockSpec((width, d), lambda i: (0, 0))],
        out_specs=row(d),
        compiler_params=_cparams(("parallel",), vmem),
    )(x, o, wo)


def _t5_bucket(dist):
    max_exact = N_BUCKETS // 2
    d = np.asarray(dist, dtype=np.int32)
    large = max_exact + (np.log(np.maximum(d, max_exact) / max_exact) / np.log(MAX_DISTANCE / max_exact)
                         * (N_BUCKETS - max_exact)).astype(np.int32)
    return np.where(d < max_exact, d, np.minimum(large, N_BUCKETS - 1)).astype(np.int32)


def _group_bias(rel_bias, g, heads):
    buckets = _t5_bucket(np.arange(N_DIL_KEYS) * GROUPS[g][1])
    return rel_bias[:, g * heads:(g + 1) * heads].astype(F32)[buckets]


def _prompt_bias_vectors(rel_bias, heads):
    vecs = []
    for g in range(len(GROUPS)):
        rev = _group_bias(rel_bias, g, heads)[::-1]
        vecs.append(jnp.pad(rev, ((0, 2 * BLK - N_DIL_KEYS), (0, 0))).T)
    return jnp.concatenate(vecs, axis=0)


def _step_bias_vectors(rel_bias, g, heads, n_buf, parts):
    rows = n_buf // parts
    bias = _group_bias(rel_bias, g, heads)
    at = lambda dist: bias[np.clip(dist // GROUPS[g][1], 0, N_DIL_KEYS - 1)]
    first = np.arange(parts)[:, None] * rows + np.arange(rows + BLK)[None, :]
    yc = jnp.transpose(at(n_buf + BLK - first), (0, 2, 1))
    yn = at(BLK - np.arange(2 * BLK)).T
    return yc, yn


def _kv_rows_kernel(k_ref, v_ref, o_ref, *, heads, dh):
    rows = k_ref.shape[0]
    o_ref[0, :, 0] = k_ref[...].reshape(rows, heads, dh)
    o_ref[0, :, 1] = v_ref[...].reshape(rows, heads, dh)


def _kv_rows(qkv, g, batch, seq, n_rows, heads, dh):
    n_g = len(GROUPS)
    width = heads * dh
    tm = _tile(n_rows, 256)
    assert seq % tm == 0 and n_rows % tm == 0
    first = (seq - n_rows) // tm
    per_seq = seq // tm
    col = lambda comp: pl.BlockSpec((tm, width), lambda b, i: (b * per_seq + first + i, comp * n_g + g))
    return pl.pallas_call(
        functools.partial(_kv_rows_kernel, heads=heads, dh=dh),
        name=f"kv_rows_g{g}_n{n_rows}",
        out_shape=jax.ShapeDtypeStruct((batch, n_rows, 2, heads, dh), qkv.dtype),
        grid=(batch, n_rows // tm),
        in_specs=[col(1), col(2)],
        out_specs=pl.BlockSpec((1, tm, 2, heads, dh), lambda b, i: (b, i, 0, 0, 0)),
        compiler_params=_cparams(("parallel", "parallel"), 8 * tm * width * 4),
    )(qkv, qkv)


def _trunk(x3, st, p, tiles, bank, emit):
    batch, seq, d = x3.shape
    m = batch * seq
    x = x3.reshape(m, d)
    new = []
    depth = p['ffn_w1'].shape[0]
    for i in range(depth):
        kind = i % N_MIXERS
        g_mix = p['norm_mix_g'][i][None]
        if kind == 0:
            j = i // N_MIXERS
            n_taps, c = p['conv_dw_w'].shape[1:]
            buf = jnp.zeros((batch, n_taps - 1, c), F32) if st is None else st[i][0]
            if ('glu', j) in bank:
                u = _norm_proj(x, g_mix, bank['glu', j], 'glu', tiles['proj_m'])
            else:
                u, *halves = _norm_proj(x, g_mix, p['conv_w_glu'], 'glu', tiles['proj_m'], layer=j, emit=True)
                bank['glu', j] = tuple(halves)
            u3 = u.reshape(batch, seq, c)
            tm = _tile(seq, tiles['conv_m'])
            halo = _halos(u3, buf, tm, CONV_HALO)
            dww = jnp.repeat(p['conv_dw_w'][j], SUBLANES, axis=0)
            w_out = bank['conv_out', j] if ('conv_out', j) in bank else p['conv_w_out'][j].astype(BF16)
            x = _conv_tail(x, u, halo, dww, p['conv_dw_b'][j][None], p['conv_ln_g'][j][None],
                           p['conv_ln_b'][j][None], w_out, j, tm, n_taps)
            new.append((_last_rows(buf, u3, n_taps - 1),))
        elif kind == 1:
            n_taps, c = p['lru_conv_w'].shape
            if st is None:
                h0 = jnp.zeros((batch, c), F32)
                cb = jnp.zeros((batch, n_taps - 1, c), F32)
            else:
                h0, cb = st[i]
            if 'lru_in' in bank:
                ybr, xbr = _norm_proj(x, g_mix, bank['lru_in'], 'split', tiles['proj_m'])
            else:
                ybr, xbr, *halves = _norm_proj(x, g_mix, p['lru_w_in'], 'split', tiles['proj_m'], emit=True)
                bank['lru_in'] = tuple(halves)
            w_out = bank['lru_out'] if 'lru_out' in bank else p['lru_w_out'].astype(BF16)
            xbr3 = xbr.reshape(batch, seq, c)
            tm = _tile(seq, tiles['lru_m'])
            halo = _halos(xbr3, cb, tm, LRU_HALO)
            cw = jnp.pad(p['lru_conv_w'], ((0, SUBLANES - n_taps), (0, 0)))
            x, hl = _lru_tail(x, ybr, xbr, halo, h0[:, None], cw, p['lru_conv_b'][None], p['lru_w_a'],
                              p['lru_b_a'][None], p['lru_w_x'], p['lru_b_x'][None], p['lru_lambda'][None],
                              w_out, batch, tm, n_taps)
            new.append((hl[:, 0], _last_rows(cb, xbr3, n_taps - 1)))
        else:
            rel_bias = p['rel_bias']
            heads = rel_bias.shape[1] // len(GROUPS)
            width = p['att_w_o'].shape[0]
            dh = width // heads
            if 'qkv' in bank:
                qkv = _norm_proj(x, g_mix, bank['qkv'], 'plain', tiles['proj_m'], tn_pref=1024)
            else:
                qkv, bank['qkv'] = _norm_proj(x, g_mix, p['att_w_qkv'], 'plain', tiles['proj_m'], emit=True)
            if st is None:
                assert seq % CHUNK == 0
                o = _attn_prompt(qkv, _prompt_bias_vectors(rel_bias, heads), batch, seq, heads, dh)
                rows = [_kv_rows(qkv, g, batch, seq, min(win, seq), heads, dh) for g, (win, _) in enumerate(GROUPS)]
            else:
                assert seq <= BLK
                o = _attn_step(qkv, st[i], rel_bias, batch, seq, heads, dh)
                rows = [_kv_rows(qkv, g, batch, seq, seq, heads, dh) for g in range(len(GROUPS))]
            x = _attn_out(x, o, p['att_w_o'], tiles['attn_out_m'])
            new.append(rows)
        g_ffn, g_fin, final = p['norm_ffn_g'][i][None], p['final_norm_g'][None], i == depth - 1
        ffn_tiles = dict(tm_pref=tiles['ffn_m'], tf_pref=tiles['ffn_f'])
        jobs = _cast_jobs(p, i + 1) if not emit and i + 1 < depth else []
        if ('ffn_w1', i) not in bank:
            x, bank['ffn_w1', i], bank['ffn_w2', i] = _ffn(x, g_ffn, p['ffn_w1'], p['ffn_w2'], i, g_fin, final,
                                                           emit=True, **ffn_tiles)
        elif jobs and _can_cast(m, p['ffn_w1'].shape[-1], [mat.shape[-2] for _, mat, _ in jobs], **ffn_tiles):
            x, *casts = _ffn(x, g_ffn, bank['ffn_w1', i], bank['ffn_w2', i], i, g_fin, final, **ffn_tiles,
                             cast=[(mat, mat_layer) for _, mat, mat_layer in jobs])
            bank.update({key: cast for (key, _, _), cast in zip(jobs, casts)})
        else:
            x = _ffn(x, g_ffn, bank['ffn_w1', i], bank['ffn_w2', i], i, g_fin, final, **ffn_tiles)
        yield
    return x.reshape(batch, seq, d), new


def _cast_jobs(p, layer):
    jobs = [(('ffn_w1', layer), p['ffn_w1'], layer), (('ffn_w2', layer), p['ffn_w2'], layer)]
    kind = layer % N_MIXERS
    if kind == 0:
        j = layer // N_MIXERS
        jobs += [(('glu', j), p['conv_w_glu'], j), (('conv_out', j), p['conv_w_out'], j)]
    elif kind == 1:
        jobs += [('lru_in', p['lru_w_in'], None), ('lru_out', p['lru_w_out'], None)]
    else:
        jobs += [('qkv', p['att_w_qkv'], None)]
    return jobs


def _run_alternating(*gens):
    results = [None] * len(gens)
    live = list(range(len(gens)))
    while live:
        for k in list(live):
            try:
                next(gens[k])
            except StopIteration as done:
                results[k] = done.value
                live.remove(k)
    return results


PROMPT_TILES = dict(proj_m=1024, conv_m=256, lru_m=256, attn_out_m=512, ffn_m=512, ffn_f=1024)
SAMPLE_TILES = dict(proj_m=512, conv_m=256, lru_m=256, attn_out_m=512, ffn_m=512, ffn_f=512)
MATMUL_WEIGHTS = ('lru_w_a', 'lru_w_x', 'att_w_o')


def kernel(x_prompt, x_sample, state_conv_l0, state_lru_h_l1, state_lru_conv_l1, cache_kv_w128_l2, cache_kv_w512_l2, cache_kv_w2048_l2, state_conv_l3, norm_mix_g, norm_ffn_g, final_norm_g, ffn_w1, ffn_w2, conv_w_glu, conv_dw_w, conv_dw_b, conv_ln_g, conv_ln_b, conv_w_out, lru_w_in, lru_conv_w, lru_conv_b, lru_w_a, lru_b_a, lru_w_x, lru_b_x, lru_lambda, lru_w_out, att_w_qkv, att_w_o, rel_bias):
    p = dict(norm_mix_g=norm_mix_g, norm_ffn_g=norm_ffn_g, final_norm_g=final_norm_g, ffn_w1=ffn_w1,
             ffn_w2=ffn_w2, conv_w_glu=conv_w_glu, conv_dw_w=conv_dw_w, conv_dw_b=conv_dw_b,
             conv_ln_g=conv_ln_g, conv_ln_b=conv_ln_b, conv_w_out=conv_w_out, lru_w_in=lru_w_in,
             lru_conv_w=lru_conv_w, lru_conv_b=lru_conv_b, lru_w_a=lru_w_a, lru_b_a=lru_b_a, lru_w_x=lru_w_x,
             lru_b_x=lru_b_x, lru_lambda=lru_lambda, lru_w_out=lru_w_out, att_w_qkv=att_w_qkv,
             att_w_o=att_w_o, rel_bias=rel_bias)
    for name in MATMUL_WEIGHTS:
        p[name] = p[name].astype(BF16)
    bank = {}
    sample = _trunk(x_sample, [(state_conv_l0,), (state_lru_h_l1, state_lru_conv_l1),
                               (cache_kv_w128_l2, cache_kv_w512_l2, cache_kv_w2048_l2),
                               (state_conv_l3,)], p, SAMPLE_TILES, bank, emit=True)
    prompt = _trunk(x_prompt, None, p, PROMPT_TILES, bank, emit=False)
    (y_sample, sst), (y_prompt, pst) = _run_alternating(sample, prompt)
    return (y_prompt, y_sample,
            pst[0][0], sst[0][0],
            pst[1][0], sst[1][0],
            pst[1][1], sst[1][1],
            pst[2][0], sst[2][0],
            pst[2][1], sst[2][1],
            pst[2][2], sst[2][2],
            pst[3][0], sst[3][0])
```

```python
import functools

import numpy as np
import jax
import jax.numpy as jnp
from jax import lax
from jax.experimental import pallas as pl
from jax.experimental.pallas import tpu as pltpu

RMS_EPS = 1e-6
LN_EPS = 1e-5
LRU_C = 8.0
NEG_INF = -1e30
N_MIXERS = 3
GROUPS = ((128, 1), (512, 4), (2048, 16))
N_DIL_KEYS = 129
BLK = 128
CHUNK = BLK * max(d for _, d in GROUPS)
N_BUCKETS = 32
MAX_DISTANCE = 2048

V7X_VMEM_BYTES = 64 * 1024 * 1024
MIN_VMEM_LIMIT_BYTES = 32 * 1024 * 1024
VMEM_RESERVE_BYTES = 8 * 1024 * 1024
SUBLANES = 8
BF16_SUBLANES = 16
CONV_HALO = 32
LRU_HALO = 8

BF16 = jnp.bfloat16
F32 = jnp.float32


def _cparams(semantics, vmem_bytes):
    limit = int(min(max(vmem_bytes * 3 // 2, MIN_VMEM_LIMIT_BYTES), V7X_VMEM_BYTES - VMEM_RESERVE_BYTES))
    return pltpu.CompilerParams(dimension_semantics=semantics, vmem_limit_bytes=limit)


def _tile(n, pref):
    if n <= pref:
        return n
    t = pref
    while n % t:
        t //= 2
    return t


def _rms_to_bf16(x, g):
    ms = jnp.mean(x * x, axis=-1, keepdims=True)
    return (x * lax.rsqrt(ms + RMS_EPS) * g).astype(BF16)


def _dot(a, b):
    return jnp.dot(a, b, preferred_element_type=F32)


def _nt_dot(a, b):
    return lax.dot_general(a, b, (((1,), (1,)), ((), ())), preferred_element_type=F32)


def _norm_proj_kernel(x_ref, g_ref, *refs, mode, emit):
    n_w = 1 if mode == 'plain' else 2
    n_o = 2 if mode == 'split' else 1
    w_refs, o_refs = refs[:n_w], refs[n_w:n_w + n_o]
    wb_refs, h_ref = refs[n_w + n_o:-1], refs[-1]

    @pl.when(pl.program_id(1) == 0)
    def _():
        h_ref[...] = _rms_to_bf16(x_ref[...], g_ref[...])

    ws = [w_ref[...].astype(BF16) for w_ref in w_refs]
    if emit:
        for wb_ref, w in zip(wb_refs, ws):
            wb_ref[...] = w
    h = h_ref[...]
    if mode == 'plain':
        o_refs[0][...] = _dot(h, ws[0])
    elif mode == 'glu':
        o_refs[0][...] = _dot(h, ws[0]) * jax.nn.sigmoid(_dot(h, ws[1]))
    else:
        o_refs[0][...] = jax.nn.gelu(_dot(h, ws[0]), approximate=True)
        o_refs[1][...] = _dot(h, ws[1])


def _wspec(w, layer, block, index_map):
    if w.ndim == 2:
        return pl.BlockSpec(block, index_map)
    return pl.BlockSpec((None,) + block, lambda *ids: (layer,) + index_map(*ids))


def _norm_proj(x, g, w, mode, tm_pref=512, tn_pref=512, layer=0, emit=False):
    m, d = x.shape
    halves = mode != 'plain'
    pair = isinstance(w, tuple)
    n_out = w[0].shape[-1] if pair else (w.shape[-1] // 2 if halves else w.shape[-1])
    tm = _tile(m, tm_pref)
    tn = _tile(n_out, tn_pref)
    nj = n_out // tn
    w_args = list(w) if pair else ([w, w] if halves else [w])
    shift = [0, 0] if pair else [0, nj]
    in_specs = [pl.BlockSpec((tm, d), lambda i, j: (i, 0)), pl.BlockSpec((1, d), lambda i, j: (0, 0))]
    in_specs += [_wspec(wk, layer, (d, tn), lambda i, j, s=s: (0, j + s)) for wk, s in zip(w_args, shift)]
    n_o = 2 if mode == 'split' else 1
    o_shapes = [jax.ShapeDtypeStruct((m, n_out), F32)] * n_o
    o_specs = [pl.BlockSpec((tm, tn), lambda i, j: (i, j))] * n_o
    if emit:
        assert m == tm, "each weight block must be visited once"
        o_shapes += [jax.ShapeDtypeStruct((d, n_out), BF16)] * len(w_args)
        o_specs += [pl.BlockSpec((d, tn), lambda i, j: (0, j))] * len(w_args)
    wbytes = w_args[0].dtype.itemsize
    vmem = (2 * tm * d * 4 + tm * d * 2 + 2 * len(w_args) * d * tn * wbytes + 2 * n_o * tm * tn * 4
            + (3 * len(w_args) * d * tn * 2 if emit else 0))
    outs = pl.pallas_call(
        functools.partial(_norm_proj_kernel, mode=mode, emit=emit),
        name=f"norm_proj_{mode}_m{m}",
        out_shape=o_shapes,
        grid=(m // tm, nj),
        in_specs=in_specs,
        out_specs=o_specs,
        scratch_shapes=[pltpu.VMEM((tm, d), BF16)],
        compiler_params=_cparams(("parallel", "arbitrary"), vmem),
    )(x, g, *w_args)
    return outs[0] if len(outs) == 1 else tuple(outs)


def _ffn_kernel(x_ref, g_ref, w1_ref, w2_ref, fg_ref, *rest, final, emit, n_cast):
    if n_cast:
        o_ref, h_ref = rest[n_cast], rest[-1]
        for src_ref, dst_ref in zip(rest[:n_cast], rest[n_cast + 1:-1]):
            dst_ref[...] = src_ref[...].astype(BF16)
    elif emit:
        o_ref, w1b_ref, w2b_ref, h_ref = rest
    else:
        o_ref, h_ref = rest
    f = pl.program_id(1)

    @pl.when(f == 0)
    def _():
        x = x_ref[...]
        h_ref[...] = _rms_to_bf16(x, g_ref[...])
        o_ref[...] = x

    w1 = w1_ref[...].astype(BF16)
    w2 = w2_ref[...].astype(BF16)
    if emit:
        w1b_ref[...] = w1
        w2b_ref[...] = w2
    hid = jnp.square(jnp.maximum(_dot(h_ref[...], w1), 0.0)).astype(BF16)
    o_ref[...] += _dot(hid, w2)

    if final:
        @pl.when(f == pl.num_programs(1) - 1)
        def _():
            y = o_ref[...]
            ms = jnp.mean(y * y, axis=-1, keepdims=True)
            o_ref[...] = y * lax.rsqrt(ms + RMS_EPS) * fg_ref[...]


def _can_cast(m, dff, row_counts, tm_pref, tf_pref):
    steps = (m // _tile(m, tm_pref)) * (dff // _tile(dff, tf_pref))
    return all(rows % (BF16_SUBLANES * steps) == 0 for rows in row_counts)


def _ffn(x, g, w1, w2, layer, final_g, final, tm_pref=512, tf_pref=1024, emit=False, cast=()):
    m, d = x.shape
    dff = w1.shape[-1]
    tm = _tile(m, tm_pref)
    tf = _tile(dff, tf_pref)
    nf = dff // tf
    wbytes = w1.dtype.itemsize
    vmem = (2 * tm * d * 4 + tm * d * 2 + 2 * 2 * d * tf * wbytes + 2 * tm * d * 4 + 2 * tm * tf * 4
            + (6 * d * tf * 2 if emit else 0))
    o_shape = [jax.ShapeDtypeStruct((m, d), F32)]
    o_spec = [pl.BlockSpec((tm, d), lambda i, f: (i, 0))]
    in_specs = [pl.BlockSpec((tm, d), lambda i, f: (i, 0)),
                pl.BlockSpec((1, d), lambda i, f: (0, 0)),
                _wspec(w1, layer, (d, tf), lambda i, f: (0, f)),
                _wspec(w2, layer, (tf, d), lambda i, f: (f, 0)),
                pl.BlockSpec((1, d), lambda i, f: (0, 0))]
    args = [x, g, w1, w2, final_g]
    if emit:
        assert m == tm, "each weight block must be visited once"
        o_shape += [jax.ShapeDtypeStruct((d, dff), BF16), jax.ShapeDtypeStruct((dff, d), BF16)]
        o_spec += [pl.BlockSpec((d, tf), lambda i, f: (0, f)), pl.BlockSpec((tf, d), lambda i, f: (f, 0))]
    assert not (emit and cast)
    steps = (m // tm) * nf
    slab = lambda i, f: (i * nf + f, 0)
    for mat, mat_layer in cast:
        rows, cols = mat.shape[-2:]
        r = rows // steps
        assert r * steps == rows and r % BF16_SUBLANES == 0
        in_specs.append(_wspec(mat, mat_layer, (r, cols), slab))
        args.append(mat)
        o_shape.append(jax.ShapeDtypeStruct((rows, cols), BF16))
        o_spec.append(pl.BlockSpec((r, cols), slab))
        vmem += 6 * r * cols * 4
    outs = pl.pallas_call(
        functools.partial(_ffn_kernel, final=final, emit=emit, n_cast=len(cast)),
        name=f"ffn_m{m}_l{layer}",
        out_shape=o_shape,
        grid=(m // tm, nf),
        in_specs=in_specs,
        out_specs=o_spec,
        scratch_shapes=[pltpu.VMEM((tm, d), BF16)],
        compiler_params=_cparams(("arbitrary", "arbitrary"), vmem),
    )(*args)
    return outs[0] if len(outs) == 1 else tuple(outs)


def _conv_tail_kernel(x_ref, u_ref, halo_ref, dww_ref, dwb_ref, lng_ref, lnb_ref, wout_ref,
                      o_ref, uext_ref, y_ref, *, n_taps, rows, ln_rows, lanes):
    i = pl.program_id(0)
    tm, c = u_ref.shape
    cur = lax.rem(i, 2)

    @pl.when(i == 0)
    def _():
        y_ref[1] = jnp.zeros((tm, c), BF16)

    uext_ref[0:CONV_HALO, :] = halo_ref[0]
    uext_ref[CONV_HALO:CONV_HALO + tm, :] = u_ref[...]
    off = CONV_HALO - (n_taps - 1)

    def row_chunk(r, carry):
        r0 = pl.multiple_of(r * rows, rows)
        for l0 in range(0, c, lanes):
            win = uext_ref[pl.ds(r0, rows + CONV_HALO), l0:l0 + lanes]
            accs = [jnp.zeros((SUBLANES, lanes), F32)] * (rows // SUBLANES)
            for s in range(SUBLANES):
                ws = win if s == 0 else pltpu.roll(win, rows + CONV_HALO - s, axis=0)
                for a in range((CONV_HALO + SUBLANES) // SUBLANES):
                    k = a * SUBLANES + s - off
                    if 0 <= k < n_taps:
                        wk = dww_ref[k * SUBLANES:(k + 1) * SUBLANES, l0:l0 + lanes]
                        accs = [acc + wk * ws[(a + j) * SUBLANES:(a + j + 1) * SUBLANES]
                                for j, acc in enumerate(accs)]
            for j, acc in enumerate(accs):
                uext_ref[pl.ds(r0 + j * SUBLANES, SUBLANES), l0:l0 + lanes] = acc + dwb_ref[:, l0:l0 + lanes]
        return carry

    @pl.when(i < pl.num_programs(0) - 1)
    def _():
        lax.fori_loop(0, tm // rows, row_chunk, 0)

    o_ref[...] = x_ref[...] + _dot(y_ref[1 - cur], wout_ref[...])
    for r0 in range(0, tm, ln_rows):
        cv = uext_ref[r0:r0 + ln_rows, :]
        mu = jnp.mean(cv, axis=-1, keepdims=True)
        cc = cv - mu
        var = jnp.mean(cc * cc, axis=-1, keepdims=True)
        y = cc * lax.rsqrt(var + LN_EPS) * lng_ref[...] + lnb_ref[...]
        y_ref[cur, r0:r0 + ln_rows, :] = (y * jax.nn.sigmoid(y)).astype(BF16)


def _conv_tail(x, u, halo, dww, dwb, lng, lnb, wout, layer, tm, n_taps):
    m, d = x.shape
    c = u.shape[1]
    rows = _tile(tm, 128)
    ln_rows = _tile(tm, 16)
    lanes = _tile(c, 128)
    nt = m // tm
    prev = lambda i: (jnp.maximum(i - 1, 0), 0)
    this = lambda i: (jnp.minimum(i, nt - 1), 0)
    vmem = 4 * tm * d * 4 + 2 * tm * c * 4 + 2 * c * d * 2 + (tm + CONV_HALO) * c * 4 + 2 * tm * c * 2
    return pl.pallas_call(
        functools.partial(_conv_tail_kernel, n_taps=n_taps, rows=rows, ln_rows=ln_rows, lanes=lanes),
        name=f"conv_tail_m{m}",
        out_shape=jax.ShapeDtypeStruct((m, d), F32),
        grid=(nt + 1,),
        in_specs=[pl.BlockSpec((tm, d), prev),
                  pl.BlockSpec((tm, c), this),
                  pl.BlockSpec((1, CONV_HALO, c), lambda i: (jnp.minimum(i, nt - 1), 0, 0)),
                  pl.BlockSpec(dww.shape, lambda i: (0, 0)),
                  pl.BlockSpec((1, c), lambda i: (0, 0)),
                  pl.BlockSpec((1, c), lambda i: (0, 0)),
                  pl.BlockSpec((1, c), lambda i: (0, 0)),
                  _wspec(wout, layer, (c, d), lambda i: (0, 0))],
        out_specs=pl.BlockSpec((tm, d), prev),
        scratch_shapes=[pltpu.VMEM((tm + CONV_HALO, c), F32), pltpu.VMEM((2, tm, c), BF16)],
        compiler_params=_cparams(("arbitrary",), vmem),
    )(x, u, halo, dww, dwb, lng, lnb, wout)


def _last_rows(past, seq, n):
    if seq.shape[1] >= n:
        return seq[:, seq.shape[1] - n:]
    return jnp.concatenate([past, seq], axis=1)[:, -n:]


def _halos(seq, past, tm, halo_rows):
    b, l, c = seq.shape
    p = past.shape[1]
    first = jnp.concatenate([jnp.zeros((b, halo_rows - p, c), seq.dtype), past], axis=1)[:, None]
    nt = l // tm
    if nt == 1:
        return first.reshape(b, halo_rows, c)
    rest = seq.reshape(b, nt, tm, c)[:, :-1, tm - halo_rows:]
    return jnp.concatenate([first, rest], axis=1).reshape(b * nt, halo_rows, c)


def _lru_tail_kernel(x_ref, ybr_ref, xbr_ref, halo_ref, h0_ref, cw_ref, cb_ref, wa_ref, ba_ref,
                     wx_ref, bx_ref, lam_ref, wout_ref, o_ref, hlast_ref,
                     xext_ref, a_ref, b_ref, hc_ref, *, n_taps, lanes):
    tm, c = xbr_ref.shape
    n_blocks, blk, _ = wa_ref.shape
    t = pl.program_id(1)

    @pl.when(t == 0)
    def _():
        hc_ref[...] = jnp.broadcast_to(h0_ref[0], hc_ref.shape)

    xext_ref[0:LRU_HALO, :] = halo_ref[0]
    xext_ref[LRU_HALO:LRU_HALO + tm, :] = xbr_ref[...]
    off = LRU_HALO - (n_taps - 1)

    sp = jax.nn.softplus(-lam_ref[...])
    for n in range(n_blocks):
        sl = slice(n * blk, (n + 1) * blk)
        win = xext_ref[:, sl]
        xc = cb_ref[:, sl]
        for k in range(n_taps):
            lead, shift = divmod(off + k, SUBLANES)
            ws = win if shift == 0 else pltpu.roll(win, tm + LRU_HALO - shift, axis=0)
            xc = xc + cw_ref[k:k + 1, sl] * ws[lead * SUBLANES:lead * SUBLANES + tm]
        xcb = xc.astype(BF16)
        r = jax.nn.sigmoid(_dot(xcb, wa_ref[n]) + ba_ref[:, sl])
        gi = jax.nn.sigmoid(_dot(xcb, wx_ref[n]) + bx_ref[:, sl])
        log_a = -LRU_C * r * sp[:, sl]
        a = jnp.exp(log_a)
        a_ref[:, sl] = a
        b_ref[:, sl] = jnp.sqrt(-jnp.tanh(log_a) * (a * a + 1.0)) * (gi * xc)

    row = lax.broadcasted_iota(jnp.int32, (SUBLANES, lanes), 0)

    for l0 in range(0, c, lanes):
        def group(gidx, h):
            r0 = pl.multiple_of(gidx * SUBLANES, SUBLANES)
            av = a_ref[pl.ds(r0, SUBLANES), l0:l0 + lanes]
            bv = b_ref[pl.ds(r0, SUBLANES), l0:l0 + lanes]
            for s in (1, 2, 4):
                a_sh = jnp.where(row >= s, pltpu.roll(av, s, axis=0), 1.0)
                b_sh = jnp.where(row >= s, pltpu.roll(bv, s, axis=0), 0.0)
                bv = av * b_sh + bv
                av = av * a_sh
            hs = av * h + bv
            b_ref[pl.ds(r0, SUBLANES), l0:l0 + lanes] = hs
            return jnp.broadcast_to(hs[SUBLANES - 1:SUBLANES, :], (SUBLANES, lanes))

        h_fin = lax.fori_loop(0, tm // SUBLANES, group, hc_ref[:, l0:l0 + lanes], unroll=2)
        hc_ref[:, l0:l0 + lanes] = h_fin

    gated = (b_ref[...] * ybr_ref[...]).astype(BF16)
    o_ref[...] = x_ref[...] + _dot(gated, wout_ref[...])

    @pl.when(t == pl.num_programs(1) - 1)
    def _():
        hlast_ref[0] = hc_ref[...]


def _lru_tail(x, ybr, xbr, halo, h0, cw, cb, wa, ba, wx, bx, lam, wout, batch, tm, n_taps):
    m, d = x.shape
    c = xbr.shape[1]
    nt = m // batch // tm
    lanes = _tile(c, 2048)
    row2 = lambda b, t: (b * nt + t, 0)
    const2 = lambda b, t: (0, 0)
    vec = pl.BlockSpec((1, c), const2)
    vmem = (4 * tm * d * 4 + 4 * tm * c * 4 + 2 * c * d * 2 + 4 * wa.size * 2
            + (tm + LRU_HALO) * c * 4 + 2 * tm * c * 4)
    return pl.pallas_call(
        functools.partial(_lru_tail_kernel, n_taps=n_taps, lanes=lanes),
        name=f"lru_tail_m{m}",
        out_shape=[jax.ShapeDtypeStruct((m, d), F32),
                   jax.ShapeDtypeStruct((batch, SUBLANES, c), F32)],
        grid=(batch, nt),
        in_specs=[pl.BlockSpec((tm, d), row2),
                  pl.BlockSpec((tm, c), row2),
                  pl.BlockSpec((tm, c), row2),
                  pl.BlockSpec((1, LRU_HALO, c), lambda b, t: (b * nt + t, 0, 0)),
                  pl.BlockSpec((1, 1, c), lambda b, t: (b, 0, 0)),
                  pl.BlockSpec(cw.shape, const2),
                  vec,
                  pl.BlockSpec(wa.shape, lambda b, t: (0, 0, 0)),
                  vec,
                  pl.BlockSpec(wx.shape, lambda b, t: (0, 0, 0)),
                  vec, vec,
                  pl.BlockSpec((c, d), const2)],
        out_specs=[pl.BlockSpec((tm, d), row2),
                   pl.BlockSpec((1, SUBLANES, c), lambda b, t: (b, 0, 0))],
        scratch_shapes=[pltpu.VMEM((tm + LRU_HALO, c), F32), pltpu.VMEM((tm, c), F32),
                        pltpu.VMEM((tm, c), F32), pltpu.VMEM((SUBLANES, c), F32)],
        compiler_params=_cparams(("parallel", "arbitrary"), vmem),
    )(x, ybr, xbr, halo, h0, cw, cb, wa, ba, wx, bx, lam, wout)


def _shifted_rows(vec, n_rows):
    return pltpu.roll(jnp.broadcast_to(vec, (n_rows, vec.shape[1])), 0, 1, stride=1, stride_axis=0)


def _block_softmax(s_parts, v_parts):
    m = s_parts[0].max(axis=-1, keepdims=True)
    for s in s_parts[1:]:
        m = jnp.maximum(m, s.max(axis=-1, keepdims=True))
    o, den = None, None
    for s, v in zip(s_parts, v_parts):
        p = jnp.exp(s - m)
        d = p.sum(axis=-1, keepdims=True)
        pv = _dot(p.astype(BF16), v)
        o, den = (pv, d) if o is None else (o + pv, den + d)
    return o, m, den


def _block_softmax_mxu_sum(s_parts, v_parts, exp=jnp.exp):
    m = s_parts[0].max(axis=-1, keepdims=True)
    for s in s_parts[1:]:
        m = jnp.maximum(m, s.max(axis=-1, keepdims=True))
    acc = None
    for s, v in zip(s_parts, v_parts):
        pv = _dot(exp(s - m).astype(BF16), v)
        acc = pv if acc is None else acc + pv
    dh = acc.shape[1] // 2
    return acc[:, :dh], m, acc[:, dh:]


def _merge(state, o_b, m_b, l_b, exp=jnp.exp):
    if state is None:
        return o_b, m_b, l_b
    acc, m_old, l_old = state
    m_new = jnp.maximum(m_old, m_b)
    a_old = exp(m_old - m_new)
    a_b = exp(m_b - m_new)
    return acc * a_old + o_b * a_b, m_new, l_old * a_old + l_b * a_b


def _attn_prompt_kernel(*refs, heads, dh):
    n_g = len(GROUPS)
    ins = refs[:5 * n_g]
    bvec_ref, o_ref, acc_ref, m_ref, l_ref = refs[5 * n_g:]
    first_chunk = pl.program_id(1) == 0
    h = pl.program_id(2)
    log2e = float(np.log2(np.e))
    scale = dh ** -0.5 * log2e
    qi = lax.broadcasted_iota(jnp.int32, (BLK, 2 * BLK), 0)
    kj = lax.broadcasted_iota(jnp.int32, (BLK, 2 * BLK), 1)
    in_band = (kj - qi >= 0) & (kj - qi <= BLK)
    ones = jnp.ones((BLK, dh), BF16)
    for g, (_, dil) in enumerate(GROUPS):
        q_ref, k_ref, v_ref, kp_ref, vp_ref = ins[5 * g:5 * g + 5]
        bias = jnp.where(in_band, _shifted_rows(bvec_ref[pl.ds(g * heads + h, 1), :] * log2e, BLK), NEG_INF)
        bias_p, bias_c = bias[:, :BLK], bias[:, BLK:]
        for r in range(dil):
            prev_rows = pl.ds(r, BLK, stride=dil)
            k_prev = kp_ref[prev_rows, :].astype(BF16)
            v_prev = jnp.concatenate([vp_ref[prev_rows, :].astype(BF16), ones], axis=1)
            for nb in range(CHUNK // (dil * BLK)):
                rows = pl.ds(r + dil * BLK * nb, BLK, stride=dil)
                q = q_ref[rows, :].astype(BF16)
                k_cur = k_ref[rows, :].astype(BF16)
                v_cur = jnp.concatenate([v_ref[rows, :].astype(BF16), ones], axis=1)
                s_p = _nt_dot(q, k_prev) * scale + bias_p
                if nb == 0:
                    s_p = jnp.where(first_chunk, NEG_INF, s_p)
                s_c = _nt_dot(q, k_cur) * scale + bias_c
                o_b, m_b, l_b = _block_softmax_mxu_sum([s_p, s_c], [v_prev, v_cur], exp=jnp.exp2)
                m_b = jnp.broadcast_to(m_b, (BLK, dh))
                state = None if g == 0 else (acc_ref[rows, :], m_ref[rows, :], l_ref[rows, :])
                acc, m_new, l_new = _merge(state, o_b, m_b, l_b, exp=jnp.exp2)
                if g == n_g - 1:
                    o_ref[rows, :] = acc / l_new
                else:
                    acc_ref[rows, :] = acc
                    m_ref[rows, :] = m_new
                    l_ref[rows, :] = l_new
                k_prev, v_prev = k_cur, v_cur


def _attn_prompt(qkv, bvec, batch, seq, heads, dh):
    n_g = len(GROUPS)
    nc = seq // CHUNK
    col = lambda comp, g: (lambda b, n, h: (b * nc + n, (comp * n_g + g) * heads + h))
    in_specs, args = [], []
    for g, (win, _) in enumerate(GROUPS):
        per = CHUNK // win
        prev = lambda comp, g=g, per=per: (
            lambda b, n, h: (jnp.maximum((b * nc + n) * per - 1, 0), (comp * n_g + g) * heads + h))
        in_specs += [pl.BlockSpec((CHUNK, dh), col(0, g)), pl.BlockSpec((CHUNK, dh), col(1, g)),
                     pl.BlockSpec((CHUNK, dh), col(2, g)),
                     pl.BlockSpec((win, dh), prev(1)), pl.BlockSpec((win, dh), prev(2))]
        args += [qkv] * 5
    in_specs.append(pl.BlockSpec(bvec.shape, lambda b, n, h: (0, 0)))
    vmem = 2 * (9 * CHUNK + 2 * sum(w for w, _ in GROUPS)) * dh * 4 + 2 * CHUNK * dh * 4 + 3 * CHUNK * dh * 4
    return pl.pallas_call(
        functools.partial(_attn_prompt_kernel, heads=heads, dh=dh),
        name="attn_prompt",
        out_shape=jax.ShapeDtypeStruct((batch * seq, heads * dh), F32),
        grid=(batch, nc, heads),
        in_specs=in_specs,
        out_specs=pl.BlockSpec((CHUNK, dh), lambda b, n, h: (b * nc + n, h)),
        scratch_shapes=[pltpu.VMEM((CHUNK, dh), F32)] * 3,
        compiler_params=_cparams(("parallel", "parallel", "parallel"), vmem),
    )(*args, bvec)


def _attn_step_kernel(*refs, heads, dh, n_bufs):
    n_g = len(GROUPS)
    ins = refs[:6 * n_g]
    o_ref, acc_ref, m_ref, l_ref, knp_ref, vnp_ref = refs[6 * n_g:]
    part = pl.program_id(1)
    t = o_ref.shape[1]
    scale = dh ** -0.5
    row_stride = 2 * heads

    def merge_into(hs, o_b, m_b, l_b):
        state = (acc_ref[:, hs], m_ref[:, hs], l_ref[:, hs])
        acc, m_new, l_new = _merge(state, o_b, jnp.broadcast_to(m_b, (t, dh)), jnp.broadcast_to(l_b, (t, dh)))
        acc_ref[:, hs] = acc
        m_ref[:, hs] = m_new
        l_ref[:, hs] = l_new

    def cache_piece(g, first_row):
        dil = GROUPS[g][1]
        q_ref, _, _, cache_ref, yc_ref, _ = ins[6 * g:6 * g + 6]
        rows = cache_ref.shape[1] // row_stride
        tc = lax.broadcasted_iota(jnp.int32, (t, rows), 0)
        cc = lax.broadcasted_iota(jnp.int32, (t, rows), 1) + first_row
        dist_c = n_bufs[g] + tc - cc
        ok_c = ((dist_c & (dil - 1)) == 0) & (dist_c <= (N_DIL_KEYS - 1) * dil)
        for h in range(heads):
            hs = slice(h * dh, (h + 1) * dh)
            q = q_ref[0, :, hs].astype(BF16)
            kc = cache_ref[0, pl.ds(h, rows, stride=row_stride), :].astype(BF16)
            vc = cache_ref[0, pl.ds(heads + h, rows, stride=row_stride), :].astype(BF16)
            bias = _shifted_rows(yc_ref[0, h:h + 1, :], t)[:, BLK:]
            s_c = jnp.where(ok_c, _nt_dot(q, kc) * scale + bias, NEG_INF)
            merge_into(hs, *_block_softmax([s_c], [vc]))

    @pl.when(part == 0)
    def _():
        acc_ref[...] = jnp.zeros_like(acc_ref)
        l_ref[...] = jnp.zeros_like(l_ref)
        m_ref[...] = jnp.full_like(m_ref, NEG_INF)
        tn = lax.broadcasted_iota(jnp.int32, (t, BLK), 0)
        cn = lax.broadcasted_iota(jnp.int32, (t, BLK), 1)
        dist_n = tn - cn
        for g, (_, dil) in enumerate(GROUPS):
            q_ref, kn_ref, vn_ref, _, _, yn_ref = ins[6 * g:6 * g + 6]
            ok_n = (dist_n >= 0) & ((dist_n & (dil - 1)) == 0) & (dist_n <= (N_DIL_KEYS - 1) * dil)
            knp_ref[...] = jnp.zeros_like(knp_ref)
            vnp_ref[...] = jnp.zeros_like(vnp_ref)
            knp_ref[0:t, :] = kn_ref[0]
            vnp_ref[0:t, :] = vn_ref[0]
            for h in range(heads):
                hs = slice(h * dh, (h + 1) * dh)
                q = q_ref[0, :, hs].astype(BF16)
                bias = _shifted_rows(yn_ref[h:h + 1, :], t)[:, BLK:]
                s_n = jnp.where(ok_n, _nt_dot(q, knp_ref[:, hs].astype(BF16)) * scale + bias, NEG_INF)
                merge_into(hs, *_block_softmax([s_n], [vnp_ref[:, hs].astype(BF16)]))
        for g in range(n_g):
            if ins[6 * g + 3].shape[1] == n_bufs[g] * row_stride:
                cache_piece(g, 0)

    for g in range(n_g):
        rows = ins[6 * g + 3].shape[1] // row_stride
        if rows != n_bufs[g]:
            cache_piece(g, part * rows)

    @pl.when(part == pl.num_programs(1) - 1)
    def _():
        o_ref[0] = acc_ref[...] / l_ref[...]


def _attn_step(qkv, caches, rel_bias, batch, t, heads, dh, n_parts=2):
    n_g = len(GROUPS)
    width = heads * dh
    qkv3 = qkv.reshape(batch, t, qkv.shape[1])
    in_specs, args, n_bufs = [], [], []
    vmem = 8 * t * width * 4
    for g in range(n_g):
        n_buf = caches[g].shape[1]
        parts = n_parts if n_buf % (n_parts * BLK) == 0 else 1
        rows = n_buf // parts
        flat = caches[g].reshape(batch, n_buf * 2 * heads, dh)
        yc, yn = _step_bias_vectors(rel_bias, g, heads, n_buf, parts)
        new = lambda comp, g=g: pl.BlockSpec((1, t, width), lambda b, s: (b, 0, comp * n_g + g))
        part_of = (lambda s: s) if parts == n_parts else (lambda s: 0)
        in_specs += [new(0), new(1), new(2),
                     pl.BlockSpec((1, rows * 2 * heads, dh), lambda b, s, f=part_of: (b, f(s), 0)),
                     pl.BlockSpec((1, heads, rows + BLK), lambda b, s, f=part_of: (f(s), 0, 0)),
                     pl.BlockSpec(yn.shape, lambda b, s: (0, 0))]
        args += [qkv3, qkv3, qkv3, flat, yc, yn]
        n_bufs.append(n_buf)
        vmem += 2 * rows * 2 * width * 4 + 6 * rows * dh * 4
    return pl.pallas_call(
        functools.partial(_attn_step_kernel, heads=heads, dh=dh, n_bufs=tuple(n_bufs)),
        name="attn_step",
        out_shape=jax.ShapeDtypeStruct((batch, t, width), F32),
        grid=(batch, n_parts),
        in_specs=in_specs,
        out_specs=pl.BlockSpec((1, t, width), lambda b, s: (b, 0, 0)),
        scratch_shapes=[pltpu.VMEM((t, width), F32)] * 3 + [pltpu.VMEM((BLK, width), F32)] * 2,
        compiler_params=_cparams(("parallel", "arbitrary"), vmem),
    )(*args).reshape(batch * t, width)


def _attn_out_kernel(x_ref, o_ref, wo_ref, out_ref):
    out_ref[...] = x_ref[...] + _dot(o_ref[...].astype(BF16), wo_ref[...])


def _attn_out(x, o, wo, tm_pref=512):
    m, d = x.shape
    width = wo.shape[0]
    tm = _tile(m, tm_pref)
    row = lambda n: pl.BlockSpec((tm, n), lambda i: (i, 0))
    vmem = 4 * tm * d * 4 + 2 * tm * width * 4 + 2 * width * d * 2
    return pl.pallas_call(
        _attn_out_kernel,
        name=f"attn_out_m{m}",
        out_shape=jax.ShapeDtypeStruct((m, d), F32),
        grid=(m // tm,),
        in_specs=[row(d), row(width), pl.BlockSpec((width, d), lambda i: (0, 0))],
        out_specs=row(d),
        compiler_params=_cparams(("parallel",), vmem),
    )(x, o, wo)


def _t5_bucket(dist):
    max_exact = N_BUCKETS // 2
    d = np.asarray(dist, dtype=np.int32)
    large = max_exact + (np.log(np.maximum(d, max_exact) / max_exact) / np.log(MAX_DISTANCE / max_exact)
                         * (N_BUCKETS - max_exact)).astype(np.int32)
    return np.where(d < max_exact, d, np.minimum(large, N_BUCKETS - 1)).astype(np.int32)


def _group_bias(rel_bias, g, heads):
    buckets = _t5_bucket(np.arange(N_DIL_KEYS) * GROUPS[g][1])
    return rel_bias[:, g * heads:(g + 1) * heads].astype(F32)[buckets]


def _prompt_bias_vectors(rel_bias, heads):
    vecs = []
    for g in range(len(GROUPS)):
        rev = _group_bias(rel_bias, g, heads)[::-1]
        vecs.append(jnp.pad(rev, ((0, 2 * BLK - N_DIL_KEYS), (0, 0))).T)
    return jnp.concatenate(vecs, axis=0)


def _step_bias_vectors(rel_bias, g, heads, n_buf, parts):
    rows = n_buf // parts
    bias = _group_bias(rel_bias, g, heads)
    at = lambda dist: bias[np.clip(dist // GROUPS[g][1], 0, N_DIL_KEYS - 1)]
    first = np.arange(parts)[:, None] * rows + np.arange(rows + BLK)[None, :]
    yc = jnp.transpose(at(n_buf + BLK - first), (0, 2, 1))
    yn = at(BLK - np.arange(2 * BLK)).T
    return yc, yn


def _kv_rows_kernel(k_ref, v_ref, o_ref, *, heads, dh):
    rows = k_ref.shape[0]
    o_ref[0, :, 0] = k_ref[...].reshape(rows, heads, dh)
    o_ref[0, :, 1] = v_ref[...].reshape(rows, heads, dh)


def _kv_rows(qkv, g, batch, seq, n_rows, heads, dh):
    n_g = len(GROUPS)
    width = heads * dh
    tm = _tile(n_rows, 256)
    assert seq % tm == 0 and n_rows % tm == 0
    first = (seq - n_rows) // tm
    per_seq = seq // tm
    col = lambda comp: pl.BlockSpec((tm, width), lambda b, i: (b * per_seq + first + i, comp * n_g + g))
    return pl.pallas_call(
        functools.partial(_kv_rows_kernel, heads=heads, dh=dh),
        name=f"kv_rows_g{g}_n{n_rows}",
        out_shape=jax.ShapeDtypeStruct((batch, n_rows, 2, heads, dh), qkv.dtype),
        grid=(batch, n_rows // tm),
        in_specs=[col(1), col(2)],
        out_specs=pl.BlockSpec((1, tm, 2, heads, dh), lambda b, i: (b, i, 0, 0, 0)),
        compiler_params=_cparams(("parallel", "parallel"), 8 * tm * width * 4),
    )(qkv, qkv)


def _trunk(x3, st, p, tiles, bank, emit):
    batch, seq, d = x3.shape
    m = batch * seq
    x = x3.reshape(m, d)
    new = []
    depth = p['ffn_w1'].shape[0]
    for i in range(depth):
        kind = i % N_MIXERS
        g_mix = p['norm_mix_g'][i][None]
        if kind == 0:
            j = i // N_MIXERS
            n_taps, c = p['conv_dw_w'].shape[1:]
            buf = jnp.zeros((batch, n_taps - 1, c), F32) if st is None else st[i][0]
            if ('glu', j) in bank:
                u = _norm_proj(x, g_mix, bank['glu', j], 'glu', tiles['proj_m'])
            else:
                u, *halves = _norm_proj(x, g_mix, p['conv_w_glu'], 'glu', tiles['proj_m'], layer=j, emit=True)
                bank['glu', j] = tuple(halves)
            u3 = u.reshape(batch, seq, c)
            tm = _tile(seq, tiles['conv_m'])
            halo = _halos(u3, buf, tm, CONV_HALO)
            dww = jnp.repeat(p['conv_dw_w'][j], SUBLANES, axis=0)
            w_out = bank['conv_out', j] if ('conv_out', j) in bank else p['conv_w_out'][j].astype(BF16)
            x = _conv_tail(x, u, halo, dww, p['conv_dw_b'][j][None], p['conv_ln_g'][j][None],
                           p['conv_ln_b'][j][None], w_out, j, tm, n_taps)
            new.append((_last_rows(buf, u3, n_taps - 1),))
        elif kind == 1:
            n_taps, c = p['lru_conv_w'].shape
            if st is None:
                h0 = jnp.zeros((batch, c), F32)
                cb = jnp.zeros((batch, n_taps - 1, c), F32)
            else:
                h0, cb = st[i]
            if 'lru_in' in bank:
                ybr, xbr = _norm_proj(x, g_mix, bank['lru_in'], 'split', tiles['proj_m'])
            else:
                ybr, xbr, *halves = _norm_proj(x, g_mix, p['lru_w_in'], 'split', tiles['proj_m'], emit=True)
                bank['lru_in'] = tuple(halves)
            w_out = bank['lru_out'] if 'lru_out' in bank else p['lru_w_out'].astype(BF16)
            xbr3 = xbr.reshape(batch, seq, c)
            tm = _tile(seq, tiles['lru_m'])
            halo = _halos(xbr3, cb, tm, LRU_HALO)
            cw = jnp.pad(p['lru_conv_w'], ((0, SUBLANES - n_taps), (0, 0)))
            x, hl = _lru_tail(x, ybr, xbr, halo, h0[:, None], cw, p['lru_conv_b'][None], p['lru_w_a'],
                              p['lru_b_a'][None], p['lru_w_x'], p['lru_b_x'][None], p['lru_lambda'][None],
                              w_out, batch, tm, n_taps)
            new.append((hl[:, 0], _last_rows(cb, xbr3, n_taps - 1)))
        else:
            rel_bias = p['rel_bias']
            heads = rel_bias.shape[1] // len(GROUPS)
            width = p['att_w_o'].shape[0]
            dh = width // heads
            if 'qkv' in bank:
                qkv = _norm_proj(x, g_mix, bank['qkv'], 'plain', tiles['proj_m'], tn_pref=1024)
            else:
                qkv, bank['qkv'] = _norm_proj(x, g_mix, p['att_w_qkv'], 'plain', tiles['proj_m'], emit=True)
            if st is None:
                assert seq % CHUNK == 0
                o = _attn_prompt(qkv, _prompt_bias_vectors(rel_bias, heads), batch, seq, heads, dh)
                rows = [_kv_rows(qkv, g, batch, seq, min(win, seq), heads, dh) for g, (win, _) in enumerate(GROUPS)]
            else:
                assert seq <= BLK
                o = _attn_step(qkv, st[i], rel_bias, batch, seq, heads, dh)
                rows = [_kv_rows(qkv, g, batch, seq, seq, heads, dh) for g in range(len(GROUPS))]
            x = _attn_out(x, o, p['att_w_o'], tiles['attn_out_m'])
            new.append(rows)
        g_ffn, g_fin, final = p['norm_ffn_g'][i][None], p['final_norm_g'][None], i == depth - 1
        ffn_tiles = dict(tm_pref=tiles['ffn_m'], tf_pref=tiles['ffn_f'])
        jobs = _cast_jobs(p, i + 1) if not emit and i + 1 < depth else []
        if ('ffn_w1', i) not in bank:
            x, bank['ffn_w1', i], bank['ffn_w2', i] = _ffn(x, g_ffn, p['ffn_w1'], p['ffn_w2'], i, g_fin, final,
                                                           emit=True, **ffn_tiles)
        elif jobs and _can_cast(m, p['ffn_w1'].shape[-1], [mat.shape[-2] for _, mat, _ in jobs], **ffn_tiles):
            x, *casts = _ffn(x, g_ffn, bank['ffn_w1', i], bank['ffn_w2', i], i, g_fin, final, **ffn_tiles,
                             cast=[(mat, mat_layer) for _, mat, mat_layer in jobs])
            bank.update({key: cast for (key, _, _), cast in zip(jobs, casts)})
        else:
            x = _ffn(x, g_ffn, bank['ffn_w1', i], bank['ffn_w2', i], i, g_fin, final, **ffn_tiles)
        yield
    return x.reshape(batch, seq, d), new


def _cast_jobs(p, layer):
    jobs = [(('ffn_w1', layer), p['ffn_w1'], layer), (('ffn_w2', layer), p['ffn_w2'], layer)]
    kind = layer % N_MIXERS
    if kind == 0:
        j = layer // N_MIXERS
        jobs += [(('glu', j), p['conv_w_glu'], j), (('conv_out', j), p['conv_w_out'], j)]
    elif kind == 1:
        jobs += [('lru_in', p['lru_w_in'], None), ('lru_out', p['lru_w_out'], None)]
    else:
        jobs += [('qkv', p['att_w_qkv'], None)]
    return jobs


def _run_alternating(*gens):
    results = [None] * len(gens)
    live = list(range(len(gens)))
    while live:
        for k in list(live):
            try:
                next(gens[k])
            except StopIteration as done:
                results[k] = done.value
                live.remove(k)
    return results


PROMPT_TILES = dict(proj_m=1024, conv_m=256, lru_m=256, attn_out_m=512, ffn_m=512, ffn_f=1024)
SAMPLE_TILES = dict(proj_m=512, conv_m=256, lru_m=256, attn_out_m=512, ffn_m=512, ffn_f=512)
MATMUL_WEIGHTS = ('lru_w_a', 'lru_w_x', 'att_w_o')


def kernel(x_prompt, x_sample, state_conv_l0, state_lru_h_l1, state_lru_conv_l1, cache_kv_w128_l2, cache_kv_w512_l2, cache_kv_w2048_l2, state_conv_l3, norm_mix_g, norm_ffn_g, final_norm_g, ffn_w1, ffn_w2, conv_w_glu, conv_dw_w, conv_dw_b, conv_ln_g, conv_ln_b, conv_w_out, lru_w_in, lru_conv_w, lru_conv_b, lru_w_a, lru_b_a, lru_w_x, lru_b_x, lru_lambda, lru_w_out, att_w_qkv, att_w_o, rel_bias):
    p = dict(norm_mix_g=norm_mix_g, norm_ffn_g=norm_ffn_g, final_norm_g=final_norm_g, ffn_w1=ffn_w1,
             ffn_w2=ffn_w2, conv_w_glu=conv_w_glu, conv_dw_w=conv_dw_w, conv_dw_b=conv_dw_b,
             conv_ln_g=conv_ln_g, conv_ln_b=conv_ln_b, conv_w_out=conv_w_out, lru_w_in=lru_w_in,
             lru_conv_w=lru_conv_w, lru_conv_b=lru_conv_b, lru_w_a=lru_w_a, lru_b_a=lru_b_a, lru_w_x=lru_w_x,
             lru_b_x=lru_b_x, lru_lambda=lru_lambda, lru_w_out=lru_w_out, att_w_qkv=att_w_qkv,
             att_w_o=att_w_o, rel_bias=rel_bias)
    for name in MATMUL_WEIGHTS:
        p[name] = p[name].astype(BF16)
    bank = {}
    sample = _trunk(x_sample, [(state_conv_l0,), (state_lru_h_l1, state_lru_conv_l1),
                               (cache_kv_w128_l2, cache_kv_w512_l2, cache_kv_w2048_l2),
                               (state_conv_l3,)], p, SAMPLE_TILES, bank, emit=True)
    prompt = _trunk(x_prompt, None, p, PROMPT_TILES, bank, emit=False)
    (y_sample, sst), (y_prompt, pst) = _run_alternating(sample, prompt)
    return (y_prompt, y_sample,
            pst[0][0], sst[0][0],
            pst[1][0], sst[1][0],
            pst[1][1], sst[1][1],
            pst[2][0], sst[2][0],
            pst[2][1], sst[2][1],
            pst[2][2], sst[2][2],
            pst[3][0], sst[3][0])
```
